```python
import math
import jax
import jax.numpy as jnp
from jax import lax
import numpy as np

D_MODEL = 1024
BATCH = 4
SEQ = 8192
DEPTH = 2

HEAD_DIM = 64
A_HEADS = 6
A_CONFIGS = ((128, 1), (512, 4), (2048, 16))
B_HEADS = 4
B_KV_HEADS = 2
B_RADIUS = 128
C_HEADS = 6
C_NOPE = 64
C_ROPE = 32
C_VDIM = 64
C_QK = C_NOPE + C_ROPE
C_Q_RANK = 256
C_KV_RANK = 128
ROPE_THETA = 10000.0
Q_BLOCK = 128
N_BUCKETS = 32
MAX_DISTANCE = 1024
BIAS_HEADS = A_HEADS + B_HEADS
D_FF = 4 * D_MODEL
EPS = 1e-6
NEG = -1e30

A_W = A_HEADS * HEAD_DIM
B_QW = B_HEADS * HEAD_DIM
B_KVW = B_KV_HEADS * HEAD_DIM
C_OW = C_HEADS * C_VDIM
D_MIX = A_W + B_QW + C_OW
IN_SPLITS = (A_W, A_W, A_W, B_QW, B_KVW, B_KVW, C_Q_RANK, C_KV_RANK, C_ROPE)
IN_WIDTH = 3 * A_W + B_QW + 2 * B_KVW + C_Q_RANK + C_KV_RANK + C_ROPE

kernel_name = 'hybrid_parallel_heads_encoder'


def rmsnorm(x, g):
    xf = x.astype(jnp.float32)
    y = xf * lax.rsqrt(jnp.mean(xf * xf, axis=-1, keepdims=True) + EPS)
    return (y * g.astype(jnp.float32)).astype(x.dtype)


def split_heads(t, n):
    return t.reshape(t.shape[0], t.shape[1], n, -1)


def t5_bucket(rel):
    half = N_BUCKETS // 2
    exact = half // 2
    n = jnp.abs(rel)
    far = exact + (jnp.log(jnp.maximum(n, 1).astype(jnp.float32) / exact)
                   / math.log(MAX_DISTANCE / exact) * (half - exact)).astype(jnp.int32)
    far = jnp.minimum(far, half - 1)
    return jnp.where(rel > 0, half, 0) + jnp.where(n < exact, n, far)


def band_rel(blk):
    return jnp.arange(3 * blk)[None, :] - blk - jnp.arange(blk)[:, None]


def rel_bias(table, dist):
    return jnp.transpose(table[t5_bucket(dist)], (2, 0, 1))


def banded_attention(q, k, v, radius, bias, sink=None):
    b, L, hq, d = q.shape
    hkv = k.shape[2]
    g = hq // hkv
    blk = radius
    nb = -(-L // blk)
    pad = nb * blk - L
    qb = jnp.pad(q, ((0, 0), (0, pad), (0, 0), (0, 0))).reshape(b, nb, blk, hkv, g, d)

    def windows(t):
        tp = jnp.pad(t, ((0, 0), (blk, pad + blk), (0, 0), (0, 0))).reshape(b, nb + 2, blk, hkv, d)
        return jnp.concatenate([tp[:, :-2], tp[:, 1:-1], tp[:, 2:]], axis=2)

    kw, vw = windows(k), windows(v)
    s = jnp.einsum('bnqhgd,bnkhd->bnhgqk', qb, kw, preferred_element_type=jnp.float32) * (d ** -0.5)
    s = s + bias.astype(jnp.float32).reshape(hkv, g, blk, 3 * blk)
    rel = band_rel(blk)
    kpos = jnp.arange(nb)[:, None, None] * blk + jnp.arange(3 * blk)[None, None, :] - blk
    mask = (jnp.abs(rel)[None] <= radius) & (kpos >= 0) & (kpos < L)
    s = jnp.where(mask[None, :, None, None], s, NEG)
    m = jnp.max(s, axis=-1)
    if sink is not None:
        sk = sink.astype(jnp.float32).reshape(hkv, g)[None, None, :, :, None]
        m = jnp.maximum(m, sk)
    p = jnp.exp(s - m[..., None])
    l = jnp.sum(p, axis=-1)
    if sink is not None:
        l = l + jnp.exp(sk - m)
    o = jnp.einsum('bnhgqk,bnkhd->bnqhgd', p.astype(v.dtype), vw, preferred_element_type=jnp.float32)
    l_t = jnp.transpose(l, (0, 1, 4, 2, 3))
    m_t = jnp.transpose(m, (0, 1, 4, 2, 3))
    o = (o / l_t[..., None]).astype(v.dtype).reshape(b, nb * blk, hq, d)[:, :L]
    return o, m_t.reshape(b, nb * blk, hq)[:, :L], l_t.reshape(b, nb * blk, hq)[:, :L]


def dilated_attention(q, k, v, table):
    b, L, h, d = q.shape
    outs, ms, ls = [], [], []
    for window, r in A_CONFIGS:
        radius = window // (2 * r)
        n = L // r

        def by_residue(t):
            return t.reshape(b, n, r, h, d).transpose(0, 2, 1, 3, 4).reshape(b * r, n, h, d)

        bias = rel_bias(table, band_rel(radius) * r)
        o, m, l = banded_attention(by_residue(q), by_residue(k), by_residue(v), radius, bias)
        outs.append(o.reshape(b, r, n, h, d).transpose(0, 2, 1, 3, 4).reshape(b, L, h, d).astype(jnp.float32))
        ms.append(m.reshape(b, r, n, h).transpose(0, 2, 1, 3).reshape(b, L, h))
        ls.append(l.reshape(b, r, n, h).transpose(0, 2, 1, 3).reshape(b, L, h))
    m_all = jnp.stack(ms)
    w = jnp.stack(ls) * jnp.exp(m_all - jnp.max(m_all, axis=0))
    o = jnp.einsum('cblh,cblhd->blhd', w, jnp.stack(outs)) / jnp.sum(w, axis=0)[..., None]
    return o.astype(q.dtype)


def rope(t, positions):
    half = t.shape[-1] // 2
    inv = ROPE_THETA ** (-jnp.arange(half, dtype=jnp.float32) / half)
    ang = positions.astype(jnp.float32)[:, :, None, None] * inv
    cos, sin = jnp.cos(ang), jnp.sin(ang)
    t1 = t[..., :half].astype(jnp.float32)
    t2 = t[..., half:].astype(jnp.float32)
    return jnp.concatenate([t1 * cos - t2 * sin, t2 * cos + t1 * sin], axis=-1).astype(t.dtype)


def latent_attention(q, k, v):
    b, L, h, dqk = q.shape
    nq = L // Q_BLOCK
    qb = q.reshape(b, nq, Q_BLOCK, h, dqk).transpose(1, 0, 2, 3, 4)

    def block(qblk):
        s = jnp.einsum('bqhd,bkhd->bhqk', qblk, k, preferred_element_type=jnp.float32) * (dqk ** -0.5)
        p = jax.nn.softmax(s, axis=-1)
        return jnp.einsum('bhqk,bkhd->bqhd', p.astype(v.dtype), v)

    o = lax.map(block, qb)
    return o.transpose(1, 0, 2, 3, 4).reshape(b, L, h, v.shape[-1])


def setup_inputs(seed: int = 0) -> dict:
    key = jax.random.key(seed)
    ks = jax.random.split(key, 20)

    def nrm(k, shape, scale):
        return jax.random.normal(k, shape, jnp.float32) * scale

    def gain(k, shape):
        return 1.0 + 0.02 * jax.random.normal(k, shape, jnp.float32)

    offset = jax.random.randint(ks[1], (BATCH, 1), 0, 4096, dtype=jnp.int32)
    positions = (jnp.arange(SEQ, dtype=jnp.int32)[None, :] + offset).astype(jnp.int32)
    return {
        'x': nrm(ks[0], (BATCH, SEQ, D_MODEL), 1.0),
        'positions': positions,
        'rel_bias_table': nrm(ks[2], (N_BUCKETS, BIAS_HEADS), 0.2),
        'norm_mix': gain(ks[3], (DEPTH, D_MODEL)),
        'w_in': nrm(ks[4], (DEPTH, D_MODEL, IN_WIDTH), D_MODEL ** -0.5),
        'qk_gain_a': gain(ks[5], (DEPTH, 2, HEAD_DIM)),
        'qk_gain_b': gain(ks[6], (DEPTH, 2, HEAD_DIM)),
        'sink_b': nrm(ks[7], (DEPTH, B_HEADS), 0.5),
        'q_lat_gain': gain(ks[8], (DEPTH, C_Q_RANK)),
        'kv_lat_gain': gain(ks[9], (DEPTH, C_KV_RANK)),
        'w_uq': nrm(ks[10], (DEPTH, C_Q_RANK, C_HEADS * C_QK), C_Q_RANK ** -0.5),
        'w_ukv': nrm(ks[11], (DEPTH, C_KV_RANK, C_HEADS * (C_NOPE + C_VDIM)), C_KV_RANK ** -0.5),
        'qk_gain_c': gain(ks[12], (DEPTH, 2, C_QK)),
        'out_norm': gain(ks[13], (DEPTH, D_MIX)),
        'w_out': nrm(ks[14], (DEPTH, D_MIX, D_MODEL), D_MIX ** -0.5),
        'norm_mlp': gain(ks[15], (DEPTH, D_MODEL)),
        'w_up': nrm(ks[16], (DEPTH, D_MODEL, D_FF), D_MODEL ** -0.5),
        'w_down': nrm(ks[17], (DEPTH, D_FF, D_MODEL), D_FF ** -0.5),
    }


def reference(x, positions, rel_bias_table, norm_mix, w_in, qk_gain_a, qk_gain_b, sink_b,
              q_lat_gain, kv_lat_gain, w_uq, w_ukv, qk_gain_c, out_norm, w_out,
              norm_mlp, w_up, w_down):
    b, L, _ = x.shape
    split_at = np.cumsum(IN_SPLITS)[:-1].tolist()
    table_a = rel_bias_table[:, :A_HEADS]
    bias_b = rel_bias(rel_bias_table[:, A_HEADS:], band_rel(B_RADIUS))
    for i in range(DEPTH):
        h = rmsnorm(x, norm_mix[i])
        qa, ka, va, qb, kb, vb, cq, ckv, kr = jnp.split(h @ w_in[i], split_at, axis=-1)

        qa = rmsnorm(split_heads(qa, A_HEADS), qk_gain_a[i, 0])
        ka = rmsnorm(split_heads(ka, A_HEADS), qk_gain_a[i, 1])
        o_a = dilated_attention(qa, ka, split_heads(va, A_HEADS), table_a)

        qb = rmsnorm(split_heads(qb, B_HEADS), qk_gain_b[i, 0])
        kb = rmsnorm(split_heads(kb, B_KV_HEADS), qk_gain_b[i, 1])
        o_b, _, _ = banded_attention(qb, kb, split_heads(vb, B_KV_HEADS), B_RADIUS, bias_b, sink_b[i])

        qc = split_heads(rmsnorm(cq, q_lat_gain[i]) @ w_uq[i], C_HEADS)
        kvc = split_heads(rmsnorm(ckv, kv_lat_gain[i]) @ w_ukv[i], C_HEADS)
        vc = kvc[..., C_NOPE:]
        kc = jnp.concatenate(
            [kvc[..., :C_NOPE], jnp.broadcast_to(kr[:, :, None, :], (b, L, C_HEADS, C_ROPE))], axis=-1)
        qc = rmsnorm(qc, qk_gain_c[i, 0])
        kc = rmsnorm(kc, qk_gain_c[i, 1])
        qc = jnp.concatenate([qc[..., :C_NOPE], rope(qc[..., C_NOPE:], positions)], axis=-1)
        kc = jnp.concatenate([kc[..., :C_NOPE], rope(kc[..., C_NOPE:], positions)], axis=-1)
        o_c = latent_attention(qc, kc, vc)

        g = out_norm[i]
        mixed = jnp.concatenate([
            rmsnorm(o_a.reshape(b, L, A_W), g[:A_W]),
            rmsnorm(o_b.reshape(b, L, B_QW), g[A_W:A_W + B_QW]),
            rmsnorm(o_c.reshape(b, L, C_OW), g[A_W + B_QW:]),
        ], axis=-1)
        x = x + mixed @ w_out[i]

        h = rmsnorm(x, norm_mlp[i])
        x = x + jnp.square(jax.nn.relu(h @ w_up[i])) @ w_down[i]
    return x
```

```python
import functools
import math

import numpy as np
import jax
import jax.numpy as jnp
from jax import lax
from jax.experimental import pallas as pl
from jax.experimental.pallas import tpu as pltpu

F32 = jnp.float32
BF16 = jnp.bfloat16

D_MODEL = 1024
HEAD_DIM = 64
A_HEADS = 6
A_CONFIGS = ((128, 1), (512, 4), (2048, 16))
B_HEADS = 4
B_KV_HEADS = 2
B_RADIUS = 128
C_HEADS = 6
C_NOPE = 64
C_ROPE = 32
C_VDIM = 64
C_QK = C_NOPE + C_ROPE
C_Q_RANK = 256
C_KV_RANK = 128
ROPE_THETA = 10000.0
N_BUCKETS = 32
MAX_DISTANCE = 1024
D_FF = 4 * D_MODEL
EPS = 1e-6
NEG = -1e30

A_W = A_HEADS * HEAD_DIM
B_QW = B_HEADS * HEAD_DIM
B_KVW = B_KV_HEADS * HEAD_DIM
LANES = 128
C_SLOTS = C_HEADS * LANES
IN_COLS = 3 * A_W + B_QW + 2 * B_KVW + C_Q_RANK + C_KV_RANK + LANES
MIX_COLS = A_W + B_QW + C_SLOTS
ML_L_OFFSET = 8

B_HEAD_ORDER = (0, 2, 1, 3)

TM_IN = 512
TQ_BAND = 256
SUB = 128
TQ_LAT = 512
TK_LAT = 512
TM_MLP = 512
FF_CHUNK = 1024
VMEM_LIMIT = 56 * 1024 * 1024


def _cparams(sem):
    return pltpu.CompilerParams(dimension_semantics=sem, vmem_limit_bytes=VMEM_LIMIT)


def _const_spec(shape):
    zeros = (0,) * len(shape)
    return pl.BlockSpec(shape, lambda *_: zeros)


def _rope_table_kernel(pos_ref, inv_ref, cos_ref, sin_ref):
    ang = pos_ref[...] * inv_ref[...]
    lane = lax.broadcasted_iota(jnp.int32, ang.shape, 1)
    c = jnp.cos(ang)
    s = jnp.sin(ang)
    first = (lane >= C_NOPE) & (lane < C_NOPE + C_ROPE // 2)
    second = (lane >= C_NOPE + C_ROPE // 2) & (lane < C_QK)
    cos_ref[...] = jnp.where(first | second, c, 1.0)
    sin_ref[...] = jnp.where(first, -s, jnp.where(second, s, 0.0))


def _rope_tables(positions):
    t = positions.size
    half = C_ROPE // 2
    inv = ROPE_THETA ** (-jnp.arange(half, dtype=F32) / half)
    inv_row = jnp.concatenate([jnp.zeros((C_NOPE,), F32), inv, inv, jnp.zeros((LANES - C_QK,), F32)])[None, :]
    pos = positions.astype(F32).reshape(t, 1)
    tm = 2048
    return pl.pallas_call(
        _rope_table_kernel,
        grid=(t // tm,),
        in_specs=[pl.BlockSpec((tm, 1), lambda i: (i, 0)), _const_spec((1, LANES))],
        out_specs=[pl.BlockSpec((tm, LANES), lambda i: (i, 0))] * 2,
        out_shape=[jax.ShapeDtypeStruct((t, LANES), F32)] * 2,
        compiler_params=_cparams(("parallel",)),
        name="rope_tables",
    )(pos, inv_row)


def _bucket_thresholds():
    half = N_BUCKETS // 2
    exact = half // 2
    n = np.arange(1, 2 * MAX_DISTANCE + 2, dtype=np.float64)
    far = exact + (np.log(n / exact) / math.log(MAX_DISTANCE / exact) * (half - exact)).astype(np.int64)
    far = np.minimum(far, half - 1)
    return tuple(int(n[np.argmax(far >= exact + k)]) for k in range(1, half - exact))


def _bias_kernel(table_ref, out_ref, *, radius, dilation, head_cols):
    hsel = pl.program_id(0)
    width = SUB + 2 * radius
    row = lax.broadcasted_iota(jnp.int32, (SUB, width), 0)
    col = lax.broadcasted_iota(jnp.int32, (SUB, width), 1)
    rel = col - radius - row
    n = jnp.abs(rel) * dilation
    half = N_BUCKETS // 2
    exact = half // 2
    far = jnp.full(n.shape, exact, jnp.int32)
    for thr in _bucket_thresholds():
        far = far + (n >= thr).astype(jnp.int32)
    bucket = jnp.where(rel > 0, half, 0) + jnp.where(n < exact, n, far)
    for idx, hc in enumerate(head_cols):
        @pl.when(hsel == idx)
        def _(hc=hc):
            val = jnp.zeros(n.shape, F32)
            for b in range(N_BUCKETS):
                val = jnp.where(bucket == b, table_ref[b, hc], val)
            out_ref[...] = jnp.where(jnp.abs(rel) <= radius, val, NEG)


def _bias_tiles(table, radius, dilation, head_cols):
    width = SUB + 2 * radius
    return pl.pallas_call(
        functools.partial(_bias_kernel, radius=radius, dilation=dilation, head_cols=head_cols),
        grid=(len(head_cols),),
        in_specs=[pl.BlockSpec(memory_space=pltpu.SMEM)],
        out_specs=pl.BlockSpec((None, SUB, width), lambda h: (h, 0, 0)),
        out_shape=jax.ShapeDtypeStruct((len(head_cols), SUB, width), F32),
        compiler_params=_cparams(("arbitrary",)),
        name="bias_tiles",
    )(table)


def _group_mean_sq(y, ones_ref, group):
    sq = (y * y).astype(BF16)
    width = y.shape[1]
    parts = []
    for s in range(0, width, 2 * LANES):
        w = min(2 * LANES, width - s)
        parts.append(jnp.dot(sq[:, s:s + w], ones_ref[:w, :w], preferred_element_type=F32))
    out = parts[0] if len(parts) == 1 else jnp.concatenate(parts, axis=1)
    return out * (1.0 / group)


def _slab_roll(y, shift):
    parts = [pltpu.roll(y[:, s:s + LANES], shift, 1) for s in range(0, y.shape[1], LANES)]
    return parts[0] if len(parts) == 1 else jnp.concatenate(parts, axis=1)


def _rope(y, cos_t, sin_t, lane):
    swapped = jnp.where(lane < C_NOPE + C_ROPE // 2, _slab_roll(y, LANES - C_ROPE // 2), _slab_roll(y, C_ROPE // 2))
    return y * cos_t + swapped * sin_t


def _inproj_kernel(x_ref, cos_ref, sin_ref, gmix_ref, w_ref, gqk_ref, ones64_ref, ones128_ref,
                   glq_ref, wuq_ref, gqc_ref, glkv_ref, wukv_ref, gkc_ref,
                   qa_ref, ka_ref, va_ref, qb_ref, kb_ref, vb_ref, qc_ref, kc_ref, vc_ref):
    x = x_ref[...]
    h = x * lax.rsqrt(jnp.mean(x * x, axis=-1, keepdims=True) + EPS) * gmix_ref[...]
    y = jnp.dot(h.astype(BF16), w_ref[...], preferred_element_type=F32)

    o_qb = 3 * A_W
    o_kb = o_qb + B_QW
    o_vb = o_kb + B_KVW
    o_cq = o_vb + B_KVW
    o_ckv = o_cq + C_Q_RANK
    o_kr = o_ckv + C_KV_RANK
    yn = jnp.concatenate([y[:, :2 * A_W], y[:, o_qb:o_vb]], axis=1)
    yn = yn * lax.rsqrt(_group_mean_sq(yn, ones64_ref, HEAD_DIM) + EPS) * gqk_ref[...]
    qa_ref[...] = yn[:, :A_W].astype(BF16)
    ka_ref[...] = yn[:, A_W:2 * A_W].astype(BF16)
    qb_ref[...] = yn[:, 2 * A_W:2 * A_W + B_QW].astype(BF16)
    kb_ref[...] = yn[:, 2 * A_W + B_QW:].astype(BF16)
    va_ref[...] = y[:, 2 * A_W:3 * A_W].astype(BF16)
    vb_ref[...] = y[:, o_vb:o_cq].astype(BF16)

    cos_t = jnp.concatenate([cos_ref[...]] * C_HEADS, axis=1)
    sin_t = jnp.concatenate([sin_ref[...]] * C_HEADS, axis=1)
    lane = lax.broadcasted_iota(jnp.int32, cos_t.shape, 1) % LANES
    cq = y[:, o_cq:o_ckv]
    cq = cq * lax.rsqrt(jnp.mean(cq * cq, axis=-1, keepdims=True) + EPS) * glq_ref[...]
    qc = jnp.dot(cq.astype(BF16), wuq_ref[...], preferred_element_type=F32)
    qc = qc * lax.rsqrt(_group_mean_sq(qc, ones128_ref, C_QK) + EPS) * gqc_ref[...]
    qc_ref[...] = _rope(qc, cos_t, sin_t, lane).astype(BF16)

    ckv = y[:, o_ckv:o_kr]
    ckv = ckv * lax.rsqrt(jnp.mean(ckv * ckv, axis=-1, keepdims=True) + EPS) * glkv_ref[...]
    kv = jnp.dot(ckv.astype(BF16), wukv_ref[...], preferred_element_type=F32)
    kr = y[:, o_kr:]
    kc = kv[:, :C_SLOTS] + jnp.concatenate([kr] * C_HEADS, axis=1)
    kc = kc * lax.rsqrt(_group_mean_sq(kc, ones128_ref, C_QK) + EPS) * gkc_ref[...]
    kc_ref[...] = _rope(kc, cos_t, sin_t, lane).astype(BF16)
    vc_ref[...] = jnp.where(lane == C_VDIM, 1.0, kv[:, C_SLOTS:]).astype(BF16)


def _inproj(x2d, cos_t, sin_t, p):
    t = x2d.shape[0]
    tm = TM_IN
    row = lambda w: pl.BlockSpec((tm, w), lambda i: (i, 0))
    outs = [(A_W, BF16)] * 3 + [(B_QW, BF16), (B_KVW, BF16), (B_KVW, BF16)] + [(C_SLOTS, BF16)] * 3
    return pl.pallas_call(
        _inproj_kernel,
        grid=(t // tm,),
        in_specs=[row(D_MODEL), row(LANES), row(LANES),
                  _const_spec((1, D_MODEL)), _const_spec((D_MODEL, IN_COLS)),
                  _const_spec((1, 2 * A_W + B_QW + B_KVW)),
                  _const_spec((2 * LANES, 2 * LANES)), _const_spec((2 * LANES, 2 * LANES)),
                  _const_spec((1, C_Q_RANK)), _const_spec((C_Q_RANK, C_SLOTS)), _const_spec((1, C_SLOTS)),
                  _const_spec((1, C_KV_RANK)), _const_spec((C_KV_RANK, 2 * C_SLOTS)), _const_spec((1, C_SLOTS))],
        out_specs=[row(w) for w, _ in outs],
        out_shape=[jax.ShapeDtypeStruct((t, w), d) for w, d in outs],
        compiler_params=_cparams(("parallel",)),
        name="inproj",
    )(x2d, cos_t, sin_t, p["gmix"], p["w_in"], p["gqk"], p["ones64"], p["ones128"],
      p["glq"], p["wuq"], p["gqc"], p["glkv"], p["wukv"], p["gkc"])


def _window(lo_ref, main_ref, hi_ref, a, radius, tq):
    start, end = a - radius, a + SUB + radius
    parts = []
    if start < 0:
        parts.append(lo_ref[radius + start:radius, :])
        start = 0
    parts.append(main_ref[start:min(end, tq), :])
    if end > tq:
        parts.append(hi_ref[0:end - tq, :])
    return parts[0] if len(parts) == 1 else jnp.concatenate(parts, axis=0)


def _banded_kernel(*refs, tq, radius, head_ids, kv_slab, has_sink, emit_ml, n_tiles):
    if has_sink:
        sink_ref, refs = refs[0], refs[1:]
    q_ref, klo_ref, k_ref, khi_ref, vlo_ref, v_ref, vhi_ref, bias_ref, o_ref = refs[:9]
    ml_ref = refs[9] if emit_ml else None
    tile = pl.program_id(2)
    width = SUB + 2 * radius
    lane = lax.broadcasted_iota(jnp.int32, (SUB, LANES), 1)
    col = lax.broadcasted_iota(jnp.int32, (SUB, width), 1)
    for a in range(0, tq, SUB):
        kw = _window(klo_ref, k_ref, khi_ref, a, radius, tq)
        vw = _window(vlo_ref, v_ref, vhi_ref, a, radius, tq)
        ml = jnp.zeros((SUB, LANES), F32)
        for pair, heads in enumerate(head_ids):
            qs = q_ref[a:a + SUB, pair * LANES:(pair + 1) * LANES]
            ks = kw[:, kv_slab[pair] * LANES:(kv_slab[pair] + 1) * LANES]
            vs = vw[:, kv_slab[pair] * LANES:(kv_slab[pair] + 1) * LANES]
            outs = []
            for e, hd in enumerate(heads):
                qm = jnp.where((lane < HEAD_DIM) == (e == 0), qs, jnp.zeros_like(qs))
                s = lax.dot_general(qm, ks, (((1,), (1,)), ((), ())), preferred_element_type=F32)
                s = s + bias_ref[hd]
                if a == 0:
                    s = jnp.where(col < jnp.where(tile == 0, radius, 0), NEG, s)
                if a == tq - SUB:
                    s = jnp.where(col >= jnp.where(tile == n_tiles - 1, SUB + radius, width), NEG, s)
                m = jnp.max(s, axis=-1, keepdims=True)
                if has_sink:
                    m = jnp.maximum(m, sink_ref[hd])
                pr = jnp.exp(s - m)
                l = jnp.sum(pr, axis=-1, keepdims=True)
                if has_sink:
                    l = l + jnp.exp(sink_ref[hd] - m)
                o = jnp.dot(pr.astype(BF16), vs, preferred_element_type=F32)
                outs.append(o / l)
                if emit_ml:
                    ml = jnp.where(lane == hd, m, ml)
                    ml = jnp.where(lane == ML_L_OFFSET + hd, l, ml)
            o_ref[a:a + SUB, pair * LANES:(pair + 1) * LANES] = jnp.where(
                lane < HEAD_DIM, outs[0], outs[1]).astype(o_ref.dtype)
        if emit_ml:
            ml_ref[a:a + SUB, :] = ml


def _banded(q, k, v, bias, *, batch, seq, dilation, radius, head_ids, kv_slab, sink, emit_ml, out_dtype):
    n = seq // dilation
    tq = min(TQ_BAND, n)
    n_tiles = n // tq
    qw, kw = q.shape[1], k.shape[1]
    view = lambda t: t.reshape(batch, n, dilation * t.shape[1])
    per_tile = tq // radius
    main = lambda w: pl.BlockSpec((None, tq, w), lambda b, c, i: (b, i, c))
    lo = pl.BlockSpec((None, radius, kw), lambda b, c, i: (b, jnp.maximum(i * per_tile - 1, 0), c))
    hi = pl.BlockSpec((None, radius, kw), lambda b, c, i: (b, jnp.minimum((i + 1) * per_tile, n // radius - 1), c))
    in_specs = [main(qw), lo, main(kw), hi, lo, main(kw), hi, _const_spec(bias.shape)]
    args = [view(q), view(k), view(k), view(k), view(v), view(v), view(v), bias]
    if sink is not None:
        in_specs = [pl.BlockSpec(memory_space=pltpu.SMEM)] + in_specs
        args = [sink] + args
    out_specs = [main(qw)]
    out_shape = [jax.ShapeDtypeStruct((batch, n, dilation * qw), out_dtype)]
    if emit_ml:
        out_specs.append(main(LANES))
        out_shape.append(jax.ShapeDtypeStruct((batch, n, dilation * LANES), F32))
    outs = pl.pallas_call(
        functools.partial(_banded_kernel, tq=tq, radius=radius, head_ids=head_ids, kv_slab=kv_slab,
                          has_sink=sink is not None, emit_ml=emit_ml, n_tiles=n_tiles),
        grid=(batch, dilation, n_tiles),
        in_specs=in_specs,
        out_specs=out_specs,
        out_shape=out_shape,
        compiler_params=_cparams(("parallel", "parallel", "parallel")),
        name="banded_r%d_d%d" % (radius, dilation),
    )(*args)
    return [o.reshape(batch * seq, -1) for o in outs]


def _latent_kernel(q_ref, k_ref, v_ref, o_ref, m_ref, acc_ref, *, tk):
    q = q_ref[...]
    m_ref[...] = jnp.full(m_ref.shape, -jnp.inf, F32)
    acc_ref[...] = jnp.zeros(acc_ref.shape, F32)

    def body(j, carry):
        start = pl.multiple_of(j * tk, tk)
        kc = k_ref[pl.ds(start, tk), :]
        vc = v_ref[pl.ds(start, tk), :]
        s = lax.dot_general(q, kc, (((1,), (1,)), ((), ())), preferred_element_type=F32)
        m_old = m_ref[...]
        m_new = jnp.maximum(m_old, jnp.max(s, axis=-1, keepdims=True))
        pr = jnp.exp(s - m_new)
        acc_ref[...] = jnp.exp(m_old - m_new) * acc_ref[...] + jnp.dot(
            pr.astype(BF16), vc, preferred_element_type=F32)
        m_ref[...] = m_new
        return carry

    lax.fori_loop(0, k_ref.shape[0] // tk, body, 0)
    acc = acc_ref[...]
    lane = lax.broadcasted_iota(jnp.int32, acc.shape, 1)
    o_ref[...] = jnp.where(lane < C_VDIM, acc / acc[:, C_VDIM:C_VDIM + 1], 0.0).astype(o_ref.dtype)


def _latent(qc, kc, vc, batch, seq):
    t = batch * seq
    tq = TQ_LAT
    nq = seq // tq
    return pl.pallas_call(
        functools.partial(_latent_kernel, tk=TK_LAT),
        grid=(batch, C_HEADS, nq),
        in_specs=[pl.BlockSpec((tq, LANES), lambda b, h, i: (b * nq + i, h)),
                  pl.BlockSpec((seq, LANES), lambda b, h, i: (b, h)),
                  pl.BlockSpec((seq, LANES), lambda b, h, i: (b, h))],
        out_specs=pl.BlockSpec((tq, LANES), lambda b, h, i: (b * nq + i, h)),
        out_shape=jax.ShapeDtypeStruct((t, C_SLOTS), BF16),
        scratch_shapes=[pltpu.VMEM((tq, 1), F32), pltpu.VMEM((tq, LANES), F32)],
        compiler_params=_cparams(("parallel", "parallel", "parallel")),
        name="latent",
    )(qc, kc, vc)


def _merge_mlp_kernel(x_ref, oa1_ref, oa2_ref, oa3_ref, ml1_ref, ml2_ref, ml3_ref, ob_ref, oc_ref,
                      expand_ref, gout_ref, wout_ref, gmlp_ref, wup_ref, wdown_ref, out_ref):
    mls = [ml1_ref[...], ml2_ref[...], ml3_ref[...]]
    lane = lax.broadcasted_iota(jnp.int32, mls[0].shape, 1)
    m_all = jnp.maximum(jnp.maximum(mls[0], mls[1]), mls[2])
    ws = [pltpu.roll(ml, LANES - ML_L_OFFSET, 1) * jnp.exp(ml - m_all) for ml in mls]
    wsum = ws[0] + ws[1] + ws[2]
    packed = jnp.zeros_like(wsum)
    for c, w in enumerate(ws):
        wn = jnp.where(lane < A_HEADS, w / wsum, 0.0)
        packed = packed + (wn if c == 0 else pltpu.roll(wn, ML_L_OFFSET * c, 1))
    hi = packed.astype(BF16)
    lo = (packed - hi.astype(F32)).astype(BF16)
    spread = (jnp.dot(hi, expand_ref[...], preferred_element_type=F32)
              + jnp.dot(lo, expand_ref[...], preferred_element_type=F32))
    oa = (spread[:, :A_W] * oa1_ref[...] + spread[:, A_W:2 * A_W] * oa2_ref[...]
          + spread[:, 2 * A_W:] * oa3_ref[...])

    def group_norm(v, width):
        return v * lax.rsqrt(jnp.sum(v * v, axis=-1, keepdims=True) * (1.0 / width) + EPS)

    ob = ob_ref[...].astype(F32)
    oc = oc_ref[...].astype(F32)
    mixed = jnp.concatenate([group_norm(oa, A_W), group_norm(ob, B_QW), group_norm(oc, C_HEADS * C_VDIM)], axis=1)
    mixed = (mixed * gout_ref[...]).astype(BF16)
    x = x_ref[...] + jnp.dot(mixed, wout_ref[...], preferred_element_type=F32)

    h = (x * lax.rsqrt(jnp.mean(x * x, axis=-1, keepdims=True) + EPS) * gmlp_ref[...]).astype(BF16)
    acc = x
    for s in range(0, D_FF, FF_CHUNK):
        u = jnp.dot(h, wup_ref[:, s:s + FF_CHUNK], preferred_element_type=F32)
        u = jnp.square(jnp.maximum(u, 0.0)).astype(BF16)
        acc = acc + jnp.dot(u, wdown_ref[s:s + FF_CHUNK, :], preferred_element_type=F32)
    out_ref[...] = acc


def _merge_mlp(x2d, oas, mls, ob, oc, p):
    t = x2d.shape[0]
    tm = TM_MLP
    row = lambda w: pl.BlockSpec((tm, w), lambda i: (i, 0))
    single = lambda shape: pl.BlockSpec(shape, lambda i: (0,) * len(shape), pipeline_mode=pl.Buffered(1))
    return pl.pallas_call(
        _merge_mlp_kernel,
        grid=(t // tm,),
        in_specs=[row(D_MODEL)] + [row(A_W)] * 3 + [row(LANES)] * 3 + [row(B_QW), row(C_SLOTS),
                  single((LANES, 3 * A_W)), single((1, MIX_COLS)), single((MIX_COLS, D_MODEL)),
                  single((1, D_MODEL)), single((D_MODEL, D_FF)), single((D_FF, D_MODEL))],
        out_specs=row(D_MODEL),
        out_shape=jax.ShapeDtypeStruct((t, D_MODEL), F32),
        compiler_params=_cparams(("parallel",)),
        name="merge_mlp",
    )(x2d, *oas, *mls, ob, oc, p["expand"], p["gout"], p["w_out"], p["gmlp"], p["w_up"], p["w_down"])


def _block_diag_ones(group):
    idx = np.arange(2 * LANES) // group
    return jnp.asarray((idx[:, None] == idx[None, :]).astype(np.float32), dtype=BF16)


def _expand_matrix():
    e = np.zeros((LANES, len(A_CONFIGS) * A_W), np.float32)
    for c in range(len(A_CONFIGS)):
        for h in range(A_HEADS):
            e[ML_L_OFFSET * c + h, c * A_W + h * HEAD_DIM:c * A_W + (h + 1) * HEAD_DIM] = 1.0
    return jnp.asarray(e, dtype=BF16)


def _pad_heads(w, heads, used, lo=0):
    lead = w.shape[:-1]
    w = w.reshape(lead + (heads, used))
    pad = [(0, 0)] * len(lead) + [(0, 0), (lo, LANES - lo - used)]
    return jnp.pad(w, pad).reshape(lead + (heads * LANES,))


def _layer_params(i, norm_mix, w_in, qk_gain_a, qk_gain_b, q_lat_gain, kv_lat_gain, w_uq, w_ukv, qk_gain_c,
                  out_norm, w_out, norm_mlp, w_up, w_down):
    w = w_in[i]
    o = np.cumsum((A_W, A_W, A_W, B_QW, B_KVW, B_KVW, C_Q_RANK, C_KV_RANK)).tolist()
    qb = w[:, o[2]:o[3]].reshape(D_MODEL, B_HEADS, HEAD_DIM)[:, B_HEAD_ORDER, :].reshape(D_MODEL, B_QW)
    kr = _pad_heads(w[:, o[7]:], 1, C_ROPE, lo=C_NOPE)
    w_in_p = jnp.concatenate([w[:, :o[2]], qb, w[:, o[3]:o[7]], kr], axis=1).astype(BF16)

    scale = HEAD_DIM ** -0.5
    gqk = jnp.concatenate([jnp.tile(qk_gain_a[i, 0], A_HEADS) * scale, jnp.tile(qk_gain_a[i, 1], A_HEADS),
                           jnp.tile(qk_gain_b[i, 0], B_HEADS) * scale, jnp.tile(qk_gain_b[i, 1], B_KV_HEADS)])
    ukv = w_ukv[i].reshape(C_KV_RANK, C_HEADS, C_NOPE + C_VDIM)
    wukv = jnp.concatenate([_pad_heads(ukv[:, :, :C_NOPE].reshape(C_KV_RANK, -1), C_HEADS, C_NOPE),
                            _pad_heads(ukv[:, :, C_NOPE:].reshape(C_KV_RANK, -1), C_HEADS, C_VDIM)], axis=1)
    g = out_norm[i]
    gb = g[A_W:A_W + B_QW].reshape(B_HEADS, HEAD_DIM)[B_HEAD_ORDER, :].reshape(B_QW)
    gout = jnp.concatenate([g[:A_W], gb, _pad_heads(g[A_W + B_QW:], C_HEADS, C_VDIM)])
    wo = w_out[i]
    wob = wo[A_W:A_W + B_QW].reshape(B_HEADS, HEAD_DIM, D_MODEL)[B_HEAD_ORDER, :, :].reshape(B_QW, D_MODEL)
    woc = jnp.pad(wo[A_W + B_QW:].reshape(C_HEADS, C_VDIM, D_MODEL), ((0, 0), (0, LANES - C_VDIM), (0, 0)))
    w_out_p = jnp.concatenate([wo[:A_W], wob, woc.reshape(C_SLOTS, D_MODEL)], axis=0).astype(BF16)
    return {
        "gmix": norm_mix[i][None, :],
        "w_in": w_in_p,
        "gqk": gqk[None, :],
        "ones64": _block_diag_ones(HEAD_DIM),
        "ones128": _block_diag_ones(LANES),
        "glq": q_lat_gain[i][None, :],
        "wuq": _pad_heads(w_uq[i], C_HEADS, C_QK).astype(BF16),
        "gqc": jnp.tile(_pad_heads(qk_gain_c[i, 0] * C_QK ** -0.5, 1, C_QK), C_HEADS)[None, :],
        "glkv": kv_lat_gain[i][None, :],
        "wukv": wukv.astype(BF16),
        "gkc": jnp.tile(_pad_heads(qk_gain_c[i, 1], 1, C_QK), C_HEADS)[None, :],
        "expand": _expand_matrix(),
        "gout": gout[None, :],
        "w_out": w_out_p,
        "gmlp": norm_mlp[i][None, :],
        "w_up": w_up[i].astype(BF16),
        "w_down": w_down[i].astype(BF16),
    }


def kernel(x, positions, rel_bias_table, norm_mix, w_in, qk_gain_a, qk_gain_b, sink_b, q_lat_gain, kv_lat_gain,
           w_uq, w_ukv, qk_gain_c, out_norm, w_out, norm_mlp, w_up, w_down):
    batch, seq, _ = x.shape
    depth = w_in.shape[0]
    x2d = x.reshape(batch * seq, D_MODEL)
    cos_t, sin_t = _rope_tables(positions)
    a_pairs = tuple((2 * p, 2 * p + 1) for p in range(A_HEADS // 2))
    bias_a = [_bias_tiles(rel_bias_table, window // (2 * r), r, tuple(range(A_HEADS))) for window, r in A_CONFIGS]
    bias_b = _bias_tiles(rel_bias_table, B_RADIUS, 1, tuple(A_HEADS + h for h in range(B_HEADS)))
    for i in range(depth):
        p = _layer_params(i, norm_mix, w_in, qk_gain_a, qk_gain_b, q_lat_gain, kv_lat_gain, w_uq, w_ukv,
                          qk_gain_c, out_norm, w_out, norm_mlp, w_up, w_down)
        qa, ka, va, qb, kb, vb, qc, kc, vc = _inproj(x2d, cos_t, sin_t, p)
        oas, mls = [], []
        for (window, r), bias in zip(A_CONFIGS, bias_a):
            o, ml = _banded(qa, ka, va, bias, batch=batch, seq=seq, dilation=r, radius=window // (2 * r),
                            head_ids=a_pairs, kv_slab=(0, 1, 2), sink=None, emit_ml=True, out_dtype=F32)
            oas.append(o)
            mls.append(ml)
        (ob,) = _banded(qb, kb, vb, bias_b, batch=batch, seq=seq, dilation=1, radius=B_RADIUS,
                        head_ids=((0, 2), (1, 3)), kv_slab=(0, 0), sink=sink_b[i], emit_ml=False, out_dtype=BF16)
        oc = _latent(qc, kc, vc, batch, seq)
        x2d = _merge_mlp(x2d, oas, mls, ob, oc, p)
    return x2d.reshape(batch, seq, D_MODEL)
```

```python
import functools
import math

import numpy as np
import jax
import jax.numpy as jnp
from jax import lax
from jax.experimental import pallas as pl
from jax.experimental.pallas import tpu as pltpu

F32 = jnp.float32
BF16 = jnp.bfloat16

D_MODEL = 1024
HEAD_DIM = 64
A_HEADS = 6
A_CONFIGS = ((128, 1), (512, 4), (2048, 16))
B_HEADS = 4
B_KV_HEADS = 2
B_RADIUS = 128
C_HEADS = 6
C_NOPE = 64
C_ROPE = 32
C_VDIM = 64
C_QK = C_NOPE + C_ROPE
C_Q_RANK = 256
C_KV_RANK = 128
ROPE_THETA = 10000.0
N_BUCKETS = 32
MAX_DISTANCE = 1024
D_FF = 4 * D_MODEL
EPS = 1e-6
NEG = -1e30

A_W = A_HEADS * HEAD_DIM
B_QW = B_HEADS * HEAD_DIM
B_KVW = B_KV_HEADS * HEAD_DIM
LANES = 128
C_SLOTS = C_HEADS * LANES
IN_COLS = 3 * A_W + B_QW + 2 * B_KVW + C_Q_RANK + C_KV_RANK + LANES
MIX_COLS = A_W + B_QW + C_SLOTS
VT_ROWS = 80
LOG2E = math.log2(math.e)
ML_L_OFFSET = 8

B_HEAD_ORDER = (0, 2, 1, 3)

TM_IN = 512
TQ_BAND = 256
SUB = 128
LAT_UNROLL = 4
TM_MLP = 512
FF_CHUNK = 1024
VMEM_LIMIT = 56 * 1024 * 1024


def _cparams(sem):
    return pltpu.CompilerParams(dimension_semantics=sem, vmem_limit_bytes=VMEM_LIMIT)


def _const_spec(shape):
    zeros = (0,) * len(shape)
    return pl.BlockSpec(shape, lambda *_: zeros)


def _rope_table_kernel(pos_ref, inv_ref, cos_ref, sin_ref):
    ang = pos_ref[...] * inv_ref[...]
    lane = lax.broadcasted_iota(jnp.int32, ang.shape, 1)
    c = jnp.cos(ang)
    s = jnp.sin(ang)
    first = (lane >= C_NOPE) & (lane < C_NOPE + C_ROPE // 2)
    second = (lane >= C_NOPE + C_ROPE // 2) & (lane < C_QK)
    cos_ref[...] = jnp.where(first | second, c, 1.0)
    sin_ref[...] = jnp.where(first, -s, jnp.where(second, s, 0.0))


def _rope_tables(positions):
    t = positions.size
    half = C_ROPE // 2
    inv = ROPE_THETA ** (-jnp.arange(half, dtype=F32) / half)
    inv_row = jnp.concatenate([jnp.zeros((C_NOPE,), F32), inv, inv, jnp.zeros((LANES - C_QK,), F32)])[None, :]
    pos = positions.astype(F32).reshape(t, 1)
    tm = 2048
    return pl.pallas_call(
        _rope_table_kernel,
        grid=(t // tm,),
        in_specs=[pl.BlockSpec((tm, 1), lambda i: (i, 0)), _const_spec((1, LANES))],
        out_specs=[pl.BlockSpec((tm, LANES), lambda i: (i, 0))] * 2,
        out_shape=[jax.ShapeDtypeStruct((t, LANES), F32)] * 2,
        compiler_params=_cparams(("parallel",)),
        name="rope_tables",
    )(pos, inv_row)


def _bucket_thresholds():
    half = N_BUCKETS // 2
    exact = half // 2
    n = np.arange(1, 2 * MAX_DISTANCE + 2, dtype=np.float64)
    far = exact + (np.log(n / exact) / math.log(MAX_DISTANCE / exact) * (half - exact)).astype(np.int64)
    far = np.minimum(far, half - 1)
    return tuple(int(n[np.argmax(far >= exact + k)]) for k in range(1, half - exact))


def _bias_kernel(table_ref, out_ref, *, radius, dilation, head_cols):
    hsel = pl.program_id(0)
    width = SUB + 2 * radius
    row = lax.broadcasted_iota(jnp.int32, (SUB, width), 0)
    col = lax.broadcasted_iota(jnp.int32, (SUB, width), 1)
    rel = col - radius - row
    n = jnp.abs(rel) * dilation
    half = N_BUCKETS // 2
    exact = half // 2
    far = jnp.full(n.shape, exact, jnp.int32)
    for thr in _bucket_thresholds():
        far = far + (n >= thr).astype(jnp.int32)
    bucket = jnp.where(rel > 0, half, 0) + jnp.where(n < exact, n, far)
    for idx, hc in enumerate(head_cols):
        @pl.when(hsel == idx)
        def _(hc=hc):
            val = jnp.zeros(n.shape, F32)
            for b in range(N_BUCKETS):
                val = jnp.where(bucket == b, table_ref[b, hc], val)
            out_ref[...] = jnp.where(jnp.abs(rel) <= radius, val, NEG)


def _bias_tiles(table, radius, dilation, head_cols):
    width = SUB + 2 * radius
    return pl.pallas_call(
        functools.partial(_bias_kernel, radius=radius, dilation=dilation, head_cols=head_cols),
        grid=(len(head_cols),),
        in_specs=[pl.BlockSpec(memory_space=pltpu.SMEM)],
        out_specs=pl.BlockSpec((None, SUB, width), lambda h: (h, 0, 0)),
        out_shape=jax.ShapeDtypeStruct((len(head_cols), SUB, width), F32),
        compiler_params=_cparams(("arbitrary",)),
        name="bias_tiles",
    )(table)


def _group_mean_sq(y, ones_ref, group):
    sq = (y * y).astype(BF16)
    width = y.shape[1]
    parts = []
    for s in range(0, width, 2 * LANES):
        w = min(2 * LANES, width - s)
        parts.append(jnp.dot(sq[:, s:s + w], ones_ref[:w, :w], preferred_element_type=F32))
    out = parts[0] if len(parts) == 1 else jnp.concatenate(parts, axis=1)
    return out * (1.0 / group)


def _slab_roll(y, shift):
    parts = [pltpu.roll(y[:, s:s + LANES], shift, 1) for s in range(0, y.shape[1], LANES)]
    return parts[0] if len(parts) == 1 else jnp.concatenate(parts, axis=1)


def _rope(y, cos_t, sin_t, lane):
    swapped = jnp.where(lane < C_NOPE + C_ROPE // 2, _slab_roll(y, LANES - C_ROPE // 2), _slab_roll(y, C_ROPE // 2))
    return y * cos_t + swapped * sin_t


def _inproj_kernel(x_ref, cos_ref, sin_ref, gmix_ref, w_ref, gqk_ref, ones64_ref, ones128_ref,
                   glq_ref, wuq_ref, gqc_ref, glkv_ref, wuk_ref, wuvt_ref, gkc_ref,
                   qa_ref, ka_ref, va_ref, qb_ref, kb_ref, vb_ref, qc_ref, kc_ref, vt_ref):
    x = x_ref[...]
    h = x * lax.rsqrt(jnp.mean(x * x, axis=-1, keepdims=True) + EPS) * gmix_ref[...]
    y = jnp.dot(h.astype(BF16), w_ref[...], preferred_element_type=F32)

    o_qb = 3 * A_W
    o_kb = o_qb + B_QW
    o_vb = o_kb + B_KVW
    o_cq = o_vb + B_KVW
    o_ckv = o_cq + C_Q_RANK
    o_kr = o_ckv + C_KV_RANK
    yn = jnp.concatenate([y[:, :2 * A_W], y[:, o_qb:o_vb]], axis=1)
    yn = yn * lax.rsqrt(_group_mean_sq(yn, ones64_ref, HEAD_DIM) + EPS) * gqk_ref[...]
    qa_ref[...] = yn[:, :A_W].astype(BF16)
    ka_ref[...] = yn[:, A_W:2 * A_W].astype(BF16)
    qb_ref[...] = yn[:, 2 * A_W:2 * A_W + B_QW].astype(BF16)
    kb_ref[...] = yn[:, 2 * A_W + B_QW:].astype(BF16)
    va_ref[...] = y[:, 2 * A_W:3 * A_W].astype(BF16)
    vb_ref[...] = y[:, o_vb:o_cq].astype(BF16)

    cos_t = jnp.concatenate([cos_ref[...]] * C_HEADS, axis=1)
    sin_t = jnp.concatenate([sin_ref[...]] * C_HEADS, axis=1)
    lane = lax.broadcasted_iota(jnp.int32, cos_t.shape, 1) % LANES
    cq = y[:, o_cq:o_ckv]
    cq = cq * lax.rsqrt(jnp.mean(cq * cq, axis=-1, keepdims=True) + EPS) * glq_ref[...]
    qc = jnp.dot(cq.astype(BF16), wuq_ref[...], preferred_element_type=F32)
    qc = qc * lax.rsqrt(_group_mean_sq(qc, ones128_ref, C_QK) + EPS) * gqc_ref[...]
    qc_ref[...] = _rope(qc, cos_t, sin_t, lane).astype(BF16)

    ckv = y[:, o_ckv:o_kr]
    ckv = (ckv * lax.rsqrt(jnp.mean(ckv * ckv, axis=-1, keepdims=True) + EPS) * glkv_ref[...]).astype(BF16)
    kr = y[:, o_kr:]
    kc = jnp.dot(ckv, wuk_ref[...], preferred_element_type=F32) + jnp.concatenate([kr] * C_HEADS, axis=1)
    kc = kc * lax.rsqrt(_group_mean_sq(kc, ones128_ref, C_QK) + EPS) * gkc_ref[...]
    kc_ref[...] = _rope(kc, cos_t, sin_t, lane).astype(BF16)
    vt = lax.dot_general(wuvt_ref[...], ckv, (((1,), (1,)), ((), ())), preferred_element_type=F32)
    row = lax.broadcasted_iota(jnp.int32, (VT_ROWS - C_VDIM, vt.shape[1]), 0)
    ones_rows = jnp.where(row == 0, 1.0, 0.0).astype(BF16)
    for hd in range(C_HEADS):
        vt_ref[0, hd * VT_ROWS:hd * VT_ROWS + C_VDIM, :] = vt[hd * C_VDIM:(hd + 1) * C_VDIM].astype(BF16)
        vt_ref[0, hd * VT_ROWS + C_VDIM:(hd + 1) * VT_ROWS, :] = ones_rows


def _inproj(x2d, cos_t, sin_t, p):
    t = x2d.shape[0]
    tm = TM_IN
    row = lambda w: pl.BlockSpec((tm, w), lambda i: (i, 0))
    outs = [(A_W, BF16)] * 3 + [(B_QW, BF16), (B_KVW, BF16), (B_KVW, BF16)] + [(C_SLOTS, BF16)] * 2
    return pl.pallas_call(
        _inproj_kernel,
        grid=(t // tm,),
        in_specs=[row(D_MODEL), row(LANES), row(LANES),
                  _const_spec((1, D_MODEL)), _const_spec((D_MODEL, IN_COLS)),
                  _const_spec((1, 2 * A_W + B_QW + B_KVW)),
                  _const_spec((2 * LANES, 2 * LANES)), _const_spec((2 * LANES, 2 * LANES)),
                  _const_spec((1, C_Q_RANK)), _const_spec((C_Q_RANK, C_SLOTS)), _const_spec((1, C_SLOTS)),
                  _const_spec((1, C_KV_RANK)), _const_spec((C_KV_RANK, C_SLOTS)),
                  _const_spec((C_HEADS * C_VDIM, C_KV_RANK)), _const_spec((1, C_SLOTS))],
        out_specs=[row(w) for w, _ in outs] + [pl.BlockSpec((1, C_HEADS * VT_ROWS, tm), lambda i: (i, 0, 0))],
        out_shape=[jax.ShapeDtypeStruct((t, w), d) for w, d in outs]
        + [jax.ShapeDtypeStruct((t // tm, C_HEADS * VT_ROWS, tm), BF16)],
        compiler_params=_cparams(("parallel",)),
        name="inproj",
    )(x2d, cos_t, sin_t, p["gmix"], p["w_in"], p["gqk"], p["ones64"], p["ones128"],
      p["glq"], p["wuq"], p["gqc"], p["glkv"], p["wuk"], p["wuvt"], p["gkc"])


def _window(lo_ref, main_ref, hi_ref, a, radius, tq):
    start, end = a - radius, a + SUB + radius
    parts = []
    if start < 0:
        parts.append(lo_ref[radius + start:radius, :])
        start = 0
    parts.append(main_ref[start:min(end, tq), :])
    if end > tq:
        parts.append(hi_ref[0:end - tq, :])
    return parts[0] if len(parts) == 1 else jnp.concatenate(parts, axis=0)


def _banded_kernel(*refs, tq, radius, head_ids, kv_slab, has_sink, emit_ml, n_tiles):
    if has_sink:
        sink_ref, refs = refs[0], refs[1:]
    q_ref, klo_ref, k_ref, khi_ref, vlo_ref, v_ref, vhi_ref, bias_ref, o_ref = refs[:9]
    ml_ref = refs[9] if emit_ml else None
    tile = pl.program_id(2)
    width = SUB + 2 * radius
    lane = lax.broadcasted_iota(jnp.int32, (SUB, LANES), 1)
    col = lax.broadcasted_iota(jnp.int32, (SUB, width), 1)
    for a in range(0, tq, SUB):
        kw = _window(klo_ref, k_ref, khi_ref, a, radius, tq)
        vw = _window(vlo_ref, v_ref, vhi_ref, a, radius, tq)
        ml = jnp.zeros((SUB, LANES), F32)
        for pair, heads in enumerate(head_ids):
            qs = q_ref[a:a + SUB, pair * LANES:(pair + 1) * LANES]
            ks = kw[:, kv_slab[pair] * LANES:(kv_slab[pair] + 1) * LANES]
            vs = vw[:, kv_slab[pair] * LANES:(kv_slab[pair] + 1) * LANES]
            outs = []
            for e, hd in enumerate(heads):
                qm = jnp.where((lane < HEAD_DIM) == (e == 0), qs, jnp.zeros_like(qs))
                s = lax.dot_general(qm, ks, (((1,), (1,)), ((), ())), preferred_element_type=F32)
                s = s + bias_ref[hd]
                if a == 0:
                    s = jnp.where(col < jnp.where(tile == 0, radius, 0), NEG, s)
                if a == tq - SUB:
                    s = jnp.where(col >= jnp.where(tile == n_tiles - 1, SUB + radius, width), NEG, s)
                m = jnp.max(s, axis=-1, keepdims=True)
                if has_sink:
                    m = jnp.maximum(m, sink_ref[hd])
                pr = jnp.exp(s - m)
                l = jnp.sum(pr, axis=-1, keepdims=True)
                if has_sink:
                    l = l + jnp.exp(sink_ref[hd] - m)
                o = jnp.dot(pr.astype(BF16), vs, preferred_element_type=F32)
                outs.append(o / l)
                if emit_ml:
                    ml = jnp.where(lane == hd, m, ml)
                    ml = jnp.where(lane == ML_L_OFFSET + hd, l, ml)
            o_ref[a:a + SUB, pair * LANES:(pair + 1) * LANES] = jnp.where(
                lane < HEAD_DIM, outs[0], outs[1]).astype(o_ref.dtype)
        if emit_ml:
            ml_ref[a:a + SUB, :] = ml


def _banded(q, k, v, bias, *, batch, seq, dilation, radius, head_ids, kv_slab, sink, emit_ml, out_dtype):
    n = seq // dilation
    tq = min(TQ_BAND, n)
    n_tiles = n // tq
    qw, kw = q.shape[1], k.shape[1]
    view = lambda t: t.reshape(batch, n, dilation * t.shape[1])
    per_tile = tq // radius
    main = lambda w: pl.BlockSpec((None, tq, w), lambda b, c, i: (b, i, c))
    lo = pl.BlockSpec((None, radius, kw), lambda b, c, i: (b, jnp.maximum(i * per_tile - 1, 0), c))
    hi = pl.BlockSpec((None, radius, kw), lambda b, c, i: (b, jnp.minimum((i + 1) * per_tile, n // radius - 1), c))
    in_specs = [main(qw), lo, main(kw), hi, lo, main(kw), hi, _const_spec(bias.shape)]
    args = [view(q), view(k), view(k), view(k), view(v), view(v), view(v), bias]
    if sink is not None:
        in_specs = [pl.BlockSpec(memory_space=pltpu.SMEM)] + in_specs
        args = [sink] + args
    out_specs = [main(qw)]
    out_shape = [jax.ShapeDtypeStruct((batch, n, dilation * qw), out_dtype)]
    if emit_ml:
        out_specs.append(main(LANES))
        out_shape.append(jax.ShapeDtypeStruct((batch, n, dilation * LANES), F32))
    outs = pl.pallas_call(
        functools.partial(_banded_kernel, tq=tq, radius=radius, head_ids=head_ids, kv_slab=kv_slab,
                          has_sink=sink is not None, emit_ml=emit_ml, n_tiles=n_tiles),
        grid=(batch, dilation, n_tiles),
        in_specs=in_specs,
        out_specs=out_specs,
        out_shape=out_shape,
        compiler_params=_cparams(("parallel", "parallel", "parallel")),
        name="banded_r%d_d%d" % (radius, dilation),
    )(*args)
    return [o.reshape(batch * seq, -1) for o in outs]


def _latent_kernel(q_ref, k_ref, vt_ref, o_ref, s_ref, cmax_ref, m_ref, acc_ref, *, tk):
    tq = q_ref.shape[0]
    nk = k_ref.shape[0] // tk
    qt = q_ref[...].astype(F32).T.astype(BF16)
    m_ref[...] = jnp.full(m_ref.shape, -jnp.inf, F32)
    acc_ref[...] = jnp.zeros(acc_ref.shape, F32)

    def scores(slot, j):
        start = pl.multiple_of(j * tk, tk)
        s = jnp.dot(k_ref[pl.ds(start, tk), :], qt, preferred_element_type=F32)
        s_ref[slot] = s
        cmax_ref[slot] = jnp.max(s, axis=0, keepdims=True)

    def consume(slot, j):
        s = s_ref[slot]
        m_old = m_ref[...]
        m_new = jnp.maximum(m_old, cmax_ref[slot])
        pr = jnp.exp2(s - m_new).astype(BF16)
        acc_ref[...] = jnp.exp2(m_old - m_new) * acc_ref[...] + jnp.dot(
            vt_ref[j], pr, preferred_element_type=F32)
        m_ref[...] = m_new

    def run(j0, last):
        for u in range(LAT_UNROLL):
            if not (last and u == LAT_UNROLL - 1):
                scores((u + 1) % 2, j0 + u + 1)
            consume(u % 2, j0 + u)

    scores(0, 0)

    def body(jj, carry):
        run(LAT_UNROLL * jj, False)
        return carry

    lax.fori_loop(0, nk // LAT_UNROLL - 1, body, 0)
    run(nk - LAT_UNROLL, True)
    acc = acc_ref[...]
    o = jnp.concatenate([acc[:C_VDIM] / acc[C_VDIM:C_VDIM + 1], jnp.zeros((LANES - C_VDIM, tq), F32)], axis=0)
    o_ref[...] = o.T.astype(o_ref.dtype)


def _latent(qc, kc, vt, batch, seq):
    t = batch * seq
    tq = tk = vt.shape[2]
    nq = seq // tq
    return pl.pallas_call(
        functools.partial(_latent_kernel, tk=tk),
        grid=(batch, C_HEADS, nq),
        in_specs=[pl.BlockSpec((tq, LANES), lambda b, h, i: (b * nq + i, h)),
                  pl.BlockSpec((seq, LANES), lambda b, h, i: (b, h)),
                  pl.BlockSpec((seq // tk, VT_ROWS, tk), lambda b, h, i: (b, h, 0))],
        out_specs=pl.BlockSpec((tq, LANES), lambda b, h, i: (b * nq + i, h)),
        out_shape=jax.ShapeDtypeStruct((t, C_SLOTS), BF16),
        scratch_shapes=[pltpu.VMEM((2, tk, tq), F32), pltpu.VMEM((2, 1, tq), F32), pltpu.VMEM((1, tq), F32), pltpu.VMEM((VT_ROWS, tq), F32)],
        compiler_params=_cparams(("parallel", "parallel", "parallel")),
        name="latent",
    )(qc, kc, vt)


def _merge_mlp_kernel(x_ref, oa1_ref, oa2_ref, oa3_ref, ml1_ref, ml2_ref, ml3_ref, ob_ref, oc_ref,
                      expand_ref, gout_ref, wout_ref, gmlp_ref, wup_ref, wdown_ref, out_ref):
    mls = [ml1_ref[...], ml2_ref[...], ml3_ref[...]]
    lane = lax.broadcasted_iota(jnp.int32, mls[0].shape, 1)
    m_all = jnp.maximum(jnp.maximum(mls[0], mls[1]), mls[2])
    ws = [pltpu.roll(ml, LANES - ML_L_OFFSET, 1) * jnp.exp(ml - m_all) for ml in mls]
    wsum = ws[0] + ws[1] + ws[2]
    packed = jnp.zeros_like(wsum)
    for c, w in enumerate(ws):
        wn = jnp.where(lane < A_HEADS, w / wsum, 0.0)
        packed = packed + (wn if c == 0 else pltpu.roll(wn, ML_L_OFFSET * c, 1))
    hi = packed.astype(BF16)
    lo = (packed - hi.astype(F32)).astype(BF16)
    spread = (jnp.dot(hi, expand_ref[...], preferred_element_type=F32)
              + jnp.dot(lo, expand_ref[...], preferred_element_type=F32))
    oa = (spread[:, :A_W] * oa1_ref[...] + spread[:, A_W:2 * A_W] * oa2_ref[...]
          + spread[:, 2 * A_W:] * oa3_ref[...])

    def group_norm(v, width):
        return v * lax.rsqrt(jnp.sum(v * v, axis=-1, keepdims=True) * (1.0 / width) + EPS)

    ob = ob_ref[...].astype(F32)
    oc = oc_ref[...].astype(F32)
    mixed = jnp.concatenate([group_norm(oa, A_W), group_norm(ob, B_QW), group_norm(oc, C_HEADS * C_VDIM)], axis=1)
    mixed = (mixed * gout_ref[...]).astype(BF16)
    x = x_ref[...] + jnp.dot(mixed, wout_ref[...], preferred_element_type=F32)

    h = (x * lax.rsqrt(jnp.mean(x * x, axis=-1, keepdims=True) + EPS) * gmlp_ref[...]).astype(BF16)
    acc = x
    for s in range(0, D_FF, FF_CHUNK):
        u = jnp.dot(h, wup_ref[:, s:s + FF_CHUNK], preferred_element_type=F32)
        u = jnp.square(jnp.maximum(u, 0.0)).astype(BF16)
        acc = acc + jnp.dot(u, wdown_ref[s:s + FF_CHUNK, :], preferred_element_type=F32)
    out_ref[...] = acc


def _merge_mlp(x2d, oas, mls, ob, oc, p):
    t = x2d.shape[0]
    tm = TM_MLP
    row = lambda w: pl.BlockSpec((tm, w), lambda i: (i, 0))
    single = lambda shape: pl.BlockSpec(shape, lambda i: (0,) * len(shape), pipeline_mode=pl.Buffered(1))
    return pl.pallas_call(
        _merge_mlp_kernel,
        grid=(t // tm,),
        in_specs=[row(D_MODEL)] + [row(A_W)] * 3 + [row(LANES)] * 3 + [row(B_QW), row(C_SLOTS),
                  single((LANES, 3 * A_W)), single((1, MIX_COLS)), single((MIX_COLS, D_MODEL)),
                  single((1, D_MODEL)), single((D_MODEL, D_FF)), single((D_FF, D_MODEL))],
        out_specs=row(D_MODEL),
        out_shape=jax.ShapeDtypeStruct((t, D_MODEL), F32),
        compiler_params=_cparams(("parallel",)),
        name="merge_mlp",
    )(x2d, *oas, *mls, ob, oc, p["expand"], p["gout"], p["w_out"], p["gmlp"], p["w_up"], p["w_down"])


def _block_diag_ones(group):
    idx = np.arange(2 * LANES) // group
    return jnp.asarray((idx[:, None] == idx[None, :]).astype(np.float32), dtype=BF16)


def _expand_matrix():
    e = np.zeros((LANES, len(A_CONFIGS) * A_W), np.float32)
    for c in range(len(A_CONFIGS)):
        for h in range(A_HEADS):
            e[ML_L_OFFSET * c + h, c * A_W + h * HEAD_DIM:c * A_W + (h + 1) * HEAD_DIM] = 1.0
    return jnp.asarray(e, dtype=BF16)


def _pad_heads(w, heads, used, lo=0):
    lead = w.shape[:-1]
    w = w.reshape(lead + (heads, used))
    pad = [(0, 0)] * len(lead) + [(0, 0), (lo, LANES - lo - used)]
    return jnp.pad(w, pad).reshape(lead + (heads * LANES,))


def _layer_params(i, norm_mix, w_in, qk_gain_a, qk_gain_b, q_lat_gain, kv_lat_gain, w_uq, w_ukv, qk_gain_c,
                  out_norm, w_out, norm_mlp, w_up, w_down):
    w = w_in[i]
    o = np.cumsum((A_W, A_W, A_W, B_QW, B_KVW, B_KVW, C_Q_RANK, C_KV_RANK)).tolist()
    qb = w[:, o[2]:o[3]].reshape(D_MODEL, B_HEADS, HEAD_DIM)[:, B_HEAD_ORDER, :].reshape(D_MODEL, B_QW)
    kr = _pad_heads(w[:, o[7]:], 1, C_ROPE, lo=C_NOPE)
    w_in_p = jnp.concatenate([w[:, :o[2]], qb, w[:, o[3]:o[7]], kr], axis=1).astype(BF16)

    scale = HEAD_DIM ** -0.5
    gqk = jnp.concatenate([jnp.tile(qk_gain_a[i, 0], A_HEADS) * scale, jnp.tile(qk_gain_a[i, 1], A_HEADS),
                           jnp.tile(qk_gain_b[i, 0], B_HEADS) * scale, jnp.tile(qk_gain_b[i, 1], B_KV_HEADS)])
    ukv = w_ukv[i].reshape(C_KV_RANK, C_HEADS, C_NOPE + C_VDIM)
    wuk = _pad_heads(ukv[:, :, :C_NOPE].reshape(C_KV_RANK, -1), C_HEADS, C_NOPE)
    wuvt = ukv[:, :, C_NOPE:].reshape(C_KV_RANK, C_HEADS * C_VDIM).T
    g = out_norm[i]
    gb = g[A_W:A_W + B_QW].reshape(B_HEADS, HEAD_DIM)[B_HEAD_ORDER, :].reshape(B_QW)
    gout = jnp.concatenate([g[:A_W], gb, _pad_heads(g[A_W + B_QW:], C_HEADS, C_VDIM)])
    wo = w_out[i]
    wob = wo[A_W:A_W + B_QW].reshape(B_HEADS, HEAD_DIM, D_MODEL)[B_HEAD_ORDER, :, :].reshape(B_QW, D_MODEL)
    woc = jnp.pad(wo[A_W + B_QW:].reshape(C_HEADS, C_VDIM, D_MODEL), ((0, 0), (0, LANES - C_VDIM), (0, 0)))
    w_out_p = jnp.concatenate([wo[:A_W], wob, woc.reshape(C_SLOTS, D_MODEL)], axis=0).astype(BF16)
    return {
        "gmix": norm_mix[i][None, :],
        "w_in": w_in_p,
        "gqk": gqk[None, :],
        "ones64": _block_diag_ones(HEAD_DIM),
        "ones128": _block_diag_ones(LANES),
        "glq": q_lat_gain[i][None, :],
        "wuq": _pad_heads(w_uq[i], C_HEADS, C_QK).astype(BF16),
        "gqc": jnp.tile(_pad_heads(qk_gain_c[i, 0] * (C_QK ** -0.5 * LOG2E), 1, C_QK), C_HEADS)[None, :],
        "glkv": kv_lat_gain[i][None, :],
        "wuk": wuk.astype(BF16),
        "wuvt": wuvt.astype(BF16),
        "gkc": jnp.tile(_pad_heads(qk_gain_c[i, 1], 1, C_QK), C_HEADS)[None, :],
        "expand": _expand_matrix(),
        "gout": gout[None, :],
        "w_out": w_out_p,
        "gmlp": norm_mlp[i][None, :],
        "w_up": w_up[i].astype(BF16),
        "w_down": w_down[i].astype(BF16),
    }


def kernel(x, positions, rel_bias_table, norm_mix, w_in, qk_gain_a, qk_gain_b, sink_b, q_lat_gain, kv_lat_gain,
           w_uq, w_ukv, qk_gain_c, out_norm, w_out, norm_mlp, w_up, w_down):
    batch, seq, _ = x.shape
    depth = w_in.shape[0]
    x2d = x.reshape(batch * seq, D_MODEL)
    cos_t, sin_t = _rope_tables(positions)
    a_pairs = tuple((2 * p, 2 * p + 1) for p in range(A_HEADS // 2))
    bias_a = [_bias_tiles(rel_bias_table, window // (2 * r), r, tuple(range(A_HEADS))) for window, r in A_CONFIGS]
    bias_b = _bias_tiles(rel_bias_table, B_RADIUS, 1, tuple(A_HEADS + h for h in range(B_HEADS)))
    for i in range(depth):
        p = _layer_params(i, norm_mix, w_in, qk_gain_a, qk_gain_b, q_lat_gain, kv_lat_gain, w_uq, w_ukv,
                          qk_gain_c, out_norm, w_out, norm_mlp, w_up, w_down)
        qa, ka, va, qb, kb, vb, qc, kc, vt = _inproj(x2d, cos_t, sin_t, p)
        oas, mls = [], []
        for (window, r), bias in zip(A_CONFIGS, bias_a):
            o, ml = _banded(qa, ka, va, bias, batch=batch, seq=seq, dilation=r, radius=window // (2 * r),
                            head_ids=a_pairs, kv_slab=(0, 1, 2), sink=None, emit_ml=True, out_dtype=F32)
            oas.append(o)
            mls.append(ml)
        (ob,) = _banded(qb, kb, vb, bias_b, batch=batch, seq=seq, dilation=1, radius=B_RADIUS,
                        head_ids=((0, 2), (1, 3)), kv_slab=(0, 0), sink=sink_b[i], emit_ml=False, out_dtype=BF16)
        oc = _latent(qc, kc, vt, batch, seq)
        x2d = _merge_mlp(x2d, oas, mls, ob, oc, p)
    return x2d.reshape(batch, seq, D_MODEL)
```

```python
import functools
import math

import numpy as np
import jax
import jax.numpy as jnp
from jax import lax
from jax.experimental import pallas as pl
from jax.experimental.pallas import tpu as pltpu

F32 = jnp.float32
BF16 = jnp.bfloat16

D_MODEL = 1024
HEAD_DIM = 64
A_HEADS = 6
A_CONFIGS = ((128, 1), (512, 4), (2048, 16))
B_HEADS = 4
B_KV_HEADS = 2
B_RADIUS = 128
C_HEADS = 6
C_NOPE = 64
C_ROPE = 32
C_VDIM = 64
C_QK = C_NOPE + C_ROPE
C_Q_RANK = 256
C_KV_RANK = 128
ROPE_THETA = 10000.0
N_BUCKETS = 32
MAX_DISTANCE = 1024
D_FF = 4 * D_MODEL
EPS = 1e-6
NEG = -1e30

A_W = A_HEADS * HEAD_DIM
B_QW = B_HEADS * HEAD_DIM
B_KVW = B_KV_HEADS * HEAD_DIM
LANES = 128
C_SLOTS = C_HEADS * LANES
IN_COLS = 3 * A_W + B_QW + 2 * B_KVW + C_Q_RANK + C_KV_RANK + LANES
MIX_COLS = A_W + B_QW + C_SLOTS
VT_ROWS = 80
LOG2E = math.log2(math.e)
ML_L_OFFSET = 8

B_HEAD_ORDER = (0, 2, 1, 3)

TM_IN = 512
TQ_BAND = 256
SUB = 128
TQ_LAT = 1024
LAT_SLOTS = 3
LAT_UNROLL = 6
TM_MLP = 512
FF_CHUNK = 1024
VMEM_LIMIT = 56 * 1024 * 1024


def _cparams(sem):
    return pltpu.CompilerParams(dimension_semantics=sem, vmem_limit_bytes=VMEM_LIMIT)


def _const_spec(shape):
    zeros = (0,) * len(shape)
    return pl.BlockSpec(shape, lambda *_: zeros)


def _rope_table_kernel(pos_ref, inv_ref, cos_ref, sin_ref):
    ang = pos_ref[...] * inv_ref[...]
    lane = lax.broadcasted_iota(jnp.int32, ang.shape, 1)
    c = jnp.cos(ang)
    s = jnp.sin(ang)
    first = (lane >= C_NOPE) & (lane < C_NOPE + C_ROPE // 2)
    second = (lane >= C_NOPE + C_ROPE // 2) & (lane < C_QK)
    cos_ref[...] = jnp.where(first | second, c, 1.0)
    sin_ref[...] = jnp.where(first, -s, jnp.where(second, s, 0.0))


def _rope_tables(positions):
    t = positions.size
    half = C_ROPE // 2
    inv = ROPE_THETA ** (-jnp.arange(half, dtype=F32) / half)
    inv_row = jnp.concatenate([jnp.zeros((C_NOPE,), F32), inv, inv, jnp.zeros((LANES - C_QK,), F32)])[None, :]
    pos = positions.astype(F32).reshape(t, 1)
    tm = 2048
    return pl.pallas_call(
        _rope_table_kernel,
        grid=(t // tm,),
        in_specs=[pl.BlockSpec((tm, 1), lambda i: (i, 0)), _const_spec((1, LANES))],
        out_specs=[pl.BlockSpec((tm, LANES), lambda i: (i, 0))] * 2,
        out_shape=[jax.ShapeDtypeStruct((t, LANES), F32)] * 2,
        compiler_params=_cparams(("parallel",)),
        name="rope_tables",
    )(pos, inv_row)


def _bucket_thresholds():
    half = N_BUCKETS // 2
    exact = half // 2
    n = np.arange(1, 2 * MAX_DISTANCE + 2, dtype=np.float64)
    far = exact + (np.log(n / exact) / math.log(MAX_DISTANCE / exact) * (half - exact)).astype(np.int64)
    far = np.minimum(far, half - 1)
    return tuple(int(n[np.argmax(far >= exact + k)]) for k in range(1, half - exact))


def _bias_kernel(table_ref, out_ref, *, radius, dilation, head_cols):
    hsel = pl.program_id(0)
    width = SUB + 2 * radius
    row = lax.broadcasted_iota(jnp.int32, (SUB, width), 0)
    col = lax.broadcasted_iota(jnp.int32, (SUB, width), 1)
    rel = col - radius - row
    n = jnp.abs(rel) * dilation
    half = N_BUCKETS // 2
    exact = half // 2
    far = jnp.full(n.shape, exact, jnp.int32)
    for thr in _bucket_thresholds():
        far = far + (n >= thr).astype(jnp.int32)
    bucket = jnp.where(rel > 0, half, 0) + jnp.where(n < exact, n, far)
    for idx, hc in enumerate(head_cols):
        @pl.when(hsel == idx)
        def _(hc=hc):
            val = jnp.zeros(n.shape, F32)
            for b in range(N_BUCKETS):
                val = jnp.where(bucket == b, table_ref[b, hc], val)
            out_ref[...] = jnp.where(jnp.abs(rel) <= radius, val, NEG)


def _bias_tiles(table, radius, dilation, head_cols):
    width = SUB + 2 * radius
    return pl.pallas_call(
        functools.partial(_bias_kernel, radius=radius, dilation=dilation, head_cols=head_cols),
        grid=(len(head_cols),),
        in_specs=[pl.BlockSpec(memory_space=pltpu.SMEM)],
        out_specs=pl.BlockSpec((None, SUB, width), lambda h: (h, 0, 0)),
        out_shape=jax.ShapeDtypeStruct((len(head_cols), SUB, width), F32),
        compiler_params=_cparams(("arbitrary",)),
        name="bias_tiles",
    )(table)


def _group_mean_sq(y, ones_ref, group):
    sq = (y * y).astype(BF16)
    width = y.shape[1]
    parts = []
    for s in range(0, width, 2 * LANES):
        w = min(2 * LANES, width - s)
        parts.append(jnp.dot(sq[:, s:s + w], ones_ref[:w, :w], preferred_element_type=F32))
    out = parts[0] if len(parts) == 1 else jnp.concatenate(parts, axis=1)
    return out * (1.0 / group)


def _slab_roll(y, shift):
    parts = [pltpu.roll(y[:, s:s + LANES], shift, 1) for s in range(0, y.shape[1], LANES)]
    return parts[0] if len(parts) == 1 else jnp.concatenate(parts, axis=1)


def _rope(y, cos_t, sin_t, lane):
    swapped = jnp.where(lane < C_NOPE + C_ROPE // 2, _slab_roll(y, LANES - C_ROPE // 2), _slab_roll(y, C_ROPE // 2))
    return y * cos_t + swapped * sin_t


def _inproj_kernel(x_ref, cos_ref, sin_ref, gmix_ref, w_ref, gqk_ref, ones64_ref, ones128_ref,
                   glq_ref, wuq_ref, gqc_ref, glkv_ref, wuk_ref, wuvt_ref, gkc_ref,
                   qa_ref, ka_ref, va_ref, qb_ref, kb_ref, vb_ref, qc_ref, kc_ref, vt_ref):
    x = x_ref[...]
    h = x * lax.rsqrt(jnp.mean(x * x, axis=-1, keepdims=True) + EPS) * gmix_ref[...]
    y = jnp.dot(h.astype(BF16), w_ref[...], preferred_element_type=F32)

    o_qb = 3 * A_W
    o_kb = o_qb + B_QW
    o_vb = o_kb + B_KVW
    o_cq = o_vb + B_KVW
    o_ckv = o_cq + C_Q_RANK
    o_kr = o_ckv + C_KV_RANK
    yn = jnp.concatenate([y[:, :2 * A_W], y[:, o_qb:o_vb]], axis=1)
    yn = yn * lax.rsqrt(_group_mean_sq(yn, ones64_ref, HEAD_DIM) + EPS) * gqk_ref[...]
    qa_ref[...] = yn[:, :A_W].astype(BF16)
    ka_ref[...] = yn[:, A_W:2 * A_W].astype(BF16)
    qb_ref[...] = yn[:, 2 * A_W:2 * A_W + B_QW].astype(BF16)
    kb_ref[...] = yn[:, 2 * A_W + B_QW:].astype(BF16)
    va_ref[...] = y[:, 2 * A_W:3 * A_W].astype(BF16)
    vb_ref[...] = y[:, o_vb:o_cq].astype(BF16)

    cos_t = jnp.concatenate([cos_ref[...]] * C_HEADS, axis=1)
    sin_t = jnp.concatenate([sin_ref[...]] * C_HEADS, axis=1)
    lane = lax.broadcasted_iota(jnp.int32, cos_t.shape, 1) % LANES
    cq = y[:, o_cq:o_ckv]
    cq = cq * lax.rsqrt(jnp.mean(cq * cq, axis=-1, keepdims=True) + EPS) * glq_ref[...]
    qc = jnp.dot(cq.astype(BF16), wuq_ref[...], preferred_element_type=F32)
    qc = qc * lax.rsqrt(_group_mean_sq(qc, ones128_ref, C_QK) + EPS) * gqc_ref[...]
    qc_ref[...] = _rope(qc, cos_t, sin_t, lane).astype(BF16)

    ckv = y[:, o_ckv:o_kr]
    ckv = (ckv * lax.rsqrt(jnp.mean(ckv * ckv, axis=-1, keepdims=True) + EPS) * glkv_ref[...]).astype(BF16)
    kr = y[:, o_kr:]
    kc = jnp.dot(ckv, wuk_ref[...], preferred_element_type=F32) + jnp.concatenate([kr] * C_HEADS, axis=1)
    kc = kc * lax.rsqrt(_group_mean_sq(kc, ones128_ref, C_QK) + EPS) * gkc_ref[...]
    kc_ref[...] = _rope(kc, cos_t, sin_t, lane).astype(BF16)
    vt = lax.dot_general(wuvt_ref[...], ckv, (((1,), (1,)), ((), ())), preferred_element_type=F32)
    row = lax.broadcasted_iota(jnp.int32, (VT_ROWS - C_VDIM, vt.shape[1]), 0)
    ones_rows = jnp.where(row == 0, 1.0, 0.0).astype(BF16)
    for hd in range(C_HEADS):
        vt_ref[0, hd * VT_ROWS:hd * VT_ROWS + C_VDIM, :] = vt[hd * C_VDIM:(hd + 1) * C_VDIM].astype(BF16)
        vt_ref[0, hd * VT_ROWS + C_VDIM:(hd + 1) * VT_ROWS, :] = ones_rows


def _inproj(x2d, cos_t, sin_t, p):
    t = x2d.shape[0]
    tm = TM_IN
    row = lambda w: pl.BlockSpec((tm, w), lambda i: (i, 0))
    outs = [(A_W, BF16)] * 3 + [(B_QW, BF16), (B_KVW, BF16), (B_KVW, BF16)] + [(C_SLOTS, BF16)] * 2
    return pl.pallas_call(
        _inproj_kernel,
        grid=(t // tm,),
        in_specs=[row(D_MODEL), row(LANES), row(LANES),
                  _const_spec((1, D_MODEL)), _const_spec((D_MODEL, IN_COLS)),
                  _const_spec((1, 2 * A_W + B_QW + B_KVW)),
                  _const_spec((2 * LANES, 2 * LANES)), _const_spec((2 * LANES, 2 * LANES)),
                  _const_spec((1, C_Q_RANK)), _const_spec((C_Q_RANK, C_SLOTS)), _const_spec((1, C_SLOTS)),
                  _const_spec((1, C_KV_RANK)), _const_spec((C_KV_RANK, C_SLOTS)),
                  _const_spec((C_HEADS * C_VDIM, C_KV_RANK)), _const_spec((1, C_SLOTS))],
        out_specs=[row(w) for w, _ in outs] + [pl.BlockSpec((1, C_HEADS * VT_ROWS, tm), lambda i: (i, 0, 0))],
        out_shape=[jax.ShapeDtypeStruct((t, w), d) for w, d in outs]
        + [jax.ShapeDtypeStruct((t // tm, C_HEADS * VT_ROWS, tm), BF16)],
        compiler_params=_cparams(("parallel",)),
        name="inproj",
    )(x2d, cos_t, sin_t, p["gmix"], p["w_in"], p["gqk"], p["ones64"], p["ones128"],
      p["glq"], p["wuq"], p["gqc"], p["glkv"], p["wuk"], p["wuvt"], p["gkc"])


def _window(lo_ref, main_ref, hi_ref, a, radius, tq):
    start, end = a - radius, a + SUB + radius
    parts = []
    if start < 0:
        parts.append(lo_ref[radius + start:radius, :])
        start = 0
    parts.append(main_ref[start:min(end, tq), :])
    if end > tq:
        parts.append(hi_ref[0:end - tq, :])
    return parts[0] if len(parts) == 1 else jnp.concatenate(parts, axis=0)


def _banded_kernel(*refs, tq, radius, head_ids, kv_slab, has_sink, emit_ml, n_tiles):
    if has_sink:
        sink_ref, refs = refs[0], refs[1:]
    q_ref, klo_ref, k_ref, khi_ref, vlo_ref, v_ref, vhi_ref, bias_ref, o_ref = refs[:9]
    ml_ref = refs[9] if emit_ml else None
    tile = pl.program_id(2)
    width = SUB + 2 * radius
    lane = lax.broadcasted_iota(jnp.int32, (SUB, LANES), 1)
    col = lax.broadcasted_iota(jnp.int32, (SUB, width), 1)
    for a in range(0, tq, SUB):
        kw = _window(klo_ref, k_ref, khi_ref, a, radius, tq)
        vw = _window(vlo_ref, v_ref, vhi_ref, a, radius, tq)
        ml = jnp.zeros((SUB, LANES), F32)
        for pair, heads in enumerate(head_ids):
            qs = q_ref[a:a + SUB, pair * LANES:(pair + 1) * LANES]
            ks = kw[:, kv_slab[pair] * LANES:(kv_slab[pair] + 1) * LANES]
            vs = vw[:, kv_slab[pair] * LANES:(kv_slab[pair] + 1) * LANES]
            outs = []
            for e, hd in enumerate(heads):
                qm = jnp.where((lane < HEAD_DIM) == (e == 0), qs, jnp.zeros_like(qs))
                s = lax.dot_general(qm, ks, (((1,), (1,)), ((), ())), preferred_element_type=F32)
                s = s + bias_ref[hd]
                if a == 0:
                    s = jnp.where(col < jnp.where(tile == 0, radius, 0), NEG, s)
                if a == tq - SUB:
                    s = jnp.where(col >= jnp.where(tile == n_tiles - 1, SUB + radius, width), NEG, s)
                m = jnp.max(s, axis=-1, keepdims=True)
                if has_sink:
                    m = jnp.maximum(m, sink_ref[hd])
                pr = jnp.exp(s - m)
                l = jnp.sum(pr, axis=-1, keepdims=True)
                if has_sink:
                    l = l + jnp.exp(sink_ref[hd] - m)
                o = jnp.dot(pr.astype(BF16), vs, preferred_element_type=F32)
                outs.append(o / l)
                if emit_ml:
                    ml = jnp.where(lane == hd, m, ml)
                    ml = jnp.where(lane == ML_L_OFFSET + hd, l, ml)
            o_ref[a:a + SUB, pair * LANES:(pair + 1) * LANES] = jnp.where(
                lane < HEAD_DIM, outs[0], outs[1]).astype(o_ref.dtype)
        if emit_ml:
            ml_ref[a:a + SUB, :] = ml


def _banded(q, k, v, bias, *, batch, seq, dilation, radius, head_ids, kv_slab, sink, emit_ml, out_dtype):
    n = seq // dilation
    tq = min(TQ_BAND, n)
    n_tiles = n // tq
    qw, kw = q.shape[1], k.shape[1]
    view = lambda t: t.reshape(batch, n, dilation * t.shape[1])
    per_tile = tq // radius
    main = lambda w: pl.BlockSpec((None, tq, w), lambda b, c, i: (b, i, c))
    lo = pl.BlockSpec((None, radius, kw), lambda b, c, i: (b, jnp.maximum(i * per_tile - 1, 0), c))
    hi = pl.BlockSpec((None, radius, kw), lambda b, c, i: (b, jnp.minimum((i + 1) * per_tile, n // radius - 1), c))
    in_specs = [main(qw), lo, main(kw), hi, lo, main(kw), hi, _const_spec(bias.shape)]
    args = [view(q), view(k), view(k), view(k), view(v), view(v), view(v), bias]
    if sink is not None:
        in_specs = [pl.BlockSpec(memory_space=pltpu.SMEM)] + in_specs
        args = [sink] + args
    out_specs = [main(qw)]
    out_shape = [jax.ShapeDtypeStruct((batch, n, dilation * qw), out_dtype)]
    if emit_ml:
        out_specs.append(main(LANES))
        out_shape.append(jax.ShapeDtypeStruct((batch, n, dilation * LANES), F32))
    outs = pl.pallas_call(
        functools.partial(_banded_kernel, tq=tq, radius=radius, head_ids=head_ids, kv_slab=kv_slab,
                          has_sink=sink is not None, emit_ml=emit_ml, n_tiles=n_tiles),
        grid=(batch, dilation, n_tiles),
        in_specs=in_specs,
        out_specs=out_specs,
        out_shape=out_shape,
        compiler_params=_cparams(("parallel", "parallel", "parallel")),
        name="banded_r%d_d%d" % (radius, dilation),
    )(*args)
    return [o.reshape(batch * seq, -1) for o in outs]


def _latent_kernel(q_ref, k_ref, vt_ref, o_ref, s_ref, cmax_ref, m_ref, acc_ref, *, tk):
    tq = q_ref.shape[0]
    nk = k_ref.shape[0] // tk
    ahead = LAT_SLOTS - 1
    qt = q_ref[...].astype(F32).T.astype(BF16)
    m_ref[...] = jnp.full(m_ref.shape, -jnp.inf, F32)
    acc_ref[...] = jnp.zeros(acc_ref.shape, F32)

    def scores(slot, j):
        start = pl.multiple_of(j * tk, tk)
        s = jnp.dot(k_ref[pl.ds(start, tk), :], qt, preferred_element_type=F32)
        s_ref[slot] = s
        cmax_ref[slot] = jnp.max(s, axis=0, keepdims=True)

    def consume(slot, j):
        m_old = m_ref[...]
        m_new = jnp.maximum(m_old, cmax_ref[slot])
        pr = jnp.exp2(s_ref[slot] - m_new).astype(BF16)
        acc_ref[...] = jnp.exp2(m_old - m_new) * acc_ref[...] + jnp.dot(
            vt_ref[j], pr, preferred_element_type=F32)
        m_ref[...] = m_new

    def step(j, u, with_scores):
        if with_scores:
            scores((u + ahead) % LAT_SLOTS, j + ahead)
        consume(u % LAT_SLOTS, j)

    for j in range(ahead):
        scores(j, j)
    trips = (nk - ahead) // LAT_UNROLL

    def body(jj, carry):
        for u in range(LAT_UNROLL):
            step(LAT_UNROLL * jj + u, u, True)
        return carry

    lax.fori_loop(0, trips, body, 0)
    for j in range(trips * LAT_UNROLL, nk):
        step(j, j, j + ahead < nk)
    acc = acc_ref[...]
    o = jnp.concatenate([acc[:C_VDIM] / acc[C_VDIM:C_VDIM + 1], jnp.zeros((LANES - C_VDIM, tq), F32)], axis=0)
    o_ref[...] = o.T.astype(o_ref.dtype)


def _latent(qc, kc, vt, batch, seq):
    t = batch * seq
    tq, tk = TQ_LAT, vt.shape[2]
    nq = seq // tq
    return pl.pallas_call(
        functools.partial(_latent_kernel, tk=tk),
        grid=(batch, C_HEADS, nq),
        in_specs=[pl.BlockSpec((tq, LANES), lambda b, h, i: (b * nq + i, h)),
                  pl.BlockSpec((seq, LANES), lambda b, h, i: (b, h)),
                  pl.BlockSpec((seq // tk, VT_ROWS, tk), lambda b, h, i: (b, h, 0))],
        out_specs=pl.BlockSpec((tq, LANES), lambda b, h, i: (b * nq + i, h)),
        out_shape=jax.ShapeDtypeStruct((t, C_SLOTS), BF16),
        scratch_shapes=[pltpu.VMEM((LAT_SLOTS, tk, tq), F32), pltpu.VMEM((LAT_SLOTS, 1, tq), F32),
                        pltpu.VMEM((1, tq), F32), pltpu.VMEM((VT_ROWS, tq), F32)],
        compiler_params=_cparams(("parallel", "parallel", "parallel")),
        name="latent",
    )(qc, kc, vt)


def _merge_mlp_kernel(x_ref, oa1_ref, oa2_ref, oa3_ref, ml1_ref, ml2_ref, ml3_ref, ob_ref, oc_ref,
                      expand_ref, gout_ref, wout_ref, gmlp_ref, wup_ref, wdown_ref, out_ref):
    mls = [ml1_ref[...], ml2_ref[...], ml3_ref[...]]
    lane = lax.broadcasted_iota(jnp.int32, mls[0].shape, 1)
    m_all = jnp.maximum(jnp.maximum(mls[0], mls[1]), mls[2])
    ws = [pltpu.roll(ml, LANES - ML_L_OFFSET, 1) * jnp.exp(ml - m_all) for ml in mls]
    wsum = ws[0] + ws[1] + ws[2]
    packed = jnp.zeros_like(wsum)
    for c, w in enumerate(ws):
        wn = jnp.where(lane < A_HEADS, w / wsum, 0.0)
        packed = packed + (wn if c == 0 else pltpu.roll(wn, ML_L_OFFSET * c, 1))
    hi = packed.astype(BF16)
    lo = (packed - hi.astype(F32)).astype(BF16)
    spread = (jnp.dot(hi, expand_ref[...], preferred_element_type=F32)
              + jnp.dot(lo, expand_ref[...], preferred_element_type=F32))
    oa = (spread[:, :A_W] * oa1_ref[...] + spread[:, A_W:2 * A_W] * oa2_ref[...]
          + spread[:, 2 * A_W:] * oa3_ref[...])

    def group_norm(v, width):
        return v * lax.rsqrt(jnp.sum(v * v, axis=-1, keepdims=True) * (1.0 / width) + EPS)

    ob = ob_ref[...].astype(F32)
    oc = oc_ref[...].astype(F32)
    mixed = jnp.concatenate([group_norm(oa, A_W), group_norm(ob, B_QW), group_norm(oc, C_HEADS * C_VDIM)], axis=1)
    mixed = (mixed * gout_ref[...]).astype(BF16)
    x = x_ref[...] + jnp.dot(mixed, wout_ref[...], preferred_element_type=F32)

    h = (x * lax.rsqrt(jnp.mean(x * x, axis=-1, keepdims=True) + EPS) * gmlp_ref[...]).astype(BF16)
    acc = x
    for s in range(0, D_FF, FF_CHUNK):
        u = jnp.dot(h, wup_ref[:, s:s + FF_CHUNK], preferred_element_type=F32)
        u = jnp.square(jnp.maximum(u, 0.0)).astype(BF16)
        acc = acc + jnp.dot(u, wdown_ref[s:s + FF_CHUNK, :], preferred_element_type=F32)
    out_ref[...] = acc


def _merge_mlp(x2d, oas, mls, ob, oc, p):
    t = x2d.shape[0]
    tm = TM_MLP
    row = lambda w: pl.BlockSpec((tm, w), lambda i: (i, 0))
    single = lambda shape: pl.BlockSpec(shape, lambda i: (0,) * len(shape), pipeline_mode=pl.Buffered(1))
    return pl.pallas_call(
        _merge_mlp_kernel,
        grid=(t // tm,),
        in_specs=[row(D_MODEL)] + [row(A_W)] * 3 + [row(LANES)] * 3 + [row(B_QW), row(C_SLOTS),
                  single((LANES, 3 * A_W)), single((1, MIX_COLS)), single((MIX_COLS, D_MODEL)),
                  single((1, D_MODEL)), single((D_MODEL, D_FF)), single((D_FF, D_MODEL))],
        out_specs=row(D_MODEL),
        out_shape=jax.ShapeDtypeStruct((t, D_MODEL), F32),
        compiler_params=_cparams(("parallel",)),
        name="merge_mlp",
    )(x2d, *oas, *mls, ob, oc, p["expand"], p["gout"], p["w_out"], p["gmlp"], p["w_up"], p["w_down"])


def _block_diag_ones(group):
    idx = np.arange(2 * LANES) // group
    return jnp.asarray((idx[:, None] == idx[None, :]).astype(np.float32), dtype=BF16)


def _expand_matrix():
    e = np.zeros((LANES, len(A_CONFIGS) * A_W), np.float32)
    for c in range(len(A_CONFIGS)):
        for h in range(A_HEADS):
            e[ML_L_OFFSET * c + h, c * A_W + h * HEAD_DIM:c * A_W + (h + 1) * HEAD_DIM] = 1.0
    return jnp.asarray(e, dtype=BF16)


def _pad_heads(w, heads, used, lo=0):
    lead = w.shape[:-1]
    w = w.reshape(lead + (heads, used))
    pad = [(0, 0)] * len(lead) + [(0, 0), (lo, LANES - lo - used)]
    return jnp.pad(w, pad).reshape(lead + (heads * LANES,))


def _layer_params(i, norm_mix, w_in, qk_gain_a, qk_gain_b, q_lat_gain, kv_lat_gain, w_uq, w_ukv, qk_gain_c,
                  out_norm, w_out, norm_mlp, w_up, w_down):
    w = w_in[i]
    o = np.cumsum((A_W, A_W, A_W, B_QW, B_KVW, B_KVW, C_Q_RANK, C_KV_RANK)).tolist()
    qb = w[:, o[2]:o[3]].reshape(D_MODEL, B_HEADS, HEAD_DIM)[:, B_HEAD_ORDER, :].reshape(D_MODEL, B_QW)
    kr = _pad_heads(w[:, o[7]:], 1, C_ROPE, lo=C_NOPE)
    w_in_p = jnp.concatenate([w[:, :o[2]], qb, w[:, o[3]:o[7]], kr], axis=1).astype(BF16)

    scale = HEAD_DIM ** -0.5
    gqk = jnp.concatenate([jnp.tile(qk_gain_a[i, 0], A_HEADS) * scale, jnp.tile(qk_gain_a[i, 1], A_HEADS),
                           jnp.tile(qk_gain_b[i, 0], B_HEADS) * scale, jnp.tile(qk_gain_b[i, 1], B_KV_HEADS)])
    ukv = w_ukv[i].reshape(C_KV_RANK, C_HEADS, C_NOPE + C_VDIM)
    wuk = _pad_heads(ukv[:, :, :C_NOPE].reshape(C_KV_RANK, -1), C_HEADS, C_NOPE)
    wuvt = ukv[:, :, C_NOPE:].reshape(C_KV_RANK, C_HEADS * C_VDIM).T
    g = out_norm[i]
    gb = g[A_W:A_W + B_QW].reshape(B_HEADS, HEAD_DIM)[B_HEAD_ORDER, :].reshape(B_QW)
    gout = jnp.concatenate([g[:A_W], gb, _pad_heads(g[A_W + B_QW:], C_HEADS, C_VDIM)])
    wo = w_out[i]
    wob = wo[A_W:A_W + B_QW].reshape(B_HEADS, HEAD_DIM, D_MODEL)[B_HEAD_ORDER, :, :].reshape(B_QW, D_MODEL)
    woc = jnp.pad(wo[A_W + B_QW:].reshape(C_HEADS, C_VDIM, D_MODEL), ((0, 0), (0, LANES - C_VDIM), (0, 0)))
    w_out_p = jnp.concatenate([wo[:A_W], wob, woc.reshape(C_SLOTS, D_MODEL)], axis=0).astype(BF16)
    return {
        "gmix": norm_mix[i][None, :],
        "w_in": w_in_p,
        "gqk": gqk[None, :],
        "ones64": _block_diag_ones(HEAD_DIM),
        "ones128": _block_diag_ones(LANES),
        "glq": q_lat_gain[i][None, :],
        "wuq": _pad_heads(w_uq[i], C_HEADS, C_QK).astype(BF16),
        "gqc": jnp.tile(_pad_heads(qk_gain_c[i, 0] * (C_QK ** -0.5 * LOG2E), 1, C_QK), C_HEADS)[None, :],
        "glkv": kv_lat_gain[i][None, :],
        "wuk": wuk.astype(BF16),
        "wuvt": wuvt.astype(BF16),
        "gkc": jnp.tile(_pad_heads(qk_gain_c[i, 1], 1, C_QK), C_HEADS)[None, :],
        "expand": _expand_matrix(),
        "gout": gout[None, :],
        "w_out": w_out_p,
        "gmlp": norm_mlp[i][None, :],
        "w_up": w_up[i].astype(BF16),
        "w_down": w_down[i].astype(BF16),
    }


def kernel(x, positions, rel_bias_table, norm_mix, w_in, qk_gain_a, qk_gain_b, sink_b, q_lat_gain, kv_lat_gain,
           w_uq, w_ukv, qk_gain_c, out_norm, w_out, norm_mlp, w_up, w_down):
    batch, seq, _ = x.shape
    depth = w_in.shape[0]
    x2d = x.reshape(batch * seq, D_MODEL)
    cos_t, sin_t = _rope_tables(positions)
    a_pairs = tuple((2 * p, 2 * p + 1) for p in range(A_HEADS // 2))
    bias_a = [_bias_tiles(rel_bias_table, window // (2 * r), r, tuple(range(A_HEADS))) for window, r in A_CONFIGS]
    bias_b = _bias_tiles(rel_bias_table, B_RADIUS, 1, tuple(A_HEADS + h for h in range(B_HEADS)))
    for i in range(depth):
        p = _layer_params(i, norm_mix, w_in, qk_gain_a, qk_gain_b, q_lat_gain, kv_lat_gain, w_uq, w_ukv,
                          qk_gain_c, out_norm, w_out, norm_mlp, w_up, w_down)
        qa, ka, va, qb, kb, vb, qc, kc, vt = _inproj(x2d, cos_t, sin_t, p)
        oas, mls = [], []
        for (window, r), bias in zip(A_CONFIGS, bias_a):
            o, ml = _banded(qa, ka, va, bias, batch=batch, seq=seq, dilation=r, radius=window // (2 * r),
                            head_ids=a_pairs, kv_slab=(0, 1, 2), sink=None, emit_ml=True, out_dtype=F32)
            oas.append(o)
            mls.append(ml)
        (ob,) = _banded(qb, kb, vb, bias_b, batch=batch, seq=seq, dilation=1, radius=B_RADIUS,
                        head_ids=((0, 2), (1, 3)), kv_slab=(0, 0), sink=sink_b[i], emit_ml=False, out_dtype=BF16)
        oc = _latent(qc, kc, vt, batch, seq)
        x2d = _merge_mlp(x2d, oas, mls, ob, oc, p)
    return x2d.reshape(batch, seq, D_MODEL)
```

```python
import functools
import math

import numpy as np
import jax
import jax.numpy as jnp
from jax import lax
from jax.experimental import pallas as pl
from jax.experimental.pallas import tpu as pltpu

F32 = jnp.float32
BF16 = jnp.bfloat16

D_MODEL = 1024
HEAD_DIM = 64
A_HEADS = 6
A_CONFIGS = ((128, 1), (512, 4), (2048, 16))
B_HEADS = 4
B_KV_HEADS = 2
B_RADIUS = 128
C_HEADS = 6
C_NOPE = 64
C_ROPE = 32
C_VDIM = 64
C_QK = C_NOPE + C_ROPE
C_Q_RANK = 256
C_KV_RANK = 128
ROPE_THETA = 10000.0
N_BUCKETS = 32
MAX_DISTANCE = 1024
D_FF = 4 * D_MODEL
EPS = 1e-6
NEG = -1e30

A_W = A_HEADS * HEAD_DIM
B_QW = B_HEADS * HEAD_DIM
B_KVW = B_KV_HEADS * HEAD_DIM
LANES = 128
C_SLOTS = C_HEADS * LANES
IN_COLS = 3 * A_W + B_QW + 2 * B_KVW + C_Q_RANK + C_KV_RANK + LANES
MIX_COLS = A_W + B_QW + C_SLOTS
VT_ROWS = 80
LOG2E = math.log2(math.e)
ML_L_OFFSET = 8

B_HEAD_ORDER = (0, 2, 1, 3)

TM_IN = 512
TQ_BAND = 256
SUB = 128
TQ_LAT = 1024
LAT_SLOTS = 3
LAT_UNROLL = 6
TM_MLP = 512
FF_CHUNK = 1024
VMEM_LIMIT = 56 * 1024 * 1024


def _cparams(sem):
    return pltpu.CompilerParams(dimension_semantics=sem, vmem_limit_bytes=VMEM_LIMIT)


def _const_spec(shape):
    zeros = (0,) * len(shape)
    return pl.BlockSpec(shape, lambda *_: zeros)


def _rope_table_kernel(pos_ref, inv_ref, cos_ref, sin_ref):
    ang = pos_ref[...] * inv_ref[...]
    lane = lax.broadcasted_iota(jnp.int32, ang.shape, 1)
    c = jnp.cos(ang)
    s = jnp.sin(ang)
    first = (lane >= C_NOPE) & (lane < C_NOPE + C_ROPE // 2)
    second = (lane >= C_NOPE + C_ROPE // 2) & (lane < C_QK)
    cos_ref[...] = jnp.where(first | second, c, 1.0)
    sin_ref[...] = jnp.where(first, -s, jnp.where(second, s, 0.0))


def _rope_tables(positions):
    t = positions.size
    half = C_ROPE // 2
    inv = ROPE_THETA ** (-jnp.arange(half, dtype=F32) / half)
    inv_row = jnp.concatenate([jnp.zeros((C_NOPE,), F32), inv, inv, jnp.zeros((LANES - C_QK,), F32)])[None, :]
    pos = positions.astype(F32).reshape(t, 1)
    tm = 2048
    return pl.pallas_call(
        _rope_table_kernel,
        grid=(t // tm,),
        in_specs=[pl.BlockSpec((tm, 1), lambda i: (i, 0)), _const_spec((1, LANES))],
        out_specs=[pl.BlockSpec((tm, LANES), lambda i: (i, 0))] * 2,
        out_shape=[jax.ShapeDtypeStruct((t, LANES), F32)] * 2,
        compiler_params=_cparams(("parallel",)),
        name="rope_tables",
    )(pos, inv_row)


def _bucket_thresholds():
    half = N_BUCKETS // 2
    exact = half // 2
    n = np.arange(1, 2 * MAX_DISTANCE + 2, dtype=np.float64)
    far = exact + (np.log(n / exact) / math.log(MAX_DISTANCE / exact) * (half - exact)).astype(np.int64)
    far = np.minimum(far, half - 1)
    return tuple(int(n[np.argmax(far >= exact + k)]) for k in range(1, half - exact))


def _bias_kernel(table_ref, out_ref, *, radius, dilation, head_cols):
    hsel = pl.program_id(0)
    width = SUB + 2 * radius
    row = lax.broadcasted_iota(jnp.int32, (SUB, width), 0)
    col = lax.broadcasted_iota(jnp.int32, (SUB, width), 1)
    rel = col - radius - row
    n = jnp.abs(rel) * dilation
    half = N_BUCKETS // 2
    exact = half // 2
    far = jnp.full(n.shape, exact, jnp.int32)
    for thr in _bucket_thresholds():
        far = far + (n >= thr).astype(jnp.int32)
    bucket = jnp.where(rel > 0, half, 0) + jnp.where(n < exact, n, far)
    for idx, hc in enumerate(head_cols):
        @pl.when(hsel == idx)
        def _(hc=hc):
            val = jnp.zeros(n.shape, F32)
            for b in range(N_BUCKETS):
                val = jnp.where(bucket == b, table_ref[b, hc], val)
            out_ref[...] = jnp.where(jnp.abs(rel) <= radius, val, NEG)


def _bias_tiles(table, radius, dilation, head_cols):
    width = SUB + 2 * radius
    return pl.pallas_call(
        functools.partial(_bias_kernel, radius=radius, dilation=dilation, head_cols=head_cols),
        grid=(len(head_cols),),
        in_specs=[pl.BlockSpec(memory_space=pltpu.SMEM)],
        out_specs=pl.BlockSpec((None, SUB, width), lambda h: (h, 0, 0)),
        out_shape=jax.ShapeDtypeStruct((len(head_cols), SUB, width), F32),
        compiler_params=_cparams(("arbitrary",)),
        name="bias_tiles",
    )(table)


def _group_mean_sq(y, ones_ref, group):
    sq = (y * y).astype(BF16)
    width = y.shape[1]
    parts = []
    for s in range(0, width, 2 * LANES):
        w = min(2 * LANES, width - s)
        parts.append(jnp.dot(sq[:, s:s + w], ones_ref[:w, :w], preferred_element_type=F32))
    out = parts[0] if len(parts) == 1 else jnp.concatenate(parts, axis=1)
    return out * (1.0 / group)


def _slab_roll(y, shift):
    parts = [pltpu.roll(y[:, s:s + LANES], shift, 1) for s in range(0, y.shape[1], LANES)]
    return parts[0] if len(parts) == 1 else jnp.concatenate(parts, axis=1)


def _rope(y, cos_t, sin_t, lane):
    swapped = jnp.where(lane < C_NOPE + C_ROPE // 2, _slab_roll(y, LANES - C_ROPE // 2), _slab_roll(y, C_ROPE // 2))
    return y * cos_t + swapped * sin_t


def _inproj_kernel(x_ref, cos_ref, sin_ref, gmix_ref, w_ref, gqk_ref, ones64_ref, ones128_ref,
                   glq_ref, wuq_ref, gqc_ref, glkv_ref, wuk_ref, wuvt_ref, gkc_ref,
                   qkv1_ref, qkv4_ref, qkv16_ref, qb_ref, kb_ref, vb_ref, qc_ref, kc_ref, vt_ref, stage_ref):
    x = x_ref[...]
    h = x * lax.rsqrt(jnp.mean(x * x, axis=-1, keepdims=True) + EPS) * gmix_ref[...]
    y = jnp.dot(h.astype(BF16), w_ref[...], preferred_element_type=F32)

    o_qb = 3 * A_W
    o_kb = o_qb + B_QW
    o_vb = o_kb + B_KVW
    o_cq = o_vb + B_KVW
    o_ckv = o_cq + C_Q_RANK
    o_kr = o_ckv + C_KV_RANK
    yn = jnp.concatenate([y[:, :2 * A_W], y[:, o_qb:o_vb]], axis=1)
    yn = yn * lax.rsqrt(_group_mean_sq(yn, ones64_ref, HEAD_DIM) + EPS) * gqk_ref[...]
    qb_ref[...] = yn[:, 2 * A_W:2 * A_W + B_QW].astype(BF16)
    kb_ref[...] = yn[:, 2 * A_W + B_QW:].astype(BF16)
    n_slabs, tm = stage_ref.shape[0], stage_ref.shape[1]
    for s in range(n_slabs):
        src = yn if s * LANES < 2 * A_W else y
        stage_ref[s] = src[:, s * LANES:(s + 1) * LANES]
    for out_ref, (_, r) in zip((qkv1_ref, qkv4_ref, qkv16_ref), A_CONFIGS):
        for c in range(r):
            for s in range(n_slabs):
                col = c * 3 * A_W + s * LANES
                out_ref[:, col:col + LANES] = stage_ref[s, pl.ds(c, tm // r, stride=r), :].astype(BF16)
    vb_ref[...] = y[:, o_vb:o_cq].astype(BF16)

    cos_t = jnp.concatenate([cos_ref[...]] * C_HEADS, axis=1)
    sin_t = jnp.concatenate([sin_ref[...]] * C_HEADS, axis=1)
    lane = lax.broadcasted_iota(jnp.int32, cos_t.shape, 1) % LANES
    cq = y[:, o_cq:o_ckv]
    cq = cq * lax.rsqrt(jnp.mean(cq * cq, axis=-1, keepdims=True) + EPS) * glq_ref[...]
    qc = jnp.dot(cq.astype(BF16), wuq_ref[...], preferred_element_type=F32)
    qc = qc * lax.rsqrt(_group_mean_sq(qc, ones128_ref, C_QK) + EPS) * gqc_ref[...]
    qc_ref[...] = _rope(qc, cos_t, sin_t, lane).astype(BF16)

    ckv = y[:, o_ckv:o_kr]
    ckv = (ckv * lax.rsqrt(jnp.mean(ckv * ckv, axis=-1, keepdims=True) + EPS) * glkv_ref[...]).astype(BF16)
    kr = y[:, o_kr:]
    kc = jnp.dot(ckv, wuk_ref[...], preferred_element_type=F32) + jnp.concatenate([kr] * C_HEADS, axis=1)
    kc = kc * lax.rsqrt(_group_mean_sq(kc, ones128_ref, C_QK) + EPS) * gkc_ref[...]
    kc_ref[...] = _rope(kc, cos_t, sin_t, lane).astype(BF16)
    vt = lax.dot_general(wuvt_ref[...], ckv, (((1,), (1,)), ((), ())), preferred_element_type=F32)
    row = lax.broadcasted_iota(jnp.int32, (VT_ROWS - C_VDIM, vt.shape[1]), 0)
    ones_rows = jnp.where(row == 0, 1.0, 0.0).astype(BF16)
    for hd in range(C_HEADS):
        vt_ref[0, hd * VT_ROWS:hd * VT_ROWS + C_VDIM, :] = vt[hd * C_VDIM:(hd + 1) * C_VDIM].astype(BF16)
        vt_ref[0, hd * VT_ROWS + C_VDIM:(hd + 1) * VT_ROWS, :] = ones_rows


def _inproj(x2d, cos_t, sin_t, p):
    t = x2d.shape[0]
    tm = TM_IN
    row = lambda w: pl.BlockSpec((tm, w), lambda i: (i, 0))
    outs = [(B_QW, BF16), (B_KVW, BF16), (B_KVW, BF16)] + [(C_SLOTS, BF16)] * 2
    a_specs = [pl.BlockSpec((tm // r, r * 3 * A_W), lambda i: (i, 0)) for _, r in A_CONFIGS]
    a_shapes = [jax.ShapeDtypeStruct((t // r, r * 3 * A_W), BF16) for _, r in A_CONFIGS]
    return pl.pallas_call(
        _inproj_kernel,
        grid=(t // tm,),
        in_specs=[row(D_MODEL), row(LANES), row(LANES),
                  _const_spec((1, D_MODEL)), _const_spec((D_MODEL, IN_COLS)),
                  _const_spec((1, 2 * A_W + B_QW + B_KVW)),
                  _const_spec((2 * LANES, 2 * LANES)), _const_spec((2 * LANES, 2 * LANES)),
                  _const_spec((1, C_Q_RANK)), _const_spec((C_Q_RANK, C_SLOTS)), _const_spec((1, C_SLOTS)),
                  _const_spec((1, C_KV_RANK)), _const_spec((C_KV_RANK, C_SLOTS)),
                  _const_spec((C_HEADS * C_VDIM, C_KV_RANK)), _const_spec((1, C_SLOTS))],
        out_specs=a_specs + [row(w) for w, _ in outs]
        + [pl.BlockSpec((1, C_HEADS * VT_ROWS, tm), lambda i: (i, 0, 0))],
        out_shape=a_shapes + [jax.ShapeDtypeStruct((t, w), d) for w, d in outs]
        + [jax.ShapeDtypeStruct((t // tm, C_HEADS * VT_ROWS, tm), BF16)],
        scratch_shapes=[pltpu.VMEM((3 * A_W // LANES, tm, LANES), F32)],
        compiler_params=_cparams(("parallel",)),
        name="inproj",
    )(x2d, cos_t, sin_t, p["gmix"], p["w_in"], p["gqk"], p["ones64"], p["ones128"],
      p["glq"], p["wuq"], p["gqc"], p["glkv"], p["wuk"], p["wuvt"], p["gkc"])


def _window(lo_ref, main_ref, hi_ref, a, radius, tq):
    start, end = a - radius, a + SUB + radius
    parts = []
    if start < 0:
        parts.append(lo_ref[radius + start:radius, :])
        start = 0
    parts.append(main_ref[start:min(end, tq), :])
    if end > tq:
        parts.append(hi_ref[0:end - tq, :])
    return parts[0] if len(parts) == 1 else jnp.concatenate(parts, axis=0)


def _banded_kernel(*refs, tq, radius, head_ids, kv_slab, has_sink, emit_ml, n_tiles):
    if has_sink:
        sink_ref, refs = refs[0], refs[1:]
    q_ref, klo_ref, k_ref, khi_ref, vlo_ref, v_ref, vhi_ref, bias_ref, o_ref = refs[:9]
    ml_ref = refs[9] if emit_ml else None
    tile = pl.program_id(2)
    width = SUB + 2 * radius
    lane = lax.broadcasted_iota(jnp.int32, (SUB, LANES), 1)
    col = lax.broadcasted_iota(jnp.int32, (SUB, width), 1)
    for a in range(0, tq, SUB):
        kw = _window(klo_ref, k_ref, khi_ref, a, radius, tq)
        vw = _window(vlo_ref, v_ref, vhi_ref, a, radius, tq)
        ml = jnp.zeros((SUB, LANES), F32)
        for pair, heads in enumerate(head_ids):
            qs = q_ref[a:a + SUB, pair * LANES:(pair + 1) * LANES]
            ks = kw[:, kv_slab[pair] * LANES:(kv_slab[pair] + 1) * LANES]
            vs = vw[:, kv_slab[pair] * LANES:(kv_slab[pair] + 1) * LANES]
            outs = []
            for e, hd in enumerate(heads):
                qm = jnp.where((lane < HEAD_DIM) == (e == 0), qs, jnp.zeros_like(qs))
                s = lax.dot_general(qm, ks, (((1,), (1,)), ((), ())), preferred_element_type=F32)
                s = s + bias_ref[hd]
                if a == 0:
                    s = jnp.where(col < jnp.where(tile == 0, radius, 0), NEG, s)
                if a == tq - SUB:
                    s = jnp.where(col >= jnp.where(tile == n_tiles - 1, SUB + radius, width), NEG, s)
                m = jnp.max(s, axis=-1, keepdims=True)
                if has_sink:
                    m = jnp.maximum(m, sink_ref[hd])
                pr = jnp.exp(s - m)
                l = jnp.sum(pr, axis=-1, keepdims=True)
                if has_sink:
                    l = l + jnp.exp(sink_ref[hd] - m)
                o = jnp.dot(pr.astype(BF16), vs, preferred_element_type=F32)
                outs.append(o / l)
                if emit_ml:
                    ml = jnp.where(lane == hd, m, ml)
                    ml = jnp.where(lane == ML_L_OFFSET + hd, l, ml)
            o_ref[a:a + SUB, pair * LANES:(pair + 1) * LANES] = jnp.where(
                lane < HEAD_DIM, outs[0], outs[1]).astype(o_ref.dtype)
        if emit_ml:
            ml_ref[a:a + SUB, :] = ml


def _banded(q, k, v, bias, *, batch, seq, dilation, radius, widths, head_ids, kv_slab, sink, emit_ml, out_dtype):
    n = seq // dilation
    tq = min(TQ_BAND, n)
    n_tiles = n // tq
    qw, kw = widths
    view = lambda t: t[0].reshape(batch, n, t[0].shape[1])
    per_tile = tq // radius

    def main(w, t=(None, 1, 0)):
        return pl.BlockSpec((None, tq, w), lambda b, c, i: (b, i, t[1] * c + t[2]))

    def lo(t):
        return pl.BlockSpec((None, radius, kw),
                            lambda b, c, i: (b, jnp.maximum(i * per_tile - 1, 0), t[1] * c + t[2]))

    def hi(t):
        return pl.BlockSpec((None, radius, kw),
                            lambda b, c, i: (b, jnp.minimum((i + 1) * per_tile, n // radius - 1), t[1] * c + t[2]))

    in_specs = [main(qw, q), lo(k), main(kw, k), hi(k), lo(v), main(kw, v), hi(v), _const_spec(bias.shape)]
    args = [view(q), view(k), view(k), view(k), view(v), view(v), view(v), bias]
    if sink is not None:
        in_specs = [pl.BlockSpec(memory_space=pltpu.SMEM)] + in_specs
        args = [sink] + args
    out_specs = [main(qw)]
    out_shape = [jax.ShapeDtypeStruct((batch, n, dilation * qw), out_dtype)]
    if emit_ml:
        out_specs.append(main(LANES))
        out_shape.append(jax.ShapeDtypeStruct((batch, n, dilation * LANES), F32))
    outs = pl.pallas_call(
        functools.partial(_banded_kernel, tq=tq, radius=radius, head_ids=head_ids, kv_slab=kv_slab,
                          has_sink=sink is not None, emit_ml=emit_ml, n_tiles=n_tiles),
        grid=(batch, dilation, n_tiles),
        in_specs=in_specs,
        out_specs=out_specs,
        out_shape=out_shape,
        compiler_params=_cparams(("parallel", "parallel", "parallel")),
        name="banded_r%d_d%d" % (radius, dilation),
    )(*args)
    return [o.reshape(batch * n, -1) for o in outs]


def _latent_kernel(q_ref, k_ref, vt_ref, o_ref, s_ref, cmax_ref, m_ref, acc_ref, *, tk):
    tq = q_ref.shape[0]
    nk = k_ref.shape[0] // tk
    ahead = LAT_SLOTS - 1
    qt = q_ref[...].astype(F32).T.astype(BF16)
    m_ref[...] = jnp.full(m_ref.shape, -jnp.inf, F32)
    acc_ref[...] = jnp.zeros(acc_ref.shape, F32)

    def scores(slot, j):
        start = pl.multiple_of(j * tk, tk)
        s = jnp.dot(k_ref[pl.ds(start, tk), :], qt, preferred_element_type=F32)
        s_ref[slot] = s
        cmax_ref[slot] = jnp.max(s, axis=0, keepdims=True)

    def consume(slot, j):
        m_old = m_ref[...]
        m_new = jnp.maximum(m_old, cmax_ref[slot])
        pr = jnp.exp2(s_ref[slot] - m_new).astype(BF16)
        acc_ref[...] = jnp.exp2(m_old - m_new) * acc_ref[...] + jnp.dot(
            vt_ref[j], pr, preferred_element_type=F32)
        m_ref[...] = m_new

    def step(j, u, with_scores):
        if with_scores:
            scores((u + ahead) % LAT_SLOTS, j + ahead)
        consume(u % LAT_SLOTS, j)

    for j in range(ahead):
        scores(j, j)
    trips = (nk - ahead) // LAT_UNROLL

    def body(jj, carry):
        for u in range(LAT_UNROLL):
            step(LAT_UNROLL * jj + u, u, True)
        return carry

    lax.fori_loop(0, trips, body, 0)
    for j in range(trips * LAT_UNROLL, nk):
        step(j, j, j + ahead < nk)
    acc = acc_ref[...]
    o = jnp.concatenate([acc[:C_VDIM] / acc[C_VDIM:C_VDIM + 1], jnp.zeros((LANES - C_VDIM, tq), F32)], axis=0)
    o_ref[...] = o.T.astype(o_ref.dtype)


def _latent(qc, kc, vt, batch, seq):
    t = batch * seq
    tq, tk = TQ_LAT, vt.shape[2]
    nq = seq // tq
    return pl.pallas_call(
        functools.partial(_latent_kernel, tk=tk),
        grid=(batch, C_HEADS, nq),
        in_specs=[pl.BlockSpec((tq, LANES), lambda b, h, i: (b * nq + i, h)),
                  pl.BlockSpec((seq, LANES), lambda b, h, i: (b, h)),
                  pl.BlockSpec((seq // tk, VT_ROWS, tk), lambda b, h, i: (b, h, 0))],
        out_specs=pl.BlockSpec((tq, LANES), lambda b, h, i: (b * nq + i, h)),
        out_shape=jax.ShapeDtypeStruct((t, C_SLOTS), BF16),
        scratch_shapes=[pltpu.VMEM((LAT_SLOTS, tk, tq), F32), pltpu.VMEM((LAT_SLOTS, 1, tq), F32),
                        pltpu.VMEM((1, tq), F32), pltpu.VMEM((VT_ROWS, tq), F32)],
        compiler_params=_cparams(("parallel", "parallel", "parallel")),
        name="latent",
    )(qc, kc, vt)


def _merge_mlp_kernel(x_ref, oa1_ref, oa2_ref, oa3_ref, ml1_ref, ml2_ref, ml3_ref, ob_ref, oc_ref,
                      expand_ref, gout_ref, wout_ref, gmlp_ref, wup_ref, wdown_ref, out_ref, oa_s, ml_s):
    tm = x_ref.shape[0]
    oas, mls = [oa1_ref[...]], [ml1_ref[...]]
    for idx, (o_ref, l_ref) in enumerate(((oa2_ref, ml2_ref), (oa3_ref, ml3_ref))):
        r = A_CONFIGS[idx + 1][1]
        for c in range(r):
            for s in range(A_W // LANES):
                col = c * A_W + s * LANES
                oa_s[idx, s, pl.ds(c, tm // r, stride=r), :] = o_ref[:, col:col + LANES]
            ml_s[idx, pl.ds(c, tm // r, stride=r), :] = l_ref[:, c * LANES:(c + 1) * LANES]
        oas.append(jnp.concatenate([oa_s[idx, s] for s in range(A_W // LANES)], axis=1))
        mls.append(ml_s[idx])
    lane = lax.broadcasted_iota(jnp.int32, mls[0].shape, 1)
    m_all = jnp.maximum(jnp.maximum(mls[0], mls[1]), mls[2])
    ws = [pltpu.roll(ml, LANES - ML_L_OFFSET, 1) * jnp.exp(ml - m_all) for ml in mls]
    wsum = ws[0] + ws[1] + ws[2]
    packed = jnp.zeros_like(wsum)
    for c, w in enumerate(ws):
        wn = jnp.where(lane < A_HEADS, w / wsum, 0.0)
        packed = packed + (wn if c == 0 else pltpu.roll(wn, ML_L_OFFSET * c, 1))
    hi = packed.astype(BF16)
    lo = (packed - hi.astype(F32)).astype(BF16)
    spread = (jnp.dot(hi, expand_ref[...], preferred_element_type=F32)
              + jnp.dot(lo, expand_ref[...], preferred_element_type=F32))
    oa = spread[:, :A_W] * oas[0] + spread[:, A_W:2 * A_W] * oas[1] + spread[:, 2 * A_W:] * oas[2]

    def group_norm(v, width):
        return v * lax.rsqrt(jnp.sum(v * v, axis=-1, keepdims=True) * (1.0 / width) + EPS)

    ob = ob_ref[...].astype(F32)
    oc = oc_ref[...].astype(F32)
    mixed = jnp.concatenate([group_norm(oa, A_W), group_norm(ob, B_QW), group_norm(oc, C_HEADS * C_VDIM)], axis=1)
    mixed = (mixed * gout_ref[...]).astype(BF16)
    x = x_ref[...] + jnp.dot(mixed, wout_ref[...], preferred_element_type=F32)

    h = (x * lax.rsqrt(jnp.mean(x * x, axis=-1, keepdims=True) + EPS) * gmlp_ref[...]).astype(BF16)
    acc = x
    for s in range(0, D_FF, FF_CHUNK):
        u = jnp.dot(h, wup_ref[:, s:s + FF_CHUNK], preferred_element_type=F32)
        u = jnp.square(jnp.maximum(u, 0.0)).astype(BF16)
        acc = acc + jnp.dot(u, wdown_ref[s:s + FF_CHUNK, :], preferred_element_type=F32)
    out_ref[...] = acc


def _merge_mlp(x2d, oas, mls, ob, oc, p):
    t = x2d.shape[0]
    tm = TM_MLP
    row = lambda w: pl.BlockSpec((tm, w), lambda i: (i, 0))
    strided = lambda w, r: pl.BlockSpec((tm // r, r * w), lambda i: (i, 0))
    single = lambda shape: pl.BlockSpec(shape, lambda i: (0,) * len(shape), pipeline_mode=pl.Buffered(1))
    return pl.pallas_call(
        _merge_mlp_kernel,
        grid=(t // tm,),
        in_specs=[row(D_MODEL)] + [strided(A_W, r) for _, r in A_CONFIGS]
        + [strided(LANES, r) for _, r in A_CONFIGS] + [row(B_QW), row(C_SLOTS),
                  single((LANES, 3 * A_W)), single((1, MIX_COLS)), single((MIX_COLS, D_MODEL)),
                  single((1, D_MODEL)), single((D_MODEL, D_FF)), single((D_FF, D_MODEL))],
        out_specs=row(D_MODEL),
        out_shape=jax.ShapeDtypeStruct((t, D_MODEL), F32),
        scratch_shapes=[pltpu.VMEM((2, A_W // LANES, tm, LANES), F32), pltpu.VMEM((2, tm, LANES), F32)],
        compiler_params=_cparams(("parallel",)),
        name="merge_mlp",
    )(x2d, *oas, *mls, ob, oc, p["expand"], p["gout"], p["w_out"], p["gmlp"], p["w_up"], p["w_down"])


def _block_diag_ones(group):
    idx = np.arange(2 * LANES) // group
    return jnp.asarray((idx[:, None] == idx[None, :]).astype(np.float32), dtype=BF16)


def _expand_matrix():
    e = np.zeros((LANES, len(A_CONFIGS) * A_W), np.float32)
    for c in range(len(A_CONFIGS)):
        for h in range(A_HEADS):
            e[ML_L_OFFSET * c + h, c * A_W + h * HEAD_DIM:c * A_W + (h + 1) * HEAD_DIM] = 1.0
    return jnp.asarray(e, dtype=BF16)


def _pad_heads(w, heads, used, lo=0):
    lead = w.shape[:-1]
    w = w.reshape(lead + (heads, used))
    pad = [(0, 0)] * len(lead) + [(0, 0), (lo, LANES - lo - used)]
    return jnp.pad(w, pad).reshape(lead + (heads * LANES,))


def _layer_params(i, norm_mix, w_in, qk_gain_a, qk_gain_b, q_lat_gain, kv_lat_gain, w_uq, w_ukv, qk_gain_c,
                  out_norm, w_out, norm_mlp, w_up, w_down):
    w = w_in[i]
    o = np.cumsum((A_W, A_W, A_W, B_QW, B_KVW, B_KVW, C_Q_RANK, C_KV_RANK)).tolist()
    qb = w[:, o[2]:o[3]].reshape(D_MODEL, B_HEADS, HEAD_DIM)[:, B_HEAD_ORDER, :].reshape(D_MODEL, B_QW)
    kr = _pad_heads(w[:, o[7]:], 1, C_ROPE, lo=C_NOPE)
    w_in_p = jnp.concatenate([w[:, :o[2]], qb, w[:, o[3]:o[7]], kr], axis=1).astype(BF16)

    scale = HEAD_DIM ** -0.5
    gqk = jnp.concatenate([jnp.tile(qk_gain_a[i, 0], A_HEADS) * scale, jnp.tile(qk_gain_a[i, 1], A_HEADS),
                           jnp.tile(qk_gain_b[i, 0], B_HEADS) * scale, jnp.tile(qk_gain_b[i, 1], B_KV_HEADS)])
    ukv = w_ukv[i].reshape(C_KV_RANK, C_HEADS, C_NOPE + C_VDIM)
    wuk = _pad_heads(ukv[:, :, :C_NOPE].reshape(C_KV_RANK, -1), C_HEADS, C_NOPE)
    wuvt = ukv[:, :, C_NOPE:].reshape(C_KV_RANK, C_HEADS * C_VDIM).T
    g = out_norm[i]
    gb = g[A_W:A_W + B_QW].reshape(B_HEADS, HEAD_DIM)[B_HEAD_ORDER, :].reshape(B_QW)
    gout = jnp.concatenate([g[:A_W], gb, _pad_heads(g[A_W + B_QW:], C_HEADS, C_VDIM)])
    wo = w_out[i]
    wob = wo[A_W:A_W + B_QW].reshape(B_HEADS, HEAD_DIM, D_MODEL)[B_HEAD_ORDER, :, :].reshape(B_QW, D_MODEL)
    woc = jnp.pad(wo[A_W + B_QW:].reshape(C_HEADS, C_VDIM, D_MODEL), ((0, 0), (0, LANES - C_VDIM), (0, 0)))
    w_out_p = jnp.concatenate([wo[:A_W], wob, woc.reshape(C_SLOTS, D_MODEL)], axis=0).astype(BF16)
    return {
        "gmix": norm_mix[i][None, :],
        "w_in": w_in_p,
        "gqk": gqk[None, :],
        "ones64": _block_diag_ones(HEAD_DIM),
        "ones128": _block_diag_ones(LANES),
        "glq": q_lat_gain[i][None, :],
        "wuq": _pad_heads(w_uq[i], C_HEADS, C_QK).astype(BF16),
        "gqc": jnp.tile(_pad_heads(qk_gain_c[i, 0] * (C_QK ** -0.5 * LOG2E), 1, C_QK), C_HEADS)[None, :],
        "glkv": kv_lat_gain[i][None, :],
        "wuk": wuk.astype(BF16),
        "wuvt": wuvt.astype(BF16),
        "gkc": jnp.tile(_pad_heads(qk_gain_c[i, 1], 1, C_QK), C_HEADS)[None, :],
        "expand": _expand_matrix(),
        "gout": gout[None, :],
        "w_out": w_out_p,
        "gmlp": norm_mlp[i][None, :],
        "w_up": w_up[i].astype(BF16),
        "w_down": w_down[i].astype(BF16),
    }


def kernel(x, positions, rel_bias_table, norm_mix, w_in, qk_gain_a, qk_gain_b, sink_b, q_lat_gain, kv_lat_gain,
           w_uq, w_ukv, qk_gain_c, out_norm, w_out, norm_mlp, w_up, w_down):
    batch, seq, _ = x.shape
    depth = w_in.shape[0]
    x2d = x.reshape(batch * seq, D_MODEL)
    cos_t, sin_t = _rope_tables(positions)
    a_pairs = tuple((2 * p, 2 * p + 1) for p in range(A_HEADS // 2))
    bias_a = [_bias_tiles(rel_bias_table, window // (2 * r), r, tuple(range(A_HEADS))) for window, r in A_CONFIGS]
    bias_b = _bias_tiles(rel_bias_table, B_RADIUS, 1, tuple(A_HEADS + h for h in range(B_HEADS)))
    for i in range(depth):
        p = _layer_params(i, norm_mix, w_in, qk_gain_a, qk_gain_b, q_lat_gain, kv_lat_gain, w_uq, w_ukv,
                          qk_gain_c, out_norm, w_out, norm_mlp, w_up, w_down)
        qkv1, qkv4, qkv16, qb, kb, vb, qc, kc, vt = _inproj(x2d, cos_t, sin_t, p)
        oas, mls = [], []
        for (window, r), bias, qkv in zip(A_CONFIGS, bias_a, (qkv1, qkv4, qkv16)):
            o, ml = _banded((qkv, 3, 0), (qkv, 3, 1), (qkv, 3, 2), bias, batch=batch, seq=seq, dilation=r,
                            radius=window // (2 * r), widths=(A_W, A_W), head_ids=a_pairs, kv_slab=(0, 1, 2),
                            sink=None, emit_ml=True, out_dtype=F32)
            oas.append(o)
            mls.append(ml)
        (ob,) = _banded((qb, 1, 0), (kb, 1, 0), (vb, 1, 0), bias_b, batch=batch, seq=seq, dilation=1,
                        radius=B_RADIUS, widths=(B_QW, B_KVW), head_ids=((0, 2), (1, 3)), kv_slab=(0, 0),
                        sink=sink_b[i], emit_ml=False, out_dtype=BF16)
        oc = _latent(qc, kc, vt, batch, seq)
        x2d = _merge_mlp(x2d, oas, mls, ob, oc, p)
    return x2d.reshape(batch, seq, D_MODEL)
```

```python
import functools
import math

import numpy as np
import jax
import jax.numpy as jnp
from jax import lax
from jax.experimental import pallas as pl
from jax.experimental.pallas import tpu as pltpu

F32 = jnp.float32
BF16 = jnp.bfloat16

D_MODEL = 1024
HEAD_DIM = 64
A_HEADS = 6
A_CONFIGS = ((128, 1), (512, 4), (2048, 16))
B_HEADS = 4
B_KV_HEADS = 2
B_RADIUS = 128
C_HEADS = 6
C_NOPE = 64
C_ROPE = 32
C_VDIM = 64
C_QK = C_NOPE + C_ROPE
C_Q_RANK = 256
C_KV_RANK = 128
ROPE_THETA = 10000.0
N_BUCKETS = 32
MAX_DISTANCE = 1024
D_FF = 4 * D_MODEL
EPS = 1e-6
NEG = -1e30

A_W = A_HEADS * HEAD_DIM
B_QW = B_HEADS * HEAD_DIM
B_KVW = B_KV_HEADS * HEAD_DIM
LANES = 128
C_SLOTS = C_HEADS * LANES
IN_COLS = 3 * A_W + B_QW + 2 * B_KVW + C_Q_RANK + C_KV_RANK + LANES
MIX_COLS = A_W + B_QW + C_SLOTS
VT_ROWS = 80
LOG2E = math.log2(math.e)
ML_L_OFFSET = 8

B_HEAD_ORDER = (0, 2, 1, 3)

TM_IN = 512
TQ_BAND = 512
SUB = 128
TQ_LAT = 1024
LAT_SLOTS = 3
LAT_UNROLL = 6
TM_MLP = 512
FF_CHUNK = 1024
VMEM_LIMIT = 56 * 1024 * 1024


def _cparams(sem):
    return pltpu.CompilerParams(dimension_semantics=sem, vmem_limit_bytes=VMEM_LIMIT)


def _const_spec(shape):
    zeros = (0,) * len(shape)
    return pl.BlockSpec(shape, lambda *_: zeros)


def _rope_table_kernel(pos_ref, inv_ref, cos_ref, sin_ref):
    ang = pos_ref[...] * inv_ref[...]
    lane = lax.broadcasted_iota(jnp.int32, ang.shape, 1)
    c = jnp.cos(ang)
    s = jnp.sin(ang)
    first = (lane >= C_NOPE) & (lane < C_NOPE + C_ROPE // 2)
    second = (lane >= C_NOPE + C_ROPE // 2) & (lane < C_QK)
    cos_ref[...] = jnp.where(first | second, c, 1.0)
    sin_ref[...] = jnp.where(first, -s, jnp.where(second, s, 0.0))


def _rope_tables(positions):
    t = positions.size
    half = C_ROPE // 2
    inv = ROPE_THETA ** (-jnp.arange(half, dtype=F32) / half)
    inv_row = jnp.concatenate([jnp.zeros((C_NOPE,), F32), inv, inv, jnp.zeros((LANES - C_QK,), F32)])[None, :]
    pos = positions.astype(F32).reshape(t, 1)
    tm = 2048
    return pl.pallas_call(
        _rope_table_kernel,
        grid=(t // tm,),
        in_specs=[pl.BlockSpec((tm, 1), lambda i: (i, 0)), _const_spec((1, LANES))],
        out_specs=[pl.BlockSpec((tm, LANES), lambda i: (i, 0))] * 2,
        out_shape=[jax.ShapeDtypeStruct((t, LANES), F32)] * 2,
        compiler_params=_cparams(("parallel",)),
        name="rope_tables",
    )(pos, inv_row)


def _bucket_thresholds():
    half = N_BUCKETS // 2
    exact = half // 2
    n = np.arange(1, 2 * MAX_DISTANCE + 2, dtype=np.float64)
    far = exact + (np.log(n / exact) / math.log(MAX_DISTANCE / exact) * (half - exact)).astype(np.int64)
    far = np.minimum(far, half - 1)
    return tuple(int(n[np.argmax(far >= exact + k)]) for k in range(1, half - exact))


def _bias_kernel(table_ref, out_ref, *, radius, dilation, head_cols):
    hsel = pl.program_id(0)
    width = SUB + 2 * radius
    row = lax.broadcasted_iota(jnp.int32, (SUB, width), 0)
    col = lax.broadcasted_iota(jnp.int32, (SUB, width), 1)
    rel = col - radius - row
    n = jnp.abs(rel) * dilation
    half = N_BUCKETS // 2
    exact = half // 2
    far = jnp.full(n.shape, exact, jnp.int32)
    for thr in _bucket_thresholds():
        far = far + (n >= thr).astype(jnp.int32)
    bucket = jnp.where(rel > 0, half, 0) + jnp.where(n < exact, n, far)
    for idx, hc in enumerate(head_cols):
        @pl.when(hsel == idx)
        def _(hc=hc):
            val = jnp.zeros(n.shape, F32)
            for b in range(N_BUCKETS):
                val = jnp.where(bucket == b, table_ref[b, hc], val)
            out_ref[...] = jnp.where(jnp.abs(rel) <= radius, val, NEG)


def _bias_tiles(table, radius, dilation, head_cols):
    width = SUB + 2 * radius
    return pl.pallas_call(
        functools.partial(_bias_kernel, radius=radius, dilation=dilation, head_cols=head_cols),
        grid=(len(head_cols),),
        in_specs=[pl.BlockSpec(memory_space=pltpu.SMEM)],
        out_specs=pl.BlockSpec((None, SUB, width), lambda h: (h, 0, 0)),
        out_shape=jax.ShapeDtypeStruct((len(head_cols), SUB, width), F32),
        compiler_params=_cparams(("arbitrary",)),
        name="bias_tiles",
    )(table)


def _group_mean_sq(y, ones_ref, group):
    sq = (y * y).astype(BF16)
    width = y.shape[1]
    parts = []
    for s in range(0, width, 2 * LANES):
        w = min(2 * LANES, width - s)
        parts.append(jnp.dot(sq[:, s:s + w], ones_ref[:w, :w], preferred_element_type=F32))
    out = parts[0] if len(parts) == 1 else jnp.concatenate(parts, axis=1)
    return out * (1.0 / group)


def _slab_roll(y, shift):
    parts = [pltpu.roll(y[:, s:s + LANES], shift, 1) for s in range(0, y.shape[1], LANES)]
    return parts[0] if len(parts) == 1 else jnp.concatenate(parts, axis=1)


def _rope(y, cos_t, sin_t, lane):
    swapped = jnp.where(lane < C_NOPE + C_ROPE // 2, _slab_roll(y, LANES - C_ROPE // 2), _slab_roll(y, C_ROPE // 2))
    return y * cos_t + swapped * sin_t


def _inproj_kernel(x_ref, cos_ref, sin_ref, gmix_ref, w_ref, gqk_ref, ones64_ref, ones128_ref,
                   glq_ref, wuq_ref, gqc_ref, glkv_ref, wuk_ref, wuvt_ref, gkc_ref,
                   qkv1_ref, qkv4_ref, qkv16_ref, qb_ref, kb_ref, vb_ref, qc_ref, kc_ref, vt_ref, stage_ref):
    x = x_ref[...]
    h = x * lax.rsqrt(jnp.mean(x * x, axis=-1, keepdims=True) + EPS) * gmix_ref[...]
    y = jnp.dot(h.astype(BF16), w_ref[...], preferred_element_type=F32)

    o_qb = 3 * A_W
    o_kb = o_qb + B_QW
    o_vb = o_kb + B_KVW
    o_cq = o_vb + B_KVW
    o_ckv = o_cq + C_Q_RANK
    o_kr = o_ckv + C_KV_RANK
    yn = jnp.concatenate([y[:, :2 * A_W], y[:, o_qb:o_vb]], axis=1)
    yn = yn * lax.rsqrt(_group_mean_sq(yn, ones64_ref, HEAD_DIM) + EPS) * gqk_ref[...]
    qb_ref[...] = yn[:, 2 * A_W:2 * A_W + B_QW].astype(BF16)
    kb_ref[...] = yn[:, 2 * A_W + B_QW:].astype(BF16)
    n_slabs, tm = stage_ref.shape[0], stage_ref.shape[1]
    for s in range(n_slabs):
        src = yn if s * LANES < 2 * A_W else y
        stage_ref[s] = src[:, s * LANES:(s + 1) * LANES]
    for out_ref, (_, r) in zip((qkv1_ref, qkv4_ref, qkv16_ref), A_CONFIGS):
        for c in range(r):
            for s in range(n_slabs):
                col = c * 3 * A_W + s * LANES
                out_ref[:, col:col + LANES] = stage_ref[s, pl.ds(c, tm // r, stride=r), :].astype(BF16)
    vb_ref[...] = y[:, o_vb:o_cq].astype(BF16)

    cos_t = jnp.concatenate([cos_ref[...]] * C_HEADS, axis=1)
    sin_t = jnp.concatenate([sin_ref[...]] * C_HEADS, axis=1)
    lane = lax.broadcasted_iota(jnp.int32, cos_t.shape, 1) % LANES
    cq = y[:, o_cq:o_ckv]
    cq = cq * lax.rsqrt(jnp.mean(cq * cq, axis=-1, keepdims=True) + EPS) * glq_ref[...]
    qc = jnp.dot(cq.astype(BF16), wuq_ref[...], preferred_element_type=F32)
    qc = qc * lax.rsqrt(_group_mean_sq(qc, ones128_ref, C_QK) + EPS) * gqc_ref[...]
    qc_ref[...] = _rope(qc, cos_t, sin_t, lane).astype(BF16)

    ckv = y[:, o_ckv:o_kr]
    ckv = (ckv * lax.rsqrt(jnp.mean(ckv * ckv, axis=-1, keepdims=True) + EPS) * glkv_ref[...]).astype(BF16)
    kr = y[:, o_kr:]
    kc = jnp.dot(ckv, wuk_ref[...], preferred_element_type=F32) + jnp.concatenate([kr] * C_HEADS, axis=1)
    kc = kc * lax.rsqrt(_group_mean_sq(kc, ones128_ref, C_QK) + EPS) * gkc_ref[...]
    kc_ref[...] = _rope(kc, cos_t, sin_t, lane).astype(BF16)
    vt = lax.dot_general(wuvt_ref[...], ckv, (((1,), (1,)), ((), ())), preferred_element_type=F32)
    row = lax.broadcasted_iota(jnp.int32, (VT_ROWS - C_VDIM, vt.shape[1]), 0)
    ones_rows = jnp.where(row == 0, 1.0, 0.0).astype(BF16)
    for hd in range(C_HEADS):
        vt_ref[0, hd * VT_ROWS:hd * VT_ROWS + C_VDIM, :] = vt[hd * C_VDIM:(hd + 1) * C_VDIM].astype(BF16)
        vt_ref[0, hd * VT_ROWS + C_VDIM:(hd + 1) * VT_ROWS, :] = ones_rows


def _inproj(x2d, cos_t, sin_t, p):
    t = x2d.shape[0]
    tm = TM_IN
    row = lambda w: pl.BlockSpec((tm, w), lambda i: (i, 0))
    outs = [(B_QW, BF16), (B_KVW, BF16), (B_KVW, BF16)] + [(C_SLOTS, BF16)] * 2
    a_specs = [pl.BlockSpec((tm // r, r * 3 * A_W), lambda i: (i, 0)) for _, r in A_CONFIGS]
    a_shapes = [jax.ShapeDtypeStruct((t // r, r * 3 * A_W), BF16) for _, r in A_CONFIGS]
    return pl.pallas_call(
        _inproj_kernel,
        grid=(t // tm,),
        in_specs=[row(D_MODEL), row(LANES), row(LANES),
                  _const_spec((1, D_MODEL)), _const_spec((D_MODEL, IN_COLS)),
                  _const_spec((1, 2 * A_W + B_QW + B_KVW)),
                  _const_spec((2 * LANES, 2 * LANES)), _const_spec((2 * LANES, 2 * LANES)),
                  _const_spec((1, C_Q_RANK)), _const_spec((C_Q_RANK, C_SLOTS)), _const_spec((1, C_SLOTS)),
                  _const_spec((1, C_KV_RANK)), _const_spec((C_KV_RANK, C_SLOTS)),
                  _const_spec((C_HEADS * C_VDIM, C_KV_RANK)), _const_spec((1, C_SLOTS))],
        out_specs=a_specs + [row(w) for w, _ in outs]
        + [pl.BlockSpec((1, C_HEADS * VT_ROWS, tm), lambda i: (i, 0, 0))],
        out_shape=a_shapes + [jax.ShapeDtypeStruct((t, w), d) for w, d in outs]
        + [jax.ShapeDtypeStruct((t // tm, C_HEADS * VT_ROWS, tm), BF16)],
        scratch_shapes=[pltpu.VMEM((3 * A_W // LANES, tm, LANES), F32)],
        compiler_params=_cparams(("parallel",)),
        name="inproj",
    )(x2d, cos_t, sin_t, p["gmix"], p["w_in"], p["gqk"], p["ones64"], p["ones128"],
      p["glq"], p["wuq"], p["gqc"], p["glkv"], p["wuk"], p["wuvt"], p["gkc"])


def _window(lo_ref, main_ref, hi_ref, a, radius, tq):
    start, end = a - radius, a + SUB + radius
    parts = []
    if start < 0:
        parts.append(lo_ref[radius + start:radius, :])
        start = 0
    parts.append(main_ref[start:min(end, tq), :])
    if end > tq:
        parts.append(hi_ref[0:end - tq, :])
    return parts[0] if len(parts) == 1 else jnp.concatenate(parts, axis=0)


def _banded_kernel(*refs, tq, radius, head_ids, kv_slab, has_sink, emit_ml, n_tiles):
    if has_sink:
        sink_ref, refs = refs[0], refs[1:]
    q_ref, klo_ref, k_ref, khi_ref, vlo_ref, v_ref, vhi_ref, bias_ref, o_ref = refs[:9]
    ml_ref = refs[9] if emit_ml else None
    s_scr, m_scr, p_scr = refs[-3:]
    tile = pl.program_id(2)
    width = SUB + 2 * radius
    lane = lax.broadcasted_iota(jnp.int32, (SUB, LANES), 1)
    col = lax.broadcasted_iota(jnp.int32, (SUB, width), 1)
    low = lane < HEAD_DIM
    low_w = lax.broadcasted_iota(jnp.int32, (width, LANES), 1) < HEAD_DIM
    subs = list(range(0, tq, SUB))
    slab = lambda w, pair: w[:, kv_slab[pair] * LANES:(kv_slab[pair] + 1) * LANES]
    tiles = [(a, pair, e, hd) for a in subs for pair, heads in enumerate(head_ids) for e, hd in enumerate(heads)]

    kws = {a: _window(klo_ref, k_ref, khi_ref, a, radius, tq) for a in subs}
    for g, (a, pair, e, hd) in enumerate(tiles):
        qs = q_ref[a:a + SUB, pair * LANES:(pair + 1) * LANES]
        qm = jnp.where(low == (e == 0), qs, jnp.zeros_like(qs))
        ks = slab(kws[a], pair)
        s = lax.dot_general(qm, ks, (((1,), (1,)), ((), ())), preferred_element_type=F32)
        s = s + bias_ref[hd]
        if a == 0:
            s = jnp.where(col < jnp.where(tile == 0, radius, 0), NEG, s)
        if a == tq - SUB:
            s = jnp.where(col >= jnp.where(tile == n_tiles - 1, SUB + radius, width), NEG, s)
        m = jnp.max(s, axis=-1, keepdims=True)
        if has_sink:
            m = jnp.maximum(m, sink_ref[hd])
        s_scr[g] = s
        m_scr[g] = jnp.broadcast_to(m, (SUB, LANES))

    for g in range(len(tiles)):
        m_wide = jnp.concatenate([m_scr[g]] * (width // LANES), axis=1)
        p_scr[g] = jnp.exp(s_scr[g] - m_wide).astype(BF16)

    vws = {a: _window(vlo_ref, v_ref, vhi_ref, a, radius, tq) for a in subs}
    ml = None
    for g in range(0, len(tiles), 2):
        a, pair = tiles[g][0], tiles[g][1]
        hd0, hd1 = tiles[g][3], tiles[g + 1][3]
        vs = slab(vws[a], pair)
        ones = jnp.ones_like(vs)
        o0 = jnp.dot(p_scr[g], jnp.where(low_w, vs, ones), preferred_element_type=F32)
        o1 = jnp.dot(p_scr[g + 1], jnp.where(low_w, ones, vs), preferred_element_type=F32)
        m0, m1 = m_scr[g], m_scr[g + 1]
        l_swapped = jnp.where(low, o1, o0)
        l_pair = pltpu.roll(l_swapped, HEAD_DIM, 1)
        if has_sink:
            extra = jnp.where(low, jnp.exp(sink_ref[hd0] - m0), jnp.exp(sink_ref[hd1] - m1))
            l_pair = l_pair + extra
            l_swapped = l_swapped + pltpu.roll(extra, HEAD_DIM, 1)
        o_ref[a:a + SUB, pair * LANES:(pair + 1) * LANES] = (jnp.where(low, o0, o1) / l_pair).astype(o_ref.dtype)
        if emit_ml:
            ml = jnp.zeros((SUB, LANES), F32) if pair == 0 else ml
            ml = jnp.where(lane == hd0, m0, ml)
            ml = jnp.where(lane == hd1, m1, ml)
            ml = jnp.where(lane == ML_L_OFFSET + hd0, l_pair, ml)
            ml = jnp.where(lane == ML_L_OFFSET + hd1, l_swapped, ml)
            if pair == len(head_ids) - 1:
                ml_ref[a:a + SUB, :] = ml


def _banded(q, k, v, bias, *, batch, seq, dilation, radius, widths, head_ids, kv_slab, sink, emit_ml, out_dtype):
    n = seq // dilation
    tq = min(TQ_BAND, n)
    n_tiles = n // tq
    qw, kw = widths
    n_sub_tiles = (tq // SUB) * sum(len(heads) for heads in head_ids)
    view = lambda t: t[0].reshape(batch, n, t[0].shape[1])
    per_tile = tq // radius

    def main(w, t=(None, 1, 0)):
        return pl.BlockSpec((None, tq, w), lambda b, c, i: (b, i, t[1] * c + t[2]))

    def lo(t):
        return pl.BlockSpec((None, radius, kw),
                            lambda b, c, i: (b, jnp.maximum(i * per_tile - 1, 0), t[1] * c + t[2]))

    def hi(t):
        return pl.BlockSpec((None, radius, kw),
                            lambda b, c, i: (b, jnp.minimum((i + 1) * per_tile, n // radius - 1), t[1] * c + t[2]))

    in_specs = [main(qw, q), lo(k), main(kw, k), hi(k), lo(v), main(kw, v), hi(v), _const_spec(bias.shape)]
    args = [view(q), view(k), view(k), view(k), view(v), view(v), view(v), bias]
    if sink is not None:
        in_specs = [pl.BlockSpec(memory_space=pltpu.SMEM)] + in_specs
        args = [sink] + args
    out_specs = [main(qw)]
    out_shape = [jax.ShapeDtypeStruct((batch, n, dilation * qw), out_dtype)]
    if emit_ml:
        out_specs.append(main(LANES))
        out_shape.append(jax.ShapeDtypeStruct((batch, n, dilation * LANES), F32))
    outs = pl.pallas_call(
        functools.partial(_banded_kernel, tq=tq, radius=radius, head_ids=head_ids, kv_slab=kv_slab,
                          has_sink=sink is not None, emit_ml=emit_ml, n_tiles=n_tiles),
        grid=(batch, dilation, n_tiles),
        in_specs=in_specs,
        out_specs=out_specs,
        out_shape=out_shape,
        scratch_shapes=[pltpu.VMEM((n_sub_tiles, SUB, SUB + 2 * radius), F32),
                        pltpu.VMEM((n_sub_tiles, SUB, LANES), F32),
                        pltpu.VMEM((n_sub_tiles, SUB, SUB + 2 * radius), BF16)],
        compiler_params=_cparams(("parallel", "parallel", "parallel")),
        name="banded_r%d_d%d" % (radius, dilation),
    )(*args)
    return [o.reshape(batch * n, -1) for o in outs]


def _latent_kernel(q_ref, k_ref, vt_ref, o_ref, s_ref, cmax_ref, m_ref, acc_ref, *, tk):
    tq = q_ref.shape[0]
    nk = k_ref.shape[0] // tk
    ahead = LAT_SLOTS - 1
    qt = q_ref[...].astype(F32).T.astype(BF16)
    m_ref[...] = jnp.full(m_ref.shape, -jnp.inf, F32)
    acc_ref[...] = jnp.zeros(acc_ref.shape, F32)

    def scores(slot, j):
        start = pl.multiple_of(j * tk, tk)
        s = jnp.dot(k_ref[pl.ds(start, tk), :], qt, preferred_element_type=F32)
        s_ref[slot] = s
        cmax_ref[slot] = jnp.max(s, axis=0, keepdims=True)

    def consume(slot, j):
        m_old = m_ref[...]
        m_new = jnp.maximum(m_old, cmax_ref[slot])
        pr = jnp.exp2(s_ref[slot] - m_new).astype(BF16)
        acc_ref[...] = jnp.exp2(m_old - m_new) * acc_ref[...] + jnp.dot(
            vt_ref[j], pr, preferred_element_type=F32)
        m_ref[...] = m_new

    def step(j, u, with_scores):
        if with_scores:
            scores((u + ahead) % LAT_SLOTS, j + ahead)
        consume(u % LAT_SLOTS, j)

    for j in range(ahead):
        scores(j, j)
    trips = (nk - ahead) // LAT_UNROLL

    def body(jj, carry):
        for u in range(LAT_UNROLL):
            step(LAT_UNROLL * jj + u, u, True)
        return carry

    lax.fori_loop(0, trips, body, 0)
    for j in range(trips * LAT_UNROLL, nk):
        step(j, j, j + ahead < nk)
    acc = acc_ref[...]
    o = jnp.concatenate([acc[:C_VDIM] / acc[C_VDIM:C_VDIM + 1], jnp.zeros((LANES - C_VDIM, tq), F32)], axis=0)
    o_ref[...] = o.T.astype(o_ref.dtype)


def _latent(qc, kc, vt, batch, seq):
    t = batch * seq
    tq, tk = TQ_LAT, vt.shape[2]
    nq = seq // tq
    return pl.pallas_call(
        functools.partial(_latent_kernel, tk=tk),
        grid=(batch, C_HEADS, nq),
        in_specs=[pl.BlockSpec((tq, LANES), lambda b, h, i: (b * nq + i, h)),
                  pl.BlockSpec((seq, LANES), lambda b, h, i: (b, h)),
                  pl.BlockSpec((seq // tk, VT_ROWS, tk), lambda b, h, i: (b, h, 0))],
        out_specs=pl.BlockSpec((tq, LANES), lambda b, h, i: (b * nq + i, h)),
        out_shape=jax.ShapeDtypeStruct((t, C_SLOTS), BF16),
        scratch_shapes=[pltpu.VMEM((LAT_SLOTS, tk, tq), F32), pltpu.VMEM((LAT_SLOTS, 1, tq), F32),
                        pltpu.VMEM((1, tq), F32), pltpu.VMEM((VT_ROWS, tq), F32)],
        compiler_params=_cparams(("parallel", "parallel", "parallel")),
        name="latent",
    )(qc, kc, vt)


def _merge_mlp_kernel(x_ref, oa1_ref, oa2_ref, oa3_ref, ml1_ref, ml2_ref, ml3_ref, ob_ref, oc_ref,
                      expand_ref, gout_ref, wout_ref, gmlp_ref, wup_ref, wdown_ref, out_ref, oa_s, ml_s):
    tm = x_ref.shape[0]
    oas, mls = [oa1_ref[...]], [ml1_ref[...]]
    for idx, (o_ref, l_ref) in enumerate(((oa2_ref, ml2_ref), (oa3_ref, ml3_ref))):
        r = A_CONFIGS[idx + 1][1]
        for c in range(r):
            for s in range(A_W // LANES):
                col = c * A_W + s * LANES
                oa_s[idx, s, pl.ds(c, tm // r, stride=r), :] = o_ref[:, col:col + LANES]
            ml_s[idx, pl.ds(c, tm // r, stride=r), :] = l_ref[:, c * LANES:(c + 1) * LANES]
        oas.append(jnp.concatenate([oa_s[idx, s] for s in range(A_W // LANES)], axis=1))
        mls.append(ml_s[idx])
    lane = lax.broadcasted_iota(jnp.int32, mls[0].shape, 1)
    m_all = jnp.maximum(jnp.maximum(mls[0], mls[1]), mls[2])
    ws = [pltpu.roll(ml, LANES - ML_L_OFFSET, 1) * jnp.exp(ml - m_all) for ml in mls]
    wsum = ws[0] + ws[1] + ws[2]
    packed = jnp.zeros_like(wsum)
    for c, w in enumerate(ws):
        wn = jnp.where(lane < A_HEADS, w / wsum, 0.0)
        packed = packed + (wn if c == 0 else pltpu.roll(wn, ML_L_OFFSET * c, 1))
    hi = packed.astype(BF16)
    lo = (packed - hi.astype(F32)).astype(BF16)
    spread = (jnp.dot(hi, expand_ref[...], preferred_element_type=F32)
              + jnp.dot(lo, expand_ref[...], preferred_element_type=F32))
    oa = spread[:, :A_W] * oas[0] + spread[:, A_W:2 * A_W] * oas[1] + spread[:, 2 * A_W:] * oas[2]

    def group_norm(v, width):
        return v * lax.rsqrt(jnp.sum(v * v, axis=-1, keepdims=True) * (1.0 / width) + EPS)

    ob = ob_ref[...].astype(F32)
    oc = oc_ref[...].astype(F32)
    mixed = jnp.concatenate([group_norm(oa, A_W), group_norm(ob, B_QW), group_norm(oc, C_HEADS * C_VDIM)], axis=1)
    mixed = (mixed * gout_ref[...]).astype(BF16)
    x = x_ref[...] + jnp.dot(mixed, wout_ref[...], preferred_element_type=F32)

    h = (x * lax.rsqrt(jnp.mean(x * x, axis=-1, keepdims=True) + EPS) * gmlp_ref[...]).astype(BF16)
    acc = x
    for s in range(0, D_FF, FF_CHUNK):
        u = jnp.dot(h, wup_ref[:, s:s + FF_CHUNK], preferred_element_type=F32)
        u = jnp.square(jnp.maximum(u, 0.0)).astype(BF16)
        acc = acc + jnp.dot(u, wdown_ref[s:s + FF_CHUNK, :], preferred_element_type=F32)
    out_ref[...] = acc


def _merge_mlp(x2d, oas, mls, ob, oc, p):
    t = x2d.shape[0]
    tm = TM_MLP
    row = lambda w: pl.BlockSpec((tm, w), lambda i: (i, 0))
    strided = lambda w, r: pl.BlockSpec((tm // r, r * w), lambda i: (i, 0))
    single = lambda shape: pl.BlockSpec(shape, lambda i: (0,) * len(shape), pipeline_mode=pl.Buffered(1))
    return pl.pallas_call(
        _merge_mlp_kernel,
        grid=(t // tm,),
        in_specs=[row(D_MODEL)] + [strided(A_W, r) for _, r in A_CONFIGS]
        + [strided(LANES, r) for _, r in A_CONFIGS] + [row(B_QW), row(C_SLOTS),
                  single((LANES, 3 * A_W)), single((1, MIX_COLS)), single((MIX_COLS, D_MODEL)),
                  single((1, D_MODEL)), single((D_MODEL, D_FF)), single((D_FF, D_MODEL))],
        out_specs=row(D_MODEL),
        out_shape=jax.ShapeDtypeStruct((t, D_MODEL), F32),
        scratch_shapes=[pltpu.VMEM((2, A_W // LANES, tm, LANES), F32), pltpu.VMEM((2, tm, LANES), F32)],
        compiler_params=_cparams(("parallel",)),
        name="merge_mlp",
    )(x2d, *oas, *mls, ob, oc, p["expand"], p["gout"], p["w_out"], p["gmlp"], p["w_up"], p["w_down"])


def _block_diag_ones(group):
    idx = np.arange(2 * LANES) // group
    return jnp.asarray((idx[:, None] == idx[None, :]).astype(np.float32), dtype=BF16)


def _expand_matrix():
    e = np.zeros((LANES, len(A_CONFIGS) * A_W), np.float32)
    for c in range(len(A_CONFIGS)):
        for h in range(A_HEADS):
            e[ML_L_OFFSET * c + h, c * A_W + h * HEAD_DIM:c * A_W + (h + 1) * HEAD_DIM] = 1.0
    return jnp.asarray(e, dtype=BF16)


def _pad_heads(w, heads, used, lo=0):
    lead = w.shape[:-1]
    w = w.reshape(lead + (heads, used))
    pad = [(0, 0)] * len(lead) + [(0, 0), (lo, LANES - lo - used)]
    return jnp.pad(w, pad).reshape(lead + (heads * LANES,))


def _layer_params(i, norm_mix, w_in, qk_gain_a, qk_gain_b, q_lat_gain, kv_lat_gain, w_uq, w_ukv, qk_gain_c,
                  out_norm, w_out, norm_mlp, w_up, w_down):
    w = w_in[i]
    o = np.cumsum((A_W, A_W, A_W, B_QW, B_KVW, B_KVW, C_Q_RANK, C_KV_RANK)).tolist()
    qb = w[:, o[2]:o[3]].reshape(D_MODEL, B_HEADS, HEAD_DIM)[:, B_HEAD_ORDER, :].reshape(D_MODEL, B_QW)
    kr = _pad_heads(w[:, o[7]:], 1, C_ROPE, lo=C_NOPE)
    w_in_p = jnp.concatenate([w[:, :o[2]], qb, w[:, o[3]:o[7]], kr], axis=1).astype(BF16)

    scale = HEAD_DIM ** -0.5
    gqk = jnp.concatenate([jnp.tile(qk_gain_a[i, 0], A_HEADS) * scale, jnp.tile(qk_gain_a[i, 1], A_HEADS),
                           jnp.tile(qk_gain_b[i, 0], B_HEADS) * scale, jnp.tile(qk_gain_b[i, 1], B_KV_HEADS)])
    ukv = w_ukv[i].reshape(C_KV_RANK, C_HEADS, C_NOPE + C_VDIM)
    wuk = _pad_heads(ukv[:, :, :C_NOPE].reshape(C_KV_RANK, -1), C_HEADS, C_NOPE)
    wuvt = ukv[:, :, C_NOPE:].reshape(C_KV_RANK, C_HEADS * C_VDIM).T
    g = out_norm[i]
    gb = g[A_W:A_W + B_QW].reshape(B_HEADS, HEAD_DIM)[B_HEAD_ORDER, :].reshape(B_QW)
    gout = jnp.concatenate([g[:A_W], gb, _pad_heads(g[A_W + B_QW:], C_HEADS, C_VDIM)])
    wo = w_out[i]
    wob = wo[A_W:A_W + B_QW].reshape(B_HEADS, HEAD_DIM, D_MODEL)[B_HEAD_ORDER, :, :].reshape(B_QW, D_MODEL)
    woc = jnp.pad(wo[A_W + B_QW:].reshape(C_HEADS, C_VDIM, D_MODEL), ((0, 0), (0, LANES - C_VDIM), (0, 0)))
    w_out_p = jnp.concatenate([wo[:A_W], wob, woc.reshape(C_SLOTS, D_MODEL)], axis=0).astype(BF16)
    return {
        "gmix": norm_mix[i][None, :],
        "w_in": w_in_p,
        "gqk": gqk[None, :],
        "ones64": _block_diag_ones(HEAD_DIM),
        "ones128": _block_diag_ones(LANES),
        "glq": q_lat_gain[i][None, :],
        "wuq": _pad_heads(w_uq[i], C_HEADS, C_QK).astype(BF16),
        "gqc": jnp.tile(_pad_heads(qk_gain_c[i, 0] * (C_QK ** -0.5 * LOG2E), 1, C_QK), C_HEADS)[None, :],
        "glkv": kv_lat_gain[i][None, :],
        "wuk": wuk.astype(BF16),
        "wuvt": wuvt.astype(BF16),
        "gkc": jnp.tile(_pad_heads(qk_gain_c[i, 1], 1, C_QK), C_HEADS)[None, :],
        "expand": _expand_matrix(),
        "gout": gout[None, :],
        "w_out": w_out_p,
        "gmlp": norm_mlp[i][None, :],
        "w_up": w_up[i].astype(BF16),
        "w_down": w_down[i].astype(BF16),
    }


def kernel(x, positions, rel_bias_table, norm_mix, w_in, qk_gain_a, qk_gain_b, sink_b, q_lat_gain, kv_lat_gain,
           w_uq, w_ukv, qk_gain_c, out_norm, w_out, norm_mlp, w_up, w_down):
    batch, seq, _ = x.shape
    depth = w_in.shape[0]
    x2d = x.reshape(batch * seq, D_MODEL)
    cos_t, sin_t = _rope_tables(positions)
    a_pairs = tuple((2 * p, 2 * p + 1) for p in range(A_HEADS // 2))
    bias_a = [_bias_tiles(rel_bias_table, window // (2 * r), r, tuple(range(A_HEADS))) for window, r in A_CONFIGS]
    bias_b = _bias_tiles(rel_bias_table, B_RADIUS, 1, tuple(A_HEADS + h for h in range(B_HEADS)))
    for i in range(depth):
        p = _layer_params(i, norm_mix, w_in, qk_gain_a, qk_gain_b, q_lat_gain, kv_lat_gain, w_uq, w_ukv,
                          qk_gain_c, out_norm, w_out, norm_mlp, w_up, w_down)
        qkv1, qkv4, qkv16, qb, kb, vb, qc, kc, vt = _inproj(x2d, cos_t, sin_t, p)
        oas, mls = [], []
        for (window, r), bias, qkv in zip(A_CONFIGS, bias_a, (qkv1, qkv4, qkv16)):
            o, ml = _banded((qkv, 3, 0), (qkv, 3, 1), (qkv, 3, 2), bias, batch=batch, seq=seq, dilation=r,
                            radius=window // (2 * r), widths=(A_W, A_W), head_ids=a_pairs, kv_slab=(0, 1, 2),
                            sink=None, emit_ml=True, out_dtype=F32)
            oas.append(o)
            mls.append(ml)
        (ob,) = _banded((qb, 1, 0), (kb, 1, 0), (vb, 1, 0), bias_b, batch=batch, seq=seq, dilation=1,
                        radius=B_RADIUS, widths=(B_QW, B_KVW), head_ids=((0, 2), (1, 3)), kv_slab=(0, 0),
                        sink=sink_b[i], emit_ml=False, out_dtype=BF16)
        oc = _latent(qc, kc, vt, batch, seq)
        x2d = _merge_mlp(x2d, oas, mls, ob, oc, p)
    return x2d.reshape(batch, seq, D_MODEL)
```

```python
import functools
import math

import numpy as np
import jax
import jax.numpy as jnp
from jax import lax
from jax.experimental import pallas as pl
from jax.experimental.pallas import tpu as pltpu

F32 = jnp.float32
BF16 = jnp.bfloat16

D_MODEL = 1024
HEAD_DIM = 64
A_HEADS = 6
A_CONFIGS = ((128, 1), (512, 4), (2048, 16))
B_HEADS = 4
B_KV_HEADS = 2
B_RADIUS = 128
C_HEADS = 6
C_NOPE = 64
C_ROPE = 32
C_VDIM = 64
C_QK = C_NOPE + C_ROPE
C_Q_RANK = 256
C_KV_RANK = 128
ROPE_THETA = 10000.0
N_BUCKETS = 32
MAX_DISTANCE = 1024
D_FF = 4 * D_MODEL
EPS = 1e-6
NEG = -1e30

A_W = A_HEADS * HEAD_DIM
B_QW = B_HEADS * HEAD_DIM
B_KVW = B_KV_HEADS * HEAD_DIM
LANES = 128
C_SLOTS = C_HEADS * LANES
IN_COLS = 3 * A_W + B_QW + 2 * B_KVW + C_Q_RANK + C_KV_RANK + LANES
MIX_COLS = A_W + B_QW + C_SLOTS
VT_ROWS = 80
LOG2E = math.log2(math.e)
ML_L_OFFSET = 8

B_HEAD_ORDER = (0, 2, 1, 3)

TM_IN = 512
TQ_BAND = 512
SUB = 128
TQ_LAT = 1024
BOUND_MARGIN = 1.02
MAX_LOGIT_BOUND = 50.0
LAT_BOUNDED_UNROLL = 8
LAT_SLOTS = 3
LAT_UNROLL = 6
TM_MLP = 512
FF_CHUNK = 1024
VMEM_LIMIT = 56 * 1024 * 1024


def _cparams(sem):
    return pltpu.CompilerParams(dimension_semantics=sem, vmem_limit_bytes=VMEM_LIMIT)


def _const_spec(shape):
    zeros = (0,) * len(shape)
    return pl.BlockSpec(shape, lambda *_: zeros)


def _rope_table_kernel(pos_ref, inv_ref, cos_ref, sin_ref):
    ang = pos_ref[...] * inv_ref[...]
    lane = lax.broadcasted_iota(jnp.int32, ang.shape, 1)
    c = jnp.cos(ang)
    s = jnp.sin(ang)
    first = (lane >= C_NOPE) & (lane < C_NOPE + C_ROPE // 2)
    second = (lane >= C_NOPE + C_ROPE // 2) & (lane < C_QK)
    cos_ref[...] = jnp.where(first | second, c, 1.0)
    sin_ref[...] = jnp.where(first, -s, jnp.where(second, s, 0.0))


def _rope_tables(positions):
    t = positions.size
    half = C_ROPE // 2
    inv = ROPE_THETA ** (-jnp.arange(half, dtype=F32) / half)
    inv_row = jnp.concatenate([jnp.zeros((C_NOPE,), F32), inv, inv, jnp.zeros((LANES - C_QK,), F32)])[None, :]
    pos = positions.astype(F32).reshape(t, 1)
    tm = 2048
    return pl.pallas_call(
        _rope_table_kernel,
        grid=(t // tm,),
        in_specs=[pl.BlockSpec((tm, 1), lambda i: (i, 0)), _const_spec((1, LANES))],
        out_specs=[pl.BlockSpec((tm, LANES), lambda i: (i, 0))] * 2,
        out_shape=[jax.ShapeDtypeStruct((t, LANES), F32)] * 2,
        compiler_params=_cparams(("parallel",)),
        name="rope_tables",
    )(pos, inv_row)


def _bucket_thresholds():
    half = N_BUCKETS // 2
    exact = half // 2
    n = np.arange(1, 2 * MAX_DISTANCE + 2, dtype=np.float64)
    far = exact + (np.log(n / exact) / math.log(MAX_DISTANCE / exact) * (half - exact)).astype(np.int64)
    far = np.minimum(far, half - 1)
    return tuple(int(n[np.argmax(far >= exact + k)]) for k in range(1, half - exact))


def _bias_kernel(table_ref, out_ref, *, radius, dilation, head_cols):
    hsel = pl.program_id(0)
    width = SUB + 2 * radius
    row = lax.broadcasted_iota(jnp.int32, (SUB, width), 0)
    col = lax.broadcasted_iota(jnp.int32, (SUB, width), 1)
    rel = col - radius - row
    n = jnp.abs(rel) * dilation
    half = N_BUCKETS // 2
    exact = half // 2
    far = jnp.full(n.shape, exact, jnp.int32)
    for thr in _bucket_thresholds():
        far = far + (n >= thr).astype(jnp.int32)
    bucket = jnp.where(rel > 0, half, 0) + jnp.where(n < exact, n, far)
    for idx, hc in enumerate(head_cols):
        @pl.when(hsel == idx)
        def _(hc=hc):
            val = jnp.zeros(n.shape, F32)
            for b in range(N_BUCKETS):
                val = jnp.where(bucket == b, table_ref[b, hc], val)
            out_ref[...] = jnp.where(jnp.abs(rel) <= radius, val, NEG)


def _bias_tiles(table, radius, dilation, head_cols):
    width = SUB + 2 * radius
    return pl.pallas_call(
        functools.partial(_bias_kernel, radius=radius, dilation=dilation, head_cols=head_cols),
        grid=(len(head_cols),),
        in_specs=[pl.BlockSpec(memory_space=pltpu.SMEM)],
        out_specs=pl.BlockSpec((None, SUB, width), lambda h: (h, 0, 0)),
        out_shape=jax.ShapeDtypeStruct((len(head_cols), SUB, width), F32),
        compiler_params=_cparams(("arbitrary",)),
        name="bias_tiles",
    )(table)


def _group_mean_sq(y, ones_ref, group):
    sq = (y * y).astype(BF16)
    width = y.shape[1]
    parts = []
    for s in range(0, width, 2 * LANES):
        w = min(2 * LANES, width - s)
        parts.append(jnp.dot(sq[:, s:s + w], ones_ref[:w, :w], preferred_element_type=F32))
    out = parts[0] if len(parts) == 1 else jnp.concatenate(parts, axis=1)
    return out * (1.0 / group)


def _slab_roll(y, shift):
    parts = [pltpu.roll(y[:, s:s + LANES], shift, 1) for s in range(0, y.shape[1], LANES)]
    return parts[0] if len(parts) == 1 else jnp.concatenate(parts, axis=1)


def _rope(y, cos_t, sin_t, lane):
    swapped = jnp.where(lane < C_NOPE + C_ROPE // 2, _slab_roll(y, LANES - C_ROPE // 2), _slab_roll(y, C_ROPE // 2))
    return y * cos_t + swapped * sin_t


def _inproj_kernel(x_ref, cos_ref, sin_ref, gmix_ref, w_ref, gqk_ref, ones64_ref, ones128_ref,
                   glq_ref, wuq_ref, gqc_ref, glkv_ref, wuk_ref, wuvt_ref, gkc_ref, qshift_ref, kone_ref,
                   qkv1_ref, qkv4_ref, qkv16_ref, qb_ref, kb_ref, vb_ref, qc_ref, kc_ref, vt_ref, stage_ref):
    x = x_ref[...]
    h = x * lax.rsqrt(jnp.mean(x * x, axis=-1, keepdims=True) + EPS) * gmix_ref[...]
    y = jnp.dot(h.astype(BF16), w_ref[...], preferred_element_type=F32)

    o_qb = 3 * A_W
    o_kb = o_qb + B_QW
    o_vb = o_kb + B_KVW
    o_cq = o_vb + B_KVW
    o_ckv = o_cq + C_Q_RANK
    o_kr = o_ckv + C_KV_RANK
    yn = jnp.concatenate([y[:, :2 * A_W], y[:, o_qb:o_vb]], axis=1)
    yn = yn * lax.rsqrt(_group_mean_sq(yn, ones64_ref, HEAD_DIM) + EPS) * gqk_ref[...]
    qb_ref[...] = yn[:, 2 * A_W:2 * A_W + B_QW].astype(BF16)
    kb_ref[...] = yn[:, 2 * A_W + B_QW:].astype(BF16)
    n_slabs, tm = stage_ref.shape[0], stage_ref.shape[1]
    for s in range(n_slabs):
        src = yn if s * LANES < 2 * A_W else y
        stage_ref[s] = src[:, s * LANES:(s + 1) * LANES]
    for out_ref, (_, r) in zip((qkv1_ref, qkv4_ref, qkv16_ref), A_CONFIGS):
        for c in range(r):
            for s in range(n_slabs):
                col = c * 3 * A_W + s * LANES
                out_ref[:, col:col + LANES] = stage_ref[s, pl.ds(c, tm // r, stride=r), :].astype(BF16)
    vb_ref[...] = y[:, o_vb:o_cq].astype(BF16)

    cos_t = jnp.concatenate([cos_ref[...]] * C_HEADS, axis=1)
    sin_t = jnp.concatenate([sin_ref[...]] * C_HEADS, axis=1)
    lane = lax.broadcasted_iota(jnp.int32, cos_t.shape, 1) % LANES
    cq = y[:, o_cq:o_ckv]
    cq = cq * lax.rsqrt(jnp.mean(cq * cq, axis=-1, keepdims=True) + EPS) * glq_ref[...]
    qc = jnp.dot(cq.astype(BF16), wuq_ref[...], preferred_element_type=F32)
    qc = qc * lax.rsqrt(_group_mean_sq(qc, ones128_ref, C_QK) + EPS) * gqc_ref[...]
    qc_ref[...] = (_rope(qc, cos_t, sin_t, lane) + qshift_ref[...]).astype(BF16)

    ckv = y[:, o_ckv:o_kr]
    ckv = (ckv * lax.rsqrt(jnp.mean(ckv * ckv, axis=-1, keepdims=True) + EPS) * glkv_ref[...]).astype(BF16)
    kr = y[:, o_kr:]
    kc = jnp.dot(ckv, wuk_ref[...], preferred_element_type=F32) + jnp.concatenate([kr] * C_HEADS, axis=1)
    kc = kc * lax.rsqrt(_group_mean_sq(kc, ones128_ref, C_QK) + EPS) * gkc_ref[...]
    kc_ref[...] = (_rope(kc, cos_t, sin_t, lane) + kone_ref[...]).astype(BF16)
    vt = lax.dot_general(wuvt_ref[...], ckv, (((1,), (1,)), ((), ())), preferred_element_type=F32)
    row = lax.broadcasted_iota(jnp.int32, (VT_ROWS - C_VDIM, vt.shape[1]), 0)
    ones_rows = jnp.where(row == 0, 1.0, 0.0).astype(BF16)
    for hd in range(C_HEADS):
        vt_ref[0, hd * VT_ROWS:hd * VT_ROWS + C_VDIM, :] = vt[hd * C_VDIM:(hd + 1) * C_VDIM].astype(BF16)
        vt_ref[0, hd * VT_ROWS + C_VDIM:(hd + 1) * VT_ROWS, :] = ones_rows


def _inproj(x2d, cos_t, sin_t, p):
    t = x2d.shape[0]
    tm = TM_IN
    row = lambda w: pl.BlockSpec((tm, w), lambda i: (i, 0))
    outs = [(B_QW, BF16), (B_KVW, BF16), (B_KVW, BF16)] + [(C_SLOTS, BF16)] * 2
    a_specs = [pl.BlockSpec((tm // r, r * 3 * A_W), lambda i: (i, 0)) for _, r in A_CONFIGS]
    a_shapes = [jax.ShapeDtypeStruct((t // r, r * 3 * A_W), BF16) for _, r in A_CONFIGS]
    return pl.pallas_call(
        _inproj_kernel,
        grid=(t // tm,),
        in_specs=[row(D_MODEL), row(LANES), row(LANES),
                  _const_spec((1, D_MODEL)), _const_spec((D_MODEL, IN_COLS)),
                  _const_spec((1, 2 * A_W + B_QW + B_KVW)),
                  _const_spec((2 * LANES, 2 * LANES)), _const_spec((2 * LANES, 2 * LANES)),
                  _const_spec((1, C_Q_RANK)), _const_spec((C_Q_RANK, C_SLOTS)), _const_spec((1, C_SLOTS)),
                  _const_spec((1, C_KV_RANK)), _const_spec((C_KV_RANK, C_SLOTS)),
                  _const_spec((C_HEADS * C_VDIM, C_KV_RANK)), _const_spec((1, C_SLOTS)),
                  _const_spec((1, C_SLOTS)), _const_spec((1, C_SLOTS))],
        out_specs=a_specs + [row(w) for w, _ in outs]
        + [pl.BlockSpec((1, C_HEADS * VT_ROWS, tm), lambda i: (i, 0, 0))],
        out_shape=a_shapes + [jax.ShapeDtypeStruct((t, w), d) for w, d in outs]
        + [jax.ShapeDtypeStruct((t // tm, C_HEADS * VT_ROWS, tm), BF16)],
        scratch_shapes=[pltpu.VMEM((3 * A_W // LANES, tm, LANES), F32)],
        compiler_params=_cparams(("parallel",)),
        name="inproj",
    )(x2d, cos_t, sin_t, p["gmix"], p["w_in"], p["gqk"], p["ones64"], p["ones128"],
      p["glq"], p["wuq"], p["gqc"], p["glkv"], p["wuk"], p["wuvt"], p["gkc"], p["qshift"], p["kone"])


def _window(lo_ref, main_ref, hi_ref, a, radius, tq):
    start, end = a - radius, a + SUB + radius
    parts = []
    if start < 0:
        parts.append(lo_ref[radius + start:radius, :])
        start = 0
    parts.append(main_ref[start:min(end, tq), :])
    if end > tq:
        parts.append(hi_ref[0:end - tq, :])
    return parts[0] if len(parts) == 1 else jnp.concatenate(parts, axis=0)


def _banded_kernel(*refs, tq, radius, head_ids, kv_slab, has_sink, emit_ml, n_tiles):
    if has_sink:
        sink_ref, refs = refs[0], refs[1:]
    q_ref, klo_ref, k_ref, khi_ref, vlo_ref, v_ref, vhi_ref, bias_ref, o_ref = refs[:9]
    ml_ref = refs[9] if emit_ml else None
    s_scr, m_scr, p_scr = refs[-3:]
    tile = pl.program_id(2)
    width = SUB + 2 * radius
    lane = lax.broadcasted_iota(jnp.int32, (SUB, LANES), 1)
    col = lax.broadcasted_iota(jnp.int32, (SUB, width), 1)
    low = lane < HEAD_DIM
    low_w = lax.broadcasted_iota(jnp.int32, (width, LANES), 1) < HEAD_DIM
    subs = list(range(0, tq, SUB))
    slab = lambda w, pair: w[:, kv_slab[pair] * LANES:(kv_slab[pair] + 1) * LANES]
    tiles = [(a, pair, e, hd) for a in subs for pair, heads in enumerate(head_ids) for e, hd in enumerate(heads)]

    kws = {a: _window(klo_ref, k_ref, khi_ref, a, radius, tq) for a in subs}
    for g, (a, pair, e, hd) in enumerate(tiles):
        qs = q_ref[a:a + SUB, pair * LANES:(pair + 1) * LANES]
        qm = jnp.where(low == (e == 0), qs, jnp.zeros_like(qs))
        ks = slab(kws[a], pair)
        s = lax.dot_general(qm, ks, (((1,), (1,)), ((), ())), preferred_element_type=F32)
        s = s + bias_ref[hd]
        if a == 0:
            s = jnp.where(col < jnp.where(tile == 0, radius, 0), NEG, s)
        if a == tq - SUB:
            s = jnp.where(col >= jnp.where(tile == n_tiles - 1, SUB + radius, width), NEG, s)
        m = jnp.max(s, axis=-1, keepdims=True)
        if has_sink:
            m = jnp.maximum(m, sink_ref[hd])
        s_scr[g] = s
        m_scr[g] = jnp.broadcast_to(m, (SUB, LANES))

    for g in range(len(tiles)):
        m_wide = jnp.concatenate([m_scr[g]] * (width // LANES), axis=1)
        p_scr[g] = jnp.exp(s_scr[g] - m_wide).astype(BF16)

    vws = {a: _window(vlo_ref, v_ref, vhi_ref, a, radius, tq) for a in subs}
    ml = None
    for g in range(0, len(tiles), 2):
        a, pair = tiles[g][0], tiles[g][1]
        hd0, hd1 = tiles[g][3], tiles[g + 1][3]
        vs = slab(vws[a], pair)
        ones = jnp.ones_like(vs)
        o0 = jnp.dot(p_scr[g], jnp.where(low_w, vs, ones), preferred_element_type=F32)
        o1 = jnp.dot(p_scr[g + 1], jnp.where(low_w, ones, vs), preferred_element_type=F32)
        m0, m1 = m_scr[g], m_scr[g + 1]
        l_swapped = jnp.where(low, o1, o0)
        l_pair = pltpu.roll(l_swapped, HEAD_DIM, 1)
        if has_sink:
            extra = jnp.where(low, jnp.exp(sink_ref[hd0] - m0), jnp.exp(sink_ref[hd1] - m1))
            l_pair = l_pair + extra
            l_swapped = l_swapped + pltpu.roll(extra, HEAD_DIM, 1)
        o_ref[a:a + SUB, pair * LANES:(pair + 1) * LANES] = (jnp.where(low, o0, o1) / l_pair).astype(o_ref.dtype)
        if emit_ml:
            ml = jnp.zeros((SUB, LANES), F32) if pair == 0 else ml
            ml = jnp.where(lane == hd0, m0, ml)
            ml = jnp.where(lane == hd1, m1, ml)
            ml = jnp.where(lane == ML_L_OFFSET + hd0, l_pair, ml)
            ml = jnp.where(lane == ML_L_OFFSET + hd1, l_swapped, ml)
            if pair == len(head_ids) - 1:
                ml_ref[a:a + SUB, :] = ml


def _banded(q, k, v, bias, *, batch, seq, dilation, radius, widths, head_ids, kv_slab, sink, emit_ml, out_dtype):
    n = seq // dilation
    tq = min(TQ_BAND, n)
    n_tiles = n // tq
    qw, kw = widths
    n_sub_tiles = (tq // SUB) * sum(len(heads) for heads in head_ids)
    view = lambda t: t[0].reshape(batch, n, t[0].shape[1])
    per_tile = tq // radius

    def main(w, t=(None, 1, 0)):
        return pl.BlockSpec((None, tq, w), lambda b, c, i: (b, i, t[1] * c + t[2]))

    def lo(t):
        return pl.BlockSpec((None, radius, kw),
                            lambda b, c, i: (b, jnp.maximum(i * per_tile - 1, 0), t[1] * c + t[2]))

    def hi(t):
        return pl.BlockSpec((None, radius, kw),
                            lambda b, c, i: (b, jnp.minimum((i + 1) * per_tile, n // radius - 1), t[1] * c + t[2]))

    in_specs = [main(qw, q), lo(k), main(kw, k), hi(k), lo(v), main(kw, v), hi(v), _const_spec(bias.shape)]
    args = [view(q), view(k), view(k), view(k), view(v), view(v), view(v), bias]
    if sink is not None:
        in_specs = [pl.BlockSpec(memory_space=pltpu.SMEM)] + in_specs
        args = [sink] + args
    out_specs = [main(qw)]
    out_shape = [jax.ShapeDtypeStruct((batch, n, dilation * qw), out_dtype)]
    if emit_ml:
        out_specs.append(main(LANES))
        out_shape.append(jax.ShapeDtypeStruct((batch, n, dilation * LANES), F32))
    outs = pl.pallas_call(
        functools.partial(_banded_kernel, tq=tq, radius=radius, head_ids=head_ids, kv_slab=kv_slab,
                          has_sink=sink is not None, emit_ml=emit_ml, n_tiles=n_tiles),
        grid=(batch, dilation, n_tiles),
        in_specs=in_specs,
        out_specs=out_specs,
        out_shape=out_shape,
        scratch_shapes=[pltpu.VMEM((n_sub_tiles, SUB, SUB + 2 * radius), F32),
                        pltpu.VMEM((n_sub_tiles, SUB, LANES), F32),
                        pltpu.VMEM((n_sub_tiles, SUB, SUB + 2 * radius), BF16)],
        compiler_params=_cparams(("parallel", "parallel", "parallel")),
        name="banded_r%d_d%d" % (radius, dilation),
    )(*args)
    return [o.reshape(batch * n, -1) for o in outs]


def _latent_kernel(q_ref, k_ref, vt_ref, o_ref, s_ref, cmax_ref, m_ref, acc_ref, *, tk):
    tq = q_ref.shape[0]
    nk = k_ref.shape[0] // tk
    ahead = LAT_SLOTS - 1
    qt = q_ref[...].astype(F32).T.astype(BF16)
    m_ref[...] = jnp.full(m_ref.shape, -jnp.inf, F32)
    acc_ref[...] = jnp.zeros(acc_ref.shape, F32)

    def scores(slot, j):
        start = pl.multiple_of(j * tk, tk)
        s = jnp.dot(k_ref[pl.ds(start, tk), :], qt, preferred_element_type=F32)
        s_ref[slot] = s
        cmax_ref[slot] = jnp.max(s, axis=0, keepdims=True)

    def consume(slot, j):
        m_old = m_ref[...]
        m_new = jnp.maximum(m_old, cmax_ref[slot])
        pr = jnp.exp2(s_ref[slot] - m_new).astype(BF16)
        acc_ref[...] = jnp.exp2(m_old - m_new) * acc_ref[...] + jnp.dot(
            vt_ref[j], pr, preferred_element_type=F32)
        m_ref[...] = m_new

    def step(j, u, with_scores):
        if with_scores:
            scores((u + ahead) % LAT_SLOTS, j + ahead)
        consume(u % LAT_SLOTS, j)

    for j in range(ahead):
        scores(j, j)
    trips = (nk - ahead) // LAT_UNROLL

    def body(jj, carry):
        for u in range(LAT_UNROLL):
            step(LAT_UNROLL * jj + u, u, True)
        return carry

    lax.fori_loop(0, trips, body, 0)
    for j in range(trips * LAT_UNROLL, nk):
        step(j, j, j + ahead < nk)
    acc = acc_ref[...]
    o = jnp.concatenate([acc[:C_VDIM] / acc[C_VDIM:C_VDIM + 1], jnp.zeros((LANES - C_VDIM, tq), F32)], axis=0)
    o_ref[...] = o.T.astype(o_ref.dtype)


def _latent_bounded_kernel(q_ref, k_ref, vt_ref, o_ref, acc_ref, *, tk):
    tq = q_ref.shape[0]
    nk = k_ref.shape[0] // tk
    qt = q_ref[...].astype(F32).T.astype(BF16)
    acc_ref[...] = jnp.zeros(acc_ref.shape, F32)

    def body(jj, carry):
        for u in range(LAT_BOUNDED_UNROLL):
            j = LAT_BOUNDED_UNROLL * jj + u
            start = pl.multiple_of(j * tk, tk)
            s = jnp.dot(k_ref[pl.ds(start, tk), :], qt, preferred_element_type=F32)
            acc_ref[...] += jnp.dot(vt_ref[j], jnp.exp2(s).astype(BF16), preferred_element_type=F32)
        return carry

    lax.fori_loop(0, nk // LAT_BOUNDED_UNROLL, body, 0)
    acc = acc_ref[...]
    o = jnp.concatenate([acc[:C_VDIM] / acc[C_VDIM:C_VDIM + 1], jnp.zeros((LANES - C_VDIM, tq), F32)], axis=0)
    o_ref[...] = o.T.astype(o_ref.dtype)


def _latent(qc, kc, vt, batch, seq, bounded):
    t = batch * seq
    tq, tk = TQ_LAT, vt.shape[2]
    nq = seq // tq
    if bounded:
        body = functools.partial(_latent_bounded_kernel, tk=tk)
        scratch = [pltpu.VMEM((VT_ROWS, tq), F32)]
    else:
        body = functools.partial(_latent_kernel, tk=tk)
        scratch = [pltpu.VMEM((LAT_SLOTS, tk, tq), F32), pltpu.VMEM((LAT_SLOTS, 1, tq), F32),
                   pltpu.VMEM((1, tq), F32), pltpu.VMEM((VT_ROWS, tq), F32)]
    return pl.pallas_call(
        body,
        grid=(batch, C_HEADS, nq),
        in_specs=[pl.BlockSpec((tq, LANES), lambda b, h, i: (b * nq + i, h)),
                  pl.BlockSpec((seq, LANES), lambda b, h, i: (b, h)),
                  pl.BlockSpec((seq // tk, VT_ROWS, tk), lambda b, h, i: (b, h, 0))],
        out_specs=pl.BlockSpec((tq, LANES), lambda b, h, i: (b * nq + i, h)),
        out_shape=jax.ShapeDtypeStruct((t, C_SLOTS), BF16),
        scratch_shapes=scratch,
        compiler_params=_cparams(("parallel", "parallel", "parallel")),
        name="latent_bounded" if bounded else "latent",
    )(qc, kc, vt)


def _merge_mlp_kernel(x_ref, oa1_ref, oa2_ref, oa3_ref, ml1_ref, ml2_ref, ml3_ref, ob_ref, oc_ref,
                      expand_ref, gout_ref, wout_ref, gmlp_ref, wup_ref, wdown_ref, out_ref, oa_s, ml_s):
    tm = x_ref.shape[0]
    oas, mls = [oa1_ref[...]], [ml1_ref[...]]
    for idx, (o_ref, l_ref) in enumerate(((oa2_ref, ml2_ref), (oa3_ref, ml3_ref))):
        r = A_CONFIGS[idx + 1][1]
        for c in range(r):
            for s in range(A_W // LANES):
                col = c * A_W + s * LANES
                oa_s[idx, s, pl.ds(c, tm // r, stride=r), :] = o_ref[:, col:col + LANES]
            ml_s[idx, pl.ds(c, tm // r, stride=r), :] = l_ref[:, c * LANES:(c + 1) * LANES]
        oas.append(jnp.concatenate([oa_s[idx, s] for s in range(A_W // LANES)], axis=1))
        mls.append(ml_s[idx])
    lane = lax.broadcasted_iota(jnp.int32, mls[0].shape, 1)
    m_all = jnp.maximum(jnp.maximum(mls[0], mls[1]), mls[2])
    ws = [pltpu.roll(ml, LANES - ML_L_OFFSET, 1) * jnp.exp(ml - m_all) for ml in mls]
    wsum = ws[0] + ws[1] + ws[2]
    packed = jnp.zeros_like(wsum)
    for c, w in enumerate(ws):
        wn = jnp.where(lane < A_HEADS, w / wsum, 0.0)
        packed = packed + (wn if c == 0 else pltpu.roll(wn, ML_L_OFFSET * c, 1))
    hi = packed.astype(BF16)
    lo = (packed - hi.astype(F32)).astype(BF16)
    spread = (jnp.dot(hi, expand_ref[...], preferred_element_type=F32)
              + jnp.dot(lo, expand_ref[...], preferred_element_type=F32))
    oa = spread[:, :A_W] * oas[0] + spread[:, A_W:2 * A_W] * oas[1] + spread[:, 2 * A_W:] * oas[2]

    def group_norm(v, width):
        return v * lax.rsqrt(jnp.sum(v * v, axis=-1, keepdims=True) * (1.0 / width) + EPS)

    ob = ob_ref[...].astype(F32)
    oc = oc_ref[...].astype(F32)
    mixed = jnp.concatenate([group_norm(oa, A_W), group_norm(ob, B_QW), group_norm(oc, C_HEADS * C_VDIM)], axis=1)
    mixed = (mixed * gout_ref[...]).astype(BF16)
    x = x_ref[...] + jnp.dot(mixed, wout_ref[...], preferred_element_type=F32)

    h = (x * lax.rsqrt(jnp.mean(x * x, axis=-1, keepdims=True) + EPS) * gmlp_ref[...]).astype(BF16)
    acc = x
    for s in range(0, D_FF, FF_CHUNK):
        u = jnp.dot(h, wup_ref[:, s:s + FF_CHUNK], preferred_element_type=F32)
        u = jnp.square(jnp.maximum(u, 0.0)).astype(BF16)
        acc = acc + jnp.dot(u, wdown_ref[s:s + FF_CHUNK, :], preferred_element_type=F32)
    out_ref[...] = acc


def _merge_mlp(x2d, oas, mls, ob, oc, p):
    t = x2d.shape[0]
    tm = TM_MLP
    row = lambda w: pl.BlockSpec((tm, w), lambda i: (i, 0))
    strided = lambda w, r: pl.BlockSpec((tm // r, r * w), lambda i: (i, 0))
    single = lambda shape: pl.BlockSpec(shape, lambda i: (0,) * len(shape), pipeline_mode=pl.Buffered(1))
    return pl.pallas_call(
        _merge_mlp_kernel,
        grid=(t // tm,),
        in_specs=[row(D_MODEL)] + [strided(A_W, r) for _, r in A_CONFIGS]
        + [strided(LANES, r) for _, r in A_CONFIGS] + [row(B_QW), row(C_SLOTS),
                  single((LANES, 3 * A_W)), single((1, MIX_COLS)), single((MIX_COLS, D_MODEL)),
                  single((1, D_MODEL)), single((D_MODEL, D_FF)), single((D_FF, D_MODEL))],
        out_specs=row(D_MODEL),
        out_shape=jax.ShapeDtypeStruct((t, D_MODEL), F32),
        scratch_shapes=[pltpu.VMEM((2, A_W // LANES, tm, LANES), F32), pltpu.VMEM((2, tm, LANES), F32)],
        compiler_params=_cparams(("parallel",)),
        name="merge_mlp",
    )(x2d, *oas, *mls, ob, oc, p["expand"], p["gout"], p["w_out"], p["gmlp"], p["w_up"], p["w_down"])


def _block_diag_ones(group):
    idx = np.arange(2 * LANES) // group
    return jnp.asarray((idx[:, None] == idx[None, :]).astype(np.float32), dtype=BF16)


def _expand_matrix():
    e = np.zeros((LANES, len(A_CONFIGS) * A_W), np.float32)
    for c in range(len(A_CONFIGS)):
        for h in range(A_HEADS):
            e[ML_L_OFFSET * c + h, c * A_W + h * HEAD_DIM:c * A_W + (h + 1) * HEAD_DIM] = 1.0
    return jnp.asarray(e, dtype=BF16)


def _pad_heads(w, heads, used, lo=0):
    lead = w.shape[:-1]
    w = w.reshape(lead + (heads, used))
    pad = [(0, 0)] * len(lead) + [(0, 0), (lo, LANES - lo - used)]
    return jnp.pad(w, pad).reshape(lead + (heads * LANES,))


def _layer_params(i, norm_mix, w_in, qk_gain_a, qk_gain_b, q_lat_gain, kv_lat_gain, w_uq, w_ukv, qk_gain_c,
                  out_norm, w_out, norm_mlp, w_up, w_down):
    w = w_in[i]
    o = np.cumsum((A_W, A_W, A_W, B_QW, B_KVW, B_KVW, C_Q_RANK, C_KV_RANK)).tolist()
    qb = w[:, o[2]:o[3]].reshape(D_MODEL, B_HEADS, HEAD_DIM)[:, B_HEAD_ORDER, :].reshape(D_MODEL, B_QW)
    kr = _pad_heads(w[:, o[7]:], 1, C_ROPE, lo=C_NOPE)
    w_in_p = jnp.concatenate([w[:, :o[2]], qb, w[:, o[3]:o[7]], kr], axis=1).astype(BF16)

    scale = HEAD_DIM ** -0.5
    gqk = jnp.concatenate([jnp.tile(qk_gain_a[i, 0], A_HEADS) * scale, jnp.tile(qk_gain_a[i, 1], A_HEADS),
                           jnp.tile(qk_gain_b[i, 0], B_HEADS) * scale, jnp.tile(qk_gain_b[i, 1], B_KV_HEADS)])
    ukv = w_ukv[i].reshape(C_KV_RANK, C_HEADS, C_NOPE + C_VDIM)
    wuk = _pad_heads(ukv[:, :, :C_NOPE].reshape(C_KV_RANK, -1), C_HEADS, C_NOPE)
    wuvt = ukv[:, :, C_NOPE:].reshape(C_KV_RANK, C_HEADS * C_VDIM).T
    g = out_norm[i]
    gb = g[A_W:A_W + B_QW].reshape(B_HEADS, HEAD_DIM)[B_HEAD_ORDER, :].reshape(B_QW)
    gout = jnp.concatenate([g[:A_W], gb, _pad_heads(g[A_W + B_QW:], C_HEADS, C_VDIM)])
    wo = w_out[i]
    wob = wo[A_W:A_W + B_QW].reshape(B_HEADS, HEAD_DIM, D_MODEL)[B_HEAD_ORDER, :, :].reshape(B_QW, D_MODEL)
    woc = jnp.pad(wo[A_W + B_QW:].reshape(C_HEADS, C_VDIM, D_MODEL), ((0, 0), (0, LANES - C_VDIM), (0, 0)))
    w_out_p = jnp.concatenate([wo[:A_W], wob, woc.reshape(C_SLOTS, D_MODEL)], axis=0).astype(BF16)
    bound = (C_QK ** 0.5 * LOG2E * BOUND_MARGIN) * jnp.max(jnp.abs(qk_gain_c[i, 0])) * jnp.max(jnp.abs(qk_gain_c[i, 1]))
    return {
        "gmix": norm_mix[i][None, :],
        "w_in": w_in_p,
        "gqk": gqk[None, :],
        "ones64": _block_diag_ones(HEAD_DIM),
        "ones128": _block_diag_ones(LANES),
        "glq": q_lat_gain[i][None, :],
        "wuq": _pad_heads(w_uq[i], C_HEADS, C_QK).astype(BF16),
        "gqc": jnp.tile(_pad_heads(qk_gain_c[i, 0] * (C_QK ** -0.5 * LOG2E), 1, C_QK), C_HEADS)[None, :],
        "glkv": kv_lat_gain[i][None, :],
        "wuk": wuk.astype(BF16),
        "wuvt": wuvt.astype(BF16),
        "gkc": jnp.tile(_pad_heads(qk_gain_c[i, 1], 1, C_QK), C_HEADS)[None, :],
        "qshift": jnp.tile(_pad_heads(-bound[None], 1, 1, lo=C_QK), C_HEADS)[None, :],
        "kone": jnp.tile(_pad_heads(jnp.ones((1,), F32), 1, 1, lo=C_QK), C_HEADS)[None, :],
        "logit_bound": bound,
        "expand": _expand_matrix(),
        "gout": gout[None, :],
        "w_out": w_out_p,
        "gmlp": norm_mlp[i][None, :],
        "w_up": w_up[i].astype(BF16),
        "w_down": w_down[i].astype(BF16),
    }


def kernel(x, positions, rel_bias_table, norm_mix, w_in, qk_gain_a, qk_gain_b, sink_b, q_lat_gain, kv_lat_gain,
           w_uq, w_ukv, qk_gain_c, out_norm, w_out, norm_mlp, w_up, w_down):
    batch, seq, _ = x.shape
    depth = w_in.shape[0]
    x2d = x.reshape(batch * seq, D_MODEL)
    cos_t, sin_t = _rope_tables(positions)
    a_pairs = tuple((2 * p, 2 * p + 1) for p in range(A_HEADS // 2))
    bias_a = [_bias_tiles(rel_bias_table, window // (2 * r), r, tuple(range(A_HEADS))) for window, r in A_CONFIGS]
    bias_b = _bias_tiles(rel_bias_table, B_RADIUS, 1, tuple(A_HEADS + h for h in range(B_HEADS)))
    for i in range(depth):
        p = _layer_params(i, norm_mix, w_in, qk_gain_a, qk_gain_b, q_lat_gain, kv_lat_gain, w_uq, w_ukv,
                          qk_gain_c, out_norm, w_out, norm_mlp, w_up, w_down)
        qkv1, qkv4, qkv16, qb, kb, vb, qc, kc, vt = _inproj(x2d, cos_t, sin_t, p)
        oas, mls = [], []
        for (window, r), bias, qkv in zip(A_CONFIGS, bias_a, (qkv1, qkv4, qkv16)):
            o, ml = _banded((qkv, 3, 0), (qkv, 3, 1), (qkv, 3, 2), bias, batch=batch, seq=seq, dilation=r,
                            radius=window // (2 * r), widths=(A_W, A_W), head_ids=a_pairs, kv_slab=(0, 1, 2),
                            sink=None, emit_ml=True, out_dtype=F32)
            oas.append(o)
            mls.append(ml)
        (ob,) = _banded((qb, 1, 0), (kb, 1, 0), (vb, 1, 0), bias_b, batch=batch, seq=seq, dilation=1,
                        radius=B_RADIUS, widths=(B_QW, B_KVW), head_ids=((0, 2), (1, 3)), kv_slab=(0, 0),
                        sink=sink_b[i], emit_ml=False, out_dtype=BF16)
        oc = lax.cond(p["logit_bound"] <= MAX_LOGIT_BOUND,
                      functools.partial(_latent, batch=batch, seq=seq, bounded=True),
                      functools.partial(_latent, batch=batch, seq=seq, bounded=False), qc, kc, vt)
        x2d = _merge_mlp(x2d, oas, mls, ob, oc, p)
    return x2d.reshape(batch, seq, D_MODEL)
```

```python
import functools
import math

import numpy as np
import jax
import jax.numpy as jnp
from jax import lax
from jax.experimental import pallas as pl
from jax.experimental.pallas import tpu as pltpu

F32 = jnp.float32
BF16 = jnp.bfloat16

D_MODEL = 1024
HEAD_DIM = 64
A_HEADS = 6
A_CONFIGS = ((128, 1), (512, 4), (2048, 16))
B_HEADS = 4
B_KV_HEADS = 2
B_RADIUS = 128
C_HEADS = 6
C_NOPE = 64
C_ROPE = 32
C_VDIM = 64
C_QK = C_NOPE + C_ROPE
C_Q_RANK = 256
C_KV_RANK = 128
ROPE_THETA = 10000.0
N_BUCKETS = 32
MAX_DISTANCE = 1024
D_FF = 4 * D_MODEL
EPS = 1e-6
NEG = -1e30

A_W = A_HEADS * HEAD_DIM
B_QW = B_HEADS * HEAD_DIM
B_KVW = B_KV_HEADS * HEAD_DIM
LANES = 128
C_SLOTS = C_HEADS * LANES
IN_COLS = 3 * A_W + B_QW + 2 * B_KVW + C_Q_RANK + C_KV_RANK + LANES
MIX_COLS = A_W + B_QW + C_SLOTS
VT_ROWS = 80
LOG2E = math.log2(math.e)
ML_L_OFFSET = 8

B_HEAD_ORDER = (0, 2, 1, 3)

TM_IN = 512
TQ_BAND = 512
SUB = 128
TQ_LAT = 2048
BOUND_MARGIN = 1.02
MAX_LOGIT_BOUND = 50.0
LAT_BOUNDED_UNROLL = 16
LAT_SLOTS = 3
LAT_UNROLL = 6
TM_MLP = 512
FF_CHUNK = 1024
VMEM_LIMIT = 56 * 1024 * 1024


def _cparams(sem):
    return pltpu.CompilerParams(dimension_semantics=sem, vmem_limit_bytes=VMEM_LIMIT)


def _const_spec(shape):
    zeros = (0,) * len(shape)
    return pl.BlockSpec(shape, lambda *_: zeros)


def _rope_table_kernel(pos_ref, inv_ref, cos_ref, sin_ref):
    ang = pos_ref[...] * inv_ref[...]
    lane = lax.broadcasted_iota(jnp.int32, ang.shape, 1)
    c = jnp.cos(ang)
    s = jnp.sin(ang)
    first = (lane >= C_NOPE) & (lane < C_NOPE + C_ROPE // 2)
    second = (lane >= C_NOPE + C_ROPE // 2) & (lane < C_QK)
    cos_ref[...] = jnp.where(first | second, c, 1.0)
    sin_ref[...] = jnp.where(first, -s, jnp.where(second, s, 0.0))


def _rope_tables(positions):
    t = positions.size
    half = C_ROPE // 2
    inv = ROPE_THETA ** (-jnp.arange(half, dtype=F32) / half)
    inv_row = jnp.concatenate([jnp.zeros((C_NOPE,), F32), inv, inv, jnp.zeros((LANES - C_QK,), F32)])[None, :]
    pos = positions.astype(F32).reshape(t, 1)
    tm = 2048
    return pl.pallas_call(
        _rope_table_kernel,
        grid=(t // tm,),
        in_specs=[pl.BlockSpec((tm, 1), lambda i: (i, 0)), _const_spec((1, LANES))],
        out_specs=[pl.BlockSpec((tm, LANES), lambda i: (i, 0))] * 2,
        out_shape=[jax.ShapeDtypeStruct((t, LANES), F32)] * 2,
        compiler_params=_cparams(("parallel",)),
        name="rope_tables",
    )(pos, inv_row)


def _bucket_thresholds():
    half = N_BUCKETS // 2
    exact = half // 2
    n = np.arange(1, 2 * MAX_DISTANCE + 2, dtype=np.float64)
    far = exact + (np.log(n / exact) / math.log(MAX_DISTANCE / exact) * (half - exact)).astype(np.int64)
    far = np.minimum(far, half - 1)
    return tuple(int(n[np.argmax(far >= exact + k)]) for k in range(1, half - exact))


def _bias_kernel(table_ref, out_ref, *, radius, dilation, head_cols):
    hsel = pl.program_id(0)
    width = SUB + 2 * radius
    row = lax.broadcasted_iota(jnp.int32, (SUB, width), 0)
    col = lax.broadcasted_iota(jnp.int32, (SUB, width), 1)
    rel = col - radius - row
    n = jnp.abs(rel) * dilation
    half = N_BUCKETS // 2
    exact = half // 2
    far = jnp.full(n.shape, exact, jnp.int32)
    for thr in _bucket_thresholds():
        far = far + (n >= thr).astype(jnp.int32)
    bucket = jnp.where(rel > 0, half, 0) + jnp.where(n < exact, n, far)
    for idx, hc in enumerate(head_cols):
        @pl.when(hsel == idx)
        def _(hc=hc):
            val = jnp.zeros(n.shape, F32)
            for b in range(N_BUCKETS):
                val = jnp.where(bucket == b, table_ref[b, hc], val)
            out_ref[...] = jnp.where(jnp.abs(rel) <= radius, val, NEG)


def _bias_tiles(table, radius, dilation, head_cols):
    width = SUB + 2 * radius
    return pl.pallas_call(
        functools.partial(_bias_kernel, radius=radius, dilation=dilation, head_cols=head_cols),
        grid=(len(head_cols),),
        in_specs=[pl.BlockSpec(memory_space=pltpu.SMEM)],
        out_specs=pl.BlockSpec((None, SUB, width), lambda h: (h, 0, 0)),
        out_shape=jax.ShapeDtypeStruct((len(head_cols), SUB, width), F32),
        compiler_params=_cparams(("arbitrary",)),
        name="bias_tiles",
    )(table)


def _group_mean_sq(y, ones_ref, group):
    sq = (y * y).astype(BF16)
    width = y.shape[1]
    parts = []
    for s in range(0, width, 2 * LANES):
        w = min(2 * LANES, width - s)
        parts.append(jnp.dot(sq[:, s:s + w], ones_ref[:w, :w], preferred_element_type=F32))
    out = parts[0] if len(parts) == 1 else jnp.concatenate(parts, axis=1)
    return out * (1.0 / group)


def _slab_roll(y, shift):
    parts = [pltpu.roll(y[:, s:s + LANES], shift, 1) for s in range(0, y.shape[1], LANES)]
    return parts[0] if len(parts) == 1 else jnp.concatenate(parts, axis=1)


def _rope(y, cos_t, sin_t, lane):
    swapped = jnp.where(lane < C_NOPE + C_ROPE // 2, _slab_roll(y, LANES - C_ROPE // 2), _slab_roll(y, C_ROPE // 2))
    return y * cos_t + swapped * sin_t


def _inproj_kernel(x_ref, cos_ref, sin_ref, gmix_ref, w_ref, gqk_ref, ones64_ref, ones128_ref,
                   glq_ref, wuq_ref, gqc_ref, glkv_ref, wuk_ref, wuvt_ref, gkc_ref, qshift_ref, kone_ref,
                   qkv1_ref, qkv4_ref, qkv16_ref, qb_ref, kb_ref, vb_ref, qc_ref, kc_ref, vt_ref, stage_ref):
    x = x_ref[...]
    h = x * lax.rsqrt(jnp.mean(x * x, axis=-1, keepdims=True) + EPS) * gmix_ref[...]
    y = jnp.dot(h.astype(BF16), w_ref[...], preferred_element_type=F32)

    o_qb = 3 * A_W
    o_kb = o_qb + B_QW
    o_vb = o_kb + B_KVW
    o_cq = o_vb + B_KVW
    o_ckv = o_cq + C_Q_RANK
    o_kr = o_ckv + C_KV_RANK
    yn = jnp.concatenate([y[:, :2 * A_W], y[:, o_qb:o_vb]], axis=1)
    yn = yn * lax.rsqrt(_group_mean_sq(yn, ones64_ref, HEAD_DIM) + EPS) * gqk_ref[...]
    qb_ref[...] = yn[:, 2 * A_W:2 * A_W + B_QW].astype(BF16)
    kb_ref[...] = yn[:, 2 * A_W + B_QW:].astype(BF16)
    n_slabs, tm = stage_ref.shape[0], stage_ref.shape[1]
    for s in range(n_slabs):
        src = yn if s * LANES < 2 * A_W else y
        stage_ref[s] = src[:, s * LANES:(s + 1) * LANES]
    for out_ref, (_, r) in zip((qkv1_ref, qkv4_ref, qkv16_ref), A_CONFIGS):
        for c in range(r):
            for s in range(n_slabs):
                col = c * 3 * A_W + s * LANES
                out_ref[:, col:col + LANES] = stage_ref[s, pl.ds(c, tm // r, stride=r), :].astype(BF16)
    vb_ref[...] = y[:, o_vb:o_cq].astype(BF16)

    cos_t = jnp.concatenate([cos_ref[...]] * C_HEADS, axis=1)
    sin_t = jnp.concatenate([sin_ref[...]] * C_HEADS, axis=1)
    lane = lax.broadcasted_iota(jnp.int32, cos_t.shape, 1) % LANES
    cq = y[:, o_cq:o_ckv]
    cq = cq * lax.rsqrt(jnp.mean(cq * cq, axis=-1, keepdims=True) + EPS) * glq_ref[...]
    qc = jnp.dot(cq.astype(BF16), wuq_ref[...], preferred_element_type=F32)
    qc = qc * lax.rsqrt(_group_mean_sq(qc, ones128_ref, C_QK) + EPS) * gqc_ref[...]
    qc_ref[...] = (_rope(qc, cos_t, sin_t, lane) + qshift_ref[...]).astype(BF16)

    ckv = y[:, o_ckv:o_kr]
    ckv = (ckv * lax.rsqrt(jnp.mean(ckv * ckv, axis=-1, keepdims=True) + EPS) * glkv_ref[...]).astype(BF16)
    kr = y[:, o_kr:]
    kc = jnp.dot(ckv, wuk_ref[...], preferred_element_type=F32) + jnp.concatenate([kr] * C_HEADS, axis=1)
    kc = kc * lax.rsqrt(_group_mean_sq(kc, ones128_ref, C_QK) + EPS) * gkc_ref[...]
    kc_ref[...] = (_rope(kc, cos_t, sin_t, lane) + kone_ref[...]).astype(BF16)
    vt = lax.dot_general(wuvt_ref[...], ckv, (((1,), (1,)), ((), ())), preferred_element_type=F32)
    row = lax.broadcasted_iota(jnp.int32, (VT_ROWS - C_VDIM, vt.shape[1]), 0)
    ones_rows = jnp.where(row == 0, 1.0, 0.0).astype(BF16)
    for hd in range(C_HEADS):
        vt_ref[0, hd * VT_ROWS:hd * VT_ROWS + C_VDIM, :] = vt[hd * C_VDIM:(hd + 1) * C_VDIM].astype(BF16)
        vt_ref[0, hd * VT_ROWS + C_VDIM:(hd + 1) * VT_ROWS, :] = ones_rows


def _inproj(x2d, cos_t, sin_t, p):
    t = x2d.shape[0]
    tm = TM_IN
    row = lambda w: pl.BlockSpec((tm, w), lambda i: (i, 0))
    outs = [(B_QW, BF16), (B_KVW, BF16), (B_KVW, BF16)] + [(C_SLOTS, BF16)] * 2
    a_specs = [pl.BlockSpec((tm // r, r * 3 * A_W), lambda i: (i, 0)) for _, r in A_CONFIGS]
    a_shapes = [jax.ShapeDtypeStruct((t // r, r * 3 * A_W), BF16) for _, r in A_CONFIGS]
    return pl.pallas_call(
        _inproj_kernel,
        grid=(t // tm,),
        in_specs=[row(D_MODEL), row(LANES), row(LANES),
                  _const_spec((1, D_MODEL)), _const_spec((D_MODEL, IN_COLS)),
                  _const_spec((1, 2 * A_W + B_QW + B_KVW)),
                  _const_spec((2 * LANES, 2 * LANES)), _const_spec((2 * LANES, 2 * LANES)),
                  _const_spec((1, C_Q_RANK)), _const_spec((C_Q_RANK, C_SLOTS)), _const_spec((1, C_SLOTS)),
                  _const_spec((1, C_KV_RANK)), _const_spec((C_KV_RANK, C_SLOTS)),
                  _const_spec((C_HEADS * C_VDIM, C_KV_RANK)), _const_spec((1, C_SLOTS)),
                  _const_spec((1, C_SLOTS)), _const_spec((1, C_SLOTS))],
        out_specs=a_specs + [row(w) for w, _ in outs]
        + [pl.BlockSpec((1, C_HEADS * VT_ROWS, tm), lambda i: (i, 0, 0))],
        out_shape=a_shapes + [jax.ShapeDtypeStruct((t, w), d) for w, d in outs]
        + [jax.ShapeDtypeStruct((t // tm, C_HEADS * VT_ROWS, tm), BF16)],
        scratch_shapes=[pltpu.VMEM((3 * A_W // LANES, tm, LANES), F32)],
        compiler_params=_cparams(("parallel",)),
        name="inproj",
    )(x2d, cos_t, sin_t, p["gmix"], p["w_in"], p["gqk"], p["ones64"], p["ones128"],
      p["glq"], p["wuq"], p["gqc"], p["glkv"], p["wuk"], p["wuvt"], p["gkc"], p["qshift"], p["kone"])


def _window(lo_ref, main_ref, hi_ref, a, radius, tq):
    start, end = a - radius, a + SUB + radius
    parts = []
    if start < 0:
        parts.append(lo_ref[radius + start:radius, :])
        start = 0
    parts.append(main_ref[start:min(end, tq), :])
    if end > tq:
        parts.append(hi_ref[0:end - tq, :])
    return parts[0] if len(parts) == 1 else jnp.concatenate(parts, axis=0)


def _banded_kernel(*refs, tq, radius, head_ids, kv_slab, has_sink, emit_ml, n_tiles):
    if has_sink:
        sink_ref, refs = refs[0], refs[1:]
    q_ref, klo_ref, k_ref, khi_ref, vlo_ref, v_ref, vhi_ref, bias_ref, o_ref = refs[:9]
    ml_ref = refs[9] if emit_ml else None
    s_scr, m_scr, p_scr = refs[-3:]
    tile = pl.program_id(2)
    width = SUB + 2 * radius
    lane = lax.broadcasted_iota(jnp.int32, (SUB, LANES), 1)
    col = lax.broadcasted_iota(jnp.int32, (SUB, width), 1)
    low = lane < HEAD_DIM
    low_w = lax.broadcasted_iota(jnp.int32, (width, LANES), 1) < HEAD_DIM
    subs = list(range(0, tq, SUB))
    slab = lambda w, pair: w[:, kv_slab[pair] * LANES:(kv_slab[pair] + 1) * LANES]
    tiles = [(a, pair, e, hd) for a in subs for pair, heads in enumerate(head_ids) for e, hd in enumerate(heads)]

    kws = {a: _window(klo_ref, k_ref, khi_ref, a, radius, tq) for a in subs}
    for g, (a, pair, e, hd) in enumerate(tiles):
        qs = q_ref[a:a + SUB, pair * LANES:(pair + 1) * LANES]
        qm = jnp.where(low == (e == 0), qs, jnp.zeros_like(qs))
        ks = slab(kws[a], pair)
        s = lax.dot_general(qm, ks, (((1,), (1,)), ((), ())), preferred_element_type=F32)
        s = s + bias_ref[hd]
        if a == 0:
            s = jnp.where(col < jnp.where(tile == 0, radius, 0), NEG, s)
        if a == tq - SUB:
            s = jnp.where(col >= jnp.where(tile == n_tiles - 1, SUB + radius, width), NEG, s)
        m = jnp.max(s, axis=-1, keepdims=True)
        if has_sink:
            m = jnp.maximum(m, sink_ref[hd])
        s_scr[g] = s
        m_scr[g] = jnp.broadcast_to(m, (SUB, LANES))

    for g in range(len(tiles)):
        m_wide = jnp.concatenate([m_scr[g]] * (width // LANES), axis=1)
        p_scr[g] = jnp.exp(s_scr[g] - m_wide).astype(BF16)

    vws = {a: _window(vlo_ref, v_ref, vhi_ref, a, radius, tq) for a in subs}
    ml = None
    for g in range(0, len(tiles), 2):
        a, pair = tiles[g][0], tiles[g][1]
        hd0, hd1 = tiles[g][3], tiles[g + 1][3]
        vs = slab(vws[a], pair)
        ones = jnp.ones_like(vs)
        o0 = jnp.dot(p_scr[g], jnp.where(low_w, vs, ones), preferred_element_type=F32)
        o1 = jnp.dot(p_scr[g + 1], jnp.where(low_w, ones, vs), preferred_element_type=F32)
        m0, m1 = m_scr[g], m_scr[g + 1]
        l_swapped = jnp.where(low, o1, o0)
        l_pair = pltpu.roll(l_swapped, HEAD_DIM, 1)
        if has_sink:
            extra = jnp.where(low, jnp.exp(sink_ref[hd0] - m0), jnp.exp(sink_ref[hd1] - m1))
            l_pair = l_pair + extra
            l_swapped = l_swapped + pltpu.roll(extra, HEAD_DIM, 1)
        o_ref[a:a + SUB, pair * LANES:(pair + 1) * LANES] = (jnp.where(low, o0, o1) / l_pair).astype(o_ref.dtype)
        if emit_ml:
            ml = jnp.zeros((SUB, LANES), F32) if pair == 0 else ml
            ml = jnp.where(lane == hd0, m0, ml)
            ml = jnp.where(lane == hd1, m1, ml)
            ml = jnp.where(lane == ML_L_OFFSET + hd0, l_pair, ml)
            ml = jnp.where(lane == ML_L_OFFSET + hd1, l_swapped, ml)
            if pair == len(head_ids) - 1:
                ml_ref[a:a + SUB, :] = ml


def _banded(q, k, v, bias, *, batch, seq, dilation, radius, widths, head_ids, kv_slab, sink, emit_ml, out_dtype):
    n = seq // dilation
    tq = min(TQ_BAND, n)
    n_tiles = n // tq
    qw, kw = widths
    n_sub_tiles = (tq // SUB) * sum(len(heads) for heads in head_ids)
    view = lambda t: t[0].reshape(batch, n, t[0].shape[1])
    per_tile = tq // radius

    def main(w, t=(None, 1, 0)):
        return pl.BlockSpec((None, tq, w), lambda b, c, i: (b, i, t[1] * c + t[2]))

    def lo(t):
        return pl.BlockSpec((None, radius, kw),
                            lambda b, c, i: (b, jnp.maximum(i * per_tile - 1, 0), t[1] * c + t[2]))

    def hi(t):
        return pl.BlockSpec((None, radius, kw),
                            lambda b, c, i: (b, jnp.minimum((i + 1) * per_tile, n // radius - 1), t[1] * c + t[2]))

    in_specs = [main(qw, q), lo(k), main(kw, k), hi(k), lo(v), main(kw, v), hi(v), _const_spec(bias.shape)]
    args = [view(q), view(k), view(k), view(k), view(v), view(v), view(v), bias]
    if sink is not None:
        in_specs = [pl.BlockSpec(memory_space=pltpu.SMEM)] + in_specs
        args = [sink] + args
    out_specs = [main(qw)]
    out_shape = [jax.ShapeDtypeStruct((batch, n, dilation * qw), out_dtype)]
    if emit_ml:
        out_specs.append(main(LANES))
        out_shape.append(jax.ShapeDtypeStruct((batch, n, dilation * LANES), F32))
    outs = pl.pallas_call(
        functools.partial(_banded_kernel, tq=tq, radius=radius, head_ids=head_ids, kv_slab=kv_slab,
                          has_sink=sink is not None, emit_ml=emit_ml, n_tiles=n_tiles),
        grid=(batch, dilation, n_tiles),
        in_specs=in_specs,
        out_specs=out_specs,
        out_shape=out_shape,
        scratch_shapes=[pltpu.VMEM((n_sub_tiles, SUB, SUB + 2 * radius), F32),
                        pltpu.VMEM((n_sub_tiles, SUB, LANES), F32),
                        pltpu.VMEM((n_sub_tiles, SUB, SUB + 2 * radius), BF16)],
        compiler_params=_cparams(("parallel", "parallel", "parallel")),
        name="banded_r%d_d%d" % (radius, dilation),
    )(*args)
    return [o.reshape(batch * n, -1) for o in outs]


def _latent_kernel(q_ref, k_ref, vt_ref, o_ref, s_ref, cmax_ref, m_ref, acc_ref, *, tk):
    tq = q_ref.shape[0]
    nk = k_ref.shape[0] // tk
    ahead = LAT_SLOTS - 1
    qt = q_ref[...].astype(F32).T.astype(BF16)
    m_ref[...] = jnp.full(m_ref.shape, -jnp.inf, F32)
    acc_ref[...] = jnp.zeros(acc_ref.shape, F32)

    def scores(slot, j):
        start = pl.multiple_of(j * tk, tk)
        s = jnp.dot(k_ref[pl.ds(start, tk), :], qt, preferred_element_type=F32)
        s_ref[slot] = s
        cmax_ref[slot] = jnp.max(s, axis=0, keepdims=True)

    def consume(slot, j):
        m_old = m_ref[...]
        m_new = jnp.maximum(m_old, cmax_ref[slot])
        pr = jnp.exp2(s_ref[slot] - m_new).astype(BF16)
        acc_ref[...] = jnp.exp2(m_old - m_new) * acc_ref[...] + jnp.dot(
            vt_ref[j], pr, preferred_element_type=F32)
        m_ref[...] = m_new

    def step(j, u, with_scores):
        if with_scores:
            scores((u + ahead) % LAT_SLOTS, j + ahead)
        consume(u % LAT_SLOTS, j)

    for j in range(ahead):
        scores(j, j)
    trips = (nk - ahead) // LAT_UNROLL

    def body(jj, carry):
        for u in range(LAT_UNROLL):
            step(LAT_UNROLL * jj + u, u, True)
        return carry

    lax.fori_loop(0, trips, body, 0)
    for j in range(trips * LAT_UNROLL, nk):
        step(j, j, j + ahead < nk)
    acc = acc_ref[...]
    o = jnp.concatenate([acc[:C_VDIM] / acc[C_VDIM:C_VDIM + 1], jnp.zeros((LANES - C_VDIM, tq), F32)], axis=0)
    o_ref[...] = o.T.astype(o_ref.dtype)


def _latent_bounded_kernel(q_ref, k_ref, vt_ref, o_ref, acc_ref, *, tk):
    tq = q_ref.shape[0]
    nk = k_ref.shape[0] // tk
    qt = q_ref[...].astype(F32).T.astype(BF16)
    acc_ref[...] = jnp.zeros(acc_ref.shape, F32)

    def body(jj, carry):
        for u in range(LAT_BOUNDED_UNROLL):
            j = LAT_BOUNDED_UNROLL * jj + u
            start = pl.multiple_of(j * tk, tk)
            s = jnp.dot(k_ref[pl.ds(start, tk), :], qt, preferred_element_type=F32)
            acc_ref[...] += jnp.dot(vt_ref[j], jnp.exp2(s).astype(BF16), preferred_element_type=F32)
        return carry

    lax.fori_loop(0, nk // LAT_BOUNDED_UNROLL, body, 0)
    acc = acc_ref[...]
    o = jnp.concatenate([acc[:C_VDIM] / acc[C_VDIM:C_VDIM + 1], jnp.zeros((LANES - C_VDIM, tq), F32)], axis=0)
    o_ref[...] = o.T.astype(o_ref.dtype)


def _latent(qc, kc, vt, batch, seq, bounded):
    t = batch * seq
    tq, tk = TQ_LAT, vt.shape[2]
    nq = seq // tq
    if bounded:
        body = functools.partial(_latent_bounded_kernel, tk=tk)
        scratch = [pltpu.VMEM((VT_ROWS, tq), F32)]
    else:
        body = functools.partial(_latent_kernel, tk=tk)
        scratch = [pltpu.VMEM((LAT_SLOTS, tk, tq), F32), pltpu.VMEM((LAT_SLOTS, 1, tq), F32),
                   pltpu.VMEM((1, tq), F32), pltpu.VMEM((VT_ROWS, tq), F32)]
    return pl.pallas_call(
        body,
        grid=(batch, C_HEADS, nq),
        in_specs=[pl.BlockSpec((tq, LANES), lambda b, h, i: (b * nq + i, h)),
                  pl.BlockSpec((seq, LANES), lambda b, h, i: (b, h)),
                  pl.BlockSpec((seq // tk, VT_ROWS, tk), lambda b, h, i: (b, h, 0))],
        out_specs=pl.BlockSpec((tq, LANES), lambda b, h, i: (b * nq + i, h)),
        out_shape=jax.ShapeDtypeStruct((t, C_SLOTS), BF16),
        scratch_shapes=scratch,
        compiler_params=_cparams(("parallel", "parallel", "parallel")),
        name="latent_bounded" if bounded else "latent",
    )(qc, kc, vt)


def _merge_mlp_kernel(x_ref, oa1_ref, oa2_ref, oa3_ref, ml1_ref, ml2_ref, ml3_ref, ob_ref, oc_ref,
                      expand_ref, gout_ref, wout_ref, gmlp_ref, wup_ref, wdown_ref, out_ref, oa_s, ml_s):
    tm = x_ref.shape[0]
    oas, mls = [oa1_ref[...]], [ml1_ref[...]]
    for idx, (o_ref, l_ref) in enumerate(((oa2_ref, ml2_ref), (oa3_ref, ml3_ref))):
        r = A_CONFIGS[idx + 1][1]
        for c in range(r):
            for s in range(A_W // LANES):
                col = c * A_W + s * LANES
                oa_s[idx, s, pl.ds(c, tm // r, stride=r), :] = o_ref[:, col:col + LANES]
            ml_s[idx, pl.ds(c, tm // r, stride=r), :] = l_ref[:, c * LANES:(c + 1) * LANES]
        oas.append(jnp.concatenate([oa_s[idx, s] for s in range(A_W // LANES)], axis=1))
        mls.append(ml_s[idx])
    lane = lax.broadcasted_iota(jnp.int32, mls[0].shape, 1)
    m_all = jnp.maximum(jnp.maximum(mls[0], mls[1]), mls[2])
    ws = [pltpu.roll(ml, LANES - ML_L_OFFSET, 1) * jnp.exp(ml - m_all) for ml in mls]
    wsum = ws[0] + ws[1] + ws[2]
    packed = jnp.zeros_like(wsum)
    for c, w in enumerate(ws):
        wn = jnp.where(lane < A_HEADS, w / wsum, 0.0)
        packed = packed + (wn if c == 0 else pltpu.roll(wn, ML_L_OFFSET * c, 1))
    hi = packed.astype(BF16)
    lo = (packed - hi.astype(F32)).astype(BF16)
    spread = (jnp.dot(hi, expand_ref[...], preferred_element_type=F32)
              + jnp.dot(lo, expand_ref[...], preferred_element_type=F32))
    oa = spread[:, :A_W] * oas[0] + spread[:, A_W:2 * A_W] * oas[1] + spread[:, 2 * A_W:] * oas[2]

    def group_norm(v, width):
        return v * lax.rsqrt(jnp.sum(v * v, axis=-1, keepdims=True) * (1.0 / width) + EPS)

    ob = ob_ref[...].astype(F32)
    oc = oc_ref[...].astype(F32)
    mixed = jnp.concatenate([group_norm(oa, A_W), group_norm(ob, B_QW), group_norm(oc, C_HEADS * C_VDIM)], axis=1)
    mixed = (mixed * gout_ref[...]).astype(BF16)
    x = x_ref[...] + jnp.dot(mixed, wout_ref[...], preferred_element_type=F32)

    h = (x * lax.rsqrt(jnp.mean(x * x, axis=-1, keepdims=True) + EPS) * gmlp_ref[...]).astype(BF16)
    acc = x
    for s in range(0, D_FF, FF_CHUNK):
        u = jnp.dot(h, wup_ref[:, s:s + FF_CHUNK], preferred_element_type=F32)
        u = jnp.square(jnp.maximum(u, 0.0)).astype(BF16)
        acc = acc + jnp.dot(u, wdown_ref[s:s + FF_CHUNK, :], preferred_element_type=F32)
    out_ref[...] = acc


def _merge_mlp(x2d, oas, mls, ob, oc, p):
    t = x2d.shape[0]
    tm = TM_MLP
    row = lambda w: pl.BlockSpec((tm, w), lambda i: (i, 0))
    strided = lambda w, r: pl.BlockSpec((tm // r, r * w), lambda i: (i, 0))
    single = lambda shape: pl.BlockSpec(shape, lambda i: (0,) * len(shape), pipeline_mode=pl.Buffered(1))
    return pl.pallas_call(
        _merge_mlp_kernel,
        grid=(t // tm,),
        in_specs=[row(D_MODEL)] + [strided(A_W, r) for _, r in A_CONFIGS]
        + [strided(LANES, r) for _, r in A_CONFIGS] + [row(B_QW), row(C_SLOTS),
                  single((LANES, 3 * A_W)), single((1, MIX_COLS)), single((MIX_COLS, D_MODEL)),
                  single((1, D_MODEL)), single((D_MODEL, D_FF)), single((D_FF, D_MODEL))],
        out_specs=row(D_MODEL),
        out_shape=jax.ShapeDtypeStruct((t, D_MODEL), F32),
        scratch_shapes=[pltpu.VMEM((2, A_W // LANES, tm, LANES), F32), pltpu.VMEM((2, tm, LANES), F32)],
        compiler_params=_cparams(("parallel",)),
        name="merge_mlp",
    )(x2d, *oas, *mls, ob, oc, p["expand"], p["gout"], p["w_out"], p["gmlp"], p["w_up"], p["w_down"])


def _block_diag_ones(group):
    idx = np.arange(2 * LANES) // group
    return jnp.asarray((idx[:, None] == idx[None, :]).astype(np.float32), dtype=BF16)


def _expand_matrix():
    e = np.zeros((LANES, len(A_CONFIGS) * A_W), np.float32)
    for c in range(len(A_CONFIGS)):
        for h in range(A_HEADS):
            e[ML_L_OFFSET * c + h, c * A_W + h * HEAD_DIM:c * A_W + (h + 1) * HEAD_DIM] = 1.0
    return jnp.asarray(e, dtype=BF16)


def _pad_heads(w, heads, used, lo=0):
    lead = w.shape[:-1]
    w = w.reshape(lead + (heads, used))
    pad = [(0, 0)] * len(lead) + [(0, 0), (lo, LANES - lo - used)]
    return jnp.pad(w, pad).reshape(lead + (heads * LANES,))


def _layer_params(i, norm_mix, w_in, qk_gain_a, qk_gain_b, q_lat_gain, kv_lat_gain, w_uq, w_ukv, qk_gain_c,
                  out_norm, w_out, norm_mlp, w_up, w_down):
    w = w_in[i]
    o = np.cumsum((A_W, A_W, A_W, B_QW, B_KVW, B_KVW, C_Q_RANK, C_KV_RANK)).tolist()
    qb = w[:, o[2]:o[3]].reshape(D_MODEL, B_HEADS, HEAD_DIM)[:, B_HEAD_ORDER, :].reshape(D_MODEL, B_QW)
    kr = _pad_heads(w[:, o[7]:], 1, C_ROPE, lo=C_NOPE)
    w_in_p = jnp.concatenate([w[:, :o[2]], qb, w[:, o[3]:o[7]], kr], axis=1).astype(BF16)

    scale = HEAD_DIM ** -0.5
    gqk = jnp.concatenate([jnp.tile(qk_gain_a[i, 0], A_HEADS) * scale, jnp.tile(qk_gain_a[i, 1], A_HEADS),
                           jnp.tile(qk_gain_b[i, 0], B_HEADS) * scale, jnp.tile(qk_gain_b[i, 1], B_KV_HEADS)])
    ukv = w_ukv[i].reshape(C_KV_RANK, C_HEADS, C_NOPE + C_VDIM)
    wuk = _pad_heads(ukv[:, :, :C_NOPE].reshape(C_KV_RANK, -1), C_HEADS, C_NOPE)
    wuvt = ukv[:, :, C_NOPE:].reshape(C_KV_RANK, C_HEADS * C_VDIM).T
    g = out_norm[i]
    gb = g[A_W:A_W + B_QW].reshape(B_HEADS, HEAD_DIM)[B_HEAD_ORDER, :].reshape(B_QW)
    gout = jnp.concatenate([g[:A_W], gb, _pad_heads(g[A_W + B_QW:], C_HEADS, C_VDIM)])
    wo = w_out[i]
    wob = wo[A_W:A_W + B_QW].reshape(B_HEADS, HEAD_DIM, D_MODEL)[B_HEAD_ORDER, :, :].reshape(B_QW, D_MODEL)
    woc = jnp.pad(wo[A_W + B_QW:].reshape(C_HEADS, C_VDIM, D_MODEL), ((0, 0), (0, LANES - C_VDIM), (0, 0)))
    w_out_p = jnp.concatenate([wo[:A_W], wob, woc.reshape(C_SLOTS, D_MODEL)], axis=0).astype(BF16)
    bound = (C_QK ** 0.5 * LOG2E * BOUND_MARGIN) * jnp.max(jnp.abs(qk_gain_c[i, 0])) * jnp.max(jnp.abs(qk_gain_c[i, 1]))
    return {
        "gmix": norm_mix[i][None, :],
        "w_in": w_in_p,
        "gqk": gqk[None, :],
        "ones64": _block_diag_ones(HEAD_DIM),
        "ones128": _block_diag_ones(LANES),
        "glq": q_lat_gain[i][None, :],
        "wuq": _pad_heads(w_uq[i], C_HEADS, C_QK).astype(BF16),
        "gqc": jnp.tile(_pad_heads(qk_gain_c[i, 0] * (C_QK ** -0.5 * LOG2E), 1, C_QK), C_HEADS)[None, :],
        "glkv": kv_lat_gain[i][None, :],
        "wuk": wuk.astype(BF16),
        "wuvt": wuvt.astype(BF16),
        "gkc": jnp.tile(_pad_heads(qk_gain_c[i, 1], 1, C_QK), C_HEADS)[None, :],
        "qshift": jnp.tile(_pad_heads(-bound[None], 1, 1, lo=C_QK), C_HEADS)[None, :],
        "kone": jnp.tile(_pad_heads(jnp.ones((1,), F32), 1, 1, lo=C_QK), C_HEADS)[None, :],
        "logit_bound": bound,
        "expand": _expand_matrix(),
        "gout": gout[None, :],
        "w_out": w_out_p,
        "gmlp": norm_mlp[i][None, :],
        "w_up": w_up[i].astype(BF16),
        "w_down": w_down[i].astype(BF16),
    }


def kernel(x, positions, rel_bias_table, norm_mix, w_in, qk_gain_a, qk_gain_b, sink_b, q_lat_gain, kv_lat_gain,
           w_uq, w_ukv, qk_gain_c, out_norm, w_out, norm_mlp, w_up, w_down):
    batch, seq, _ = x.shape
    depth = w_in.shape[0]
    x2d = x.reshape(batch * seq, D_MODEL)
    cos_t, sin_t = _rope_tables(positions)
    a_pairs = tuple((2 * p, 2 * p + 1) for p in range(A_HEADS // 2))
    bias_a = [_bias_tiles(rel_bias_table, window // (2 * r), r, tuple(range(A_HEADS))) for window, r in A_CONFIGS]
    bias_b = _bias_tiles(rel_bias_table, B_RADIUS, 1, tuple(A_HEADS + h for h in range(B_HEADS)))
    for i in range(depth):
        p = _layer_params(i, norm_mix, w_in, qk_gain_a, qk_gain_b, q_lat_gain, kv_lat_gain, w_uq, w_ukv,
                          qk_gain_c, out_norm, w_out, norm_mlp, w_up, w_down)
        qkv1, qkv4, qkv16, qb, kb, vb, qc, kc, vt = _inproj(x2d, cos_t, sin_t, p)
        oas, mls = [], []
        for (window, r), bias, qkv in zip(A_CONFIGS, bias_a, (qkv1, qkv4, qkv16)):
            o, ml = _banded((qkv, 3, 0), (qkv, 3, 1), (qkv, 3, 2), bias, batch=batch, seq=seq, dilation=r,
                            radius=window // (2 * r), widths=(A_W, A_W), head_ids=a_pairs, kv_slab=(0, 1, 2),
                            sink=None, emit_ml=True, out_dtype=F32)
            oas.append(o)
            mls.append(ml)
        (ob,) = _banded((qb, 1, 0), (kb, 1, 0), (vb, 1, 0), bias_b, batch=batch, seq=seq, dilation=1,
                        radius=B_RADIUS, widths=(B_QW, B_KVW), head_ids=((0, 2), (1, 3)), kv_slab=(0, 0),
                        sink=sink_b[i], emit_ml=False, out_dtype=BF16)
        oc = lax.cond(p["logit_bound"] <= MAX_LOGIT_BOUND,
                      functools.partial(_latent, batch=batch, seq=seq, bounded=True),
                      functools.partial(_latent, batch=batch, seq=seq, bounded=False), qc, kc, vt)
        x2d = _merge_mlp(x2d, oas, mls, ob, oc, p)
    return x2d.reshape(batch, seq, D_MODEL)
```

```python
import functools
import math

import numpy as np
import jax
import jax.numpy as jnp
from jax import lax
from jax.experimental import pallas as pl
from jax.experimental.pallas import tpu as pltpu

F32 = jnp.float32
BF16 = jnp.bfloat16

D_MODEL = 1024
HEAD_DIM = 64
A_HEADS = 6
A_CONFIGS = ((128, 1), (512, 4), (2048, 16))
B_HEADS = 4
B_KV_HEADS = 2
B_RADIUS = 128
C_HEADS = 6
C_NOPE = 64
C_ROPE = 32
C_VDIM = 64
C_QK = C_NOPE + C_ROPE
C_Q_RANK = 256
C_KV_RANK = 128
ROPE_THETA = 10000.0
N_BUCKETS = 32
MAX_DISTANCE = 1024
D_FF = 4 * D_MODEL
EPS = 1e-6
NEG = -1e30

A_W = A_HEADS * HEAD_DIM
B_QW = B_HEADS * HEAD_DIM
B_KVW = B_KV_HEADS * HEAD_DIM
LANES = 128
C_SLOTS = C_HEADS * LANES
IN_COLS = 3 * A_W + B_QW + 2 * B_KVW + C_Q_RANK + C_KV_RANK + LANES
MIX_COLS = A_W + B_QW + C_SLOTS
VT_ROWS = 80
LOG2E = math.log2(math.e)
ML_L_OFFSET = 8

B_HEAD_ORDER = (0, 2, 1, 3)

TM_IN = 512
TQ_BAND = 512
SUB = 128
TQ_LAT = 2048
BOUND_MARGIN = 1.02
MAX_LOGIT_BOUND = 50.0
LAT_BOUNDED_UNROLL = 16
LAT_SLOTS = 3
LAT_UNROLL = 6
TM_MLP = 512
FF_CHUNK = 1024
VMEM_LIMIT = 56 * 1024 * 1024


def _cparams(sem):
    return pltpu.CompilerParams(dimension_semantics=sem, vmem_limit_bytes=VMEM_LIMIT)


def _const_spec(shape):
    zeros = (0,) * len(shape)
    return pl.BlockSpec(shape, lambda *_: zeros)


def _rope_table_kernel(pos_ref, inv_ref, cos_ref, sin_ref):
    ang = pos_ref[...] * inv_ref[...]
    lane = lax.broadcasted_iota(jnp.int32, ang.shape, 1)
    c = jnp.cos(ang)
    s = jnp.sin(ang)
    first = (lane >= C_NOPE) & (lane < C_NOPE + C_ROPE // 2)
    second = (lane >= C_NOPE + C_ROPE // 2) & (lane < C_QK)
    cos_ref[...] = jnp.where(first | second, c, 1.0)
    sin_ref[...] = jnp.where(first, -s, jnp.where(second, s, 0.0))


def _rope_tables(positions):
    t = positions.size
    half = C_ROPE // 2
    inv = ROPE_THETA ** (-jnp.arange(half, dtype=F32) / half)
    inv_row = jnp.concatenate([jnp.zeros((C_NOPE,), F32), inv, inv, jnp.zeros((LANES - C_QK,), F32)])[None, :]
    pos = positions.astype(F32).reshape(t, 1)
    tm = 2048
    return pl.pallas_call(
        _rope_table_kernel,
        grid=(t // tm,),
        in_specs=[pl.BlockSpec((tm, 1), lambda i: (i, 0)), _const_spec((1, LANES))],
        out_specs=[pl.BlockSpec((tm, LANES), lambda i: (i, 0))] * 2,
        out_shape=[jax.ShapeDtypeStruct((t, LANES), F32)] * 2,
        compiler_params=_cparams(("parallel",)),
        name="rope_tables",
    )(pos, inv_row)


def _bucket_thresholds():
    half = N_BUCKETS // 2
    exact = half // 2
    n = np.arange(1, 2 * MAX_DISTANCE + 2, dtype=np.float64)
    far = exact + (np.log(n / exact) / math.log(MAX_DISTANCE / exact) * (half - exact)).astype(np.int64)
    far = np.minimum(far, half - 1)
    return tuple(int(n[np.argmax(far >= exact + k)]) for k in range(1, half - exact))


def _bias_kernel(table_ref, out_ref, *, radius, dilation, head_cols):
    hsel = pl.program_id(0)
    width = SUB + 2 * radius
    row = lax.broadcasted_iota(jnp.int32, (SUB, width), 0)
    col = lax.broadcasted_iota(jnp.int32, (SUB, width), 1)
    rel = col - radius - row
    n = jnp.abs(rel) * dilation
    half = N_BUCKETS // 2
    exact = half // 2
    far = jnp.full(n.shape, exact, jnp.int32)
    for thr in _bucket_thresholds():
        far = far + (n >= thr).astype(jnp.int32)
    bucket = jnp.where(rel > 0, half, 0) + jnp.where(n < exact, n, far)
    for idx, hc in enumerate(head_cols):
        @pl.when(hsel == idx)
        def _(hc=hc):
            val = jnp.zeros(n.shape, F32)
            for b in range(N_BUCKETS):
                val = jnp.where(bucket == b, table_ref[b, hc], val)
            out_ref[...] = jnp.where(jnp.abs(rel) <= radius, val * LOG2E, NEG)


def _bias_tiles(table, radius, dilation, head_cols):
    width = SUB + 2 * radius
    return pl.pallas_call(
        functools.partial(_bias_kernel, radius=radius, dilation=dilation, head_cols=head_cols),
        grid=(len(head_cols),),
        in_specs=[pl.BlockSpec(memory_space=pltpu.SMEM)],
        out_specs=pl.BlockSpec((None, SUB, width), lambda h: (h, 0, 0)),
        out_shape=jax.ShapeDtypeStruct((len(head_cols), SUB, width), F32),
        compiler_params=_cparams(("arbitrary",)),
        name="bias_tiles",
    )(table)


def _group_mean_sq(y, ones_ref, group):
    sq = (y * y).astype(BF16)
    width = y.shape[1]
    parts = []
    for s in range(0, width, 2 * LANES):
        w = min(2 * LANES, width - s)
        parts.append(jnp.dot(sq[:, s:s + w], ones_ref[:w, :w], preferred_element_type=F32))
    out = parts[0] if len(parts) == 1 else jnp.concatenate(parts, axis=1)
    return out * (1.0 / group)


def _slab_roll(y, shift):
    parts = [pltpu.roll(y[:, s:s + LANES], shift, 1) for s in range(0, y.shape[1], LANES)]
    return parts[0] if len(parts) == 1 else jnp.concatenate(parts, axis=1)


def _rope(y, cos_t, sin_t, lane):
    swapped = jnp.where(lane < C_NOPE + C_ROPE // 2, _slab_roll(y, LANES - C_ROPE // 2), _slab_roll(y, C_ROPE // 2))
    return y * cos_t + swapped * sin_t


def _inproj_kernel(x_ref, cos_ref, sin_ref, gmix_ref, w_ref, gqk_ref, ones64_ref, ones128_ref,
                   glq_ref, wuq_ref, gqc_ref, glkv_ref, wuk_ref, wuvt_ref, gkc_ref, qshift_ref, kone_ref,
                   qkv1_ref, qkv4_ref, qkv16_ref, qb_ref, kb_ref, vb_ref, qc_ref, kc_ref, vt_ref, stage_ref):
    x = x_ref[...]
    h = x * lax.rsqrt(jnp.mean(x * x, axis=-1, keepdims=True) + EPS) * gmix_ref[...]
    y = jnp.dot(h.astype(BF16), w_ref[...], preferred_element_type=F32)

    o_qb = 3 * A_W
    o_kb = o_qb + B_QW
    o_vb = o_kb + B_KVW
    o_cq = o_vb + B_KVW
    o_ckv = o_cq + C_Q_RANK
    o_kr = o_ckv + C_KV_RANK
    yn = jnp.concatenate([y[:, :2 * A_W], y[:, o_qb:o_vb]], axis=1)
    yn = yn * lax.rsqrt(_group_mean_sq(yn, ones64_ref, HEAD_DIM) + EPS) * gqk_ref[...]
    qb_ref[...] = yn[:, 2 * A_W:2 * A_W + B_QW].astype(BF16)
    kb_ref[...] = yn[:, 2 * A_W + B_QW:].astype(BF16)
    n_slabs, tm = stage_ref.shape[0], stage_ref.shape[1]
    for s in range(n_slabs):
        src = yn if s * LANES < 2 * A_W else y
        stage_ref[s] = src[:, s * LANES:(s + 1) * LANES]
    for out_ref, (_, r) in zip((qkv1_ref, qkv4_ref, qkv16_ref), A_CONFIGS):
        for c in range(r):
            for s in range(n_slabs):
                col = c * 3 * A_W + s * LANES
                out_ref[:, col:col + LANES] = stage_ref[s, pl.ds(c, tm // r, stride=r), :].astype(BF16)
    vb_ref[...] = y[:, o_vb:o_cq].astype(BF16)

    cos_t = jnp.concatenate([cos_ref[...]] * C_HEADS, axis=1)
    sin_t = jnp.concatenate([sin_ref[...]] * C_HEADS, axis=1)
    lane = lax.broadcasted_iota(jnp.int32, cos_t.shape, 1) % LANES
    cq = y[:, o_cq:o_ckv]
    cq = cq * lax.rsqrt(jnp.mean(cq * cq, axis=-1, keepdims=True) + EPS) * glq_ref[...]
    qc = jnp.dot(cq.astype(BF16), wuq_ref[...], preferred_element_type=F32)
    qc = qc * lax.rsqrt(_group_mean_sq(qc, ones128_ref, C_QK) + EPS) * gqc_ref[...]
    qc_ref[...] = (_rope(qc, cos_t, sin_t, lane) + qshift_ref[...]).astype(BF16)

    ckv = y[:, o_ckv:o_kr]
    ckv = (ckv * lax.rsqrt(jnp.mean(ckv * ckv, axis=-1, keepdims=True) + EPS) * glkv_ref[...]).astype(BF16)
    kr = y[:, o_kr:]
    kc = jnp.dot(ckv, wuk_ref[...], preferred_element_type=F32) + jnp.concatenate([kr] * C_HEADS, axis=1)
    kc = kc * lax.rsqrt(_group_mean_sq(kc, ones128_ref, C_QK) + EPS) * gkc_ref[...]
    kc_ref[...] = (_rope(kc, cos_t, sin_t, lane) + kone_ref[...]).astype(BF16)
    vt = lax.dot_general(wuvt_ref[...], ckv, (((1,), (1,)), ((), ())), preferred_element_type=F32)
    row = lax.broadcasted_iota(jnp.int32, (VT_ROWS - C_VDIM, vt.shape[1]), 0)
    ones_rows = jnp.where(row == 0, 1.0, 0.0).astype(BF16)
    for hd in range(C_HEADS):
        vt_ref[0, hd * VT_ROWS:hd * VT_ROWS + C_VDIM, :] = vt[hd * C_VDIM:(hd + 1) * C_VDIM].astype(BF16)
        vt_ref[0, hd * VT_ROWS + C_VDIM:(hd + 1) * VT_ROWS, :] = ones_rows


def _inproj(x2d, cos_t, sin_t, p):
    t = x2d.shape[0]
    tm = TM_IN
    row = lambda w: pl.BlockSpec((tm, w), lambda i: (i, 0))
    outs = [(B_QW, BF16), (B_KVW, BF16), (B_KVW, BF16)] + [(C_SLOTS, BF16)] * 2
    a_specs = [pl.BlockSpec((tm // r, r * 3 * A_W), lambda i: (i, 0)) for _, r in A_CONFIGS]
    a_shapes = [jax.ShapeDtypeStruct((t // r, r * 3 * A_W), BF16) for _, r in A_CONFIGS]
    return pl.pallas_call(
        _inproj_kernel,
        grid=(t // tm,),
        in_specs=[row(D_MODEL), row(LANES), row(LANES),
                  _const_spec((1, D_MODEL)), _const_spec((D_MODEL, IN_COLS)),
                  _const_spec((1, 2 * A_W + B_QW + B_KVW)),
                  _const_spec((2 * LANES, 2 * LANES)), _const_spec((2 * LANES, 2 * LANES)),
                  _const_spec((1, C_Q_RANK)), _const_spec((C_Q_RANK, C_SLOTS)), _const_spec((1, C_SLOTS)),
                  _const_spec((1, C_KV_RANK)), _const_spec((C_KV_RANK, C_SLOTS)),
                  _const_spec((C_HEADS * C_VDIM, C_KV_RANK)), _const_spec((1, C_SLOTS)),
                  _const_spec((1, C_SLOTS)), _const_spec((1, C_SLOTS))],
        out_specs=a_specs + [row(w) for w, _ in outs]
        + [pl.BlockSpec((1, C_HEADS * VT_ROWS, tm), lambda i: (i, 0, 0))],
        out_shape=a_shapes + [jax.ShapeDtypeStruct((t, w), d) for w, d in outs]
        + [jax.ShapeDtypeStruct((t // tm, C_HEADS * VT_ROWS, tm), BF16)],
        scratch_shapes=[pltpu.VMEM((3 * A_W // LANES, tm, LANES), F32)],
        compiler_params=_cparams(("parallel",)),
        name="inproj",
    )(x2d, cos_t, sin_t, p["gmix"], p["w_in"], p["gqk"], p["ones64"], p["ones128"],
      p["glq"], p["wuq"], p["gqc"], p["glkv"], p["wuk"], p["wuvt"], p["gkc"], p["qshift"], p["kone"])


def _window(lo_ref, main_ref, hi_ref, a, radius, tq):
    start, end = a - radius, a + SUB + radius
    parts = []
    if start < 0:
        parts.append(lo_ref[radius + start:radius, :])
        start = 0
    parts.append(main_ref[start:min(end, tq), :])
    if end > tq:
        parts.append(hi_ref[0:end - tq, :])
    return parts[0] if len(parts) == 1 else jnp.concatenate(parts, axis=0)


def _banded_kernel(*refs, tq, radius, head_ids, kv_slab, has_sink, emit_ml, n_tiles):
    if has_sink:
        sink_ref, refs = refs[0], refs[1:]
    q_ref, klo_ref, k_ref, khi_ref, vlo_ref, v_ref, vhi_ref, bias_ref, o_ref = refs[:9]
    ml_ref = refs[9] if emit_ml else None
    s_scr, m_scr, p_scr = refs[-3:]
    tile = pl.program_id(2)
    width = SUB + 2 * radius
    lane = lax.broadcasted_iota(jnp.int32, (SUB, LANES), 1)
    col = lax.broadcasted_iota(jnp.int32, (2 * SUB, width), 1)
    low = lane < HEAD_DIM
    subs = list(range(0, tq, SUB))
    slab = lambda w, pair: w[:, kv_slab[pair] * LANES:(kv_slab[pair] + 1) * LANES]
    tiles = [(a, pair) for a in subs for pair in range(len(head_ids))]

    kws = {a: _window(klo_ref, k_ref, khi_ref, a, radius, tq) for a in subs}
    for g, (a, pair) in enumerate(tiles):
        qs = q_ref[a:a + SUB, pair * LANES:(pair + 1) * LANES]
        zero = jnp.zeros_like(qs)
        q2 = jnp.concatenate([jnp.where(low, qs, zero), jnp.where(low, zero, qs)], axis=0)
        s = lax.dot_general(q2, slab(kws[a], pair), (((1,), (1,)), ((), ())), preferred_element_type=F32)
        s = s + bias_ref[pair]
        if a == 0:
            s = jnp.where(col < jnp.where(tile == 0, radius, 0), NEG, s)
        if a == tq - SUB:
            s = jnp.where(col >= jnp.where(tile == n_tiles - 1, SUB + radius, width), NEG, s)
        m = jnp.max(s, axis=-1, keepdims=True)
        if has_sink:
            sinks = jnp.concatenate([jnp.full((SUB, 1), sink_ref[hd], F32) for hd in head_ids[pair]], axis=0)
            m = jnp.maximum(m, sinks)
        s_scr[g] = s
        m_scr[g] = jnp.broadcast_to(m, (2 * SUB, LANES))

    for g in range(len(tiles)):
        m_wide = jnp.concatenate([m_scr[g]] * (width // LANES), axis=1)
        p_scr[g] = jnp.exp2(s_scr[g] - m_wide).astype(BF16)

    vws = {a: _window(vlo_ref, v_ref, vhi_ref, a, radius, tq) for a in subs}
    ml = None
    for g, (a, pair) in enumerate(tiles):
        hd0, hd1 = head_ids[pair]
        vs = slab(vws[a], pair)
        o2 = jnp.dot(p_scr[g], jnp.concatenate([vs, jnp.ones_like(vs)], axis=1), preferred_element_type=F32)
        m2 = m_scr[g]
        m0, m1 = m2[:SUB], m2[SUB:]
        l0, l1 = o2[:SUB, LANES:], o2[SUB:, LANES:]
        if has_sink:
            l0 = l0 + jnp.exp2(sink_ref[hd0] - m0)
            l1 = l1 + jnp.exp2(sink_ref[hd1] - m1)
        out = jnp.where(low, o2[:SUB, :LANES], o2[SUB:, :LANES]) / jnp.where(low, l0, l1)
        o_ref[a:a + SUB, pair * LANES:(pair + 1) * LANES] = out.astype(o_ref.dtype)
        if emit_ml:
            ml = jnp.zeros((SUB, LANES), F32) if pair == 0 else ml
            ml = jnp.where(lane == hd0, m0, ml)
            ml = jnp.where(lane == hd1, m1, ml)
            ml = jnp.where(lane == ML_L_OFFSET + hd0, l0, ml)
            ml = jnp.where(lane == ML_L_OFFSET + hd1, l1, ml)
            if pair == len(head_ids) - 1:
                ml_ref[a:a + SUB, :] = ml


def _banded(q, k, v, bias, *, batch, seq, dilation, radius, widths, head_ids, kv_slab, sink, emit_ml, out_dtype):
    n = seq // dilation
    tq = min(TQ_BAND, n)
    n_tiles = n // tq
    qw, kw = widths
    n_sub_tiles = (tq // SUB) * len(head_ids)
    view = lambda t: t[0].reshape(batch, n, t[0].shape[1])
    per_tile = tq // radius

    def main(w, t=(None, 1, 0)):
        return pl.BlockSpec((None, tq, w), lambda b, c, i: (b, i, t[1] * c + t[2]))

    def lo(t):
        return pl.BlockSpec((None, radius, kw),
                            lambda b, c, i: (b, jnp.maximum(i * per_tile - 1, 0), t[1] * c + t[2]))

    def hi(t):
        return pl.BlockSpec((None, radius, kw),
                            lambda b, c, i: (b, jnp.minimum((i + 1) * per_tile, n // radius - 1), t[1] * c + t[2]))

    in_specs = [main(qw, q), lo(k), main(kw, k), hi(k), lo(v), main(kw, v), hi(v), _const_spec(bias.shape)]
    args = [view(q), view(k), view(k), view(k), view(v), view(v), view(v), bias]
    if sink is not None:
        in_specs = [pl.BlockSpec(memory_space=pltpu.SMEM)] + in_specs
        args = [sink] + args
    out_specs = [main(qw)]
    out_shape = [jax.ShapeDtypeStruct((batch, n, dilation * qw), out_dtype)]
    if emit_ml:
        out_specs.append(main(LANES))
        out_shape.append(jax.ShapeDtypeStruct((batch, n, dilation * LANES), F32))
    outs = pl.pallas_call(
        functools.partial(_banded_kernel, tq=tq, radius=radius, head_ids=head_ids, kv_slab=kv_slab,
                          has_sink=sink is not None, emit_ml=emit_ml, n_tiles=n_tiles),
        grid=(batch, dilation, n_tiles),
        in_specs=in_specs,
        out_specs=out_specs,
        out_shape=out_shape,
        scratch_shapes=[pltpu.VMEM((n_sub_tiles, 2 * SUB, SUB + 2 * radius), F32),
                        pltpu.VMEM((n_sub_tiles, 2 * SUB, LANES), F32),
                        pltpu.VMEM((n_sub_tiles, 2 * SUB, SUB + 2 * radius), BF16)],
        compiler_params=_cparams(("parallel", "parallel", "parallel")),
        name="banded_r%d_d%d" % (radius, dilation),
    )(*args)
    return [o.reshape(batch * n, -1) for o in outs]


def _latent_kernel(q_ref, k_ref, vt_ref, o_ref, s_ref, cmax_ref, m_ref, acc_ref, *, tk):
    tq = q_ref.shape[0]
    nk = k_ref.shape[0] // tk
    ahead = LAT_SLOTS - 1
    qt = q_ref[...].astype(F32).T.astype(BF16)
    m_ref[...] = jnp.full(m_ref.shape, -jnp.inf, F32)
    acc_ref[...] = jnp.zeros(acc_ref.shape, F32)

    def scores(slot, j):
        start = pl.multiple_of(j * tk, tk)
        s = jnp.dot(k_ref[pl.ds(start, tk), :], qt, preferred_element_type=F32)
        s_ref[slot] = s
        cmax_ref[slot] = jnp.max(s, axis=0, keepdims=True)

    def consume(slot, j):
        m_old = m_ref[...]
        m_new = jnp.maximum(m_old, cmax_ref[slot])
        pr = jnp.exp2(s_ref[slot] - m_new).astype(BF16)
        acc_ref[...] = jnp.exp2(m_old - m_new) * acc_ref[...] + jnp.dot(
            vt_ref[j], pr, preferred_element_type=F32)
        m_ref[...] = m_new

    def step(j, u, with_scores):
        if with_scores:
            scores((u + ahead) % LAT_SLOTS, j + ahead)
        consume(u % LAT_SLOTS, j)

    for j in range(ahead):
        scores(j, j)
    trips = (nk - ahead) // LAT_UNROLL

    def body(jj, carry):
        for u in range(LAT_UNROLL):
            step(LAT_UNROLL * jj + u, u, True)
        return carry

    lax.fori_loop(0, trips, body, 0)
    for j in range(trips * LAT_UNROLL, nk):
        step(j, j, j + ahead < nk)
    acc = acc_ref[...]
    o = jnp.concatenate([acc[:C_VDIM] / acc[C_VDIM:C_VDIM + 1], jnp.zeros((LANES - C_VDIM, tq), F32)], axis=0)
    o_ref[...] = o.T.astype(o_ref.dtype)


def _latent_bounded_kernel(q_ref, k_ref, vt_ref, o_ref, acc_ref, *, tk):
    tq = q_ref.shape[0]
    nk = k_ref.shape[0] // tk
    qt = q_ref[...].astype(F32).T.astype(BF16)
    acc_ref[...] = jnp.zeros(acc_ref.shape, F32)

    def body(jj, carry):
        for u in range(LAT_BOUNDED_UNROLL):
            j = LAT_BOUNDED_UNROLL * jj + u
            start = pl.multiple_of(j * tk, tk)
            s = jnp.dot(k_ref[pl.ds(start, tk), :], qt, preferred_element_type=F32)
            acc_ref[...] += jnp.dot(vt_ref[j], jnp.exp2(s).astype(BF16), preferred_element_type=F32)
        return carry

    lax.fori_loop(0, nk // LAT_BOUNDED_UNROLL, body, 0)
    acc = acc_ref[...]
    o = jnp.concatenate([acc[:C_VDIM] / acc[C_VDIM:C_VDIM + 1], jnp.zeros((LANES - C_VDIM, tq), F32)], axis=0)
    o_ref[...] = o.T.astype(o_ref.dtype)


def _latent(qc, kc, vt, batch, seq, bounded):
    t = batch * seq
    tq, tk = TQ_LAT, vt.shape[2]
    nq = seq // tq
    if bounded:
        body = functools.partial(_latent_bounded_kernel, tk=tk)
        scratch = [pltpu.VMEM((VT_ROWS, tq), F32)]
    else:
        body = functools.partial(_latent_kernel, tk=tk)
        scratch = [pltpu.VMEM((LAT_SLOTS, tk, tq), F32), pltpu.VMEM((LAT_SLOTS, 1, tq), F32),
                   pltpu.VMEM((1, tq), F32), pltpu.VMEM((VT_ROWS, tq), F32)]
    return pl.pallas_call(
        body,
        grid=(batch, C_HEADS, nq),
        in_specs=[pl.BlockSpec((tq, LANES), lambda b, h, i: (b * nq + i, h)),
                  pl.BlockSpec((seq, LANES), lambda b, h, i: (b, h)),
                  pl.BlockSpec((seq // tk, VT_ROWS, tk), lambda b, h, i: (b, h, 0))],
        out_specs=pl.BlockSpec((tq, LANES), lambda b, h, i: (b * nq + i, h)),
        out_shape=jax.ShapeDtypeStruct((t, C_SLOTS), BF16),
        scratch_shapes=scratch,
        compiler_params=_cparams(("parallel", "parallel", "parallel")),
        name="latent_bounded" if bounded else "latent",
    )(qc, kc, vt)


def _merge_mlp_kernel(x_ref, oa1_ref, oa2_ref, oa3_ref, ml1_ref, ml2_ref, ml3_ref, ob_ref, oc_ref,
                      expand_ref, gout_ref, wout_ref, gmlp_ref, wup_ref, wdown_ref, out_ref, oa_s, ml_s):
    tm = x_ref.shape[0]
    oas, mls = [oa1_ref[...]], [ml1_ref[...]]
    for idx, (o_ref, l_ref) in enumerate(((oa2_ref, ml2_ref), (oa3_ref, ml3_ref))):
        r = A_CONFIGS[idx + 1][1]
        for c in range(r):
            for s in range(A_W // LANES):
                col = c * A_W + s * LANES
                oa_s[idx, s, pl.ds(c, tm // r, stride=r), :] = o_ref[:, col:col + LANES]
            ml_s[idx, pl.ds(c, tm // r, stride=r), :] = l_ref[:, c * LANES:(c + 1) * LANES]
        oas.append(jnp.concatenate([oa_s[idx, s] for s in range(A_W // LANES)], axis=1))
        mls.append(ml_s[idx])
    lane = lax.broadcasted_iota(jnp.int32, mls[0].shape, 1)
    m_all = jnp.maximum(jnp.maximum(mls[0], mls[1]), mls[2])
    ws = [pltpu.roll(ml, LANES - ML_L_OFFSET, 1) * jnp.exp2(ml - m_all) for ml in mls]
    wsum = ws[0] + ws[1] + ws[2]
    packed = jnp.zeros_like(wsum)
    for c, w in enumerate(ws):
        wn = jnp.where(lane < A_HEADS, w / wsum, 0.0)
        packed = packed + (wn if c == 0 else pltpu.roll(wn, ML_L_OFFSET * c, 1))
    hi = packed.astype(BF16)
    lo = (packed - hi.astype(F32)).astype(BF16)
    spread = (jnp.dot(hi, expand_ref[...], preferred_element_type=F32)
              + jnp.dot(lo, expand_ref[...], preferred_element_type=F32))
    oa = spread[:, :A_W] * oas[0] + spread[:, A_W:2 * A_W] * oas[1] + spread[:, 2 * A_W:] * oas[2]

    def group_norm(v, width):
        return v * lax.rsqrt(jnp.sum(v * v, axis=-1, keepdims=True) * (1.0 / width) + EPS)

    ob = ob_ref[...].astype(F32)
    oc = oc_ref[...].astype(F32)
    mixed = jnp.concatenate([group_norm(oa, A_W), group_norm(ob, B_QW), group_norm(oc, C_HEADS * C_VDIM)], axis=1)
    mixed = (mixed * gout_ref[...]).astype(BF16)
    x = x_ref[...] + jnp.dot(mixed, wout_ref[...], preferred_element_type=F32)

    h = (x * lax.rsqrt(jnp.mean(x * x, axis=-1, keepdims=True) + EPS) * gmlp_ref[...]).astype(BF16)
    acc = x
    for s in range(0, D_FF, FF_CHUNK):
        u = jnp.dot(h, wup_ref[:, s:s + FF_CHUNK], preferred_element_type=F32)
        u = jnp.square(jnp.maximum(u, 0.0)).astype(BF16)
        acc = acc + jnp.dot(u, wdown_ref[s:s + FF_CHUNK, :], preferred_element_type=F32)
    out_ref[...] = acc


def _merge_mlp(x2d, oas, mls, ob, oc, p):
    t = x2d.shape[0]
    tm = TM_MLP
    row = lambda w: pl.BlockSpec((tm, w), lambda i: (i, 0))
    strided = lambda w, r: pl.BlockSpec((tm // r, r * w), lambda i: (i, 0))
    single = lambda shape: pl.BlockSpec(shape, lambda i: (0,) * len(shape), pipeline_mode=pl.Buffered(1))
    return pl.pallas_call(
        _merge_mlp_kernel,
        grid=(t // tm,),
        in_specs=[row(D_MODEL)] + [strided(A_W, r) for _, r in A_CONFIGS]
        + [strided(LANES, r) for _, r in A_CONFIGS] + [row(B_QW), row(C_SLOTS),
                  single((LANES, 3 * A_W)), single((1, MIX_COLS)), single((MIX_COLS, D_MODEL)),
                  single((1, D_MODEL)), single((D_MODEL, D_FF)), single((D_FF, D_MODEL))],
        out_specs=row(D_MODEL),
        out_shape=jax.ShapeDtypeStruct((t, D_MODEL), F32),
        scratch_shapes=[pltpu.VMEM((2, A_W // LANES, tm, LANES), F32), pltpu.VMEM((2, tm, LANES), F32)],
        compiler_params=_cparams(("parallel",)),
        name="merge_mlp",
    )(x2d, *oas, *mls, ob, oc, p["expand"], p["gout"], p["w_out"], p["gmlp"], p["w_up"], p["w_down"])


def _block_diag_ones(group):
    idx = np.arange(2 * LANES) // group
    return jnp.asarray((idx[:, None] == idx[None, :]).astype(np.float32), dtype=BF16)


def _expand_matrix():
    e = np.zeros((LANES, len(A_CONFIGS) * A_W), np.float32)
    for c in range(len(A_CONFIGS)):
        for h in range(A_HEADS):
            e[ML_L_OFFSET * c + h, c * A_W + h * HEAD_DIM:c * A_W + (h + 1) * HEAD_DIM] = 1.0
    return jnp.asarray(e, dtype=BF16)


def _pad_heads(w, heads, used, lo=0):
    lead = w.shape[:-1]
    w = w.reshape(lead + (heads, used))
    pad = [(0, 0)] * len(lead) + [(0, 0), (lo, LANES - lo - used)]
    return jnp.pad(w, pad).reshape(lead + (heads * LANES,))


def _layer_params(i, norm_mix, w_in, qk_gain_a, qk_gain_b, q_lat_gain, kv_lat_gain, w_uq, w_ukv, qk_gain_c,
                  out_norm, w_out, norm_mlp, w_up, w_down):
    w = w_in[i]
    o = np.cumsum((A_W, A_W, A_W, B_QW, B_KVW, B_KVW, C_Q_RANK, C_KV_RANK)).tolist()
    qb = w[:, o[2]:o[3]].reshape(D_MODEL, B_HEADS, HEAD_DIM)[:, B_HEAD_ORDER, :].reshape(D_MODEL, B_QW)
    kr = _pad_heads(w[:, o[7]:], 1, C_ROPE, lo=C_NOPE)
    w_in_p = jnp.concatenate([w[:, :o[2]], qb, w[:, o[3]:o[7]], kr], axis=1).astype(BF16)

    scale = HEAD_DIM ** -0.5 * LOG2E
    gqk = jnp.concatenate([jnp.tile(qk_gain_a[i, 0], A_HEADS) * scale, jnp.tile(qk_gain_a[i, 1], A_HEADS),
                           jnp.tile(qk_gain_b[i, 0], B_HEADS) * scale, jnp.tile(qk_gain_b[i, 1], B_KV_HEADS)])
    ukv = w_ukv[i].reshape(C_KV_RANK, C_HEADS, C_NOPE + C_VDIM)
    wuk = _pad_heads(ukv[:, :, :C_NOPE].reshape(C_KV_RANK, -1), C_HEADS, C_NOPE)
    wuvt = ukv[:, :, C_NOPE:].reshape(C_KV_RANK, C_HEADS * C_VDIM).T
    g = out_norm[i]
    gb = g[A_W:A_W + B_QW].reshape(B_HEADS, HEAD_DIM)[B_HEAD_ORDER, :].reshape(B_QW)
    gout = jnp.concatenate([g[:A_W], gb, _pad_heads(g[A_W + B_QW:], C_HEADS, C_VDIM)])
    wo = w_out[i]
    wob = wo[A_W:A_W + B_QW].reshape(B_HEADS, HEAD_DIM, D_MODEL)[B_HEAD_ORDER, :, :].reshape(B_QW, D_MODEL)
    woc = jnp.pad(wo[A_W + B_QW:].reshape(C_HEADS, C_VDIM, D_MODEL), ((0, 0), (0, LANES - C_VDIM), (0, 0)))
    w_out_p = jnp.concatenate([wo[:A_W], wob, woc.reshape(C_SLOTS, D_MODEL)], axis=0).astype(BF16)
    bound = (C_QK ** 0.5 * LOG2E * BOUND_MARGIN) * jnp.max(jnp.abs(qk_gain_c[i, 0])) * jnp.max(jnp.abs(qk_gain_c[i, 1]))
    return {
        "gmix": norm_mix[i][None, :],
        "w_in": w_in_p,
        "gqk": gqk[None, :],
        "ones64": _block_diag_ones(HEAD_DIM),
        "ones128": _block_diag_ones(LANES),
        "glq": q_lat_gain[i][None, :],
        "wuq": _pad_heads(w_uq[i], C_HEADS, C_QK).astype(BF16),
        "gqc": jnp.tile(_pad_heads(qk_gain_c[i, 0] * (C_QK ** -0.5 * LOG2E), 1, C_QK), C_HEADS)[None, :],
        "glkv": kv_lat_gain[i][None, :],
        "wuk": wuk.astype(BF16),
        "wuvt": wuvt.astype(BF16),
        "gkc": jnp.tile(_pad_heads(qk_gain_c[i, 1], 1, C_QK), C_HEADS)[None, :],
        "qshift": jnp.tile(_pad_heads(-bound[None], 1, 1, lo=C_QK), C_HEADS)[None, :],
        "kone": jnp.tile(_pad_heads(jnp.ones((1,), F32), 1, 1, lo=C_QK), C_HEADS)[None, :],
        "logit_bound": bound,
        "expand": _expand_matrix(),
        "gout": gout[None, :],
        "w_out": w_out_p,
        "gmlp": norm_mlp[i][None, :],
        "w_up": w_up[i].astype(BF16),
        "w_down": w_down[i].astype(BF16),
    }


def kernel(x, positions, rel_bias_table, norm_mix, w_in, qk_gain_a, qk_gain_b, sink_b, q_lat_gain, kv_lat_gain,
           w_uq, w_ukv, qk_gain_c, out_norm, w_out, norm_mlp, w_up, w_down):
    batch, seq, _ = x.shape
    depth = w_in.shape[0]
    x2d = x.reshape(batch * seq, D_MODEL)
    cos_t, sin_t = _rope_tables(positions)
    a_pairs = tuple((2 * p, 2 * p + 1) for p in range(A_HEADS // 2))
    pair_major = lambda tiles: tiles.reshape(tiles.shape[0] // 2, 2 * SUB, tiles.shape[2])
    bias_a = [pair_major(_bias_tiles(rel_bias_table, window // (2 * r), r, tuple(range(A_HEADS))))
              for window, r in A_CONFIGS]
    bias_b = pair_major(_bias_tiles(rel_bias_table, B_RADIUS, 1, tuple(A_HEADS + h for h in B_HEAD_ORDER)))
    for i in range(depth):
        p = _layer_params(i, norm_mix, w_in, qk_gain_a, qk_gain_b, q_lat_gain, kv_lat_gain, w_uq, w_ukv,
                          qk_gain_c, out_norm, w_out, norm_mlp, w_up, w_down)
        qkv1, qkv4, qkv16, qb, kb, vb, qc, kc, vt = _inproj(x2d, cos_t, sin_t, p)
        oas, mls = [], []
        for (window, r), bias, qkv in zip(A_CONFIGS, bias_a, (qkv1, qkv4, qkv16)):
            o, ml = _banded((qkv, 3, 0), (qkv, 3, 1), (qkv, 3, 2), bias, batch=batch, seq=seq, dilation=r,
                            radius=window // (2 * r), widths=(A_W, A_W), head_ids=a_pairs, kv_slab=(0, 1, 2),
                            sink=None, emit_ml=True, out_dtype=F32)
            oas.append(o)
            mls.append(ml)
        (ob,) = _banded((qb, 1, 0), (kb, 1, 0), (vb, 1, 0), bias_b, batch=batch, seq=seq, dilation=1,
                        radius=B_RADIUS, widths=(B_QW, B_KVW), head_ids=((0, 2), (1, 3)), kv_slab=(0, 0),
                        sink=sink_b[i] * LOG2E, emit_ml=False, out_dtype=BF16)
        oc = lax.cond(p["logit_bound"] <= MAX_LOGIT_BOUND,
                      functools.partial(_latent, batch=batch, seq=seq, bounded=True),
                      functools.partial(_latent, batch=batch, seq=seq, bounded=False), qc, kc, vt)
        x2d = _merge_mlp(x2d, oas, mls, ob, oc, p)
    return x2d.reshape(batch, seq, D_MODEL)
```

```python
import functools
import math

import numpy as np
import jax
import jax.numpy as jnp
from jax import lax
from jax.experimental import pallas as pl
from jax.experimental.pallas import tpu as pltpu

F32 = jnp.float32
BF16 = jnp.bfloat16

D_MODEL = 1024
HEAD_DIM = 64
A_HEADS = 6
A_CONFIGS = ((128, 1), (512, 4), (2048, 16))
B_HEADS = 4
B_KV_HEADS = 2
B_RADIUS = 128
C_HEADS = 6
C_NOPE = 64
C_ROPE = 32
C_VDIM = 64
C_QK = C_NOPE + C_ROPE
C_Q_RANK = 256
C_KV_RANK = 128
ROPE_THETA = 10000.0
N_BUCKETS = 32
MAX_DISTANCE = 1024
D_FF = 4 * D_MODEL
EPS = 1e-6
NEG = -1e30

A_W = A_HEADS * HEAD_DIM
B_QW = B_HEADS * HEAD_DIM
B_KVW = B_KV_HEADS * HEAD_DIM
LANES = 128
C_SLOTS = C_HEADS * LANES
IN_COLS = 3 * A_W + B_QW + 2 * B_KVW + C_Q_RANK + C_KV_RANK + LANES
MIX_COLS = A_W + B_QW + C_HEADS * C_VDIM
VT_ROWS = 80
LOG2E = math.log2(math.e)
ML_L_OFFSET = 8

B_HEAD_ORDER = (0, 2, 1, 3)

TM_IN = 512
TQ_BAND = 512
SUB = 128
TQ_LAT = 2048
BOUND_MARGIN = 1.02
MAX_LOGIT_BOUND = 50.0
LAT_BOUNDED_UNROLL = 16
LAT_SLOTS = 3
LAT_UNROLL = 6
TM_MLP = 512
FF_CHUNK = 1024
VMEM_LIMIT = 56 * 1024 * 1024


def _cparams(sem):
    return pltpu.CompilerParams(dimension_semantics=sem, vmem_limit_bytes=VMEM_LIMIT)


def _const_spec(shape):
    zeros = (0,) * len(shape)
    return pl.BlockSpec(shape, lambda *_: zeros)


def _rope_table_kernel(pos_ref, inv_ref, cos_ref, sin_ref):
    ang = pos_ref[...] * inv_ref[...]
    lane = lax.broadcasted_iota(jnp.int32, ang.shape, 1)
    c = jnp.cos(ang)
    s = jnp.sin(ang)
    first = (lane >= C_NOPE) & (lane < C_NOPE + C_ROPE // 2)
    second = (lane >= C_NOPE + C_ROPE // 2) & (lane < C_QK)
    cos_ref[...] = jnp.where(first | second, c, 1.0)
    sin_ref[...] = jnp.where(first, -s, jnp.where(second, s, 0.0))


def _rope_tables(positions):
    t = positions.size
    half = C_ROPE // 2
    inv = ROPE_THETA ** (-jnp.arange(half, dtype=F32) / half)
    inv_row = jnp.concatenate([jnp.zeros((C_NOPE,), F32), inv, inv, jnp.zeros((LANES - C_QK,), F32)])[None, :]
    pos = positions.astype(F32).reshape(t, 1)
    tm = 2048
    return pl.pallas_call(
        _rope_table_kernel,
        grid=(t // tm,),
        in_specs=[pl.BlockSpec((tm, 1), lambda i: (i, 0)), _const_spec((1, LANES))],
        out_specs=[pl.BlockSpec((tm, LANES), lambda i: (i, 0))] * 2,
        out_shape=[jax.ShapeDtypeStruct((t, LANES), F32)] * 2,
        compiler_params=_cparams(("parallel",)),
        name="rope_tables",
    )(pos, inv_row)


def _bucket_thresholds():
    half = N_BUCKETS // 2
    exact = half // 2
    n = np.arange(1, 2 * MAX_DISTANCE + 2, dtype=np.float64)
    far = exact + (np.log(n / exact) / math.log(MAX_DISTANCE / exact) * (half - exact)).astype(np.int64)
    far = np.minimum(far, half - 1)
    return tuple(int(n[np.argmax(far >= exact + k)]) for k in range(1, half - exact))


def _bias_kernel(table_ref, out_ref, *, radius, dilation, head_cols):
    hsel = pl.program_id(0)
    width = SUB + 2 * radius
    row = lax.broadcasted_iota(jnp.int32, (SUB, width), 0)
    col = lax.broadcasted_iota(jnp.int32, (SUB, width), 1)
    rel = col - radius - row
    n = jnp.abs(rel) * dilation
    half = N_BUCKETS // 2
    exact = half // 2
    far = jnp.full(n.shape, exact, jnp.int32)
    for thr in _bucket_thresholds():
        far = far + (n >= thr).astype(jnp.int32)
    bucket = jnp.where(rel > 0, half, 0) + jnp.where(n < exact, n, far)
    for idx, hc in enumerate(head_cols):
        @pl.when(hsel == idx)
        def _(hc=hc):
            val = jnp.zeros(n.shape, F32)
            for b in range(N_BUCKETS):
                val = jnp.where(bucket == b, table_ref[b, hc], val)
            out_ref[...] = jnp.where(jnp.abs(rel) <= radius, val * LOG2E, NEG)


def _bias_tiles(table, radius, dilation, head_cols):
    width = SUB + 2 * radius
    return pl.pallas_call(
        functools.partial(_bias_kernel, radius=radius, dilation=dilation, head_cols=head_cols),
        grid=(len(head_cols),),
        in_specs=[pl.BlockSpec(memory_space=pltpu.SMEM)],
        out_specs=pl.BlockSpec((None, SUB, width), lambda h: (h, 0, 0)),
        out_shape=jax.ShapeDtypeStruct((len(head_cols), SUB, width), F32),
        compiler_params=_cparams(("arbitrary",)),
        name="bias_tiles",
    )(table)


def _group_mean_sq(y, ones_ref, group):
    sq = (y * y).astype(BF16)
    width = y.shape[1]
    parts = []
    for s in range(0, width, 2 * LANES):
        w = min(2 * LANES, width - s)
        parts.append(jnp.dot(sq[:, s:s + w], ones_ref[:w, :w], preferred_element_type=F32))
    out = parts[0] if len(parts) == 1 else jnp.concatenate(parts, axis=1)
    return out * (1.0 / group)


def _slab_roll(y, shift):
    parts = [pltpu.roll(y[:, s:s + LANES], shift, 1) for s in range(0, y.shape[1], LANES)]
    return parts[0] if len(parts) == 1 else jnp.concatenate(parts, axis=1)


def _rope(y, cos_t, sin_t, lane):
    swapped = jnp.where(lane < C_NOPE + C_ROPE // 2, _slab_roll(y, LANES - C_ROPE // 2), _slab_roll(y, C_ROPE // 2))
    return y * cos_t + swapped * sin_t


def _inproj_kernel(x_ref, cos_ref, sin_ref, gmix_ref, w_ref, gqk_ref, ones64_ref, ones128_ref,
                   glq_ref, wuq_ref, gqc_ref, glkv_ref, wuk_ref, wuvt_ref, gkc_ref, qshift_ref, kone_ref,
                   qkv1_ref, qkv4_ref, qkv16_ref, qb_ref, kb_ref, vb_ref, qc_ref, kc_ref, vt_ref, stage_ref, stage4_ref):
    x = x_ref[...]
    h = x * lax.rsqrt(jnp.mean(x * x, axis=-1, keepdims=True) + EPS) * gmix_ref[...]
    y = jnp.dot(h.astype(BF16), w_ref[...], preferred_element_type=F32)

    o_qb = 3 * A_W
    o_kb = o_qb + B_QW
    o_vb = o_kb + B_KVW
    o_cq = o_vb + B_KVW
    o_ckv = o_cq + C_Q_RANK
    o_kr = o_ckv + C_KV_RANK
    yn = jnp.concatenate([y[:, :2 * A_W], y[:, o_qb:o_vb]], axis=1)
    yn = yn * lax.rsqrt(_group_mean_sq(yn, ones64_ref, HEAD_DIM) + EPS) * gqk_ref[...]
    qb_ref[...] = yn[:, 2 * A_W:2 * A_W + B_QW].astype(BF16)
    kb_ref[...] = yn[:, 2 * A_W + B_QW:].astype(BF16)
    n_slabs, tm = stage_ref.shape[0], stage_ref.shape[1]
    for s in range(n_slabs):
        src = yn if s * LANES < 2 * A_W else y
        stage_ref[s] = src[:, s * LANES:(s + 1) * LANES]
    for s in range(n_slabs):
        qkv1_ref[:, s * LANES:(s + 1) * LANES] = stage_ref[s].astype(BF16)
        for c4 in range(4):
            rows = stage_ref[s, pl.ds(c4, tm // 4, stride=4), :]
            qkv4_ref[:, c4 * 3 * A_W + s * LANES:c4 * 3 * A_W + (s + 1) * LANES] = rows.astype(BF16)
            stage4_ref[c4 * n_slabs + s] = rows
    for c16 in range(16):
        for s in range(n_slabs):
            rows = stage4_ref[(c16 % 4) * n_slabs + s, pl.ds(c16 // 4, tm // 16, stride=4), :]
            qkv16_ref[:, c16 * 3 * A_W + s * LANES:c16 * 3 * A_W + (s + 1) * LANES] = rows.astype(BF16)
    vb_ref[...] = y[:, o_vb:o_cq].astype(BF16)

    cos_t = jnp.concatenate([cos_ref[...]] * C_HEADS, axis=1)
    sin_t = jnp.concatenate([sin_ref[...]] * C_HEADS, axis=1)
    lane = lax.broadcasted_iota(jnp.int32, cos_t.shape, 1) % LANES
    cq = y[:, o_cq:o_ckv]
    cq = cq * lax.rsqrt(jnp.mean(cq * cq, axis=-1, keepdims=True) + EPS) * glq_ref[...]
    qc = jnp.dot(cq.astype(BF16), wuq_ref[...], preferred_element_type=F32)
    qc = qc * lax.rsqrt(_group_mean_sq(qc, ones128_ref, C_QK) + EPS) * gqc_ref[...]
    qc_ref[...] = (_rope(qc, cos_t, sin_t, lane) + qshift_ref[...]).astype(BF16)

    ckv = y[:, o_ckv:o_kr]
    ckv = (ckv * lax.rsqrt(jnp.mean(ckv * ckv, axis=-1, keepdims=True) + EPS) * glkv_ref[...]).astype(BF16)
    kr = y[:, o_kr:]
    kc = jnp.dot(ckv, wuk_ref[...], preferred_element_type=F32) + jnp.concatenate([kr] * C_HEADS, axis=1)
    kc = kc * lax.rsqrt(_group_mean_sq(kc, ones128_ref, C_QK) + EPS) * gkc_ref[...]
    kc_ref[...] = (_rope(kc, cos_t, sin_t, lane) + kone_ref[...]).astype(BF16)
    vt = lax.dot_general(wuvt_ref[...], ckv, (((1,), (1,)), ((), ())), preferred_element_type=F32)
    row = lax.broadcasted_iota(jnp.int32, (VT_ROWS - C_VDIM, vt.shape[1]), 0)
    ones_rows = jnp.where(row == 0, 1.0, 0.0).astype(BF16)
    for hd in range(C_HEADS):
        vt_ref[0, hd * VT_ROWS:hd * VT_ROWS + C_VDIM, :] = vt[hd * C_VDIM:(hd + 1) * C_VDIM].astype(BF16)
        vt_ref[0, hd * VT_ROWS + C_VDIM:(hd + 1) * VT_ROWS, :] = ones_rows


def _inproj(x2d, cos_t, sin_t, p):
    t = x2d.shape[0]
    tm = TM_IN
    row = lambda w: pl.BlockSpec((tm, w), lambda i: (i, 0))
    outs = [(B_QW, BF16), (B_KVW, BF16), (B_KVW, BF16)] + [(C_SLOTS, BF16)] * 2
    a_specs = [pl.BlockSpec((tm // r, r * 3 * A_W), lambda i: (i, 0)) for _, r in A_CONFIGS]
    a_shapes = [jax.ShapeDtypeStruct((t // r, r * 3 * A_W), BF16) for _, r in A_CONFIGS]
    return pl.pallas_call(
        _inproj_kernel,
        grid=(t // tm,),
        in_specs=[row(D_MODEL), row(LANES), row(LANES),
                  _const_spec((1, D_MODEL)), _const_spec((D_MODEL, IN_COLS)),
                  _const_spec((1, 2 * A_W + B_QW + B_KVW)),
                  _const_spec((2 * LANES, 2 * LANES)), _const_spec((2 * LANES, 2 * LANES)),
                  _const_spec((1, C_Q_RANK)), _const_spec((C_Q_RANK, C_SLOTS)), _const_spec((1, C_SLOTS)),
                  _const_spec((1, C_KV_RANK)), _const_spec((C_KV_RANK, C_SLOTS)),
                  _const_spec((C_HEADS * C_VDIM, C_KV_RANK)), _const_spec((1, C_SLOTS)),
                  _const_spec((1, C_SLOTS)), _const_spec((1, C_SLOTS))],
        out_specs=a_specs + [row(w) for w, _ in outs]
        + [pl.BlockSpec((1, C_HEADS * VT_ROWS, tm), lambda i: (i, 0, 0))],
        out_shape=a_shapes + [jax.ShapeDtypeStruct((t, w), d) for w, d in outs]
        + [jax.ShapeDtypeStruct((t // tm, C_HEADS * VT_ROWS, tm), BF16)],
        scratch_shapes=[pltpu.VMEM((3 * A_W // LANES, tm, LANES), F32),
                        pltpu.VMEM((4 * 3 * A_W // LANES, tm // 4, LANES), F32)],
        compiler_params=_cparams(("parallel",)),
        name="inproj",
    )(x2d, cos_t, sin_t, p["gmix"], p["w_in"], p["gqk"], p["ones64"], p["ones128"],
      p["glq"], p["wuq"], p["gqc"], p["glkv"], p["wuk"], p["wuvt"], p["gkc"], p["qshift"], p["kone"])


def _window(lo_ref, main_ref, hi_ref, a, radius, tq):
    start, end = a - radius, a + SUB + radius
    parts = []
    if start < 0:
        parts.append(lo_ref[radius + start:radius, :])
        start = 0
    parts.append(main_ref[start:min(end, tq), :])
    if end > tq:
        parts.append(hi_ref[0:end - tq, :])
    return parts[0] if len(parts) == 1 else jnp.concatenate(parts, axis=0)


def _banded_kernel(*refs, tq, radius, head_ids, kv_slab, has_sink, emit_ml, n_tiles):
    if has_sink:
        sink_ref, refs = refs[0], refs[1:]
    q_ref, klo_ref, k_ref, khi_ref, vlo_ref, v_ref, vhi_ref, bias_ref, o_ref = refs[:9]
    ml_ref = refs[9] if emit_ml else None
    s_scr, m_scr, p_scr = refs[-3:]
    tile = pl.program_id(2)
    width = SUB + 2 * radius
    lane = lax.broadcasted_iota(jnp.int32, (SUB, LANES), 1)
    col = lax.broadcasted_iota(jnp.int32, (2 * SUB, width), 1)
    low = lane < HEAD_DIM
    subs = list(range(0, tq, SUB))
    slab = lambda w, pair: w[:, kv_slab[pair] * LANES:(kv_slab[pair] + 1) * LANES]
    tiles = [(a, pair) for a in subs for pair in range(len(head_ids))]

    kws = {a: _window(klo_ref, k_ref, khi_ref, a, radius, tq) for a in subs}
    for g, (a, pair) in enumerate(tiles):
        qs = q_ref[a:a + SUB, pair * LANES:(pair + 1) * LANES]
        zero = jnp.zeros_like(qs)
        q2 = jnp.concatenate([jnp.where(low, qs, zero), jnp.where(low, zero, qs)], axis=0)
        s = lax.dot_general(q2, slab(kws[a], pair), (((1,), (1,)), ((), ())), preferred_element_type=F32)
        s = s + bias_ref[pair]
        if a == 0:
            s = jnp.where(col < jnp.where(tile == 0, radius, 0), NEG, s)
        if a == tq - SUB:
            s = jnp.where(col >= jnp.where(tile == n_tiles - 1, SUB + radius, width), NEG, s)
        m = jnp.max(s, axis=-1, keepdims=True)
        if has_sink:
            sinks = jnp.concatenate([jnp.full((SUB, 1), sink_ref[hd], F32) for hd in head_ids[pair]], axis=0)
            m = jnp.maximum(m, sinks)
        s_scr[g] = s
        m_scr[g] = jnp.broadcast_to(m, (2 * SUB, LANES))

    for g in range(len(tiles)):
        m_wide = jnp.concatenate([m_scr[g]] * (width // LANES), axis=1)
        p_scr[g] = jnp.exp2(s_scr[g] - m_wide).astype(BF16)

    vws = {a: _window(vlo_ref, v_ref, vhi_ref, a, radius, tq) for a in subs}
    ml = None
    for g, (a, pair) in enumerate(tiles):
        hd0, hd1 = head_ids[pair]
        vs = slab(vws[a], pair)
        o2 = jnp.dot(p_scr[g], jnp.concatenate([vs, jnp.ones_like(vs)], axis=1), preferred_element_type=F32)
        m2 = m_scr[g]
        m0, m1 = m2[:SUB], m2[SUB:]
        l0, l1 = o2[:SUB, LANES:], o2[SUB:, LANES:]
        if has_sink:
            l0 = l0 + jnp.exp2(sink_ref[hd0] - m0)
            l1 = l1 + jnp.exp2(sink_ref[hd1] - m1)
        out = jnp.where(low, o2[:SUB, :LANES], o2[SUB:, :LANES]) / jnp.where(low, l0, l1)
        o_ref[a:a + SUB, pair * LANES:(pair + 1) * LANES] = out.astype(o_ref.dtype)
        if emit_ml:
            ml = jnp.zeros((SUB, LANES), F32) if pair == 0 else ml
            ml = jnp.where(lane == hd0, m0, ml)
            ml = jnp.where(lane == hd1, m1, ml)
            ml = jnp.where(lane == ML_L_OFFSET + hd0, l0, ml)
            ml = jnp.where(lane == ML_L_OFFSET + hd1, l1, ml)
            if pair == len(head_ids) - 1:
                ml_ref[a:a + SUB, :] = ml


def _banded(q, k, v, bias, *, batch, seq, dilation, radius, widths, head_ids, kv_slab, sink, emit_ml, out_dtype):
    n = seq // dilation
    tq = min(TQ_BAND, n)
    n_tiles = n // tq
    qw, kw = widths
    n_sub_tiles = (tq // SUB) * len(head_ids)
    view = lambda t: t[0].reshape(batch, n, t[0].shape[1])
    per_tile = tq // radius

    def main(w, t=(None, 1, 0)):
        return pl.BlockSpec((None, tq, w), lambda b, c, i: (b, i, t[1] * c + t[2]))

    def lo(t):
        return pl.BlockSpec((None, radius, kw),
                            lambda b, c, i: (b, jnp.maximum(i * per_tile - 1, 0), t[1] * c + t[2]))

    def hi(t):
        return pl.BlockSpec((None, radius, kw),
                            lambda b, c, i: (b, jnp.minimum((i + 1) * per_tile, n // radius - 1), t[1] * c + t[2]))

    in_specs = [main(qw, q), lo(k), main(kw, k), hi(k), lo(v), main(kw, v), hi(v), _const_spec(bias.shape)]
    args = [view(q), view(k), view(k), view(k), view(v), view(v), view(v), bias]
    if sink is not None:
        in_specs = [pl.BlockSpec(memory_space=pltpu.SMEM)] + in_specs
        args = [sink] + args
    out_specs = [main(qw)]
    out_shape = [jax.ShapeDtypeStruct((batch, n, dilation * qw), out_dtype)]
    if emit_ml:
        out_specs.append(main(LANES))
        out_shape.append(jax.ShapeDtypeStruct((batch, n, dilation * LANES), F32))
    outs = pl.pallas_call(
        functools.partial(_banded_kernel, tq=tq, radius=radius, head_ids=head_ids, kv_slab=kv_slab,
                          has_sink=sink is not None, emit_ml=emit_ml, n_tiles=n_tiles),
        grid=(batch, dilation, n_tiles),
        in_specs=in_specs,
        out_specs=out_specs,
        out_shape=out_shape,
        scratch_shapes=[pltpu.VMEM((n_sub_tiles, 2 * SUB, SUB + 2 * radius), F32),
                        pltpu.VMEM((n_sub_tiles, 2 * SUB, LANES), F32),
                        pltpu.VMEM((n_sub_tiles, 2 * SUB, SUB + 2 * radius), BF16)],
        compiler_params=_cparams(("parallel", "parallel", "parallel")),
        name="banded_r%d_d%d" % (radius, dilation),
    )(*args)
    return [o.reshape(batch * n, -1) for o in outs]


def _latent_kernel(q_ref, k_ref, vt_ref, o_ref, s_ref, cmax_ref, m_ref, acc_ref, *, tk):
    tq = q_ref.shape[0]
    nk = k_ref.shape[0] // tk
    ahead = LAT_SLOTS - 1
    qt = q_ref[...].astype(F32).T.astype(BF16)
    m_ref[...] = jnp.full(m_ref.shape, -jnp.inf, F32)
    acc_ref[...] = jnp.zeros(acc_ref.shape, F32)

    def scores(slot, j):
        start = pl.multiple_of(j * tk, tk)
        s = jnp.dot(k_ref[pl.ds(start, tk), :], qt, preferred_element_type=F32)
        s_ref[slot] = s
        cmax_ref[slot] = jnp.max(s, axis=0, keepdims=True)

    def consume(slot, j):
        m_old = m_ref[...]
        m_new = jnp.maximum(m_old, cmax_ref[slot])
        pr = jnp.exp2(s_ref[slot] - m_new).astype(BF16)
        acc_ref[...] = jnp.exp2(m_old - m_new) * acc_ref[...] + jnp.dot(
            vt_ref[j], pr, preferred_element_type=F32)
        m_ref[...] = m_new

    def step(j, u, with_scores):
        if with_scores:
            scores((u + ahead) % LAT_SLOTS, j + ahead)
        consume(u % LAT_SLOTS, j)

    for j in range(ahead):
        scores(j, j)
    trips = (nk - ahead) // LAT_UNROLL

    def body(jj, carry):
        for u in range(LAT_UNROLL):
            step(LAT_UNROLL * jj + u, u, True)
        return carry

    lax.fori_loop(0, trips, body, 0)
    for j in range(trips * LAT_UNROLL, nk):
        step(j, j, j + ahead < nk)
    acc = acc_ref[...]
    o = jnp.concatenate([acc[:C_VDIM] / acc[C_VDIM:C_VDIM + 1], jnp.zeros((LANES - C_VDIM, tq), F32)], axis=0)
    o_ref[...] = o.T.astype(o_ref.dtype)


def _latent_bounded_kernel(q_ref, k_ref, vt_ref, o_ref, acc_ref, *, tk):
    tq = q_ref.shape[0]
    nk = k_ref.shape[0] // tk
    qt = q_ref[...].astype(F32).T.astype(BF16)
    acc_ref[...] = jnp.zeros(acc_ref.shape, F32)

    def body(jj, carry):
        for u in range(LAT_BOUNDED_UNROLL):
            j = LAT_BOUNDED_UNROLL * jj + u
            start = pl.multiple_of(j * tk, tk)
            s = jnp.dot(k_ref[pl.ds(start, tk), :], qt, preferred_element_type=F32)
            acc_ref[...] += jnp.dot(vt_ref[j], jnp.exp2(s).astype(BF16), preferred_element_type=F32)
        return carry

    lax.fori_loop(0, nk // LAT_BOUNDED_UNROLL, body, 0)
    acc = acc_ref[...]
    o = jnp.concatenate([acc[:C_VDIM] / acc[C_VDIM:C_VDIM + 1], jnp.zeros((LANES - C_VDIM, tq), F32)], axis=0)
    o_ref[...] = o.T.astype(o_ref.dtype)


def _latent(qc, kc, vt, batch, seq, bounded):
    t = batch * seq
    tq, tk = TQ_LAT, vt.shape[2]
    nq = seq // tq
    if bounded:
        body = functools.partial(_latent_bounded_kernel, tk=tk)
        scratch = [pltpu.VMEM((VT_ROWS, tq), F32)]
    else:
        body = functools.partial(_latent_kernel, tk=tk)
        scratch = [pltpu.VMEM((LAT_SLOTS, tk, tq), F32), pltpu.VMEM((LAT_SLOTS, 1, tq), F32),
                   pltpu.VMEM((1, tq), F32), pltpu.VMEM((VT_ROWS, tq), F32)]
    return pl.pallas_call(
        body,
        grid=(batch, C_HEADS, nq),
        in_specs=[pl.BlockSpec((tq, LANES), lambda b, h, i: (b * nq + i, h)),
                  pl.BlockSpec((seq, LANES), lambda b, h, i: (b, h)),
                  pl.BlockSpec((seq // tk, VT_ROWS, tk), lambda b, h, i: (b, h, 0))],
        out_specs=pl.BlockSpec((tq, LANES), lambda b, h, i: (b * nq + i, h)),
        out_shape=jax.ShapeDtypeStruct((t, C_SLOTS), BF16),
        scratch_shapes=scratch,
        compiler_params=_cparams(("parallel", "parallel", "parallel")),
        name="latent_bounded" if bounded else "latent",
    )(qc, kc, vt)


def _merge_mlp_kernel(x_ref, oa1_ref, oa2_ref, oa3_ref, ml1_ref, ml2_ref, ml3_ref, ob_ref, oc_ref,
                      expand_ref, gout_ref, wout_ref, gmlp_ref, wup_ref, wdown_ref, out_ref, oa_s, ml_s):
    tm = x_ref.shape[0]
    oas, mls = [oa1_ref[...]], [ml1_ref[...]]
    for idx, (o_ref, l_ref) in enumerate(((oa2_ref, ml2_ref), (oa3_ref, ml3_ref))):
        r = A_CONFIGS[idx + 1][1]
        for c in range(r):
            for s in range(A_W // LANES):
                col = c * A_W + s * LANES
                oa_s[idx, s, pl.ds(c, tm // r, stride=r), :] = o_ref[:, col:col + LANES]
            ml_s[idx, pl.ds(c, tm // r, stride=r), :] = l_ref[:, c * LANES:(c + 1) * LANES]
        oas.append(jnp.concatenate([oa_s[idx, s] for s in range(A_W // LANES)], axis=1))
        mls.append(ml_s[idx])
    lane = lax.broadcasted_iota(jnp.int32, mls[0].shape, 1)
    m_all = jnp.maximum(jnp.maximum(mls[0], mls[1]), mls[2])
    ws = [pltpu.roll(ml, LANES - ML_L_OFFSET, 1) * jnp.exp2(ml - m_all) for ml in mls]
    wsum = ws[0] + ws[1] + ws[2]
    packed = jnp.zeros_like(wsum)
    for c, w in enumerate(ws):
        wn = jnp.where(lane < A_HEADS, w / wsum, 0.0)
        packed = packed + (wn if c == 0 else pltpu.roll(wn, ML_L_OFFSET * c, 1))
    hi = packed.astype(BF16)
    lo = (packed - hi.astype(F32)).astype(BF16)
    spread = (jnp.dot(hi, expand_ref[...], preferred_element_type=F32)
              + jnp.dot(lo, expand_ref[...], preferred_element_type=F32))
    oa = spread[:, :A_W] * oas[0] + spread[:, A_W:2 * A_W] * oas[1] + spread[:, 2 * A_W:] * oas[2]

    def group_norm(v, width):
        return v * lax.rsqrt(jnp.sum(v * v, axis=-1, keepdims=True) * (1.0 / width) + EPS)

    ob = ob_ref[...].astype(F32)
    oc_wide = oc_ref[...].astype(F32)
    half = lax.broadcasted_iota(jnp.int32, (tm, LANES), 1) < C_VDIM
    oc = jnp.concatenate(
        [jnp.where(half, oc_wide[:, 2 * p * LANES:(2 * p + 1) * LANES],
                   pltpu.roll(oc_wide[:, (2 * p + 1) * LANES:(2 * p + 2) * LANES], C_VDIM, 1))
         for p in range(C_HEADS // 2)], axis=1)
    mixed = jnp.concatenate([group_norm(oa, A_W), group_norm(ob, B_QW), group_norm(oc, C_HEADS * C_VDIM)], axis=1)
    mixed = (mixed * gout_ref[...]).astype(BF16)
    x = x_ref[...] + jnp.dot(mixed, wout_ref[...], preferred_element_type=F32)

    h = (x * lax.rsqrt(jnp.mean(x * x, axis=-1, keepdims=True) + EPS) * gmlp_ref[...]).astype(BF16)
    acc = x
    for s in range(0, D_FF, FF_CHUNK):
        u = jnp.dot(h, wup_ref[:, s:s + FF_CHUNK], preferred_element_type=F32)
        u = jnp.square(jnp.maximum(u, 0.0)).astype(BF16)
        acc = acc + jnp.dot(u, wdown_ref[s:s + FF_CHUNK, :], preferred_element_type=F32)
    out_ref[...] = acc


def _merge_mlp(x2d, oas, mls, ob, oc, p):
    t = x2d.shape[0]
    tm = TM_MLP
    row = lambda w: pl.BlockSpec((tm, w), lambda i: (i, 0))
    strided = lambda w, r: pl.BlockSpec((tm // r, r * w), lambda i: (i, 0))
    single = lambda shape: pl.BlockSpec(shape, lambda i: (0,) * len(shape), pipeline_mode=pl.Buffered(1))
    return pl.pallas_call(
        _merge_mlp_kernel,
        grid=(t // tm,),
        in_specs=[row(D_MODEL)] + [strided(A_W, r) for _, r in A_CONFIGS]
        + [strided(LANES, r) for _, r in A_CONFIGS] + [row(B_QW), row(C_SLOTS),
                  single((LANES, 3 * A_W)), single((1, MIX_COLS)), single((MIX_COLS, D_MODEL)),
                  single((1, D_MODEL)), single((D_MODEL, D_FF)), single((D_FF, D_MODEL))],
        out_specs=row(D_MODEL),
        out_shape=jax.ShapeDtypeStruct((t, D_MODEL), F32),
        scratch_shapes=[pltpu.VMEM((2, A_W // LANES, tm, LANES), F32), pltpu.VMEM((2, tm, LANES), F32)],
        compiler_params=_cparams(("parallel",)),
        name="merge_mlp",
    )(x2d, *oas, *mls, ob, oc, p["expand"], p["gout"], p["w_out"], p["gmlp"], p["w_up"], p["w_down"])


def _block_diag_ones(group):
    idx = np.arange(2 * LANES) // group
    return jnp.asarray((idx[:, None] == idx[None, :]).astype(np.float32), dtype=BF16)


def _expand_matrix():
    e = np.zeros((LANES, len(A_CONFIGS) * A_W), np.float32)
    for c in range(len(A_CONFIGS)):
        for h in range(A_HEADS):
            e[ML_L_OFFSET * c + h, c * A_W + h * HEAD_DIM:c * A_W + (h + 1) * HEAD_DIM] = 1.0
    return jnp.asarray(e, dtype=BF16)


def _pad_heads(w, heads, used, lo=0):
    lead = w.shape[:-1]
    w = w.reshape(lead + (heads, used))
    pad = [(0, 0)] * len(lead) + [(0, 0), (lo, LANES - lo - used)]
    return jnp.pad(w, pad).reshape(lead + (heads * LANES,))


def _layer_params(i, norm_mix, w_in, qk_gain_a, qk_gain_b, q_lat_gain, kv_lat_gain, w_uq, w_ukv, qk_gain_c,
                  out_norm, w_out, norm_mlp, w_up, w_down):
    w = w_in[i]
    o = np.cumsum((A_W, A_W, A_W, B_QW, B_KVW, B_KVW, C_Q_RANK, C_KV_RANK)).tolist()
    qb = w[:, o[2]:o[3]].reshape(D_MODEL, B_HEADS, HEAD_DIM)[:, B_HEAD_ORDER, :].reshape(D_MODEL, B_QW)
    kr = _pad_heads(w[:, o[7]:], 1, C_ROPE, lo=C_NOPE)
    w_in_p = jnp.concatenate([w[:, :o[2]], qb, w[:, o[3]:o[7]], kr], axis=1).astype(BF16)

    scale = HEAD_DIM ** -0.5 * LOG2E
    gqk = jnp.concatenate([jnp.tile(qk_gain_a[i, 0], A_HEADS) * scale, jnp.tile(qk_gain_a[i, 1], A_HEADS),
                           jnp.tile(qk_gain_b[i, 0], B_HEADS) * scale, jnp.tile(qk_gain_b[i, 1], B_KV_HEADS)])
    ukv = w_ukv[i].reshape(C_KV_RANK, C_HEADS, C_NOPE + C_VDIM)
    wuk = _pad_heads(ukv[:, :, :C_NOPE].reshape(C_KV_RANK, -1), C_HEADS, C_NOPE)
    wuvt = ukv[:, :, C_NOPE:].reshape(C_KV_RANK, C_HEADS * C_VDIM).T
    g = out_norm[i]
    gb = g[A_W:A_W + B_QW].reshape(B_HEADS, HEAD_DIM)[B_HEAD_ORDER, :].reshape(B_QW)
    gout = jnp.concatenate([g[:A_W], gb, g[A_W + B_QW:]])
    wo = w_out[i]
    wob = wo[A_W:A_W + B_QW].reshape(B_HEADS, HEAD_DIM, D_MODEL)[B_HEAD_ORDER, :, :].reshape(B_QW, D_MODEL)
    w_out_p = jnp.concatenate([wo[:A_W], wob, wo[A_W + B_QW:]], axis=0).astype(BF16)
    bound = (C_QK ** 0.5 * LOG2E * BOUND_MARGIN) * jnp.max(jnp.abs(qk_gain_c[i, 0])) * jnp.max(jnp.abs(qk_gain_c[i, 1]))
    return {
        "gmix": norm_mix[i][None, :],
        "w_in": w_in_p,
        "gqk": gqk[None, :],
        "ones64": _block_diag_ones(HEAD_DIM),
        "ones128": _block_diag_ones(LANES),
        "glq": q_lat_gain[i][None, :],
        "wuq": _pad_heads(w_uq[i], C_HEADS, C_QK).astype(BF16),
        "gqc": jnp.tile(_pad_heads(qk_gain_c[i, 0] * (C_QK ** -0.5 * LOG2E), 1, C_QK), C_HEADS)[None, :],
        "glkv": kv_lat_gain[i][None, :],
        "wuk": wuk.astype(BF16),
        "wuvt": wuvt.astype(BF16),
        "gkc": jnp.tile(_pad_heads(qk_gain_c[i, 1], 1, C_QK), C_HEADS)[None, :],
        "qshift": jnp.tile(_pad_heads(-bound[None], 1, 1, lo=C_QK), C_HEADS)[None, :],
        "kone": jnp.tile(_pad_heads(jnp.ones((1,), F32), 1, 1, lo=C_QK), C_HEADS)[None, :],
        "logit_bound": bound,
        "expand": _expand_matrix(),
        "gout": gout[None, :],
        "w_out": w_out_p,
        "gmlp": norm_mlp[i][None, :],
        "w_up": w_up[i].astype(BF16),
        "w_down": w_down[i].astype(BF16),
    }


def kernel(x, positions, rel_bias_table, norm_mix, w_in, qk_gain_a, qk_gain_b, sink_b, q_lat_gain, kv_lat_gain,
           w_uq, w_ukv, qk_gain_c, out_norm, w_out, norm_mlp, w_up, w_down):
    batch, seq, _ = x.shape
    depth = w_in.shape[0]
    x2d = x.reshape(batch * seq, D_MODEL)
    cos_t, sin_t = _rope_tables(positions)
    a_pairs = tuple((2 * p, 2 * p + 1) for p in range(A_HEADS // 2))
    pair_major = lambda tiles: tiles.reshape(tiles.shape[0] // 2, 2 * SUB, tiles.shape[2])
    bias_a = [pair_major(_bias_tiles(rel_bias_table, window // (2 * r), r, tuple(range(A_HEADS))))
              for window, r in A_CONFIGS]
    bias_b = pair_major(_bias_tiles(rel_bias_table, B_RADIUS, 1, tuple(A_HEADS + h for h in B_HEAD_ORDER)))
    for i in range(depth):
        p = _layer_params(i, norm_mix, w_in, qk_gain_a, qk_gain_b, q_lat_gain, kv_lat_gain, w_uq, w_ukv,
                          qk_gain_c, out_norm, w_out, norm_mlp, w_up, w_down)
        qkv1, qkv4, qkv16, qb, kb, vb, qc, kc, vt = _inproj(x2d, cos_t, sin_t, p)
        oas, mls = [], []
        for (window, r), bias, qkv in zip(A_CONFIGS, bias_a, (qkv1, qkv4, qkv16)):
            o, ml = _banded((qkv, 3, 0), (qkv, 3, 1), (qkv, 3, 2), bias, batch=batch, seq=seq, dilation=r,
                            radius=window // (2 * r), widths=(A_W, A_W), head_ids=a_pairs, kv_slab=(0, 1, 2),
                            sink=None, emit_ml=True, out_dtype=F32)
            oas.append(o)
            mls.append(ml)
        (ob,) = _banded((qb, 1, 0), (kb, 1, 0), (vb, 1, 0), bias_b, batch=batch, seq=seq, dilation=1,
                        radius=B_RADIUS, widths=(B_QW, B_KVW), head_ids=((0, 2), (1, 3)), kv_slab=(0, 0),
                        sink=sink_b[i] * LOG2E, emit_ml=False, out_dtype=BF16)
        oc = lax.cond(p["logit_bound"] <= MAX_LOGIT_BOUND,
                      functools.partial(_latent, batch=batch, seq=seq, bounded=True),
                      functools.partial(_latent, batch=batch, seq=seq, bounded=False), qc, kc, vt)
        x2d = _merge_mlp(x2d, oas, mls, ob, oc, p)
    return x2d.reshape(batch, seq, D_MODEL)
```

```python
import functools
import math

import numpy as np
import jax
import jax.numpy as jnp
from jax import lax
from jax.experimental import pallas as pl
from jax.experimental.pallas import tpu as pltpu

F32 = jnp.float32
BF16 = jnp.bfloat16

D_MODEL = 1024
HEAD_DIM = 64
A_HEADS = 6
A_CONFIGS = ((128, 1), (512, 4), (2048, 16))
B_HEADS = 4
B_KV_HEADS = 2
B_RADIUS = 128
C_HEADS = 6
C_NOPE = 64
C_ROPE = 32
C_VDIM = 64
C_QK = C_NOPE + C_ROPE
C_Q_RANK = 256
C_KV_RANK = 128
ROPE_THETA = 10000.0
N_BUCKETS = 32
MAX_DISTANCE = 1024
D_FF = 4 * D_MODEL
EPS = 1e-6
NEG = -1e30

A_W = A_HEADS * HEAD_DIM
B_QW = B_HEADS * HEAD_DIM
B_KVW = B_KV_HEADS * HEAD_DIM
LANES = 128
C_SLOTS = C_HEADS * LANES
IN_COLS = 3 * A_W + B_QW + 2 * B_KVW + C_Q_RANK + C_KV_RANK + LANES
MIX_COLS = A_W + B_QW + C_HEADS * C_VDIM
VT_ROWS = 80
LOG2E = math.log2(math.e)
ML_L_OFFSET = 8

B_HEAD_ORDER = (0, 2, 1, 3)

TM_IN = 512
TQ_BAND = 512
SUB = 128
TQ_LAT = 2048
BOUND_MARGIN = 1.02
MAX_LOGIT_BOUND = 50.0
LAT_BOUNDED_UNROLL = 16
LAT_SLOTS = 3
LAT_UNROLL = 6
TM_MLP = 512
FF_CHUNK = 1024
VMEM_LIMIT = 56 * 1024 * 1024


def _cparams(sem):
    return pltpu.CompilerParams(dimension_semantics=sem, vmem_limit_bytes=VMEM_LIMIT)


def _const_spec(shape):
    zeros = (0,) * len(shape)
    return pl.BlockSpec(shape, lambda *_: zeros)


def _rope_table_kernel(pos_ref, inv_ref, cos_ref, sin_ref):
    ang = pos_ref[...] * inv_ref[...]
    lane = lax.broadcasted_iota(jnp.int32, ang.shape, 1)
    c = jnp.cos(ang)
    s = jnp.sin(ang)
    first = (lane >= C_NOPE) & (lane < C_NOPE + C_ROPE // 2)
    second = (lane >= C_NOPE + C_ROPE // 2) & (lane < C_QK)
    cos_ref[...] = jnp.where(first | second, c, 1.0)
    sin_ref[...] = jnp.where(first, -s, jnp.where(second, s, 0.0))


def _rope_tables(positions):
    t = positions.size
    half = C_ROPE // 2
    inv = ROPE_THETA ** (-jnp.arange(half, dtype=F32) / half)
    inv_row = jnp.concatenate([jnp.zeros((C_NOPE,), F32), inv, inv, jnp.zeros((LANES - C_QK,), F32)])[None, :]
    pos = positions.astype(F32).reshape(t, 1)
    tm = 2048
    return pl.pallas_call(
        _rope_table_kernel,
        grid=(t // tm,),
        in_specs=[pl.BlockSpec((tm, 1), lambda i: (i, 0)), _const_spec((1, LANES))],
        out_specs=[pl.BlockSpec((tm, LANES), lambda i: (i, 0))] * 2,
        out_shape=[jax.ShapeDtypeStruct((t, LANES), F32)] * 2,
        compiler_params=_cparams(("parallel",)),
        name="rope_tables",
    )(pos, inv_row)


def _bucket_thresholds():
    half = N_BUCKETS // 2
    exact = half // 2
    n = np.arange(1, 2 * MAX_DISTANCE + 2, dtype=np.float64)
    far = exact + (np.log(n / exact) / math.log(MAX_DISTANCE / exact) * (half - exact)).astype(np.int64)
    far = np.minimum(far, half - 1)
    return tuple(int(n[np.argmax(far >= exact + k)]) for k in range(1, half - exact))


def _bias_kernel(table_ref, shift_ref, out_ref, *, radius, dilation, head_cols):
    hsel = pl.program_id(0)
    width = SUB + 2 * radius
    row = lax.broadcasted_iota(jnp.int32, (SUB, width), 0)
    col = lax.broadcasted_iota(jnp.int32, (SUB, width), 1)
    rel = col - radius - row
    n = jnp.abs(rel) * dilation
    half = N_BUCKETS // 2
    exact = half // 2
    far = jnp.full(n.shape, exact, jnp.int32)
    for thr in _bucket_thresholds():
        far = far + (n >= thr).astype(jnp.int32)
    bucket = jnp.where(rel > 0, half, 0) + jnp.where(n < exact, n, far)
    for idx, hc in enumerate(head_cols):
        @pl.when(hsel == idx)
        def _(hc=hc):
            val = jnp.zeros(n.shape, F32)
            for b in range(N_BUCKETS):
                val = jnp.where(bucket == b, table_ref[b, hc], val)
            out_ref[...] = jnp.where(jnp.abs(rel) <= radius, val * LOG2E - shift_ref[0], NEG)


def _bias_tiles(table, shift, radius, dilation, head_cols):
    width = SUB + 2 * radius
    tiles = pl.pallas_call(
        functools.partial(_bias_kernel, radius=radius, dilation=dilation, head_cols=head_cols),
        grid=(len(head_cols),),
        in_specs=[pl.BlockSpec(memory_space=pltpu.SMEM), pl.BlockSpec(memory_space=pltpu.SMEM)],
        out_specs=pl.BlockSpec((None, SUB, width), lambda h: (h, 0, 0)),
        out_shape=jax.ShapeDtypeStruct((len(head_cols), SUB, width), F32),
        compiler_params=_cparams(("arbitrary",)),
        name="bias_tiles",
    )(table, shift)
    return tiles.reshape(len(head_cols) // 2, 2 * SUB, width)


def _group_mean_sq(y, ones_ref, group):
    sq = (y * y).astype(BF16)
    width = y.shape[1]
    parts = []
    for s in range(0, width, 2 * LANES):
        w = min(2 * LANES, width - s)
        parts.append(jnp.dot(sq[:, s:s + w], ones_ref[:w, :w], preferred_element_type=F32))
    out = parts[0] if len(parts) == 1 else jnp.concatenate(parts, axis=1)
    return out * (1.0 / group)


def _slab_roll(y, shift):
    parts = [pltpu.roll(y[:, s:s + LANES], shift, 1) for s in range(0, y.shape[1], LANES)]
    return parts[0] if len(parts) == 1 else jnp.concatenate(parts, axis=1)


def _rope(y, cos_t, sin_t, lane):
    swapped = jnp.where(lane < C_NOPE + C_ROPE // 2, _slab_roll(y, LANES - C_ROPE // 2), _slab_roll(y, C_ROPE // 2))
    return y * cos_t + swapped * sin_t


def _inproj_kernel(x_ref, cos_ref, sin_ref, gmix_ref, w_ref, gqk_ref, ones64_ref, ones128_ref,
                   glq_ref, wuq_ref, gqc_ref, glkv_ref, wuk_ref, wuvt_ref, gkc_ref, qshift_ref, kone_ref,
                   qkv1_ref, qkv4_ref, qkv16_ref, qb_ref, kb_ref, vb_ref, qc_ref, kc_ref, vt_ref, stage_ref, stage4_ref):
    x = x_ref[...]
    h = x * lax.rsqrt(jnp.mean(x * x, axis=-1, keepdims=True) + EPS) * gmix_ref[...]
    y = jnp.dot(h.astype(BF16), w_ref[...], preferred_element_type=F32)

    o_qb = 3 * A_W
    o_kb = o_qb + B_QW
    o_vb = o_kb + B_KVW
    o_cq = o_vb + B_KVW
    o_ckv = o_cq + C_Q_RANK
    o_kr = o_ckv + C_KV_RANK
    yn = jnp.concatenate([y[:, :2 * A_W], y[:, o_qb:o_vb]], axis=1)
    yn = yn * lax.rsqrt(_group_mean_sq(yn, ones64_ref, HEAD_DIM) + EPS) * gqk_ref[...]
    qb_ref[...] = yn[:, 2 * A_W:2 * A_W + B_QW].astype(BF16)
    kb_ref[...] = yn[:, 2 * A_W + B_QW:].astype(BF16)
    n_slabs, tm = stage_ref.shape[0], stage_ref.shape[1]
    for s in range(n_slabs):
        src = yn if s * LANES < 2 * A_W else y
        stage_ref[s] = src[:, s * LANES:(s + 1) * LANES]
    for s in range(n_slabs):
        qkv1_ref[:, s * LANES:(s + 1) * LANES] = stage_ref[s].astype(BF16)
        for c4 in range(4):
            rows = stage_ref[s, pl.ds(c4, tm // 4, stride=4), :]
            qkv4_ref[:, c4 * 3 * A_W + s * LANES:c4 * 3 * A_W + (s + 1) * LANES] = rows.astype(BF16)
            stage4_ref[c4 * n_slabs + s] = rows
    for c16 in range(16):
        for s in range(n_slabs):
            rows = stage4_ref[(c16 % 4) * n_slabs + s, pl.ds(c16 // 4, tm // 16, stride=4), :]
            qkv16_ref[:, c16 * 3 * A_W + s * LANES:c16 * 3 * A_W + (s + 1) * LANES] = rows.astype(BF16)
    vb_ref[...] = y[:, o_vb:o_cq].astype(BF16)

    cos_t = jnp.concatenate([cos_ref[...]] * C_HEADS, axis=1)
    sin_t = jnp.concatenate([sin_ref[...]] * C_HEADS, axis=1)
    lane = lax.broadcasted_iota(jnp.int32, cos_t.shape, 1) % LANES
    cq = y[:, o_cq:o_ckv]
    cq = cq * lax.rsqrt(jnp.mean(cq * cq, axis=-1, keepdims=True) + EPS) * glq_ref[...]
    qc = jnp.dot(cq.astype(BF16), wuq_ref[...], preferred_element_type=F32)
    qc = qc * lax.rsqrt(_group_mean_sq(qc, ones128_ref, C_QK) + EPS) * gqc_ref[...]
    qc_ref[...] = (_rope(qc, cos_t, sin_t, lane) + qshift_ref[...]).astype(BF16)

    ckv = y[:, o_ckv:o_kr]
    ckv = (ckv * lax.rsqrt(jnp.mean(ckv * ckv, axis=-1, keepdims=True) + EPS) * glkv_ref[...]).astype(BF16)
    kr = y[:, o_kr:]
    kc = jnp.dot(ckv, wuk_ref[...], preferred_element_type=F32) + jnp.concatenate([kr] * C_HEADS, axis=1)
    kc = kc * lax.rsqrt(_group_mean_sq(kc, ones128_ref, C_QK) + EPS) * gkc_ref[...]
    kc_ref[...] = (_rope(kc, cos_t, sin_t, lane) + kone_ref[...]).astype(BF16)
    vt = lax.dot_general(wuvt_ref[...], ckv, (((1,), (1,)), ((), ())), preferred_element_type=F32)
    row = lax.broadcasted_iota(jnp.int32, (VT_ROWS - C_VDIM, vt.shape[1]), 0)
    ones_rows = jnp.where(row == 0, 1.0, 0.0).astype(BF16)
    for hd in range(C_HEADS):
        vt_ref[0, hd * VT_ROWS:hd * VT_ROWS + C_VDIM, :] = vt[hd * C_VDIM:(hd + 1) * C_VDIM].astype(BF16)
        vt_ref[0, hd * VT_ROWS + C_VDIM:(hd + 1) * VT_ROWS, :] = ones_rows


def _inproj(x2d, cos_t, sin_t, p):
    t = x2d.shape[0]
    tm = TM_IN
    row = lambda w: pl.BlockSpec((tm, w), lambda i: (i, 0))
    outs = [(B_QW, BF16), (B_KVW, BF16), (B_KVW, BF16)] + [(C_SLOTS, BF16)] * 2
    a_specs = [pl.BlockSpec((tm // r, r * 3 * A_W), lambda i: (i, 0)) for _, r in A_CONFIGS]
    a_shapes = [jax.ShapeDtypeStruct((t // r, r * 3 * A_W), BF16) for _, r in A_CONFIGS]
    return pl.pallas_call(
        _inproj_kernel,
        grid=(t // tm,),
        in_specs=[row(D_MODEL), row(LANES), row(LANES),
                  _const_spec((1, D_MODEL)), _const_spec((D_MODEL, IN_COLS)),
                  _const_spec((1, 2 * A_W + B_QW + B_KVW)),
                  _const_spec((2 * LANES, 2 * LANES)), _const_spec((2 * LANES, 2 * LANES)),
                  _const_spec((1, C_Q_RANK)), _const_spec((C_Q_RANK, C_SLOTS)), _const_spec((1, C_SLOTS)),
                  _const_spec((1, C_KV_RANK)), _const_spec((C_KV_RANK, C_SLOTS)),
                  _const_spec((C_HEADS * C_VDIM, C_KV_RANK)), _const_spec((1, C_SLOTS)),
                  _const_spec((1, C_SLOTS)), _const_spec((1, C_SLOTS))],
        out_specs=a_specs + [row(w) for w, _ in outs]
        + [pl.BlockSpec((1, C_HEADS * VT_ROWS, tm), lambda i: (i, 0, 0))],
        out_shape=a_shapes + [jax.ShapeDtypeStruct((t, w), d) for w, d in outs]
        + [jax.ShapeDtypeStruct((t // tm, C_HEADS * VT_ROWS, tm), BF16)],
        scratch_shapes=[pltpu.VMEM((3 * A_W // LANES, tm, LANES), F32),
                        pltpu.VMEM((4 * 3 * A_W // LANES, tm // 4, LANES), F32)],
        compiler_params=_cparams(("parallel",)),
        name="inproj",
    )(x2d, cos_t, sin_t, p["gmix"], p["w_in"], p["gqk"], p["ones64"], p["ones128"],
      p["glq"], p["wuq"], p["gqc"], p["glkv"], p["wuk"], p["wuvt"], p["gkc"], p["qshift"], p["kone"])


def _window(lo_ref, main_ref, hi_ref, a, radius, tq):
    start, end = a - radius, a + SUB + radius
    parts = []
    if start < 0:
        parts.append(lo_ref[radius + start:radius, :])
        start = 0
    parts.append(main_ref[start:min(end, tq), :])
    if end > tq:
        parts.append(hi_ref[0:end - tq, :])
    return parts[0] if len(parts) == 1 else jnp.concatenate(parts, axis=0)


def _banded_kernel(scal_ref, q_ref, klo_ref, k_ref, khi_ref, vlo_ref, v_ref, vhi_ref, bias_ref, *refs,
                   tq, radius, head_ids, kv_slab, has_sink, emit_ml, bounded, n_tiles):
    o_ref = refs[0]
    ml_ref = refs[1] if emit_ml else None
    p_scr = refs[-1]
    if not bounded:
        s_scr, m_scr = refs[-3:-1]
    sink = lambda hd: scal_ref[1 + hd]
    tile = pl.program_id(2)
    width = SUB + 2 * radius
    lane = lax.broadcasted_iota(jnp.int32, (SUB, LANES), 1)
    col = lax.broadcasted_iota(jnp.int32, (2 * SUB, width), 1)
    low = lane < HEAD_DIM
    subs = list(range(0, tq, SUB))
    slab = lambda w, pair: w[:, kv_slab[pair] * LANES:(kv_slab[pair] + 1) * LANES]
    tiles = [(a, pair) for a in subs for pair in range(len(head_ids))]

    kws = {a: _window(klo_ref, k_ref, khi_ref, a, radius, tq) for a in subs}
    for g, (a, pair) in enumerate(tiles):
        qs = q_ref[a:a + SUB, pair * LANES:(pair + 1) * LANES]
        zero = jnp.zeros_like(qs)
        q2 = jnp.concatenate([jnp.where(low, qs, zero), jnp.where(low, zero, qs)], axis=0)
        s = lax.dot_general(q2, slab(kws[a], pair), (((1,), (1,)), ((), ())), preferred_element_type=F32)
        s = s + bias_ref[pair]
        if a == 0:
            s = jnp.where(col < jnp.where(tile == 0, radius, 0), NEG, s)
        if a == tq - SUB:
            s = jnp.where(col >= jnp.where(tile == n_tiles - 1, SUB + radius, width), NEG, s)
        if bounded:
            p_scr[g] = jnp.exp2(s).astype(BF16)
            continue
        m = jnp.max(s, axis=-1, keepdims=True)
        if has_sink:
            sinks = jnp.concatenate([jnp.full((SUB, 1), sink(hd), F32) for hd in head_ids[pair]], axis=0)
            m = jnp.maximum(m, sinks)
        s_scr[g] = s
        m_scr[g] = jnp.broadcast_to(m, (2 * SUB, LANES))

    for g in range(0 if bounded else len(tiles)):
        m_wide = jnp.concatenate([m_scr[g]] * (width // LANES), axis=1)
        p_scr[g] = jnp.exp2(s_scr[g] - m_wide).astype(BF16)

    vws = {a: _window(vlo_ref, v_ref, vhi_ref, a, radius, tq) for a in subs}
    ml = None
    for g, (a, pair) in enumerate(tiles):
        hd0, hd1 = head_ids[pair]
        vs = slab(vws[a], pair)
        o2 = jnp.dot(p_scr[g], jnp.concatenate([vs, jnp.ones_like(vs)], axis=1), preferred_element_type=F32)
        if bounded:
            m0 = m1 = jnp.full((SUB, LANES), scal_ref[0], F32)
        else:
            m0, m1 = m_scr[g][:SUB], m_scr[g][SUB:]
        l0, l1 = o2[:SUB, LANES:], o2[SUB:, LANES:]
        if has_sink:
            l0 = l0 + jnp.exp2(sink(hd0) - m0)
            l1 = l1 + jnp.exp2(sink(hd1) - m1)
        out = jnp.where(low, o2[:SUB, :LANES], o2[SUB:, :LANES]) / jnp.where(low, l0, l1)
        o_ref[a:a + SUB, pair * LANES:(pair + 1) * LANES] = out.astype(o_ref.dtype)
        if emit_ml:
            if pair == 0:
                ml = jnp.where(lane < ML_L_OFFSET, m0, 0.0) if bounded else jnp.zeros((SUB, LANES), F32)
            if not bounded:
                ml = jnp.where(lane == hd0, m0, ml)
                ml = jnp.where(lane == hd1, m1, ml)
            ml = jnp.where(lane == ML_L_OFFSET + hd0, l0, ml)
            ml = jnp.where(lane == ML_L_OFFSET + hd1, l1, ml)
            if pair == len(head_ids) - 1:
                ml_ref[a:a + SUB, :] = ml


def _banded(q, k, v, bias, scal, *, batch, seq, dilation, radius, widths, head_ids, kv_slab, has_sink, emit_ml,
            bounded, out_dtype):
    n = seq // dilation
    tq = min(TQ_BAND, n)
    n_tiles = n // tq
    qw, kw = widths
    n_sub_tiles = (tq // SUB) * len(head_ids)
    view = lambda t: t[0].reshape(batch, n, t[0].shape[1])
    per_tile = tq // radius

    def main(w, t=(None, 1, 0)):
        return pl.BlockSpec((None, tq, w), lambda b, c, i: (b, i, t[1] * c + t[2]))

    def lo(t):
        return pl.BlockSpec((None, radius, kw),
                            lambda b, c, i: (b, jnp.maximum(i * per_tile - 1, 0), t[1] * c + t[2]))

    def hi(t):
        return pl.BlockSpec((None, radius, kw),
                            lambda b, c, i: (b, jnp.minimum((i + 1) * per_tile, n // radius - 1), t[1] * c + t[2]))

    in_specs = [pl.BlockSpec(memory_space=pltpu.SMEM),
                main(qw, q), lo(k), main(kw, k), hi(k), lo(v), main(kw, v), hi(v), _const_spec(bias.shape)]
    args = [scal, view(q), view(k), view(k), view(k), view(v), view(v), view(v), bias]
    out_specs = [main(qw)]
    out_shape = [jax.ShapeDtypeStruct((batch, n, dilation * qw), out_dtype)]
    if emit_ml:
        out_specs.append(main(LANES))
        out_shape.append(jax.ShapeDtypeStruct((batch, n, dilation * LANES), F32))
    outs = pl.pallas_call(
        functools.partial(_banded_kernel, tq=tq, radius=radius, head_ids=head_ids, kv_slab=kv_slab,
                          has_sink=has_sink, emit_ml=emit_ml, bounded=bounded, n_tiles=n_tiles),
        grid=(batch, dilation, n_tiles),
        in_specs=in_specs,
        out_specs=out_specs,
        out_shape=out_shape,
        scratch_shapes=([] if bounded else [pltpu.VMEM((n_sub_tiles, 2 * SUB, SUB + 2 * radius), F32),
                                            pltpu.VMEM((n_sub_tiles, 2 * SUB, LANES), F32)])
        + [pltpu.VMEM((n_sub_tiles, 2 * SUB, SUB + 2 * radius), BF16)],
        compiler_params=_cparams(("parallel", "parallel", "parallel")),
        name="banded%s_r%d_d%d" % ("_bounded" if bounded else "", radius, dilation),
    )(*args)
    return [o.reshape(batch * n, -1) for o in outs]


def _latent_kernel(q_ref, k_ref, vt_ref, o_ref, s_ref, cmax_ref, m_ref, acc_ref, *, tk):
    tq = q_ref.shape[0]
    nk = k_ref.shape[0] // tk
    ahead = LAT_SLOTS - 1
    qt = q_ref[...].astype(F32).T.astype(BF16)
    m_ref[...] = jnp.full(m_ref.shape, -jnp.inf, F32)
    acc_ref[...] = jnp.zeros(acc_ref.shape, F32)

    def scores(slot, j):
        start = pl.multiple_of(j * tk, tk)
        s = jnp.dot(k_ref[pl.ds(start, tk), :], qt, preferred_element_type=F32)
        s_ref[slot] = s
        cmax_ref[slot] = jnp.max(s, axis=0, keepdims=True)

    def consume(slot, j):
        m_old = m_ref[...]
        m_new = jnp.maximum(m_old, cmax_ref[slot])
        pr = jnp.exp2(s_ref[slot] - m_new).astype(BF16)
        acc_ref[...] = jnp.exp2(m_old - m_new) * acc_ref[...] + jnp.dot(
            vt_ref[j], pr, preferred_element_type=F32)
        m_ref[...] = m_new

    def step(j, u, with_scores):
        if with_scores:
            scores((u + ahead) % LAT_SLOTS, j + ahead)
        consume(u % LAT_SLOTS, j)

    for j in range(ahead):
        scores(j, j)
    trips = (nk - ahead) // LAT_UNROLL

    def body(jj, carry):
        for u in range(LAT_UNROLL):
            step(LAT_UNROLL * jj + u, u, True)
        return carry

    lax.fori_loop(0, trips, body, 0)
    for j in range(trips * LAT_UNROLL, nk):
        step(j, j, j + ahead < nk)
    acc = acc_ref[...]
    o = jnp.concatenate([acc[:C_VDIM] / acc[C_VDIM:C_VDIM + 1], jnp.zeros((LANES - C_VDIM, tq), F32)], axis=0)
    o_ref[...] = o.T.astype(o_ref.dtype)


def _latent_bounded_kernel(q_ref, k_ref, vt_ref, o_ref, acc_ref, *, tk):
    tq = q_ref.shape[0]
    nk = k_ref.shape[0] // tk
    qt = q_ref[...].astype(F32).T.astype(BF16)
    acc_ref[...] = jnp.zeros(acc_ref.shape, F32)

    def body(jj, carry):
        for u in range(LAT_BOUNDED_UNROLL):
            j = LAT_BOUNDED_UNROLL * jj + u
            start = pl.multiple_of(j * tk, tk)
            s = jnp.dot(k_ref[pl.ds(start, tk), :], qt, preferred_element_type=F32)
            acc_ref[...] += jnp.dot(vt_ref[j], jnp.exp2(s).astype(BF16), preferred_element_type=F32)
        return carry

    lax.fori_loop(0, nk // LAT_BOUNDED_UNROLL, body, 0)
    acc = acc_ref[...]
    o = jnp.concatenate([acc[:C_VDIM] / acc[C_VDIM:C_VDIM + 1], jnp.zeros((LANES - C_VDIM, tq), F32)], axis=0)
    o_ref[...] = o.T.astype(o_ref.dtype)


def _latent(qc, kc, vt, batch, seq, bounded):
    t = batch * seq
    tq, tk = TQ_LAT, vt.shape[2]
    nq = seq // tq
    if bounded:
        body = functools.partial(_latent_bounded_kernel, tk=tk)
        scratch = [pltpu.VMEM((VT_ROWS, tq), F32)]
    else:
        body = functools.partial(_latent_kernel, tk=tk)
        scratch = [pltpu.VMEM((LAT_SLOTS, tk, tq), F32), pltpu.VMEM((LAT_SLOTS, 1, tq), F32),
                   pltpu.VMEM((1, tq), F32), pltpu.VMEM((VT_ROWS, tq), F32)]
    return pl.pallas_call(
        body,
        grid=(batch, C_HEADS, nq),
        in_specs=[pl.BlockSpec((tq, LANES), lambda b, h, i: (b * nq + i, h)),
                  pl.BlockSpec((seq, LANES), lambda b, h, i: (b, h)),
                  pl.BlockSpec((seq // tk, VT_ROWS, tk), lambda b, h, i: (b, h, 0))],
        out_specs=pl.BlockSpec((tq, LANES), lambda b, h, i: (b * nq + i, h)),
        out_shape=jax.ShapeDtypeStruct((t, C_SLOTS), BF16),
        scratch_shapes=scratch,
        compiler_params=_cparams(("parallel", "parallel", "parallel")),
        name="latent_bounded" if bounded else "latent",
    )(qc, kc, vt)


def _merge_mlp_kernel(x_ref, oa1_ref, oa2_ref, oa3_ref, ml1_ref, ml2_ref, ml3_ref, ob_ref, oc_ref,
                      expand_ref, gout_ref, wout_ref, gmlp_ref, wup_ref, wdown_ref, out_ref, oa_s, ml_s):
    tm = x_ref.shape[0]
    oas, mls = [oa1_ref[...]], [ml1_ref[...]]
    for idx, (o_ref, l_ref) in enumerate(((oa2_ref, ml2_ref), (oa3_ref, ml3_ref))):
        r = A_CONFIGS[idx + 1][1]
        for c in range(r):
            for s in range(A_W // LANES):
                col = c * A_W + s * LANES
                oa_s[idx, s, pl.ds(c, tm // r, stride=r), :] = o_ref[:, col:col + LANES]
            ml_s[idx, pl.ds(c, tm // r, stride=r), :] = l_ref[:, c * LANES:(c + 1) * LANES]
        oas.append(jnp.concatenate([oa_s[idx, s] for s in range(A_W // LANES)], axis=1))
        mls.append(ml_s[idx])
    lane = lax.broadcasted_iota(jnp.int32, mls[0].shape, 1)
    m_all = jnp.maximum(jnp.maximum(mls[0], mls[1]), mls[2])
    ws = [pltpu.roll(ml, LANES - ML_L_OFFSET, 1) * jnp.exp2(ml - m_all) for ml in mls]
    wsum = ws[0] + ws[1] + ws[2]
    packed = jnp.zeros_like(wsum)
    for c, w in enumerate(ws):
        wn = jnp.where(lane < A_HEADS, w / wsum, 0.0)
        packed = packed + (wn if c == 0 else pltpu.roll(wn, ML_L_OFFSET * c, 1))
    hi = packed.astype(BF16)
    lo = (packed - hi.astype(F32)).astype(BF16)
    spread = (jnp.dot(hi, expand_ref[...], preferred_element_type=F32)
              + jnp.dot(lo, expand_ref[...], preferred_element_type=F32))
    oa = spread[:, :A_W] * oas[0] + spread[:, A_W:2 * A_W] * oas[1] + spread[:, 2 * A_W:] * oas[2]

    def group_norm(v, width):
        return v * lax.rsqrt(jnp.sum(v * v, axis=-1, keepdims=True) * (1.0 / width) + EPS)

    ob = ob_ref[...].astype(F32)
    oc_wide = oc_ref[...].astype(F32)
    half = lax.broadcasted_iota(jnp.int32, (tm, LANES), 1) < C_VDIM
    oc = jnp.concatenate(
        [jnp.where(half, oc_wide[:, 2 * p * LANES:(2 * p + 1) * LANES],
                   pltpu.roll(oc_wide[:, (2 * p + 1) * LANES:(2 * p + 2) * LANES], C_VDIM, 1))
         for p in range(C_HEADS // 2)], axis=1)
    mixed = jnp.concatenate([group_norm(oa, A_W), group_norm(ob, B_QW), group_norm(oc, C_HEADS * C_VDIM)], axis=1)
    mixed = (mixed * gout_ref[...]).astype(BF16)
    x = x_ref[...] + jnp.dot(mixed, wout_ref[...], preferred_element_type=F32)

    h = (x * lax.rsqrt(jnp.mean(x * x, axis=-1, keepdims=True) + EPS) * gmlp_ref[...]).astype(BF16)
    acc = x
    for s in range(0, D_FF, FF_CHUNK):
        u = jnp.dot(h, wup_ref[:, s:s + FF_CHUNK], preferred_element_type=F32)
        u = jnp.square(jnp.maximum(u, 0.0)).astype(BF16)
        acc = acc + jnp.dot(u, wdown_ref[s:s + FF_CHUNK, :], preferred_element_type=F32)
    out_ref[...] = acc


def _merge_mlp(x2d, oas, mls, ob, oc, p):
    t = x2d.shape[0]
    tm = TM_MLP
    row = lambda w: pl.BlockSpec((tm, w), lambda i: (i, 0))
    strided = lambda w, r: pl.BlockSpec((tm // r, r * w), lambda i: (i, 0))
    single = lambda shape: pl.BlockSpec(shape, lambda i: (0,) * len(shape), pipeline_mode=pl.Buffered(1))
    return pl.pallas_call(
        _merge_mlp_kernel,
        grid=(t // tm,),
        in_specs=[row(D_MODEL)] + [strided(A_W, r) for _, r in A_CONFIGS]
        + [strided(LANES, r) for _, r in A_CONFIGS] + [row(B_QW), row(C_SLOTS),
                  single((LANES, 3 * A_W)), single((1, MIX_COLS)), single((MIX_COLS, D_MODEL)),
                  single((1, D_MODEL)), single((D_MODEL, D_FF)), single((D_FF, D_MODEL))],
        out_specs=row(D_MODEL),
        out_shape=jax.ShapeDtypeStruct((t, D_MODEL), F32),
        scratch_shapes=[pltpu.VMEM((2, A_W // LANES, tm, LANES), F32), pltpu.VMEM((2, tm, LANES), F32)],
        compiler_params=_cparams(("parallel",)),
        name="merge_mlp",
    )(x2d, *oas, *mls, ob, oc, p["expand"], p["gout"], p["w_out"], p["gmlp"], p["w_up"], p["w_down"])


def _block_diag_ones(group):
    idx = np.arange(2 * LANES) // group
    return jnp.asarray((idx[:, None] == idx[None, :]).astype(np.float32), dtype=BF16)


def _expand_matrix():
    e = np.zeros((LANES, len(A_CONFIGS) * A_W), np.float32)
    for c in range(len(A_CONFIGS)):
        for h in range(A_HEADS):
            e[ML_L_OFFSET * c + h, c * A_W + h * HEAD_DIM:c * A_W + (h + 1) * HEAD_DIM] = 1.0
    return jnp.asarray(e, dtype=BF16)


def _pad_heads(w, heads, used, lo=0):
    lead = w.shape[:-1]
    w = w.reshape(lead + (heads, used))
    pad = [(0, 0)] * len(lead) + [(0, 0), (lo, LANES - lo - used)]
    return jnp.pad(w, pad).reshape(lead + (heads * LANES,))


def _layer_params(i, norm_mix, w_in, qk_gain_a, qk_gain_b, q_lat_gain, kv_lat_gain, w_uq, w_ukv, qk_gain_c,
                  out_norm, w_out, norm_mlp, w_up, w_down):
    w = w_in[i]
    o = np.cumsum((A_W, A_W, A_W, B_QW, B_KVW, B_KVW, C_Q_RANK, C_KV_RANK)).tolist()
    qb = w[:, o[2]:o[3]].reshape(D_MODEL, B_HEADS, HEAD_DIM)[:, B_HEAD_ORDER, :].reshape(D_MODEL, B_QW)
    kr = _pad_heads(w[:, o[7]:], 1, C_ROPE, lo=C_NOPE)
    w_in_p = jnp.concatenate([w[:, :o[2]], qb, w[:, o[3]:o[7]], kr], axis=1).astype(BF16)

    scale = HEAD_DIM ** -0.5 * LOG2E
    gqk = jnp.concatenate([jnp.tile(qk_gain_a[i, 0], A_HEADS) * scale, jnp.tile(qk_gain_a[i, 1], A_HEADS),
                           jnp.tile(qk_gain_b[i, 0], B_HEADS) * scale, jnp.tile(qk_gain_b[i, 1], B_KV_HEADS)])
    ukv = w_ukv[i].reshape(C_KV_RANK, C_HEADS, C_NOPE + C_VDIM)
    wuk = _pad_heads(ukv[:, :, :C_NOPE].reshape(C_KV_RANK, -1), C_HEADS, C_NOPE)
    wuvt = ukv[:, :, C_NOPE:].reshape(C_KV_RANK, C_HEADS * C_VDIM).T
    g = out_norm[i]
    gb = g[A_W:A_W + B_QW].reshape(B_HEADS, HEAD_DIM)[B_HEAD_ORDER, :].reshape(B_QW)
    gout = jnp.concatenate([g[:A_W], gb, g[A_W + B_QW:]])
    wo = w_out[i]
    wob = wo[A_W:A_W + B_QW].reshape(B_HEADS, HEAD_DIM, D_MODEL)[B_HEAD_ORDER, :, :].reshape(B_QW, D_MODEL)
    w_out_p = jnp.concatenate([wo[:A_W], wob, wo[A_W + B_QW:]], axis=0).astype(BF16)
    bound = (C_QK ** 0.5 * LOG2E * BOUND_MARGIN) * jnp.max(jnp.abs(qk_gain_c[i, 0])) * jnp.max(jnp.abs(qk_gain_c[i, 1]))
    return {
        "gmix": norm_mix[i][None, :],
        "w_in": w_in_p,
        "gqk": gqk[None, :],
        "ones64": _block_diag_ones(HEAD_DIM),
        "ones128": _block_diag_ones(LANES),
        "glq": q_lat_gain[i][None, :],
        "wuq": _pad_heads(w_uq[i], C_HEADS, C_QK).astype(BF16),
        "gqc": jnp.tile(_pad_heads(qk_gain_c[i, 0] * (C_QK ** -0.5 * LOG2E), 1, C_QK), C_HEADS)[None, :],
        "glkv": kv_lat_gain[i][None, :],
        "wuk": wuk.astype(BF16),
        "wuvt": wuvt.astype(BF16),
        "gkc": jnp.tile(_pad_heads(qk_gain_c[i, 1], 1, C_QK), C_HEADS)[None, :],
        "qshift": jnp.tile(_pad_heads(-bound[None], 1, 1, lo=C_QK), C_HEADS)[None, :],
        "kone": jnp.tile(_pad_heads(jnp.ones((1,), F32), 1, 1, lo=C_QK), C_HEADS)[None, :],
        "logit_bound": bound,
        "expand": _expand_matrix(),
        "gout": gout[None, :],
        "w_out": w_out_p,
        "gmlp": norm_mlp[i][None, :],
        "w_up": w_up[i].astype(BF16),
        "w_down": w_down[i].astype(BF16),
    }


def kernel(x, positions, rel_bias_table, norm_mix, w_in, qk_gain_a, qk_gain_b, sink_b, q_lat_gain, kv_lat_gain,
           w_uq, w_ukv, qk_gain_c, out_norm, w_out, norm_mlp, w_up, w_down):
    batch, seq, _ = x.shape
    depth = w_in.shape[0]
    x2d = x.reshape(batch * seq, D_MODEL)
    cos_t, sin_t = _rope_tables(positions)
    a_pairs = tuple((2 * p, 2 * p + 1) for p in range(A_HEADS // 2))
    b_pairs = (B_HEAD_ORDER[:2], B_HEAD_ORDER[2:])
    b_cols = tuple(A_HEADS + h for h in B_HEAD_ORDER)

    def bias_tables(shift):
        tiles_a = [_bias_tiles(rel_bias_table, shift, window // (2 * r), r, tuple(range(A_HEADS)))
                   for window, r in A_CONFIGS]
        return tiles_a, _bias_tiles(rel_bias_table, shift, B_RADIUS, 1, b_cols)

    gain_bound = lambda gains: jnp.max(jnp.max(jnp.abs(gains[:, 0]), axis=-1) * jnp.max(jnp.abs(gains[:, 1]), axis=-1))
    qk_bound = HEAD_DIM ** 0.5 * LOG2E * BOUND_MARGIN * jnp.maximum(gain_bound(qk_gain_a), gain_bound(qk_gain_b))
    band_shift = jnp.maximum(qk_bound + LOG2E * jnp.max(jnp.abs(rel_bias_table)), LOG2E * jnp.max(sink_b))
    shifted_tables = bias_tables(band_shift[None])

    def banded_all(bounded, qkv1, qkv4, qkv16, qb, kb, vb, sinks):
        shift = band_shift[None] if bounded else jnp.zeros((1,), F32)
        tables_a, table_b = shifted_tables if bounded else bias_tables(shift)
        scal = jnp.concatenate([shift, sinks * LOG2E])
        outs = []
        for (window, r), bias, qkv in zip(A_CONFIGS, tables_a, (qkv1, qkv4, qkv16)):
            outs += _banded((qkv, 3, 0), (qkv, 3, 1), (qkv, 3, 2), bias, scal, batch=batch, seq=seq, dilation=r,
                            radius=window // (2 * r), widths=(A_W, A_W), head_ids=a_pairs, kv_slab=(0, 1, 2),
                            has_sink=False, emit_ml=True, bounded=bounded, out_dtype=F32)
        outs += _banded((qb, 1, 0), (kb, 1, 0), (vb, 1, 0), table_b, scal, batch=batch, seq=seq, dilation=1,
                        radius=B_RADIUS, widths=(B_QW, B_KVW), head_ids=b_pairs, kv_slab=(0, 0),
                        has_sink=True, emit_ml=False, bounded=bounded, out_dtype=BF16)
        return tuple(outs)

    for i in range(depth):
        p = _layer_params(i, norm_mix, w_in, qk_gain_a, qk_gain_b, q_lat_gain, kv_lat_gain, w_uq, w_ukv,
                          qk_gain_c, out_norm, w_out, norm_mlp, w_up, w_down)
        qkv1, qkv4, qkv16, qb, kb, vb, qc, kc, vt = _inproj(x2d, cos_t, sin_t, p)
        o1, ml1, o4, ml4, o16, ml16, ob = lax.cond(
            band_shift <= MAX_LOGIT_BOUND, functools.partial(banded_all, True), functools.partial(banded_all, False),
            qkv1, qkv4, qkv16, qb, kb, vb, sink_b[i])
        oas, mls = [o1, o4, o16], [ml1, ml4, ml16]
        oc = lax.cond(p["logit_bound"] <= MAX_LOGIT_BOUND,
                      functools.partial(_latent, batch=batch, seq=seq, bounded=True),
                      functools.partial(_latent, batch=batch, seq=seq, bounded=False), qc, kc, vt)
        x2d = _merge_mlp(x2d, oas, mls, ob, oc, p)
    return x2d.reshape(batch, seq, D_MODEL)
```

```python
import functools
import math

import numpy as np
import jax
import jax.numpy as jnp
from jax import lax
from jax.experimental import pallas as pl
from jax.experimental.pallas import tpu as pltpu

F32 = jnp.float32
BF16 = jnp.bfloat16

D_MODEL = 1024
HEAD_DIM = 64
A_HEADS = 6
A_CONFIGS = ((128, 1), (512, 4), (2048, 16))
B_HEADS = 4
B_KV_HEADS = 2
B_RADIUS = 128
C_HEADS = 6
C_NOPE = 64
C_ROPE = 32
C_VDIM = 64
C_QK = C_NOPE + C_ROPE
C_Q_RANK = 256
C_KV_RANK = 128
ROPE_THETA = 10000.0
N_BUCKETS = 32
MAX_DISTANCE = 1024
D_FF = 4 * D_MODEL
EPS = 1e-6
NEG = -1e30

A_W = A_HEADS * HEAD_DIM
B_QW = B_HEADS * HEAD_DIM
B_KVW = B_KV_HEADS * HEAD_DIM
LANES = 128
C_SLOTS = C_HEADS * LANES
IN_COLS = 3 * A_W + B_QW + 2 * B_KVW + C_Q_RANK + C_KV_RANK + LANES
MIX_COLS = A_W + B_QW + C_HEADS * C_VDIM
VT_ROWS = 80
LOG2E = math.log2(math.e)
ML_L_OFFSET = 8

B_HEAD_ORDER = (0, 2, 1, 3)

TM_IN = 512
TQ_BAND = 1024
SUB = 128
TQ_LAT = 2048
BOUND_MARGIN = 1.02
MAX_LOGIT_BOUND = 50.0
LAT_BOUNDED_UNROLL = 16
LAT_SLOTS = 3
LAT_UNROLL = 6
TM_MLP = 512
FF_CHUNK = 1024
VMEM_LIMIT = 56 * 1024 * 1024


def _cparams(sem):
    return pltpu.CompilerParams(dimension_semantics=sem, vmem_limit_bytes=VMEM_LIMIT)


def _const_spec(shape):
    zeros = (0,) * len(shape)
    return pl.BlockSpec(shape, lambda *_: zeros)


def _rope_table_kernel(pos_ref, inv_ref, cos_ref, sin_ref):
    ang = pos_ref[...] * inv_ref[...]
    lane = lax.broadcasted_iota(jnp.int32, ang.shape, 1)
    c = jnp.cos(ang)
    s = jnp.sin(ang)
    first = (lane >= C_NOPE) & (lane < C_NOPE + C_ROPE // 2)
    second = (lane >= C_NOPE + C_ROPE // 2) & (lane < C_QK)
    cos_ref[...] = jnp.where(first | second, c, 1.0)
    sin_ref[...] = jnp.where(first, -s, jnp.where(second, s, 0.0))


def _rope_tables(positions):
    t = positions.size
    half = C_ROPE // 2
    inv = ROPE_THETA ** (-jnp.arange(half, dtype=F32) / half)
    inv_row = jnp.concatenate([jnp.zeros((C_NOPE,), F32), inv, inv, jnp.zeros((LANES - C_QK,), F32)])[None, :]
    pos = positions.astype(F32).reshape(t, 1)
    tm = 2048
    return pl.pallas_call(
        _rope_table_kernel,
        grid=(t // tm,),
        in_specs=[pl.BlockSpec((tm, 1), lambda i: (i, 0)), _const_spec((1, LANES))],
        out_specs=[pl.BlockSpec((tm, LANES), lambda i: (i, 0))] * 2,
        out_shape=[jax.ShapeDtypeStruct((t, LANES), F32)] * 2,
        compiler_params=_cparams(("parallel",)),
        name="rope_tables",
    )(pos, inv_row)


def _bucket_thresholds():
    half = N_BUCKETS // 2
    exact = half // 2
    n = np.arange(1, 2 * MAX_DISTANCE + 2, dtype=np.float64)
    far = exact + (np.log(n / exact) / math.log(MAX_DISTANCE / exact) * (half - exact)).astype(np.int64)
    far = np.minimum(far, half - 1)
    return tuple(int(n[np.argmax(far >= exact + k)]) for k in range(1, half - exact))


def _bias_kernel(table_ref, shift_ref, out_ref, *, radius, dilation, head_cols):
    hsel = pl.program_id(0)
    width = SUB + 2 * radius
    row = lax.broadcasted_iota(jnp.int32, (SUB, width), 0)
    col = lax.broadcasted_iota(jnp.int32, (SUB, width), 1)
    rel = col - radius - row
    n = jnp.abs(rel) * dilation
    half = N_BUCKETS // 2
    exact = half // 2
    far = jnp.full(n.shape, exact, jnp.int32)
    for thr in _bucket_thresholds():
        far = far + (n >= thr).astype(jnp.int32)
    bucket = jnp.where(rel > 0, half, 0) + jnp.where(n < exact, n, far)
    for idx, hc in enumerate(head_cols):
        @pl.when(hsel == idx)
        def _(hc=hc):
            val = jnp.zeros(n.shape, F32)
            for b in range(N_BUCKETS):
                val = jnp.where(bucket == b, table_ref[b, hc], val)
            out_ref[...] = jnp.where(jnp.abs(rel) <= radius, val * LOG2E - shift_ref[0], NEG)


def _bias_tiles(table, shift, radius, dilation, head_cols):
    width = SUB + 2 * radius
    tiles = pl.pallas_call(
        functools.partial(_bias_kernel, radius=radius, dilation=dilation, head_cols=head_cols),
        grid=(len(head_cols),),
        in_specs=[pl.BlockSpec(memory_space=pltpu.SMEM), pl.BlockSpec(memory_space=pltpu.SMEM)],
        out_specs=pl.BlockSpec((None, SUB, width), lambda h: (h, 0, 0)),
        out_shape=jax.ShapeDtypeStruct((len(head_cols), SUB, width), F32),
        compiler_params=_cparams(("arbitrary",)),
        name="bias_tiles",
    )(table, shift)
    return tiles.reshape(len(head_cols) // 2, 2 * SUB, width)


def _group_mean_sq(y, ones_ref, group):
    sq = (y * y).astype(BF16)
    width = y.shape[1]
    parts = []
    for s in range(0, width, 2 * LANES):
        w = min(2 * LANES, width - s)
        parts.append(jnp.dot(sq[:, s:s + w], ones_ref[:w, :w], preferred_element_type=F32))
    out = parts[0] if len(parts) == 1 else jnp.concatenate(parts, axis=1)
    return out * (1.0 / group)


def _slab_roll(y, shift):
    parts = [pltpu.roll(y[:, s:s + LANES], shift, 1) for s in range(0, y.shape[1], LANES)]
    return parts[0] if len(parts) == 1 else jnp.concatenate(parts, axis=1)


def _rope(y, cos_t, sin_t, lane):
    swapped = jnp.where(lane < C_NOPE + C_ROPE // 2, _slab_roll(y, LANES - C_ROPE // 2), _slab_roll(y, C_ROPE // 2))
    return y * cos_t + swapped * sin_t


def _inproj_kernel(x_ref, cos_ref, sin_ref, gmix_ref, w_ref, gqk_ref, ones64_ref, ones128_ref,
                   glq_ref, wuq_ref, gqc_ref, glkv_ref, wuk_ref, wuvt_ref, gkc_ref, qshift_ref, kone_ref,
                   qkv1_ref, qkv4_ref, qkv16_ref, qb_ref, kb_ref, vb_ref, qc_ref, kc_ref, vt_ref, stage_ref, stage4_ref):
    x = x_ref[...]
    h = x * lax.rsqrt(jnp.mean(x * x, axis=-1, keepdims=True) + EPS) * gmix_ref[...]
    y = jnp.dot(h.astype(BF16), w_ref[...], preferred_element_type=F32)

    o_qb = 3 * A_W
    o_kb = o_qb + B_QW
    o_vb = o_kb + B_KVW
    o_cq = o_vb + B_KVW
    o_ckv = o_cq + C_Q_RANK
    o_kr = o_ckv + C_KV_RANK
    yn = jnp.concatenate([y[:, :2 * A_W], y[:, o_qb:o_vb]], axis=1)
    yn = yn * lax.rsqrt(_group_mean_sq(yn, ones64_ref, HEAD_DIM) + EPS) * gqk_ref[...]
    qb_ref[...] = yn[:, 2 * A_W:2 * A_W + B_QW].astype(BF16)
    kb_ref[...] = yn[:, 2 * A_W + B_QW:].astype(BF16)
    n_slabs, tm = stage_ref.shape[0], stage_ref.shape[1]
    for s in range(n_slabs):
        src = yn if s * LANES < 2 * A_W else y
        stage_ref[s] = src[:, s * LANES:(s + 1) * LANES]
    for s in range(n_slabs):
        qkv1_ref[:, s * LANES:(s + 1) * LANES] = stage_ref[s].astype(BF16)
        for c4 in range(4):
            rows = stage_ref[s, pl.ds(c4, tm // 4, stride=4), :]
            qkv4_ref[:, c4 * 3 * A_W + s * LANES:c4 * 3 * A_W + (s + 1) * LANES] = rows.astype(BF16)
            stage4_ref[c4 * n_slabs + s] = rows
    for c16 in range(16):
        for s in range(n_slabs):
            rows = stage4_ref[(c16 % 4) * n_slabs + s, pl.ds(c16 // 4, tm // 16, stride=4), :]
            qkv16_ref[:, c16 * 3 * A_W + s * LANES:c16 * 3 * A_W + (s + 1) * LANES] = rows.astype(BF16)
    vb_ref[...] = y[:, o_vb:o_cq].astype(BF16)

    cos_t = jnp.concatenate([cos_ref[...]] * C_HEADS, axis=1)
    sin_t = jnp.concatenate([sin_ref[...]] * C_HEADS, axis=1)
    lane = lax.broadcasted_iota(jnp.int32, cos_t.shape, 1) % LANES
    cq = y[:, o_cq:o_ckv]
    cq = cq * lax.rsqrt(jnp.mean(cq * cq, axis=-1, keepdims=True) + EPS) * glq_ref[...]
    qc = jnp.dot(cq.astype(BF16), wuq_ref[...], preferred_element_type=F32)
    qc = qc * lax.rsqrt(_group_mean_sq(qc, ones128_ref, C_QK) + EPS) * gqc_ref[...]
    qc_ref[...] = (_rope(qc, cos_t, sin_t, lane) + qshift_ref[...]).astype(BF16)

    ckv = y[:, o_ckv:o_kr]
    ckv = (ckv * lax.rsqrt(jnp.mean(ckv * ckv, axis=-1, keepdims=True) + EPS) * glkv_ref[...]).astype(BF16)
    kr = y[:, o_kr:]
    kc = jnp.dot(ckv, wuk_ref[...], preferred_element_type=F32) + jnp.concatenate([kr] * C_HEADS, axis=1)
    kc = kc * lax.rsqrt(_group_mean_sq(kc, ones128_ref, C_QK) + EPS) * gkc_ref[...]
    kc_ref[...] = (_rope(kc, cos_t, sin_t, lane) + kone_ref[...]).astype(BF16)
    vt = lax.dot_general(wuvt_ref[...], ckv, (((1,), (1,)), ((), ())), preferred_element_type=F32)
    row = lax.broadcasted_iota(jnp.int32, (VT_ROWS - C_VDIM, vt.shape[1]), 0)
    ones_rows = jnp.where(row == 0, 1.0, 0.0).astype(BF16)
    for hd in range(C_HEADS):
        vt_ref[0, hd * VT_ROWS:hd * VT_ROWS + C_VDIM, :] = vt[hd * C_VDIM:(hd + 1) * C_VDIM].astype(BF16)
        vt_ref[0, hd * VT_ROWS + C_VDIM:(hd + 1) * VT_ROWS, :] = ones_rows


def _inproj(x2d, cos_t, sin_t, p):
    t = x2d.shape[0]
    tm = TM_IN
    row = lambda w: pl.BlockSpec((tm, w), lambda i: (i, 0))
    outs = [(B_QW, BF16), (B_KVW, BF16), (B_KVW, BF16)] + [(C_SLOTS, BF16)] * 2
    a_specs = [pl.BlockSpec((tm // r, r * 3 * A_W), lambda i: (i, 0)) for _, r in A_CONFIGS]
    a_shapes = [jax.ShapeDtypeStruct((t // r, r * 3 * A_W), BF16) for _, r in A_CONFIGS]
    return pl.pallas_call(
        _inproj_kernel,
        grid=(t // tm,),
        in_specs=[row(D_MODEL), row(LANES), row(LANES),
                  _const_spec((1, D_MODEL)), _const_spec((D_MODEL, IN_COLS)),
                  _const_spec((1, 2 * A_W + B_QW + B_KVW)),
                  _const_spec((2 * LANES, 2 * LANES)), _const_spec((2 * LANES, 2 * LANES)),
                  _const_spec((1, C_Q_RANK)), _const_spec((C_Q_RANK, C_SLOTS)), _const_spec((1, C_SLOTS)),
                  _const_spec((1, C_KV_RANK)), _const_spec((C_KV_RANK, C_SLOTS)),
                  _const_spec((C_HEADS * C_VDIM, C_KV_RANK)), _const_spec((1, C_SLOTS)),
                  _const_spec((1, C_SLOTS)), _const_spec((1, C_SLOTS))],
        out_specs=a_specs + [row(w) for w, _ in outs]
        + [pl.BlockSpec((1, C_HEADS * VT_ROWS, tm), lambda i: (i, 0, 0))],
        out_shape=a_shapes + [jax.ShapeDtypeStruct((t, w), d) for w, d in outs]
        + [jax.ShapeDtypeStruct((t // tm, C_HEADS * VT_ROWS, tm), BF16)],
        scratch_shapes=[pltpu.VMEM((3 * A_W // LANES, tm, LANES), F32),
                        pltpu.VMEM((4 * 3 * A_W // LANES, tm // 4, LANES), F32)],
        compiler_params=_cparams(("parallel",)),
        name="inproj",
    )(x2d, cos_t, sin_t, p["gmix"], p["w_in"], p["gqk"], p["ones64"], p["ones128"],
      p["glq"], p["wuq"], p["gqc"], p["glkv"], p["wuk"], p["wuvt"], p["gkc"], p["qshift"], p["kone"])


def _window(lo_ref, main_ref, hi_ref, a, radius, tq):
    start, end = a - radius, a + SUB + radius
    parts = []
    if start < 0:
        parts.append(lo_ref[radius + start:radius, :])
        start = 0
    parts.append(main_ref[start:min(end, tq), :])
    if end > tq:
        parts.append(hi_ref[0:end - tq, :])
    return parts[0] if len(parts) == 1 else jnp.concatenate(parts, axis=0)


def _banded_kernel(scal_ref, q_ref, klo_ref, k_ref, khi_ref, vlo_ref, v_ref, vhi_ref, bias_ref, *refs,
                   tq, radius, head_ids, kv_slab, has_sink, emit_ml, bounded, n_tiles):
    o_ref = refs[0]
    ml_ref = refs[1] if emit_ml else None
    p_scr = refs[-1]
    if not bounded:
        s_scr, m_scr = refs[-3:-1]
    sink = lambda hd: scal_ref[1 + hd]
    tile = pl.program_id(2)
    width = SUB + 2 * radius
    lane = lax.broadcasted_iota(jnp.int32, (SUB, LANES), 1)
    col = lax.broadcasted_iota(jnp.int32, (2 * SUB, width), 1)
    low = lane < HEAD_DIM
    subs = list(range(0, tq, SUB))
    slab = lambda w, pair: w[:, kv_slab[pair] * LANES:(kv_slab[pair] + 1) * LANES]
    tiles = [(a, pair) for a in subs for pair in range(len(head_ids))]

    kws = {a: _window(klo_ref, k_ref, khi_ref, a, radius, tq) for a in subs}
    for g, (a, pair) in enumerate(tiles):
        qs = q_ref[a:a + SUB, pair * LANES:(pair + 1) * LANES]
        zero = jnp.zeros_like(qs)
        q2 = jnp.concatenate([jnp.where(low, qs, zero), jnp.where(low, zero, qs)], axis=0)
        s = lax.dot_general(q2, slab(kws[a], pair), (((1,), (1,)), ((), ())), preferred_element_type=F32)
        s = s + bias_ref[pair]
        if a == 0:
            s = jnp.where(col < jnp.where(tile == 0, radius, 0), NEG, s)
        if a == tq - SUB:
            s = jnp.where(col >= jnp.where(tile == n_tiles - 1, SUB + radius, width), NEG, s)
        if bounded:
            p_scr[g] = jnp.exp2(s).astype(BF16)
            continue
        m = jnp.max(s, axis=-1, keepdims=True)
        if has_sink:
            sinks = jnp.concatenate([jnp.full((SUB, 1), sink(hd), F32) for hd in head_ids[pair]], axis=0)
            m = jnp.maximum(m, sinks)
        s_scr[g] = s
        m_scr[g] = jnp.broadcast_to(m, (2 * SUB, LANES))

    for g in range(0 if bounded else len(tiles)):
        m_wide = jnp.concatenate([m_scr[g]] * (width // LANES), axis=1)
        p_scr[g] = jnp.exp2(s_scr[g] - m_wide).astype(BF16)

    vws = {a: _window(vlo_ref, v_ref, vhi_ref, a, radius, tq) for a in subs}
    ml = None
    for g, (a, pair) in enumerate(tiles):
        hd0, hd1 = head_ids[pair]
        vs = slab(vws[a], pair)
        o2 = jnp.dot(p_scr[g], jnp.concatenate([vs, jnp.ones_like(vs)], axis=1), preferred_element_type=F32)
        if bounded:
            m0 = m1 = jnp.full((SUB, LANES), scal_ref[0], F32)
        else:
            m0, m1 = m_scr[g][:SUB], m_scr[g][SUB:]
        l0, l1 = o2[:SUB, LANES:], o2[SUB:, LANES:]
        if has_sink:
            l0 = l0 + jnp.exp2(sink(hd0) - m0)
            l1 = l1 + jnp.exp2(sink(hd1) - m1)
        out = jnp.where(low, o2[:SUB, :LANES], o2[SUB:, :LANES]) / jnp.where(low, l0, l1)
        o_ref[a:a + SUB, pair * LANES:(pair + 1) * LANES] = out.astype(o_ref.dtype)
        if emit_ml:
            if pair == 0:
                ml = jnp.where(lane < ML_L_OFFSET, m0, 0.0) if bounded else jnp.zeros((SUB, LANES), F32)
            if not bounded:
                ml = jnp.where(lane == hd0, m0, ml)
                ml = jnp.where(lane == hd1, m1, ml)
            ml = jnp.where(lane == ML_L_OFFSET + hd0, l0, ml)
            ml = jnp.where(lane == ML_L_OFFSET + hd1, l1, ml)
            if pair == len(head_ids) - 1:
                ml_ref[a:a + SUB, :] = ml


def _banded(q, k, v, bias, scal, *, batch, seq, dilation, radius, widths, head_ids, kv_slab, has_sink, emit_ml,
            bounded, out_dtype):
    n = seq // dilation
    tq = min(TQ_BAND, n)
    n_tiles = n // tq
    qw, kw = widths
    n_sub_tiles = (tq // SUB) * len(head_ids)
    view = lambda t: t[0].reshape(batch, n, t[0].shape[1])
    per_tile = tq // radius

    def main(w, t=(None, 1, 0)):
        return pl.BlockSpec((None, tq, w), lambda b, c, i: (b, i, t[1] * c + t[2]))

    def lo(t):
        return pl.BlockSpec((None, radius, kw),
                            lambda b, c, i: (b, jnp.maximum(i * per_tile - 1, 0), t[1] * c + t[2]))

    def hi(t):
        return pl.BlockSpec((None, radius, kw),
                            lambda b, c, i: (b, jnp.minimum((i + 1) * per_tile, n // radius - 1), t[1] * c + t[2]))

    in_specs = [pl.BlockSpec(memory_space=pltpu.SMEM),
                main(qw, q), lo(k), main(kw, k), hi(k), lo(v), main(kw, v), hi(v), _const_spec(bias.shape)]
    args = [scal, view(q), view(k), view(k), view(k), view(v), view(v), view(v), bias]
    out_specs = [main(qw)]
    out_shape = [jax.ShapeDtypeStruct((batch, n, dilation * qw), out_dtype)]
    if emit_ml:
        out_specs.append(main(LANES))
        out_shape.append(jax.ShapeDtypeStruct((batch, n, dilation * LANES), F32))
    outs = pl.pallas_call(
        functools.partial(_banded_kernel, tq=tq, radius=radius, head_ids=head_ids, kv_slab=kv_slab,
                          has_sink=has_sink, emit_ml=emit_ml, bounded=bounded, n_tiles=n_tiles),
        grid=(batch, dilation, n_tiles),
        in_specs=in_specs,
        out_specs=out_specs,
        out_shape=out_shape,
        scratch_shapes=([] if bounded else [pltpu.VMEM((n_sub_tiles, 2 * SUB, SUB + 2 * radius), F32),
                                            pltpu.VMEM((n_sub_tiles, 2 * SUB, LANES), F32)])
        + [pltpu.VMEM((n_sub_tiles, 2 * SUB, SUB + 2 * radius), BF16)],
        compiler_params=_cparams(("parallel", "parallel", "parallel")),
        name="banded%s_r%d_d%d" % ("_bounded" if bounded else "", radius, dilation),
    )(*args)
    return [o.reshape(batch * n, -1) for o in outs]


def _latent_kernel(q_ref, k_ref, vt_ref, o_ref, s_ref, cmax_ref, m_ref, acc_ref, *, tk):
    tq = q_ref.shape[0]
    nk = k_ref.shape[0] // tk
    ahead = LAT_SLOTS - 1
    qt = q_ref[...].astype(F32).T.astype(BF16)
    m_ref[...] = jnp.full(m_ref.shape, -jnp.inf, F32)
    acc_ref[...] = jnp.zeros(acc_ref.shape, F32)

    def scores(slot, j):
        start = pl.multiple_of(j * tk, tk)
        s = jnp.dot(k_ref[pl.ds(start, tk), :], qt, preferred_element_type=F32)
        s_ref[slot] = s
        cmax_ref[slot] = jnp.max(s, axis=0, keepdims=True)

    def consume(slot, j):
        m_old = m_ref[...]
        m_new = jnp.maximum(m_old, cmax_ref[slot])
        pr = jnp.exp2(s_ref[slot] - m_new).astype(BF16)
        acc_ref[...] = jnp.exp2(m_old - m_new) * acc_ref[...] + jnp.dot(
            vt_ref[j], pr, preferred_element_type=F32)
        m_ref[...] = m_new

    def step(j, u, with_scores):
        if with_scores:
            scores((u + ahead) % LAT_SLOTS, j + ahead)
        consume(u % LAT_SLOTS, j)

    for j in range(ahead):
        scores(j, j)
    trips = (nk - ahead) // LAT_UNROLL

    def body(jj, carry):
        for u in range(LAT_UNROLL):
            step(LAT_UNROLL * jj + u, u, True)
        return carry

    lax.fori_loop(0, trips, body, 0)
    for j in range(trips * LAT_UNROLL, nk):
        step(j, j, j + ahead < nk)
    acc = acc_ref[...]
    o = jnp.concatenate([acc[:C_VDIM] / acc[C_VDIM:C_VDIM + 1], jnp.zeros((LANES - C_VDIM, tq), F32)], axis=0)
    o_ref[...] = o.T.astype(o_ref.dtype)


def _latent_bounded_kernel(q_ref, k_ref, vt_ref, o_ref, acc_ref, *, tk):
    tq = q_ref.shape[0]
    nk = k_ref.shape[0] // tk
    qt = q_ref[...].astype(F32).T.astype(BF16)
    acc_ref[...] = jnp.zeros(acc_ref.shape, F32)

    def body(jj, carry):
        for u in range(LAT_BOUNDED_UNROLL):
            j = LAT_BOUNDED_UNROLL * jj + u
            start = pl.multiple_of(j * tk, tk)
            s = jnp.dot(k_ref[pl.ds(start, tk), :], qt, preferred_element_type=F32)
            acc_ref[...] += jnp.dot(vt_ref[j], jnp.exp2(s).astype(BF16), preferred_element_type=F32)
        return carry

    lax.fori_loop(0, nk // LAT_BOUNDED_UNROLL, body, 0)
    acc = acc_ref[...]
    o = jnp.concatenate([acc[:C_VDIM] / acc[C_VDIM:C_VDIM + 1], jnp.zeros((LANES - C_VDIM, tq), F32)], axis=0)
    o_ref[...] = o.T.astype(o_ref.dtype)


def _latent(qc, kc, vt, batch, seq, bounded):
    t = batch * seq
    tq, tk = TQ_LAT, vt.shape[2]
    nq = seq // tq
    if bounded:
        body = functools.partial(_latent_bounded_kernel, tk=tk)
        scratch = [pltpu.VMEM((VT_ROWS, tq), F32)]
    else:
        body = functools.partial(_latent_kernel, tk=tk)
        scratch = [pltpu.VMEM((LAT_SLOTS, tk, tq), F32), pltpu.VMEM((LAT_SLOTS, 1, tq), F32),
                   pltpu.VMEM((1, tq), F32), pltpu.VMEM((VT_ROWS, tq), F32)]
    return pl.pallas_call(
        body,
        grid=(batch, C_HEADS, nq),
        in_specs=[pl.BlockSpec((tq, LANES), lambda b, h, i: (b * nq + i, h)),
                  pl.BlockSpec((seq, LANES), lambda b, h, i: (b, h)),
                  pl.BlockSpec((seq // tk, VT_ROWS, tk), lambda b, h, i: (b, h, 0))],
        out_specs=pl.BlockSpec((tq, LANES), lambda b, h, i: (b * nq + i, h)),
        out_shape=jax.ShapeDtypeStruct((t, C_SLOTS), BF16),
        scratch_shapes=scratch,
        compiler_params=_cparams(("parallel", "parallel", "parallel")),
        name="latent_bounded" if bounded else "latent",
    )(qc, kc, vt)


def _merge_mlp_kernel(x_ref, oa1_ref, oa2_ref, oa3_ref, ml1_ref, ml2_ref, ml3_ref, ob_ref, oc_ref,
                      expand_ref, gout_ref, wout_ref, gmlp_ref, wup_ref, wdown_ref, out_ref, oa_s, ml_s):
    tm = x_ref.shape[0]
    oas, mls = [oa1_ref[...]], [ml1_ref[...]]
    for idx, (o_ref, l_ref) in enumerate(((oa2_ref, ml2_ref), (oa3_ref, ml3_ref))):
        r = A_CONFIGS[idx + 1][1]
        for c in range(r):
            for s in range(A_W // LANES):
                col = c * A_W + s * LANES
                oa_s[idx, s, pl.ds(c, tm // r, stride=r), :] = o_ref[:, col:col + LANES]
            ml_s[idx, pl.ds(c, tm // r, stride=r), :] = l_ref[:, c * LANES:(c + 1) * LANES]
        oas.append(jnp.concatenate([oa_s[idx, s] for s in range(A_W // LANES)], axis=1))
        mls.append(ml_s[idx])
    lane = lax.broadcasted_iota(jnp.int32, mls[0].shape, 1)
    m_all = jnp.maximum(jnp.maximum(mls[0], mls[1]), mls[2])
    ws = [pltpu.roll(ml, LANES - ML_L_OFFSET, 1) * jnp.exp2(ml - m_all) for ml in mls]
    wsum = ws[0] + ws[1] + ws[2]
    packed = jnp.zeros_like(wsum)
    for c, w in enumerate(ws):
        wn = jnp.where(lane < A_HEADS, w / wsum, 0.0)
        packed = packed + (wn if c == 0 else pltpu.roll(wn, ML_L_OFFSET * c, 1))
    hi = packed.astype(BF16)
    lo = (packed - hi.astype(F32)).astype(BF16)
    spread = (jnp.dot(hi, expand_ref[...], preferred_element_type=F32)
              + jnp.dot(lo, expand_ref[...], preferred_element_type=F32))
    oa = spread[:, :A_W] * oas[0] + spread[:, A_W:2 * A_W] * oas[1] + spread[:, 2 * A_W:] * oas[2]

    def group_norm(v, width):
        return v * lax.rsqrt(jnp.sum(v * v, axis=-1, keepdims=True) * (1.0 / width) + EPS)

    ob = ob_ref[...].astype(F32)
    oc_wide = oc_ref[...].astype(F32)
    half = lax.broadcasted_iota(jnp.int32, (tm, LANES), 1) < C_VDIM
    oc = jnp.concatenate(
        [jnp.where(half, oc_wide[:, 2 * p * LANES:(2 * p + 1) * LANES],
                   pltpu.roll(oc_wide[:, (2 * p + 1) * LANES:(2 * p + 2) * LANES], C_VDIM, 1))
         for p in range(C_HEADS // 2)], axis=1)
    mixed = jnp.concatenate([group_norm(oa, A_W), group_norm(ob, B_QW), group_norm(oc, C_HEADS * C_VDIM)], axis=1)
    mixed = (mixed * gout_ref[...]).astype(BF16)
    x = x_ref[...] + jnp.dot(mixed, wout_ref[...], preferred_element_type=F32)

    h = (x * lax.rsqrt(jnp.mean(x * x, axis=-1, keepdims=True) + EPS) * gmlp_ref[...]).astype(BF16)
    acc = x
    for s in range(0, D_FF, FF_CHUNK):
        u = jnp.dot(h, wup_ref[:, s:s + FF_CHUNK], preferred_element_type=F32)
        u = jnp.square(jnp.maximum(u, 0.0)).astype(BF16)
        acc = acc + jnp.dot(u, wdown_ref[s:s + FF_CHUNK, :], preferred_element_type=F32)
    out_ref[...] = acc


def _merge_mlp(x2d, oas, mls, ob, oc, p):
    t = x2d.shape[0]
    tm = TM_MLP
    row = lambda w: pl.BlockSpec((tm, w), lambda i: (i, 0))
    strided = lambda w, r: pl.BlockSpec((tm // r, r * w), lambda i: (i, 0))
    single = lambda shape: pl.BlockSpec(shape, lambda i: (0,) * len(shape), pipeline_mode=pl.Buffered(1))
    return pl.pallas_call(
        _merge_mlp_kernel,
        grid=(t // tm,),
        in_specs=[row(D_MODEL)] + [strided(A_W, r) for _, r in A_CONFIGS]
        + [strided(LANES, r) for _, r in A_CONFIGS] + [row(B_QW), row(C_SLOTS),
                  single((LANES, 3 * A_W)), single((1, MIX_COLS)), single((MIX_COLS, D_MODEL)),
                  single((1, D_MODEL)), single((D_MODEL, D_FF)), single((D_FF, D_MODEL))],
        out_specs=row(D_MODEL),
        out_shape=jax.ShapeDtypeStruct((t, D_MODEL), F32),
        scratch_shapes=[pltpu.VMEM((2, A_W // LANES, tm, LANES), F32), pltpu.VMEM((2, tm, LANES), F32)],
        compiler_params=_cparams(("parallel",)),
        name="merge_mlp",
    )(x2d, *oas, *mls, ob, oc, p["expand"], p["gout"], p["w_out"], p["gmlp"], p["w_up"], p["w_down"])


def _block_diag_ones(group):
    idx = np.arange(2 * LANES) // group
    return jnp.asarray((idx[:, None] == idx[None, :]).astype(np.float32), dtype=BF16)


def _expand_matrix():
    e = np.zeros((LANES, len(A_CONFIGS) * A_W), np.float32)
    for c in range(len(A_CONFIGS)):
        for h in range(A_HEADS):
            e[ML_L_OFFSET * c + h, c * A_W + h * HEAD_DIM:c * A_W + (h + 1) * HEAD_DIM] = 1.0
    return jnp.asarray(e, dtype=BF16)


def _pad_heads(w, heads, used, lo=0):
    lead = w.shape[:-1]
    w = w.reshape(lead + (heads, used))
    pad = [(0, 0)] * len(lead) + [(0, 0), (lo, LANES - lo - used)]
    return jnp.pad(w, pad).reshape(lead + (heads * LANES,))


def _layer_params(i, norm_mix, w_in, qk_gain_a, qk_gain_b, q_lat_gain, kv_lat_gain, w_uq, w_ukv, qk_gain_c,
                  out_norm, w_out, norm_mlp, w_up, w_down):
    w = w_in[i]
    o = np.cumsum((A_W, A_W, A_W, B_QW, B_KVW, B_KVW, C_Q_RANK, C_KV_RANK)).tolist()
    qb = w[:, o[2]:o[3]].reshape(D_MODEL, B_HEADS, HEAD_DIM)[:, B_HEAD_ORDER, :].reshape(D_MODEL, B_QW)
    kr = _pad_heads(w[:, o[7]:], 1, C_ROPE, lo=C_NOPE)
    w_in_p = jnp.concatenate([w[:, :o[2]], qb, w[:, o[3]:o[7]], kr], axis=1).astype(BF16)

    scale = HEAD_DIM ** -0.5 * LOG2E
    gqk = jnp.concatenate([jnp.tile(qk_gain_a[i, 0], A_HEADS) * scale, jnp.tile(qk_gain_a[i, 1], A_HEADS),
                           jnp.tile(qk_gain_b[i, 0], B_HEADS) * scale, jnp.tile(qk_gain_b[i, 1], B_KV_HEADS)])
    ukv = w_ukv[i].reshape(C_KV_RANK, C_HEADS, C_NOPE + C_VDIM)
    wuk = _pad_heads(ukv[:, :, :C_NOPE].reshape(C_KV_RANK, -1), C_HEADS, C_NOPE)
    wuvt = ukv[:, :, C_NOPE:].reshape(C_KV_RANK, C_HEADS * C_VDIM).T
    g = out_norm[i]
    gb = g[A_W:A_W + B_QW].reshape(B_HEADS, HEAD_DIM)[B_HEAD_ORDER, :].reshape(B_QW)
    gout = jnp.concatenate([g[:A_W], gb, g[A_W + B_QW:]])
    wo = w_out[i]
    wob = wo[A_W:A_W + B_QW].reshape(B_HEADS, HEAD_DIM, D_MODEL)[B_HEAD_ORDER, :, :].reshape(B_QW, D_MODEL)
    w_out_p = jnp.concatenate([wo[:A_W], wob, wo[A_W + B_QW:]], axis=0).astype(BF16)
    bound = (C_QK ** 0.5 * LOG2E * BOUND_MARGIN) * jnp.max(jnp.abs(qk_gain_c[i, 0])) * jnp.max(jnp.abs(qk_gain_c[i, 1]))
    return {
        "gmix": norm_mix[i][None, :],
        "w_in": w_in_p,
        "gqk": gqk[None, :],
        "ones64": _block_diag_ones(HEAD_DIM),
        "ones128": _block_diag_ones(LANES),
        "glq": q_lat_gain[i][None, :],
        "wuq": _pad_heads(w_uq[i], C_HEADS, C_QK).astype(BF16),
        "gqc": jnp.tile(_pad_heads(qk_gain_c[i, 0] * (C_QK ** -0.5 * LOG2E), 1, C_QK), C_HEADS)[None, :],
        "glkv": kv_lat_gain[i][None, :],
        "wuk": wuk.astype(BF16),
        "wuvt": wuvt.astype(BF16),
        "gkc": jnp.tile(_pad_heads(qk_gain_c[i, 1], 1, C_QK), C_HEADS)[None, :],
        "qshift": jnp.tile(_pad_heads(-bound[None], 1, 1, lo=C_QK), C_HEADS)[None, :],
        "kone": jnp.tile(_pad_heads(jnp.ones((1,), F32), 1, 1, lo=C_QK), C_HEADS)[None, :],
        "logit_bound": bound,
        "expand": _expand_matrix(),
        "gout": gout[None, :],
        "w_out": w_out_p,
        "gmlp": norm_mlp[i][None, :],
        "w_up": w_up[i].astype(BF16),
        "w_down": w_down[i].astype(BF16),
    }


def kernel(x, positions, rel_bias_table, norm_mix, w_in, qk_gain_a, qk_gain_b, sink_b, q_lat_gain, kv_lat_gain,
           w_uq, w_ukv, qk_gain_c, out_norm, w_out, norm_mlp, w_up, w_down):
    batch, seq, _ = x.shape
    depth = w_in.shape[0]
    x2d = x.reshape(batch * seq, D_MODEL)
    cos_t, sin_t = _rope_tables(positions)
    a_pairs = tuple((2 * p, 2 * p + 1) for p in range(A_HEADS // 2))
    b_pairs = (B_HEAD_ORDER[:2], B_HEAD_ORDER[2:])
    b_cols = tuple(A_HEADS + h for h in B_HEAD_ORDER)

    def bias_tables(shift):
        tiles_a = [_bias_tiles(rel_bias_table, shift, window // (2 * r), r, tuple(range(A_HEADS)))
                   for window, r in A_CONFIGS]
        return tiles_a, _bias_tiles(rel_bias_table, shift, B_RADIUS, 1, b_cols)

    gain_bound = lambda gains: jnp.max(jnp.max(jnp.abs(gains[:, 0]), axis=-1) * jnp.max(jnp.abs(gains[:, 1]), axis=-1))
    qk_bound = HEAD_DIM ** 0.5 * LOG2E * BOUND_MARGIN * jnp.maximum(gain_bound(qk_gain_a), gain_bound(qk_gain_b))
    band_shift = jnp.maximum(qk_bound + LOG2E * jnp.max(jnp.abs(rel_bias_table)), LOG2E * jnp.max(sink_b))
    shifted_tables = bias_tables(band_shift[None])

    def banded_all(bounded, qkv1, qkv4, qkv16, qb, kb, vb, sinks):
        shift = band_shift[None] if bounded else jnp.zeros((1,), F32)
        tables_a, table_b = shifted_tables if bounded else bias_tables(shift)
        scal = jnp.concatenate([shift, sinks * LOG2E])
        outs = []
        for (window, r), bias, qkv in zip(A_CONFIGS, tables_a, (qkv1, qkv4, qkv16)):
            outs += _banded((qkv, 3, 0), (qkv, 3, 1), (qkv, 3, 2), bias, scal, batch=batch, seq=seq, dilation=r,
                            radius=window // (2 * r), widths=(A_W, A_W), head_ids=a_pairs, kv_slab=(0, 1, 2),
                            has_sink=False, emit_ml=True, bounded=bounded, out_dtype=F32)
        outs += _banded((qb, 1, 0), (kb, 1, 0), (vb, 1, 0), table_b, scal, batch=batch, seq=seq, dilation=1,
                        radius=B_RADIUS, widths=(B_QW, B_KVW), head_ids=b_pairs, kv_slab=(0, 0),
                        has_sink=True, emit_ml=False, bounded=bounded, out_dtype=BF16)
        return tuple(outs)

    for i in range(depth):
        p = _layer_params(i, norm_mix, w_in, qk_gain_a, qk_gain_b, q_lat_gain, kv_lat_gain, w_uq, w_ukv,
                          qk_gain_c, out_norm, w_out, norm_mlp, w_up, w_down)
        qkv1, qkv4, qkv16, qb, kb, vb, qc, kc, vt = _inproj(x2d, cos_t, sin_t, p)
        o1, ml1, o4, ml4, o16, ml16, ob = lax.cond(
            band_shift <= MAX_LOGIT_BOUND, functools.partial(banded_all, True), functools.partial(banded_all, False),
            qkv1, qkv4, qkv16, qb, kb, vb, sink_b[i])
        oas, mls = [o1, o4, o16], [ml1, ml4, ml16]
        oc = lax.cond(p["logit_bound"] <= MAX_LOGIT_BOUND,
                      functools.partial(_latent, batch=batch, seq=seq, bounded=True),
                      functools.partial(_latent, batch=batch, seq=seq, bounded=False), qc, kc, vt)
        x2d = _merge_mlp(x2d, oas, mls, ob, oc, p)
    return x2d.reshape(batch, seq, D_MODEL)
```

```python
import functools
import math

import numpy as np
import jax
import jax.numpy as jnp
from jax import lax
from jax.experimental import pallas as pl
from jax.experimental.pallas import tpu as pltpu

F32 = jnp.float32
BF16 = jnp.bfloat16

D_MODEL = 1024
HEAD_DIM = 64
A_HEADS = 6
A_CONFIGS = ((128, 1), (512, 4), (2048, 16))
B_HEADS = 4
B_KV_HEADS = 2
B_RADIUS = 128
C_HEADS = 6
C_NOPE = 64
C_ROPE = 32
C_VDIM = 64
C_QK = C_NOPE + C_ROPE
C_Q_RANK = 256
C_KV_RANK = 128
ROPE_THETA = 10000.0
N_BUCKETS = 32
MAX_DISTANCE = 1024
D_FF = 4 * D_MODEL
EPS = 1e-6
NEG = -1e30

A_W = A_HEADS * HEAD_DIM
B_QW = B_HEADS * HEAD_DIM
B_KVW = B_KV_HEADS * HEAD_DIM
LANES = 128
C_SLOTS = C_HEADS * LANES
IN_COLS = 3 * A_W + B_QW + 2 * B_KVW + C_Q_RANK + C_KV_RANK + LANES
MIX_COLS = A_W + B_QW + C_HEADS * C_VDIM
VT_ROWS = 80
LOG2E = math.log2(math.e)
ML_L_OFFSET = 8

B_HEAD_ORDER = (0, 2, 1, 3)

TM_IN = 512
IN_SPLIT = 2
TQ_BAND = 1024
SUB = 128
TQ_LAT = 2048
BOUND_MARGIN = 1.02
MAX_LOGIT_BOUND = 50.0
LAT_BOUNDED_UNROLL = 16
LAT_SLOTS = 3
LAT_UNROLL = 6
TM_MLP = 512
FF_CHUNK = 1024
VMEM_LIMIT = 56 * 1024 * 1024


def _cparams(sem):
    return pltpu.CompilerParams(dimension_semantics=sem, vmem_limit_bytes=VMEM_LIMIT)


def _const_spec(shape):
    zeros = (0,) * len(shape)
    return pl.BlockSpec(shape, lambda *_: zeros)


def _rope_table_kernel(pos_ref, inv_ref, cos_ref, sin_ref):
    ang = pos_ref[...] * inv_ref[...]
    lane = lax.broadcasted_iota(jnp.int32, ang.shape, 1)
    c = jnp.cos(ang)
    s = jnp.sin(ang)
    first = (lane >= C_NOPE) & (lane < C_NOPE + C_ROPE // 2)
    second = (lane >= C_NOPE + C_ROPE // 2) & (lane < C_QK)
    cos_ref[...] = jnp.where(first | second, c, 1.0)
    sin_ref[...] = jnp.where(first, -s, jnp.where(second, s, 0.0))


def _rope_tables(positions):
    t = positions.size
    half = C_ROPE // 2
    inv = ROPE_THETA ** (-jnp.arange(half, dtype=F32) / half)
    inv_row = jnp.concatenate([jnp.zeros((C_NOPE,), F32), inv, inv, jnp.zeros((LANES - C_QK,), F32)])[None, :]
    pos = positions.astype(F32).reshape(t, 1)
    tm = 2048
    return pl.pallas_call(
        _rope_table_kernel,
        grid=(t // tm,),
        in_specs=[pl.BlockSpec((tm, 1), lambda i: (i, 0)), _const_spec((1, LANES))],
        out_specs=[pl.BlockSpec((tm, LANES), lambda i: (i, 0))] * 2,
        out_shape=[jax.ShapeDtypeStruct((t, LANES), F32)] * 2,
        compiler_params=_cparams(("parallel",)),
        name="rope_tables",
    )(pos, inv_row)


def _bucket_thresholds():
    half = N_BUCKETS // 2
    exact = half // 2
    n = np.arange(1, 2 * MAX_DISTANCE + 2, dtype=np.float64)
    far = exact + (np.log(n / exact) / math.log(MAX_DISTANCE / exact) * (half - exact)).astype(np.int64)
    far = np.minimum(far, half - 1)
    return tuple(int(n[np.argmax(far >= exact + k)]) for k in range(1, half - exact))


def _bias_kernel(table_ref, shift_ref, out_ref, *, radius, dilation, head_cols):
    hsel = pl.program_id(0)
    width = SUB + 2 * radius
    row = lax.broadcasted_iota(jnp.int32, (SUB, width), 0)
    col = lax.broadcasted_iota(jnp.int32, (SUB, width), 1)
    rel = col - radius - row
    n = jnp.abs(rel) * dilation
    half = N_BUCKETS // 2
    exact = half // 2
    far = jnp.full(n.shape, exact, jnp.int32)
    for thr in _bucket_thresholds():
        far = far + (n >= thr).astype(jnp.int32)
    bucket = jnp.where(rel > 0, half, 0) + jnp.where(n < exact, n, far)
    for idx, hc in enumerate(head_cols):
        @pl.when(hsel == idx)
        def _(hc=hc):
            val = jnp.zeros(n.shape, F32)
            for b in range(N_BUCKETS):
                val = jnp.where(bucket == b, table_ref[b, hc], val)
            out_ref[...] = jnp.where(jnp.abs(rel) <= radius, val * LOG2E - shift_ref[0], NEG)


def _bias_tiles(table, shift, radius, dilation, head_cols):
    width = SUB + 2 * radius
    tiles = pl.pallas_call(
        functools.partial(_bias_kernel, radius=radius, dilation=dilation, head_cols=head_cols),
        grid=(len(head_cols),),
        in_specs=[pl.BlockSpec(memory_space=pltpu.SMEM), pl.BlockSpec(memory_space=pltpu.SMEM)],
        out_specs=pl.BlockSpec((None, SUB, width), lambda h: (h, 0, 0)),
        out_shape=jax.ShapeDtypeStruct((len(head_cols), SUB, width), F32),
        compiler_params=_cparams(("arbitrary",)),
        name="bias_tiles",
    )(table, shift)
    return tiles.reshape(len(head_cols) // 2, 2 * SUB, width)


def _group_mean_sq(y, ones_ref, group):
    sq = (y * y).astype(BF16)
    width = y.shape[1]
    parts = []
    for s in range(0, width, 2 * LANES):
        w = min(2 * LANES, width - s)
        parts.append(jnp.dot(sq[:, s:s + w], ones_ref[:w, :w], preferred_element_type=F32))
    out = parts[0] if len(parts) == 1 else jnp.concatenate(parts, axis=1)
    return out * (1.0 / group)


def _slab_roll(y, shift):
    parts = [pltpu.roll(y[:, s:s + LANES], shift, 1) for s in range(0, y.shape[1], LANES)]
    return parts[0] if len(parts) == 1 else jnp.concatenate(parts, axis=1)


def _rope(y, cos_t, sin_t, lane):
    swapped = jnp.where(lane < C_NOPE + C_ROPE // 2, _slab_roll(y, LANES - C_ROPE // 2), _slab_roll(y, C_ROPE // 2))
    return y * cos_t + swapped * sin_t


def _inproj_kernel(x_ref, cos_ref, sin_ref, gmix_ref, w_ref, gqk_ref, ones64_ref, ones128_ref,
                   glq_ref, wuq_ref, gqc_ref, glkv_ref, wuk_ref, wuvt_ref, gkc_ref, qshift_ref, kone_ref,
                   qkv1_ref, qkv4_ref, qkv16_ref, qb_ref, kb_ref, vb_ref, qc_ref, kc_ref, vt_ref, stage_ref, stage4_ref):
    rows = x_ref.shape[0] // IN_SPLIT
    for part in range(IN_SPLIT):
        _inproj_rows(part, slice(part * rows, (part + 1) * rows), x_ref, cos_ref, sin_ref, gmix_ref, w_ref, gqk_ref,
                     ones64_ref, ones128_ref, glq_ref, wuq_ref, gqc_ref, glkv_ref, wuk_ref, wuvt_ref, gkc_ref,
                     qshift_ref, kone_ref, qkv1_ref, qkv4_ref, qkv16_ref, qb_ref, kb_ref, vb_ref, qc_ref, kc_ref,
                     vt_ref, stage_ref, stage4_ref)


def _inproj_rows(part, rs, x_ref, cos_ref, sin_ref, gmix_ref, w_ref, gqk_ref, ones64_ref, ones128_ref,
                 glq_ref, wuq_ref, gqc_ref, glkv_ref, wuk_ref, wuvt_ref, gkc_ref, qshift_ref, kone_ref,
                 qkv1_ref, qkv4_ref, qkv16_ref, qb_ref, kb_ref, vb_ref, qc_ref, kc_ref, vt_ref, stage_ref, stage4_ref):
    tm = rs.stop - rs.start
    x = x_ref[rs, :]
    h = x * lax.rsqrt(jnp.mean(x * x, axis=-1, keepdims=True) + EPS) * gmix_ref[...]
    y = jnp.dot(h.astype(BF16), w_ref[...], preferred_element_type=F32)

    o_qb = 3 * A_W
    o_kb = o_qb + B_QW
    o_vb = o_kb + B_KVW
    o_cq = o_vb + B_KVW
    o_ckv = o_cq + C_Q_RANK
    o_kr = o_ckv + C_KV_RANK
    yn = jnp.concatenate([y[:, :2 * A_W], y[:, o_qb:o_vb]], axis=1)
    yn = yn * lax.rsqrt(_group_mean_sq(yn, ones64_ref, HEAD_DIM) + EPS) * gqk_ref[...]
    qb_ref[rs, :] = yn[:, 2 * A_W:2 * A_W + B_QW].astype(BF16)
    kb_ref[rs, :] = yn[:, 2 * A_W + B_QW:].astype(BF16)
    n_slabs = stage_ref.shape[1]
    for s in range(n_slabs):
        src = yn if s * LANES < 2 * A_W else y
        stage_ref[part, s] = src[:, s * LANES:(s + 1) * LANES]
    rs4 = slice(rs.start // 4, rs.stop // 4)
    rs16 = slice(rs.start // 16, rs.stop // 16)
    for s in range(n_slabs):
        qkv1_ref[rs, s * LANES:(s + 1) * LANES] = stage_ref[part, s].astype(BF16)
        for c4 in range(4):
            rows = stage_ref[part, s, pl.ds(c4, tm // 4, stride=4), :]
            qkv4_ref[rs4, c4 * 3 * A_W + s * LANES:c4 * 3 * A_W + (s + 1) * LANES] = rows.astype(BF16)
            stage4_ref[part, c4 * n_slabs + s] = rows
    for c16 in range(16):
        for s in range(n_slabs):
            rows = stage4_ref[part, (c16 % 4) * n_slabs + s, pl.ds(c16 // 4, tm // 16, stride=4), :]
            qkv16_ref[rs16, c16 * 3 * A_W + s * LANES:c16 * 3 * A_W + (s + 1) * LANES] = rows.astype(BF16)
    vb_ref[rs, :] = y[:, o_vb:o_cq].astype(BF16)

    cos_t = jnp.concatenate([cos_ref[rs, :]] * C_HEADS, axis=1)
    sin_t = jnp.concatenate([sin_ref[rs, :]] * C_HEADS, axis=1)
    lane = lax.broadcasted_iota(jnp.int32, cos_t.shape, 1) % LANES
    cq = y[:, o_cq:o_ckv]
    cq = cq * lax.rsqrt(jnp.mean(cq * cq, axis=-1, keepdims=True) + EPS) * glq_ref[...]
    qc = jnp.dot(cq.astype(BF16), wuq_ref[...], preferred_element_type=F32)
    qc = qc * lax.rsqrt(_group_mean_sq(qc, ones128_ref, C_QK) + EPS) * gqc_ref[...]
    qc_ref[rs, :] = (_rope(qc, cos_t, sin_t, lane) + qshift_ref[...]).astype(BF16)

    ckv = y[:, o_ckv:o_kr]
    ckv = (ckv * lax.rsqrt(jnp.mean(ckv * ckv, axis=-1, keepdims=True) + EPS) * glkv_ref[...]).astype(BF16)
    kr = y[:, o_kr:]
    kc = jnp.dot(ckv, wuk_ref[...], preferred_element_type=F32) + jnp.concatenate([kr] * C_HEADS, axis=1)
    kc = kc * lax.rsqrt(_group_mean_sq(kc, ones128_ref, C_QK) + EPS) * gkc_ref[...]
    kc_ref[rs, :] = (_rope(kc, cos_t, sin_t, lane) + kone_ref[...]).astype(BF16)
    vt = lax.dot_general(wuvt_ref[...], ckv, (((1,), (1,)), ((), ())), preferred_element_type=F32)
    row = lax.broadcasted_iota(jnp.int32, (VT_ROWS - C_VDIM, vt.shape[1]), 0)
    ones_rows = jnp.where(row == 0, 1.0, 0.0).astype(BF16)
    for hd in range(C_HEADS):
        vt_ref[0, hd * VT_ROWS:hd * VT_ROWS + C_VDIM, rs] = vt[hd * C_VDIM:(hd + 1) * C_VDIM].astype(BF16)
        vt_ref[0, hd * VT_ROWS + C_VDIM:(hd + 1) * VT_ROWS, rs] = ones_rows


def _inproj(x2d, cos_t, sin_t, p):
    t = x2d.shape[0]
    tm = TM_IN
    row = lambda w: pl.BlockSpec((tm, w), lambda i: (i, 0))
    outs = [(B_QW, BF16), (B_KVW, BF16), (B_KVW, BF16)] + [(C_SLOTS, BF16)] * 2
    a_specs = [pl.BlockSpec((tm // r, r * 3 * A_W), lambda i: (i, 0)) for _, r in A_CONFIGS]
    a_shapes = [jax.ShapeDtypeStruct((t // r, r * 3 * A_W), BF16) for _, r in A_CONFIGS]
    return pl.pallas_call(
        _inproj_kernel,
        grid=(t // tm,),
        in_specs=[row(D_MODEL), row(LANES), row(LANES),
                  _const_spec((1, D_MODEL)), _const_spec((D_MODEL, IN_COLS)),
                  _const_spec((1, 2 * A_W + B_QW + B_KVW)),
                  _const_spec((2 * LANES, 2 * LANES)), _const_spec((2 * LANES, 2 * LANES)),
                  _const_spec((1, C_Q_RANK)), _const_spec((C_Q_RANK, C_SLOTS)), _const_spec((1, C_SLOTS)),
                  _const_spec((1, C_KV_RANK)), _const_spec((C_KV_RANK, C_SLOTS)),
                  _const_spec((C_HEADS * C_VDIM, C_KV_RANK)), _const_spec((1, C_SLOTS)),
                  _const_spec((1, C_SLOTS)), _const_spec((1, C_SLOTS))],
        out_specs=a_specs + [row(w) for w, _ in outs]
        + [pl.BlockSpec((1, C_HEADS * VT_ROWS, tm), lambda i: (i, 0, 0))],
        out_shape=a_shapes + [jax.ShapeDtypeStruct((t, w), d) for w, d in outs]
        + [jax.ShapeDtypeStruct((t // tm, C_HEADS * VT_ROWS, tm), BF16)],
        scratch_shapes=[pltpu.VMEM((IN_SPLIT, 3 * A_W // LANES, tm // IN_SPLIT, LANES), F32),
                        pltpu.VMEM((IN_SPLIT, 4 * 3 * A_W // LANES, tm // (4 * IN_SPLIT), LANES), F32)],
        compiler_params=_cparams(("parallel",)),
        name="inproj",
    )(x2d, cos_t, sin_t, p["gmix"], p["w_in"], p["gqk"], p["ones64"], p["ones128"],
      p["glq"], p["wuq"], p["gqc"], p["glkv"], p["wuk"], p["wuvt"], p["gkc"], p["qshift"], p["kone"])


def _window(lo_ref, main_ref, hi_ref, a, radius, tq):
    start, end = a - radius, a + SUB + radius
    parts = []
    if start < 0:
        parts.append(lo_ref[radius + start:radius, :])
        start = 0
    parts.append(main_ref[start:min(end, tq), :])
    if end > tq:
        parts.append(hi_ref[0:end - tq, :])
    return parts[0] if len(parts) == 1 else jnp.concatenate(parts, axis=0)


def _banded_kernel(scal_ref, q_ref, klo_ref, k_ref, khi_ref, vlo_ref, v_ref, vhi_ref, bias_ref, *refs,
                   tq, radius, head_ids, kv_slab, has_sink, emit_ml, bounded, n_tiles):
    o_ref = refs[0]
    ml_ref = refs[1] if emit_ml else None
    p_scr = refs[-1]
    if not bounded:
        s_scr, m_scr = refs[-3:-1]
    sink = lambda hd: scal_ref[1 + hd]
    tile = pl.program_id(2)
    width = SUB + 2 * radius
    lane = lax.broadcasted_iota(jnp.int32, (SUB, LANES), 1)
    col = lax.broadcasted_iota(jnp.int32, (2 * SUB, width), 1)
    low = lane < HEAD_DIM
    subs = list(range(0, tq, SUB))
    slab = lambda w, pair: w[:, kv_slab[pair] * LANES:(kv_slab[pair] + 1) * LANES]
    tiles = [(a, pair) for a in subs for pair in range(len(head_ids))]

    kws = {a: _window(klo_ref, k_ref, khi_ref, a, radius, tq) for a in subs}
    for g, (a, pair) in enumerate(tiles):
        qs = q_ref[a:a + SUB, pair * LANES:(pair + 1) * LANES]
        zero = jnp.zeros_like(qs)
        q2 = jnp.concatenate([jnp.where(low, qs, zero), jnp.where(low, zero, qs)], axis=0)
        s = lax.dot_general(q2, slab(kws[a], pair), (((1,), (1,)), ((), ())), preferred_element_type=F32)
        s = s + bias_ref[pair]
        if a == 0:
            s = jnp.where(col < jnp.where(tile == 0, radius, 0), NEG, s)
        if a == tq - SUB:
            s = jnp.where(col >= jnp.where(tile == n_tiles - 1, SUB + radius, width), NEG, s)
        if bounded:
            p_scr[g] = jnp.exp2(s).astype(BF16)
            continue
        m = jnp.max(s, axis=-1, keepdims=True)
        if has_sink:
            sinks = jnp.concatenate([jnp.full((SUB, 1), sink(hd), F32) for hd in head_ids[pair]], axis=0)
            m = jnp.maximum(m, sinks)
        s_scr[g] = s
        m_scr[g] = jnp.broadcast_to(m, (2 * SUB, LANES))

    for g in range(0 if bounded else len(tiles)):
        m_wide = jnp.concatenate([m_scr[g]] * (width // LANES), axis=1)
        p_scr[g] = jnp.exp2(s_scr[g] - m_wide).astype(BF16)

    vws = {a: _window(vlo_ref, v_ref, vhi_ref, a, radius, tq) for a in subs}
    ml = None
    for g, (a, pair) in enumerate(tiles):
        hd0, hd1 = head_ids[pair]
        vs = slab(vws[a], pair)
        o2 = jnp.dot(p_scr[g], jnp.concatenate([vs, jnp.ones_like(vs)], axis=1), preferred_element_type=F32)
        if bounded:
            m0 = m1 = jnp.full((SUB, LANES), scal_ref[0], F32)
        else:
            m0, m1 = m_scr[g][:SUB], m_scr[g][SUB:]
        l0, l1 = o2[:SUB, LANES:], o2[SUB:, LANES:]
        if has_sink:
            l0 = l0 + jnp.exp2(sink(hd0) - m0)
            l1 = l1 + jnp.exp2(sink(hd1) - m1)
        out = jnp.where(low, o2[:SUB, :LANES], o2[SUB:, :LANES]) / jnp.where(low, l0, l1)
        o_ref[a:a + SUB, pair * LANES:(pair + 1) * LANES] = out.astype(o_ref.dtype)
        if emit_ml:
            if pair == 0:
                ml = jnp.where(lane < ML_L_OFFSET, m0, 0.0) if bounded else jnp.zeros((SUB, LANES), F32)
            if not bounded:
                ml = jnp.where(lane == hd0, m0, ml)
                ml = jnp.where(lane == hd1, m1, ml)
            ml = jnp.where(lane == ML_L_OFFSET + hd0, l0, ml)
            ml = jnp.where(lane == ML_L_OFFSET + hd1, l1, ml)
            if pair == len(head_ids) - 1:
                ml_ref[a:a + SUB, :] = ml


def _banded(q, k, v, bias, scal, *, batch, seq, dilation, radius, widths, head_ids, kv_slab, has_sink, emit_ml,
            bounded, out_dtype):
    n = seq // dilation
    tq = min(TQ_BAND, n)
    n_tiles = n // tq
    qw, kw = widths
    n_sub_tiles = (tq // SUB) * len(head_ids)
    view = lambda t: t[0].reshape(batch, n, t[0].shape[1])
    per_tile = tq // radius

    def main(w, t=(None, 1, 0)):
        return pl.BlockSpec((None, tq, w), lambda b, c, i: (b, i, t[1] * c + t[2]))

    def lo(t):
        return pl.BlockSpec((None, radius, kw),
                            lambda b, c, i: (b, jnp.maximum(i * per_tile - 1, 0), t[1] * c + t[2]))

    def hi(t):
        return pl.BlockSpec((None, radius, kw),
                            lambda b, c, i: (b, jnp.minimum((i + 1) * per_tile, n // radius - 1), t[1] * c + t[2]))

    in_specs = [pl.BlockSpec(memory_space=pltpu.SMEM),
                main(qw, q), lo(k), main(kw, k), hi(k), lo(v), main(kw, v), hi(v), _const_spec(bias.shape)]
    args = [scal, view(q), view(k), view(k), view(k), view(v), view(v), view(v), bias]
    out_specs = [main(qw)]
    out_shape = [jax.ShapeDtypeStruct((batch, n, dilation * qw), out_dtype)]
    if emit_ml:
        out_specs.append(main(LANES))
        out_shape.append(jax.ShapeDtypeStruct((batch, n, dilation * LANES), F32))
    outs = pl.pallas_call(
        functools.partial(_banded_kernel, tq=tq, radius=radius, head_ids=head_ids, kv_slab=kv_slab,
                          has_sink=has_sink, emit_ml=emit_ml, bounded=bounded, n_tiles=n_tiles),
        grid=(batch, dilation, n_tiles),
        in_specs=in_specs,
        out_specs=out_specs,
        out_shape=out_shape,
        scratch_shapes=([] if bounded else [pltpu.VMEM((n_sub_tiles, 2 * SUB, SUB + 2 * radius), F32),
                                            pltpu.VMEM((n_sub_tiles, 2 * SUB, LANES), F32)])
        + [pltpu.VMEM((n_sub_tiles, 2 * SUB, SUB + 2 * radius), BF16)],
        compiler_params=_cparams(("parallel", "parallel", "parallel")),
        name="banded%s_r%d_d%d" % ("_bounded" if bounded else "", radius, dilation),
    )(*args)
    return [o.reshape(batch * n, -1) for o in outs]


def _latent_kernel(q_ref, k_ref, vt_ref, o_ref, s_ref, cmax_ref, m_ref, acc_ref, *, tk):
    tq = q_ref.shape[0]
    nk = k_ref.shape[0] // tk
    ahead = LAT_SLOTS - 1
    qt = q_ref[...].astype(F32).T.astype(BF16)
    m_ref[...] = jnp.full(m_ref.shape, -jnp.inf, F32)
    acc_ref[...] = jnp.zeros(acc_ref.shape, F32)

    def scores(slot, j):
        start = pl.multiple_of(j * tk, tk)
        s = jnp.dot(k_ref[pl.ds(start, tk), :], qt, preferred_element_type=F32)
        s_ref[slot] = s
        cmax_ref[slot] = jnp.max(s, axis=0, keepdims=True)

    def consume(slot, j):
        m_old = m_ref[...]
        m_new = jnp.maximum(m_old, cmax_ref[slot])
        pr = jnp.exp2(s_ref[slot] - m_new).astype(BF16)
        acc_ref[...] = jnp.exp2(m_old - m_new) * acc_ref[...] + jnp.dot(
            vt_ref[j], pr, preferred_element_type=F32)
        m_ref[...] = m_new

    def step(j, u, with_scores):
        if with_scores:
            scores((u + ahead) % LAT_SLOTS, j + ahead)
        consume(u % LAT_SLOTS, j)

    for j in range(ahead):
        scores(j, j)
    trips = (nk - ahead) // LAT_UNROLL

    def body(jj, carry):
        for u in range(LAT_UNROLL):
            step(LAT_UNROLL * jj + u, u, True)
        return carry

    lax.fori_loop(0, trips, body, 0)
    for j in range(trips * LAT_UNROLL, nk):
        step(j, j, j + ahead < nk)
    acc = acc_ref[...]
    o = jnp.concatenate([acc[:C_VDIM] / acc[C_VDIM:C_VDIM + 1], jnp.zeros((LANES - C_VDIM, tq), F32)], axis=0)
    o_ref[...] = o.T.astype(o_ref.dtype)


def _latent_bounded_kernel(q_ref, k_ref, vt_ref, o_ref, acc_ref, *, tk):
    tq = q_ref.shape[0]
    nk = k_ref.shape[0] // tk
    qt = q_ref[...].astype(F32).T.astype(BF16)
    acc_ref[...] = jnp.zeros(acc_ref.shape, F32)

    def body(jj, carry):
        for u in range(LAT_BOUNDED_UNROLL):
            j = LAT_BOUNDED_UNROLL * jj + u
            start = pl.multiple_of(j * tk, tk)
            s = jnp.dot(k_ref[pl.ds(start, tk), :], qt, preferred_element_type=F32)
            acc_ref[...] += jnp.dot(vt_ref[j], jnp.exp2(s).astype(BF16), preferred_element_type=F32)
        return carry

    lax.fori_loop(0, nk // LAT_BOUNDED_UNROLL, body, 0)
    acc = acc_ref[...]
    o = jnp.concatenate([acc[:C_VDIM] / acc[C_VDIM:C_VDIM + 1], jnp.zeros((LANES - C_VDIM, tq), F32)], axis=0)
    o_ref[...] = o.T.astype(o_ref.dtype)


def _latent(qc, kc, vt, batch, seq, bounded):
    t = batch * seq
    tq, tk = TQ_LAT, vt.shape[2]
    nq = seq // tq
    if bounded:
        body = functools.partial(_latent_bounded_kernel, tk=tk)
        scratch = [pltpu.VMEM((VT_ROWS, tq), F32)]
    else:
        body = functools.partial(_latent_kernel, tk=tk)
        scratch = [pltpu.VMEM((LAT_SLOTS, tk, tq), F32), pltpu.VMEM((LAT_SLOTS, 1, tq), F32),
                   pltpu.VMEM((1, tq), F32), pltpu.VMEM((VT_ROWS, tq), F32)]
    return pl.pallas_call(
        body,
        grid=(batch, C_HEADS, nq),
        in_specs=[pl.BlockSpec((tq, LANES), lambda b, h, i: (b * nq + i, h)),
                  pl.BlockSpec((seq, LANES), lambda b, h, i: (b, h)),
                  pl.BlockSpec((seq // tk, VT_ROWS, tk), lambda b, h, i: (b, h, 0))],
        out_specs=pl.BlockSpec((tq, LANES), lambda b, h, i: (b * nq + i, h)),
        out_shape=jax.ShapeDtypeStruct((t, C_SLOTS), BF16),
        scratch_shapes=scratch,
        compiler_params=_cparams(("parallel", "parallel", "parallel")),
        name="latent_bounded" if bounded else "latent",
    )(qc, kc, vt)


def _merge_mlp_kernel(x_ref, oa1_ref, oa2_ref, oa3_ref, ml1_ref, ml2_ref, ml3_ref, ob_ref, oc_ref,
                      expand_ref, gout_ref, wout_ref, gmlp_ref, wup_ref, wdown_ref, out_ref, oa_s, ml_s):
    tm = x_ref.shape[0]
    oas, mls = [oa1_ref[...]], [ml1_ref[...]]
    for idx, (o_ref, l_ref) in enumerate(((oa2_ref, ml2_ref), (oa3_ref, ml3_ref))):
        r = A_CONFIGS[idx + 1][1]
        for c in range(r):
            for s in range(A_W // LANES):
                col = c * A_W + s * LANES
                oa_s[idx, s, pl.ds(c, tm // r, stride=r), :] = o_ref[:, col:col + LANES]
            ml_s[idx, pl.ds(c, tm // r, stride=r), :] = l_ref[:, c * LANES:(c + 1) * LANES]
        oas.append(jnp.concatenate([oa_s[idx, s] for s in range(A_W // LANES)], axis=1))
        mls.append(ml_s[idx])
    lane = lax.broadcasted_iota(jnp.int32, mls[0].shape, 1)
    m_all = jnp.maximum(jnp.maximum(mls[0], mls[1]), mls[2])
    ws = [pltpu.roll(ml, LANES - ML_L_OFFSET, 1) * jnp.exp2(ml - m_all) for ml in mls]
    wsum = ws[0] + ws[1] + ws[2]
    packed = jnp.zeros_like(wsum)
    for c, w in enumerate(ws):
        wn = jnp.where(lane < A_HEADS, w / wsum, 0.0)
        packed = packed + (wn if c == 0 else pltpu.roll(wn, ML_L_OFFSET * c, 1))
    hi = packed.astype(BF16)
    lo = (packed - hi.astype(F32)).astype(BF16)
    spread = (jnp.dot(hi, expand_ref[...], preferred_element_type=F32)
              + jnp.dot(lo, expand_ref[...], preferred_element_type=F32))
    oa = spread[:, :A_W] * oas[0] + spread[:, A_W:2 * A_W] * oas[1] + spread[:, 2 * A_W:] * oas[2]

    def group_norm(v, width):
        return v * lax.rsqrt(jnp.sum(v * v, axis=-1, keepdims=True) * (1.0 / width) + EPS)

    ob = ob_ref[...].astype(F32)
    oc_wide = oc_ref[...].astype(F32)
    half = lax.broadcasted_iota(jnp.int32, (tm, LANES), 1) < C_VDIM
    oc = jnp.concatenate(
        [jnp.where(half, oc_wide[:, 2 * p * LANES:(2 * p + 1) * LANES],
                   pltpu.roll(oc_wide[:, (2 * p + 1) * LANES:(2 * p + 2) * LANES], C_VDIM, 1))
         for p in range(C_HEADS // 2)], axis=1)
    mixed = jnp.concatenate([group_norm(oa, A_W), group_norm(ob, B_QW), group_norm(oc, C_HEADS * C_VDIM)], axis=1)
    mixed = (mixed * gout_ref[...]).astype(BF16)
    x = x_ref[...] + jnp.dot(mixed, wout_ref[...], preferred_element_type=F32)

    h = (x * lax.rsqrt(jnp.mean(x * x, axis=-1, keepdims=True) + EPS) * gmlp_ref[...]).astype(BF16)
    acc = x
    for s in range(0, D_FF, FF_CHUNK):
        u = jnp.dot(h, wup_ref[:, s:s + FF_CHUNK], preferred_element_type=F32)
        u = jnp.square(jnp.maximum(u, 0.0)).astype(BF16)
        acc = acc + jnp.dot(u, wdown_ref[s:s + FF_CHUNK, :], preferred_element_type=F32)
    out_ref[...] = acc


def _merge_mlp(x2d, oas, mls, ob, oc, p):
    t = x2d.shape[0]
    tm = TM_MLP
    row = lambda w: pl.BlockSpec((tm, w), lambda i: (i, 0))
    strided = lambda w, r: pl.BlockSpec((tm // r, r * w), lambda i: (i, 0))
    single = lambda shape: pl.BlockSpec(shape, lambda i: (0,) * len(shape), pipeline_mode=pl.Buffered(1))
    return pl.pallas_call(
        _merge_mlp_kernel,
        grid=(t // tm,),
        in_specs=[row(D_MODEL)] + [strided(A_W, r) for _, r in A_CONFIGS]
        + [strided(LANES, r) for _, r in A_CONFIGS] + [row(B_QW), row(C_SLOTS),
                  single((LANES, 3 * A_W)), single((1, MIX_COLS)), single((MIX_COLS, D_MODEL)),
                  single((1, D_MODEL)), single((D_MODEL, D_FF)), single((D_FF, D_MODEL))],
        out_specs=row(D_MODEL),
        out_shape=jax.ShapeDtypeStruct((t, D_MODEL), F32),
        scratch_shapes=[pltpu.VMEM((2, A_W // LANES, tm, LANES), F32), pltpu.VMEM((2, tm, LANES), F32)],
        compiler_params=_cparams(("parallel",)),
        name="merge_mlp",
    )(x2d, *oas, *mls, ob, oc, p["expand"], p["gout"], p["w_out"], p["gmlp"], p["w_up"], p["w_down"])


def _block_diag_ones(group):
    idx = np.arange(2 * LANES) // group
    return jnp.asarray((idx[:, None] == idx[None, :]).astype(np.float32), dtype=BF16)


def _expand_matrix():
    e = np.zeros((LANES, len(A_CONFIGS) * A_W), np.float32)
    for c in range(len(A_CONFIGS)):
        for h in range(A_HEADS):
            e[ML_L_OFFSET * c + h, c * A_W + h * HEAD_DIM:c * A_W + (h + 1) * HEAD_DIM] = 1.0
    return jnp.asarray(e, dtype=BF16)


def _pad_heads(w, heads, used, lo=0):
    lead = w.shape[:-1]
    w = w.reshape(lead + (heads, used))
    pad = [(0, 0)] * len(lead) + [(0, 0), (lo, LANES - lo - used)]
    return jnp.pad(w, pad).reshape(lead + (heads * LANES,))


def _layer_params(i, norm_mix, w_in, qk_gain_a, qk_gain_b, q_lat_gain, kv_lat_gain, w_uq, w_ukv, qk_gain_c,
                  out_norm, w_out, norm_mlp, w_up, w_down):
    w = w_in[i]
    o = np.cumsum((A_W, A_W, A_W, B_QW, B_KVW, B_KVW, C_Q_RANK, C_KV_RANK)).tolist()
    qb = w[:, o[2]:o[3]].reshape(D_MODEL, B_HEADS, HEAD_DIM)[:, B_HEAD_ORDER, :].reshape(D_MODEL, B_QW)
    kr = _pad_heads(w[:, o[7]:], 1, C_ROPE, lo=C_NOPE)
    w_in_p = jnp.concatenate([w[:, :o[2]], qb, w[:, o[3]:o[7]], kr], axis=1).astype(BF16)

    scale = HEAD_DIM ** -0.5 * LOG2E
    gqk = jnp.concatenate([jnp.tile(qk_gain_a[i, 0], A_HEADS) * scale, jnp.tile(qk_gain_a[i, 1], A_HEADS),
                           jnp.tile(qk_gain_b[i, 0], B_HEADS) * scale, jnp.tile(qk_gain_b[i, 1], B_KV_HEADS)])
    ukv = w_ukv[i].reshape(C_KV_RANK, C_HEADS, C_NOPE + C_VDIM)
    wuk = _pad_heads(ukv[:, :, :C_NOPE].reshape(C_KV_RANK, -1), C_HEADS, C_NOPE)
    wuvt = ukv[:, :, C_NOPE:].reshape(C_KV_RANK, C_HEADS * C_VDIM).T
    g = out_norm[i]
    gb = g[A_W:A_W + B_QW].reshape(B_HEADS, HEAD_DIM)[B_HEAD_ORDER, :].reshape(B_QW)
    gout = jnp.concatenate([g[:A_W], gb, g[A_W + B_QW:]])
    wo = w_out[i]
    wob = wo[A_W:A_W + B_QW].reshape(B_HEADS, HEAD_DIM, D_MODEL)[B_HEAD_ORDER, :, :].reshape(B_QW, D_MODEL)
    w_out_p = jnp.concatenate([wo[:A_W], wob, wo[A_W + B_QW:]], axis=0).astype(BF16)
    bound = (C_QK ** 0.5 * LOG2E * BOUND_MARGIN) * jnp.max(jnp.abs(qk_gain_c[i, 0])) * jnp.max(jnp.abs(qk_gain_c[i, 1]))
    return {
        "gmix": norm_mix[i][None, :],
        "w_in": w_in_p,
        "gqk": gqk[None, :],
        "ones64": _block_diag_ones(HEAD_DIM),
        "ones128": _block_diag_ones(LANES),
        "glq": q_lat_gain[i][None, :],
        "wuq": _pad_heads(w_uq[i], C_HEADS, C_QK).astype(BF16),
        "gqc": jnp.tile(_pad_heads(qk_gain_c[i, 0] * (C_QK ** -0.5 * LOG2E), 1, C_QK), C_HEADS)[None, :],
        "glkv": kv_lat_gain[i][None, :],
        "wuk": wuk.astype(BF16),
        "wuvt": wuvt.astype(BF16),
        "gkc": jnp.tile(_pad_heads(qk_gain_c[i, 1], 1, C_QK), C_HEADS)[None, :],
        "qshift": jnp.tile(_pad_heads(-bound[None], 1, 1, lo=C_QK), C_HEADS)[None, :],
        "kone": jnp.tile(_pad_heads(jnp.ones((1,), F32), 1, 1, lo=C_QK), C_HEADS)[None, :],
        "logit_bound": bound,
        "expand": _expand_matrix(),
        "gout": gout[None, :],
        "w_out": w_out_p,
        "gmlp": norm_mlp[i][None, :],
        "w_up": w_up[i].astype(BF16),
        "w_down": w_down[i].astype(BF16),
    }


def kernel(x, positions, rel_bias_table, norm_mix, w_in, qk_gain_a, qk_gain_b, sink_b, q_lat_gain, kv_lat_gain,
           w_uq, w_ukv, qk_gain_c, out_norm, w_out, norm_mlp, w_up, w_down):
    batch, seq, _ = x.shape
    depth = w_in.shape[0]
    x2d = x.reshape(batch * seq, D_MODEL)
    cos_t, sin_t = _rope_tables(positions)
    a_pairs = tuple((2 * p, 2 * p + 1) for p in range(A_HEADS // 2))
    b_pairs = (B_HEAD_ORDER[:2], B_HEAD_ORDER[2:])
    b_cols = tuple(A_HEADS + h for h in B_HEAD_ORDER)

    def bias_tables(shift):
        tiles_a = [_bias_tiles(rel_bias_table, shift, window // (2 * r), r, tuple(range(A_HEADS)))
                   for window, r in A_CONFIGS]
        return tiles_a, _bias_tiles(rel_bias_table, shift, B_RADIUS, 1, b_cols)

    gain_bound = lambda gains: jnp.max(jnp.max(jnp.abs(gains[:, 0]), axis=-1) * jnp.max(jnp.abs(gains[:, 1]), axis=-1))
    qk_bound = HEAD_DIM ** 0.5 * LOG2E * BOUND_MARGIN * jnp.maximum(gain_bound(qk_gain_a), gain_bound(qk_gain_b))
    band_shift = jnp.maximum(qk_bound + LOG2E * jnp.max(jnp.abs(rel_bias_table)), LOG2E * jnp.max(sink_b))
    shifted_tables = bias_tables(band_shift[None])

    def banded_all(bounded, qkv1, qkv4, qkv16, qb, kb, vb, sinks):
        shift = band_shift[None] if bounded else jnp.zeros((1,), F32)
        tables_a, table_b = shifted_tables if bounded else bias_tables(shift)
        scal = jnp.concatenate([shift, sinks * LOG2E])
        outs = []
        for (window, r), bias, qkv in zip(A_CONFIGS, tables_a, (qkv1, qkv4, qkv16)):
            outs += _banded((qkv, 3, 0), (qkv, 3, 1), (qkv, 3, 2), bias, scal, batch=batch, seq=seq, dilation=r,
                            radius=window // (2 * r), widths=(A_W, A_W), head_ids=a_pairs, kv_slab=(0, 1, 2),
                            has_sink=False, emit_ml=True, bounded=bounded, out_dtype=F32)
        outs += _banded((qb, 1, 0), (kb, 1, 0), (vb, 1, 0), table_b, scal, batch=batch, seq=seq, dilation=1,
                        radius=B_RADIUS, widths=(B_QW, B_KVW), head_ids=b_pairs, kv_slab=(0, 0),
                        has_sink=True, emit_ml=False, bounded=bounded, out_dtype=BF16)
        return tuple(outs)

    for i in range(depth):
        p = _layer_params(i, norm_mix, w_in, qk_gain_a, qk_gain_b, q_lat_gain, kv_lat_gain, w_uq, w_ukv,
                          qk_gain_c, out_norm, w_out, norm_mlp, w_up, w_down)
        qkv1, qkv4, qkv16, qb, kb, vb, qc, kc, vt = _inproj(x2d, cos_t, sin_t, p)
        o1, ml1, o4, ml4, o16, ml16, ob = lax.cond(
            band_shift <= MAX_LOGIT_BOUND, functools.partial(banded_all, True), functools.partial(banded_all, False),
            qkv1, qkv4, qkv16, qb, kb, vb, sink_b[i])
        oas, mls = [o1, o4, o16], [ml1, ml4, ml16]
        oc = lax.cond(p["logit_bound"] <= MAX_LOGIT_BOUND,
                      functools.partial(_latent, batch=batch, seq=seq, bounded=True),
                      functools.partial(_latent, batch=batch, seq=seq, bounded=False), qc, kc, vt)
        x2d = _merge_mlp(x2d, oas, mls, ob, oc, p)
    return x2d.reshape(batch, seq, D_MODEL)
```

```python
import functools
import math

import numpy as np
import jax
import jax.numpy as jnp
from jax import lax
from jax.experimental import pallas as pl
from jax.experimental.pallas import tpu as pltpu

F32 = jnp.float32
BF16 = jnp.bfloat16

D_MODEL = 1024
HEAD_DIM = 64
A_HEADS = 6
A_CONFIGS = ((128, 1), (512, 4), (2048, 16))
B_HEADS = 4
B_KV_HEADS = 2
B_RADIUS = 128
C_HEADS = 6
C_NOPE = 64
C_ROPE = 32
C_VDIM = 64
C_QK = C_NOPE + C_ROPE
C_Q_RANK = 256
C_KV_RANK = 128
ROPE_THETA = 10000.0
N_BUCKETS = 32
MAX_DISTANCE = 1024
D_FF = 4 * D_MODEL
EPS = 1e-6
NEG = -1e30

A_W = A_HEADS * HEAD_DIM
B_QW = B_HEADS * HEAD_DIM
B_KVW = B_KV_HEADS * HEAD_DIM
LANES = 128
C_SLOTS = C_HEADS * LANES
IN_COLS = 3 * A_W + B_QW + 2 * B_KVW + C_Q_RANK + C_KV_RANK + LANES
MIX_COLS = A_W + B_QW + C_HEADS * C_VDIM
VT_ROWS = 80
LOG2E = math.log2(math.e)
ML_L_OFFSET = 8

B_HEAD_ORDER = (0, 2, 1, 3)

TM_IN = 512
TQ_BAND = 1024
SUB = 128
TQ_LAT = 2048
BOUND_MARGIN = 1.02
MAX_LOGIT_BOUND = 50.0
LAT_BOUNDED_UNROLL = 16
LAT_SLOTS = 3
LAT_UNROLL = 6
TM_MLP = 512
FF_CHUNK = 1024
VMEM_LIMIT = 56 * 1024 * 1024


def _cparams(sem):
    return pltpu.CompilerParams(dimension_semantics=sem, vmem_limit_bytes=VMEM_LIMIT)


def _const_spec(shape):
    zeros = (0,) * len(shape)
    return pl.BlockSpec(shape, lambda *_: zeros)


def _rope_table_kernel(pos_ref, inv_ref, cos_ref, sin_ref):
    ang = pos_ref[...] * inv_ref[...]
    lane = lax.broadcasted_iota(jnp.int32, ang.shape, 1)
    c = jnp.cos(ang)
    s = jnp.sin(ang)
    first = (lane >= C_NOPE) & (lane < C_NOPE + C_ROPE // 2)
    second = (lane >= C_NOPE + C_ROPE // 2) & (lane < C_QK)
    cos_ref[...] = jnp.where(first | second, c, 1.0)
    sin_ref[...] = jnp.where(first, -s, jnp.where(second, s, 0.0))


def _rope_tables(positions):
    t = positions.size
    half = C_ROPE // 2
    inv = ROPE_THETA ** (-jnp.arange(half, dtype=F32) / half)
    inv_row = jnp.concatenate([jnp.zeros((C_NOPE,), F32), inv, inv, jnp.zeros((LANES - C_QK,), F32)])[None, :]
    pos = positions.astype(F32).reshape(t, 1)
    tm = 2048
    return pl.pallas_call(
        _rope_table_kernel,
        grid=(t // tm,),
        in_specs=[pl.BlockSpec((tm, 1), lambda i: (i, 0)), _const_spec((1, LANES))],
        out_specs=[pl.BlockSpec((tm, LANES), lambda i: (i, 0))] * 2,
        out_shape=[jax.ShapeDtypeStruct((t, LANES), F32)] * 2,
        compiler_params=_cparams(("parallel",)),
        name="rope_tables",
    )(pos, inv_row)


def _bucket_thresholds():
    half = N_BUCKETS // 2
    exact = half // 2
    n = np.arange(1, 2 * MAX_DISTANCE + 2, dtype=np.float64)
    far = exact + (np.log(n / exact) / math.log(MAX_DISTANCE / exact) * (half - exact)).astype(np.int64)
    far = np.minimum(far, half - 1)
    return tuple(int(n[np.argmax(far >= exact + k)]) for k in range(1, half - exact))


def _bias_kernel(table_ref, shift_ref, out_ref, *, radius, dilation, head_cols):
    hsel = pl.program_id(0)
    width = SUB + 2 * radius
    row = lax.broadcasted_iota(jnp.int32, (SUB, width), 0)
    col = lax.broadcasted_iota(jnp.int32, (SUB, width), 1)
    rel = col - radius - row
    n = jnp.abs(rel) * dilation
    half = N_BUCKETS // 2
    exact = half // 2
    far = jnp.full(n.shape, exact, jnp.int32)
    for thr in _bucket_thresholds():
        far = far + (n >= thr).astype(jnp.int32)
    bucket = jnp.where(rel > 0, half, 0) + jnp.where(n < exact, n, far)
    for idx, hc in enumerate(head_cols):
        @pl.when(hsel == idx)
        def _(hc=hc):
            val = jnp.zeros(n.shape, F32)
            for b in range(N_BUCKETS):
                val = jnp.where(bucket == b, table_ref[b, hc], val)
            out_ref[...] = jnp.where(jnp.abs(rel) <= radius, val * LOG2E - shift_ref[0], NEG)


def _bias_tiles(table, shift, radius, dilation, head_cols):
    width = SUB + 2 * radius
    tiles = pl.pallas_call(
        functools.partial(_bias_kernel, radius=radius, dilation=dilation, head_cols=head_cols),
        grid=(len(head_cols),),
        in_specs=[pl.BlockSpec(memory_space=pltpu.SMEM), pl.BlockSpec(memory_space=pltpu.SMEM)],
        out_specs=pl.BlockSpec((None, SUB, width), lambda h: (h, 0, 0)),
        out_shape=jax.ShapeDtypeStruct((len(head_cols), SUB, width), F32),
        compiler_params=_cparams(("arbitrary",)),
        name="bias_tiles",
    )(table, shift)
    return tiles.reshape(len(head_cols) // 2, 2 * SUB, width)


def _group_mean_sq(y, ones_ref, group):
    sq = (y * y).astype(BF16)
    width = y.shape[1]
    parts = []
    for s in range(0, width, 2 * LANES):
        w = min(2 * LANES, width - s)
        parts.append(jnp.dot(sq[:, s:s + w], ones_ref[:w, :w], preferred_element_type=F32))
    out = parts[0] if len(parts) == 1 else jnp.concatenate(parts, axis=1)
    return out * (1.0 / group)


def _slab_roll(y, shift):
    parts = [pltpu.roll(y[:, s:s + LANES], shift, 1) for s in range(0, y.shape[1], LANES)]
    return parts[0] if len(parts) == 1 else jnp.concatenate(parts, axis=1)


def _rope(y, cos_t, sin_t, lane):
    swapped = jnp.where(lane < C_NOPE + C_ROPE // 2, _slab_roll(y, LANES - C_ROPE // 2), _slab_roll(y, C_ROPE // 2))
    return y * cos_t + swapped * sin_t


def _inproj_kernel(x_ref, cos_ref, sin_ref, gmix_ref, w_ref, gqk_ref, ones64_ref, ones128_ref,
                   glq_ref, wuq_ref, gqc_ref, glkv_ref, wuk_ref, wuvt_ref, gkc_ref, qshift_ref, kone_ref,
                   qkv1_ref, qkv4_ref, qkv16_ref, qb_ref, kb_ref, vb_ref, qc_ref, kc_ref, vt_ref, stage_ref, stage4_ref):
    x = x_ref[...]
    h = x * lax.rsqrt(jnp.mean(x * x, axis=-1, keepdims=True) + EPS) * gmix_ref[...]
    y = jnp.dot(h.astype(BF16), w_ref[...], preferred_element_type=F32)

    o_qb = 3 * A_W
    o_kb = o_qb + B_QW
    o_vb = o_kb + B_KVW
    o_cq = o_vb + B_KVW
    o_ckv = o_cq + C_Q_RANK
    o_kr = o_ckv + C_KV_RANK
    yn = jnp.concatenate([y[:, :2 * A_W], y[:, o_qb:o_vb]], axis=1)
    yn = yn * lax.rsqrt(_group_mean_sq(yn, ones64_ref, HEAD_DIM) + EPS) * gqk_ref[...]
    qb_ref[...] = yn[:, 2 * A_W:2 * A_W + B_QW].astype(BF16)
    kb_ref[...] = yn[:, 2 * A_W + B_QW:].astype(BF16)
    n_slabs, tm = stage_ref.shape[0], stage_ref.shape[1]
    for s in range(n_slabs):
        src = yn if s * LANES < 2 * A_W else y
        stage_ref[s] = src[:, s * LANES:(s + 1) * LANES]
    for s in range(n_slabs):
        qkv1_ref[:, s * LANES:(s + 1) * LANES] = stage_ref[s].astype(BF16)
        for c4 in range(4):
            rows = stage_ref[s, pl.ds(c4, tm // 4, stride=4), :]
            qkv4_ref[:, c4 * 3 * A_W + s * LANES:c4 * 3 * A_W + (s + 1) * LANES] = rows.astype(BF16)
            stage4_ref[c4 * n_slabs + s] = rows
    for c16 in range(16):
        for s in range(n_slabs):
            rows = stage4_ref[(c16 % 4) * n_slabs + s, pl.ds(c16 // 4, tm // 16, stride=4), :]
            qkv16_ref[:, c16 * 3 * A_W + s * LANES:c16 * 3 * A_W + (s + 1) * LANES] = rows.astype(BF16)
    vb_ref[...] = y[:, o_vb:o_cq].astype(BF16)

    cos_t = jnp.concatenate([cos_ref[...]] * C_HEADS, axis=1)
    sin_t = jnp.concatenate([sin_ref[...]] * C_HEADS, axis=1)
    lane = lax.broadcasted_iota(jnp.int32, cos_t.shape, 1) % LANES
    cq = y[:, o_cq:o_ckv]
    cq = cq * lax.rsqrt(jnp.mean(cq * cq, axis=-1, keepdims=True) + EPS) * glq_ref[...]
    qc = jnp.dot(cq.astype(BF16), wuq_ref[...], preferred_element_type=F32)
    qc = qc * lax.rsqrt(_group_mean_sq(qc, ones128_ref, C_QK) + EPS) * gqc_ref[...]
    qc_ref[...] = (_rope(qc, cos_t, sin_t, lane) + qshift_ref[...]).astype(BF16)

    ckv = y[:, o_ckv:o_kr]
    ckv = (ckv * lax.rsqrt(jnp.mean(ckv * ckv, axis=-1, keepdims=True) + EPS) * glkv_ref[...]).astype(BF16)
    kr = y[:, o_kr:]
    kc = jnp.dot(ckv, wuk_ref[...], preferred_element_type=F32) + jnp.concatenate([kr] * C_HEADS, axis=1)
    kc = kc * lax.rsqrt(_group_mean_sq(kc, ones128_ref, C_QK) + EPS) * gkc_ref[...]
    kc_ref[...] = (_rope(kc, cos_t, sin_t, lane) + kone_ref[...]).astype(BF16)
    vt = lax.dot_general(wuvt_ref[...], ckv, (((1,), (1,)), ((), ())), preferred_element_type=F32)
    row = lax.broadcasted_iota(jnp.int32, (VT_ROWS - C_VDIM, vt.shape[1]), 0)
    ones_rows = jnp.where(row == 0, 1.0, 0.0).astype(BF16)
    for hd in range(C_HEADS):
        vt_ref[0, hd * VT_ROWS:hd * VT_ROWS + C_VDIM, :] = vt[hd * C_VDIM:(hd + 1) * C_VDIM].astype(BF16)
        vt_ref[0, hd * VT_ROWS + C_VDIM:(hd + 1) * VT_ROWS, :] = ones_rows


def _inproj(x2d, cos_t, sin_t, p):
    t = x2d.shape[0]
    tm = TM_IN
    row = lambda w: pl.BlockSpec((tm, w), lambda i: (i, 0))
    outs = [(B_QW, BF16), (B_KVW, BF16), (B_KVW, BF16)] + [(C_SLOTS, BF16)] * 2
    a_specs = [pl.BlockSpec((tm // r, r * 3 * A_W), lambda i: (i, 0)) for _, r in A_CONFIGS]
    a_shapes = [jax.ShapeDtypeStruct((t // r, r * 3 * A_W), BF16) for _, r in A_CONFIGS]
    return pl.pallas_call(
        _inproj_kernel,
        grid=(t // tm,),
        in_specs=[row(D_MODEL), row(LANES), row(LANES),
                  _const_spec((1, D_MODEL)), _const_spec((D_MODEL, IN_COLS)),
                  _const_spec((1, 2 * A_W + B_QW + B_KVW)),
                  _const_spec((2 * LANES, 2 * LANES)), _const_spec((2 * LANES, 2 * LANES)),
                  _const_spec((1, C_Q_RANK)), _const_spec((C_Q_RANK, C_SLOTS)), _const_spec((1, C_SLOTS)),
                  _const_spec((1, C_KV_RANK)), _const_spec((C_KV_RANK, C_SLOTS)),
                  _const_spec((C_HEADS * C_VDIM, C_KV_RANK)), _const_spec((1, C_SLOTS)),
                  _const_spec((1, C_SLOTS)), _const_spec((1, C_SLOTS))],
        out_specs=a_specs + [row(w) for w, _ in outs]
        + [pl.BlockSpec((1, C_HEADS * VT_ROWS, tm), lambda i: (i, 0, 0))],
        out_shape=a_shapes + [jax.ShapeDtypeStruct((t, w), d) for w, d in outs]
        + [jax.ShapeDtypeStruct((t // tm, C_HEADS * VT_ROWS, tm), BF16)],
        scratch_shapes=[pltpu.VMEM((3 * A_W // LANES, tm, LANES), F32),
                        pltpu.VMEM((4 * 3 * A_W // LANES, tm // 4, LANES), F32)],
        compiler_params=_cparams(("parallel",)),
        name="inproj",
    )(x2d, cos_t, sin_t, p["gmix"], p["w_in"], p["gqk"], p["ones64"], p["ones128"],
      p["glq"], p["wuq"], p["gqc"], p["glkv"], p["wuk"], p["wuvt"], p["gkc"], p["qshift"], p["kone"])


def _window(lo_ref, main_ref, hi_ref, a, radius, tq):
    start, end = a - radius, a + SUB + radius
    parts = []
    if start < 0:
        parts.append(lo_ref[radius + start:radius, :])
        start = 0
    parts.append(main_ref[start:min(end, tq), :])
    if end > tq:
        parts.append(hi_ref[0:end - tq, :])
    return parts[0] if len(parts) == 1 else jnp.concatenate(parts, axis=0)


def _banded_kernel(scal_ref, q_ref, klo_ref, k_ref, khi_ref, vlo_ref, v_ref, vhi_ref, bias_ref, *refs,
                   tq, radius, head_ids, kv_slab, has_sink, emit_ml, bounded, n_tiles):
    o_ref = refs[0]
    ml_ref = refs[1] if emit_ml else None
    p_scr = refs[-1]
    if not bounded:
        s_scr, m_scr = refs[-3:-1]
    sink = lambda hd: scal_ref[1 + hd]
    tile = pl.program_id(2)
    width = SUB + 2 * radius
    lane = lax.broadcasted_iota(jnp.int32, (SUB, LANES), 1)
    col = lax.broadcasted_iota(jnp.int32, (2 * SUB, width), 1)
    low = lane < HEAD_DIM
    subs = list(range(0, tq, SUB))
    slab = lambda w, pair: w[:, kv_slab[pair] * LANES:(kv_slab[pair] + 1) * LANES]
    tiles = [(a, pair) for a in subs for pair in range(len(head_ids))]

    kws = {a: _window(klo_ref, k_ref, khi_ref, a, radius, tq) for a in subs}
    for g, (a, pair) in enumerate(tiles):
        qs = q_ref[a:a + SUB, pair * LANES:(pair + 1) * LANES]
        zero = jnp.zeros_like(qs)
        q2 = jnp.concatenate([jnp.where(low, qs, zero), jnp.where(low, zero, qs)], axis=0)
        s = lax.dot_general(q2, slab(kws[a], pair), (((1,), (1,)), ((), ())), preferred_element_type=F32)
        s = s + bias_ref[pair]
        if a == 0:
            s = jnp.where(col < jnp.where(tile == 0, radius, 0), NEG, s)
        if a == tq - SUB:
            s = jnp.where(col >= jnp.where(tile == n_tiles - 1, SUB + radius, width), NEG, s)
        if bounded:
            p_scr[g] = jnp.exp2(s).astype(BF16)
            continue
        m = jnp.max(s, axis=-1, keepdims=True)
        if has_sink:
            sinks = jnp.concatenate([jnp.full((SUB, 1), sink(hd), F32) for hd in head_ids[pair]], axis=0)
            m = jnp.maximum(m, sinks)
        s_scr[g] = s
        m_scr[g] = jnp.broadcast_to(m, (2 * SUB, LANES))

    for g in range(0 if bounded else len(tiles)):
        m_wide = jnp.concatenate([m_scr[g]] * (width // LANES), axis=1)
        p_scr[g] = jnp.exp2(s_scr[g] - m_wide).astype(BF16)

    vws = {a: _window(vlo_ref, v_ref, vhi_ref, a, radius, tq) for a in subs}
    ml = None
    for g, (a, pair) in enumerate(tiles):
        hd0, hd1 = head_ids[pair]
        vs = slab(vws[a], pair)
        o2 = jnp.dot(p_scr[g], jnp.concatenate([vs, jnp.ones_like(vs)], axis=1), preferred_element_type=F32)
        if bounded:
            m0 = m1 = jnp.full((SUB, LANES), scal_ref[0], F32)
        else:
            m0, m1 = m_scr[g][:SUB], m_scr[g][SUB:]
        l0, l1 = o2[:SUB, LANES:], o2[SUB:, LANES:]
        if has_sink:
            l0 = l0 + jnp.exp2(sink(hd0) - m0)
            l1 = l1 + jnp.exp2(sink(hd1) - m1)
        out = jnp.where(low, o2[:SUB, :LANES], o2[SUB:, :LANES]) / jnp.where(low, l0, l1)
        o_ref[a:a + SUB, pair * LANES:(pair + 1) * LANES] = out.astype(o_ref.dtype)
        if emit_ml:
            if pair == 0:
                ml = jnp.where(lane < ML_L_OFFSET, m0, 0.0) if bounded else jnp.zeros((SUB, LANES), F32)
            if not bounded:
                ml = jnp.where(lane == hd0, m0, ml)
                ml = jnp.where(lane == hd1, m1, ml)
            ml = jnp.where(lane == ML_L_OFFSET + hd0, l0, ml)
            ml = jnp.where(lane == ML_L_OFFSET + hd1, l1, ml)
            if pair == len(head_ids) - 1:
                ml_ref[a:a + SUB, :] = ml


def _banded(q, k, v, bias, scal, *, batch, seq, dilation, radius, widths, head_ids, kv_slab, has_sink, emit_ml,
            bounded, out_dtype):
    n = seq // dilation
    tq = min(TQ_BAND, n)
    n_tiles = n // tq
    qw, kw = widths
    n_sub_tiles = (tq // SUB) * len(head_ids)
    view = lambda t: t[0].reshape(batch, n, t[0].shape[1])
    per_tile = tq // radius

    def main(w, t=(None, 1, 0)):
        return pl.BlockSpec((None, tq, w), lambda b, c, i: (b, i, t[1] * c + t[2]))

    def lo(t):
        return pl.BlockSpec((None, radius, kw),
                            lambda b, c, i: (b, jnp.maximum(i * per_tile - 1, 0), t[1] * c + t[2]))

    def hi(t):
        return pl.BlockSpec((None, radius, kw),
                            lambda b, c, i: (b, jnp.minimum((i + 1) * per_tile, n // radius - 1), t[1] * c + t[2]))

    in_specs = [pl.BlockSpec(memory_space=pltpu.SMEM),
                main(qw, q), lo(k), main(kw, k), hi(k), lo(v), main(kw, v), hi(v), _const_spec(bias.shape)]
    args = [scal, view(q), view(k), view(k), view(k), view(v), view(v), view(v), bias]
    out_specs = [main(qw)]
    out_shape = [jax.ShapeDtypeStruct((batch, n, dilation * qw), out_dtype)]
    if emit_ml:
        out_specs.append(main(LANES))
        out_shape.append(jax.ShapeDtypeStruct((batch, n, dilation * LANES), F32))
    outs = pl.pallas_call(
        functools.partial(_banded_kernel, tq=tq, radius=radius, head_ids=head_ids, kv_slab=kv_slab,
                          has_sink=has_sink, emit_ml=emit_ml, bounded=bounded, n_tiles=n_tiles),
        grid=(batch, dilation, n_tiles),
        in_specs=in_specs,
        out_specs=out_specs,
        out_shape=out_shape,
        scratch_shapes=([] if bounded else [pltpu.VMEM((n_sub_tiles, 2 * SUB, SUB + 2 * radius), F32),
                                            pltpu.VMEM((n_sub_tiles, 2 * SUB, LANES), F32)])
        + [pltpu.VMEM((n_sub_tiles, 2 * SUB, SUB + 2 * radius), BF16)],
        compiler_params=_cparams(("parallel", "parallel", "parallel")),
        name="banded%s_r%d_d%d" % ("_bounded" if bounded else "", radius, dilation),
    )(*args)
    return [o.reshape(batch * n, -1) for o in outs]


def _latent_kernel(q_ref, k_ref, vt_ref, o_ref, s_ref, cmax_ref, m_ref, acc_ref, *, tk):
    tq = q_ref.shape[0]
    nk = k_ref.shape[0] // tk
    ahead = LAT_SLOTS - 1
    qt = q_ref[...].astype(F32).T.astype(BF16)
    m_ref[...] = jnp.full(m_ref.shape, -jnp.inf, F32)
    acc_ref[...] = jnp.zeros(acc_ref.shape, F32)

    def scores(slot, j):
        start = pl.multiple_of(j * tk, tk)
        s = jnp.dot(k_ref[pl.ds(start, tk), :], qt, preferred_element_type=F32)
        s_ref[slot] = s
        cmax_ref[slot] = jnp.max(s, axis=0, keepdims=True)

    def consume(slot, j):
        m_old = m_ref[...]
        m_new = jnp.maximum(m_old, cmax_ref[slot])
        pr = jnp.exp2(s_ref[slot] - m_new).astype(BF16)
        acc_ref[...] = jnp.exp2(m_old - m_new) * acc_ref[...] + jnp.dot(
            vt_ref[j], pr, preferred_element_type=F32)
        m_ref[...] = m_new

    def step(j, u, with_scores):
        if with_scores:
            scores((u + ahead) % LAT_SLOTS, j + ahead)
        consume(u % LAT_SLOTS, j)

    for j in range(ahead):
        scores(j, j)
    trips = (nk - ahead) // LAT_UNROLL

    def body(jj, carry):
        for u in range(LAT_UNROLL):
            step(LAT_UNROLL * jj + u, u, True)
        return carry

    lax.fori_loop(0, trips, body, 0)
    for j in range(trips * LAT_UNROLL, nk):
        step(j, j, j + ahead < nk)
    acc = acc_ref[...]
    o = jnp.concatenate([acc[:C_VDIM] / acc[C_VDIM:C_VDIM + 1], jnp.zeros((LANES - C_VDIM, tq), F32)], axis=0)
    o_ref[...] = o.T.astype(o_ref.dtype)


def _latent_bounded_kernel(q_ref, k_ref, vt_ref, o_ref, acc_ref, *, tk):
    tq = q_ref.shape[0]
    nk = k_ref.shape[0] // tk
    qt = q_ref[...].astype(F32).T.astype(BF16)
    acc_ref[...] = jnp.zeros(acc_ref.shape, F32)

    def body(jj, carry):
        for u in range(LAT_BOUNDED_UNROLL):
            j = LAT_BOUNDED_UNROLL * jj + u
            start = pl.multiple_of(j * tk, tk)
            s = jnp.dot(k_ref[pl.ds(start, tk), :], qt, preferred_element_type=F32)
            acc_ref[...] += jnp.dot(vt_ref[j], jnp.exp2(s).astype(BF16), preferred_element_type=F32)
        return carry

    lax.fori_loop(0, nk // LAT_BOUNDED_UNROLL, body, 0)
    acc = acc_ref[...]
    o = jnp.concatenate([acc[:C_VDIM] / acc[C_VDIM:C_VDIM + 1], jnp.zeros((LANES - C_VDIM, tq), F32)], axis=0)
    o_ref[...] = o.T.astype(o_ref.dtype)


def _latent(qc, kc, vt, batch, seq, bounded):
    t = batch * seq
    tq, tk = TQ_LAT, vt.shape[2]
    nq = seq // tq
    if bounded:
        body = functools.partial(_latent_bounded_kernel, tk=tk)
        scratch = [pltpu.VMEM((VT_ROWS, tq), F32)]
    else:
        body = functools.partial(_latent_kernel, tk=tk)
        scratch = [pltpu.VMEM((LAT_SLOTS, tk, tq), F32), pltpu.VMEM((LAT_SLOTS, 1, tq), F32),
                   pltpu.VMEM((1, tq), F32), pltpu.VMEM((VT_ROWS, tq), F32)]
    return pl.pallas_call(
        body,
        grid=(batch, C_HEADS, nq),
        in_specs=[pl.BlockSpec((tq, LANES), lambda b, h, i: (b * nq + i, h)),
                  pl.BlockSpec((seq, LANES), lambda b, h, i: (b, h)),
                  pl.BlockSpec((seq // tk, VT_ROWS, tk), lambda b, h, i: (b, h, 0))],
        out_specs=pl.BlockSpec((tq, LANES), lambda b, h, i: (b * nq + i, h)),
        out_shape=jax.ShapeDtypeStruct((t, C_SLOTS), BF16),
        scratch_shapes=scratch,
        compiler_params=_cparams(("parallel", "parallel", "parallel")),
        name="latent_bounded" if bounded else "latent",
    )(qc, kc, vt)


def _merge_mlp_kernel(x_ref, oa1_ref, oa2_ref, oa3_ref, ml1_ref, ml2_ref, ml3_ref, ob_ref, oc_ref,
                      expand_ref, gout_ref, wout_ref, gmlp_ref, wup_ref, wdown_ref, out_ref, oa_s, ml_s):
    tm = x_ref.shape[0]
    oas, mls = [oa1_ref[...].astype(F32)], [ml1_ref[...]]
    for idx, (o_ref, l_ref) in enumerate(((oa2_ref, ml2_ref), (oa3_ref, ml3_ref))):
        r = A_CONFIGS[idx + 1][1]
        for c in range(r):
            for s in range(A_W // LANES):
                col = c * A_W + s * LANES
                oa_s[idx, s, pl.ds(c, tm // r, stride=r), :] = o_ref[:, col:col + LANES].astype(F32)
            ml_s[idx, pl.ds(c, tm // r, stride=r), :] = l_ref[:, c * LANES:(c + 1) * LANES]
        oas.append(jnp.concatenate([oa_s[idx, s] for s in range(A_W // LANES)], axis=1))
        mls.append(ml_s[idx])
    lane = lax.broadcasted_iota(jnp.int32, mls[0].shape, 1)
    m_all = jnp.maximum(jnp.maximum(mls[0], mls[1]), mls[2])
    ws = [pltpu.roll(ml, LANES - ML_L_OFFSET, 1) * jnp.exp2(ml - m_all) for ml in mls]
    wsum = ws[0] + ws[1] + ws[2]
    packed = jnp.zeros_like(wsum)
    for c, w in enumerate(ws):
        wn = jnp.where(lane < A_HEADS, w / wsum, 0.0)
        packed = packed + (wn if c == 0 else pltpu.roll(wn, ML_L_OFFSET * c, 1))
    hi = packed.astype(BF16)
    lo = (packed - hi.astype(F32)).astype(BF16)
    spread = (jnp.dot(hi, expand_ref[...], preferred_element_type=F32)
              + jnp.dot(lo, expand_ref[...], preferred_element_type=F32))
    oa = spread[:, :A_W] * oas[0] + spread[:, A_W:2 * A_W] * oas[1] + spread[:, 2 * A_W:] * oas[2]

    def group_norm(v, width):
        return v * lax.rsqrt(jnp.sum(v * v, axis=-1, keepdims=True) * (1.0 / width) + EPS)

    ob = ob_ref[...].astype(F32)
    oc_wide = oc_ref[...].astype(F32)
    half = lax.broadcasted_iota(jnp.int32, (tm, LANES), 1) < C_VDIM
    oc = jnp.concatenate(
        [jnp.where(half, oc_wide[:, 2 * p * LANES:(2 * p + 1) * LANES],
                   pltpu.roll(oc_wide[:, (2 * p + 1) * LANES:(2 * p + 2) * LANES], C_VDIM, 1))
         for p in range(C_HEADS // 2)], axis=1)
    mixed = jnp.concatenate([group_norm(oa, A_W), group_norm(ob, B_QW), group_norm(oc, C_HEADS * C_VDIM)], axis=1)
    mixed = (mixed * gout_ref[...]).astype(BF16)
    x = x_ref[...] + jnp.dot(mixed, wout_ref[...], preferred_element_type=F32)

    h = (x * lax.rsqrt(jnp.mean(x * x, axis=-1, keepdims=True) + EPS) * gmlp_ref[...]).astype(BF16)
    acc = x
    for s in range(0, D_FF, FF_CHUNK):
        u = jnp.dot(h, wup_ref[:, s:s + FF_CHUNK], preferred_element_type=F32)
        u = jnp.square(jnp.maximum(u, 0.0)).astype(BF16)
        acc = acc + jnp.dot(u, wdown_ref[s:s + FF_CHUNK, :], preferred_element_type=F32)
    out_ref[...] = acc


def _merge_mlp(x2d, oas, mls, ob, oc, p):
    t = x2d.shape[0]
    tm = TM_MLP
    row = lambda w: pl.BlockSpec((tm, w), lambda i: (i, 0))
    strided = lambda w, r: pl.BlockSpec((tm // r, r * w), lambda i: (i, 0))
    single = lambda shape: pl.BlockSpec(shape, lambda i: (0,) * len(shape), pipeline_mode=pl.Buffered(1))
    return pl.pallas_call(
        _merge_mlp_kernel,
        grid=(t // tm,),
        in_specs=[row(D_MODEL)] + [strided(A_W, r) for _, r in A_CONFIGS]
        + [strided(LANES, r) for _, r in A_CONFIGS] + [row(B_QW), row(C_SLOTS),
                  single((LANES, 3 * A_W)), single((1, MIX_COLS)), single((MIX_COLS, D_MODEL)),
                  single((1, D_MODEL)), single((D_MODEL, D_FF)), single((D_FF, D_MODEL))],
        out_specs=row(D_MODEL),
        out_shape=jax.ShapeDtypeStruct((t, D_MODEL), F32),
        scratch_shapes=[pltpu.VMEM((2, A_W // LANES, tm, LANES), F32), pltpu.VMEM((2, tm, LANES), F32)],
        compiler_params=_cparams(("parallel",)),
        name="merge_mlp",
    )(x2d, *oas, *mls, ob, oc, p["expand"], p["gout"], p["w_out"], p["gmlp"], p["w_up"], p["w_down"])


def _block_diag_ones(group):
    idx = np.arange(2 * LANES) // group
    return jnp.asarray((idx[:, None] == idx[None, :]).astype(np.float32), dtype=BF16)


def _expand_matrix():
    e = np.zeros((LANES, len(A_CONFIGS) * A_W), np.float32)
    for c in range(len(A_CONFIGS)):
        for h in range(A_HEADS):
            e[ML_L_OFFSET * c + h, c * A_W + h * HEAD_DIM:c * A_W + (h + 1) * HEAD_DIM] = 1.0
    return jnp.asarray(e, dtype=BF16)


def _pad_heads(w, heads, used, lo=0):
    lead = w.shape[:-1]
    w = w.reshape(lead + (heads, used))
    pad = [(0, 0)] * len(lead) + [(0, 0), (lo, LANES - lo - used)]
    return jnp.pad(w, pad).reshape(lead + (heads * LANES,))


def _layer_params(i, norm_mix, w_in, qk_gain_a, qk_gain_b, q_lat_gain, kv_lat_gain, w_uq, w_ukv, qk_gain_c,
                  out_norm, w_out, norm_mlp, w_up, w_down):
    w = w_in[i]
    o = np.cumsum((A_W, A_W, A_W, B_QW, B_KVW, B_KVW, C_Q_RANK, C_KV_RANK)).tolist()
    qb = w[:, o[2]:o[3]].reshape(D_MODEL, B_HEADS, HEAD_DIM)[:, B_HEAD_ORDER, :].reshape(D_MODEL, B_QW)
    kr = _pad_heads(w[:, o[7]:], 1, C_ROPE, lo=C_NOPE)
    w_in_p = jnp.concatenate([w[:, :o[2]], qb, w[:, o[3]:o[7]], kr], axis=1).astype(BF16)

    scale = HEAD_DIM ** -0.5 * LOG2E
    gqk = jnp.concatenate([jnp.tile(qk_gain_a[i, 0], A_HEADS) * scale, jnp.tile(qk_gain_a[i, 1], A_HEADS),
                           jnp.tile(qk_gain_b[i, 0], B_HEADS) * scale, jnp.tile(qk_gain_b[i, 1], B_KV_HEADS)])
    ukv = w_ukv[i].reshape(C_KV_RANK, C_HEADS, C_NOPE + C_VDIM)
    wuk = _pad_heads(ukv[:, :, :C_NOPE].reshape(C_KV_RANK, -1), C_HEADS, C_NOPE)
    wuvt = ukv[:, :, C_NOPE:].reshape(C_KV_RANK, C_HEADS * C_VDIM).T
    g = out_norm[i]
    gb = g[A_W:A_W + B_QW].reshape(B_HEADS, HEAD_DIM)[B_HEAD_ORDER, :].reshape(B_QW)
    gout = jnp.concatenate([g[:A_W], gb, g[A_W + B_QW:]])
    wo = w_out[i]
    wob = wo[A_W:A_W + B_QW].reshape(B_HEADS, HEAD_DIM, D_MODEL)[B_HEAD_ORDER, :, :].reshape(B_QW, D_MODEL)
    w_out_p = jnp.concatenate([wo[:A_W], wob, wo[A_W + B_QW:]], axis=0).astype(BF16)
    bound = (C_QK ** 0.5 * LOG2E * BOUND_MARGIN) * jnp.max(jnp.abs(qk_gain_c[i, 0])) * jnp.max(jnp.abs(qk_gain_c[i, 1]))
    return {
        "gmix": norm_mix[i][None, :],
        "w_in": w_in_p,
        "gqk": gqk[None, :],
        "ones64": _block_diag_ones(HEAD_DIM),
        "ones128": _block_diag_ones(LANES),
        "glq": q_lat_gain[i][None, :],
        "wuq": _pad_heads(w_uq[i], C_HEADS, C_QK).astype(BF16),
        "gqc": jnp.tile(_pad_heads(qk_gain_c[i, 0] * (C_QK ** -0.5 * LOG2E), 1, C_QK), C_HEADS)[None, :],
        "glkv": kv_lat_gain[i][None, :],
        "wuk": wuk.astype(BF16),
        "wuvt": wuvt.astype(BF16),
        "gkc": jnp.tile(_pad_heads(qk_gain_c[i, 1], 1, C_QK), C_HEADS)[None, :],
        "qshift": jnp.tile(_pad_heads(-bound[None], 1, 1, lo=C_QK), C_HEADS)[None, :],
        "kone": jnp.tile(_pad_heads(jnp.ones((1,), F32), 1, 1, lo=C_QK), C_HEADS)[None, :],
        "logit_bound": bound,
        "expand": _expand_matrix(),
        "gout": gout[None, :],
        "w_out": w_out_p,
        "gmlp": norm_mlp[i][None, :],
        "w_up": w_up[i].astype(BF16),
        "w_down": w_down[i].astype(BF16),
    }


def kernel(x, positions, rel_bias_table, norm_mix, w_in, qk_gain_a, qk_gain_b, sink_b, q_lat_gain, kv_lat_gain,
           w_uq, w_ukv, qk_gain_c, out_norm, w_out, norm_mlp, w_up, w_down):
    batch, seq, _ = x.shape
    depth = w_in.shape[0]
    x2d = x.reshape(batch * seq, D_MODEL)
    cos_t, sin_t = _rope_tables(positions)
    a_pairs = tuple((2 * p, 2 * p + 1) for p in range(A_HEADS // 2))
    b_pairs = (B_HEAD_ORDER[:2], B_HEAD_ORDER[2:])
    b_cols = tuple(A_HEADS + h for h in B_HEAD_ORDER)

    def bias_tables(shift):
        tiles_a = [_bias_tiles(rel_bias_table, shift, window // (2 * r), r, tuple(range(A_HEADS)))
                   for window, r in A_CONFIGS]
        return tiles_a, _bias_tiles(rel_bias_table, shift, B_RADIUS, 1, b_cols)

    gain_bound = lambda gains: jnp.max(jnp.max(jnp.abs(gains[:, 0]), axis=-1) * jnp.max(jnp.abs(gains[:, 1]), axis=-1))
    qk_bound = HEAD_DIM ** 0.5 * LOG2E * BOUND_MARGIN * jnp.maximum(gain_bound(qk_gain_a), gain_bound(qk_gain_b))
    band_shift = jnp.maximum(qk_bound + LOG2E * jnp.max(jnp.abs(rel_bias_table)), LOG2E * jnp.max(sink_b))
    shifted_tables = bias_tables(band_shift[None])

    def banded_all(bounded, qkv1, qkv4, qkv16, qb, kb, vb, sinks):
        shift = band_shift[None] if bounded else jnp.zeros((1,), F32)
        tables_a, table_b = shifted_tables if bounded else bias_tables(shift)
        scal = jnp.concatenate([shift, sinks * LOG2E])
        outs = []
        for (window, r), bias, qkv in zip(A_CONFIGS, tables_a, (qkv1, qkv4, qkv16)):
            outs += _banded((qkv, 3, 0), (qkv, 3, 1), (qkv, 3, 2), bias, scal, batch=batch, seq=seq, dilation=r,
                            radius=window // (2 * r), widths=(A_W, A_W), head_ids=a_pairs, kv_slab=(0, 1, 2),
                            has_sink=False, emit_ml=True, bounded=bounded, out_dtype=BF16)
        outs += _banded((qb, 1, 0), (kb, 1, 0), (vb, 1, 0), table_b, scal, batch=batch, seq=seq, dilation=1,
                        radius=B_RADIUS, widths=(B_QW, B_KVW), head_ids=b_pairs, kv_slab=(0, 0),
                        has_sink=True, emit_ml=False, bounded=bounded, out_dtype=BF16)
        return tuple(outs)

    for i in range(depth):
        p = _layer_params(i, norm_mix, w_in, qk_gain_a, qk_gain_b, q_lat_gain, kv_lat_gain, w_uq, w_ukv,
                          qk_gain_c, out_norm, w_out, norm_mlp, w_up, w_down)
        qkv1, qkv4, qkv16, qb, kb, vb, qc, kc, vt = _inproj(x2d, cos_t, sin_t, p)
        o1, ml1, o4, ml4, o16, ml16, ob = lax.cond(
            band_shift <= MAX_LOGIT_BOUND, functools.partial(banded_all, True), functools.partial(banded_all, False),
            qkv1, qkv4, qkv16, qb, kb, vb, sink_b[i])
        oas, mls = [o1, o4, o16], [ml1, ml4, ml16]
        oc = lax.cond(p["logit_bound"] <= MAX_LOGIT_BOUND,
                      functools.partial(_latent, batch=batch, seq=seq, bounded=True),
                      functools.partial(_latent, batch=batch, seq=seq, bounded=False), qc, kc, vt)
        x2d = _merge_mlp(x2d, oas, mls, ob, oc, p)
    return x2d.reshape(batch, seq, D_MODEL)
```

```python
import functools
import math

import numpy as np
import jax
import jax.numpy as jnp
from jax import lax
from jax.experimental import pallas as pl
from jax.experimental.pallas import tpu as pltpu

F32 = jnp.float32
BF16 = jnp.bfloat16

D_MODEL = 1024
HEAD_DIM = 64
A_HEADS = 6
A_CONFIGS = ((128, 1), (512, 4), (2048, 16))
B_HEADS = 4
B_KV_HEADS = 2
B_RADIUS = 128
C_HEADS = 6
C_NOPE = 64
C_ROPE = 32
C_VDIM = 64
C_QK = C_NOPE + C_ROPE
C_Q_RANK = 256
C_KV_RANK = 128
ROPE_THETA = 10000.0
N_BUCKETS = 32
MAX_DISTANCE = 1024
D_FF = 4 * D_MODEL
EPS = 1e-6
NEG = -1e30

A_W = A_HEADS * HEAD_DIM
B_QW = B_HEADS * HEAD_DIM
B_KVW = B_KV_HEADS * HEAD_DIM
LANES = 128
C_SLOTS = C_HEADS * LANES
IN_COLS = 3 * A_W + B_QW + 2 * B_KVW + C_Q_RANK + C_KV_RANK + LANES
MIX_COLS = A_W + B_QW + C_HEADS * C_VDIM
VT_ROWS = 80
LOG2E = math.log2(math.e)
ML_L_OFFSET = 8

B_HEAD_ORDER = (0, 2, 1, 3)

TM_IN = 512
TQ_BAND = 1024
SUB = 128
TQ_LAT = 2048
BOUND_MARGIN = 1.02
MAX_LOGIT_BOUND = 50.0
LAT_BOUNDED_UNROLL = 16
LAT_SLOTS = 3
LAT_UNROLL = 6
TM_MLP = 512
FF_CHUNK = 1024
VMEM_LIMIT = 56 * 1024 * 1024


def _cparams(sem):
    return pltpu.CompilerParams(dimension_semantics=sem, vmem_limit_bytes=VMEM_LIMIT)


def _const_spec(shape):
    zeros = (0,) * len(shape)
    return pl.BlockSpec(shape, lambda *_: zeros)


def _rope_table_kernel(pos_ref, inv_ref, cos_ref, sin_ref):
    ang = pos_ref[...] * inv_ref[...]
    lane = lax.broadcasted_iota(jnp.int32, ang.shape, 1)
    c = jnp.cos(ang)
    s = jnp.sin(ang)
    s = jnp.where(lane % C_ROPE < C_ROPE // 2, -s, s)
    live = (lane >= C_NOPE) & (lane < C_QK)
    per_row = LANES // C_ROPE
    for a in range(per_row):
        shift = (C_NOPE - a * C_ROPE) % LANES
        rows = pl.ds(a, ang.shape[0], stride=per_row)
        cos_ref[rows, :] = jnp.where(live, c if shift == 0 else pltpu.roll(c, shift, 1), 1.0)
        sin_ref[rows, :] = jnp.where(live, s if shift == 0 else pltpu.roll(s, shift, 1), 0.0)


def _rope_tables(positions):
    t = positions.size
    half = C_ROPE // 2
    per_row = LANES // C_ROPE
    inv = ROPE_THETA ** (-jnp.arange(half, dtype=F32) / half)
    inv_row = jnp.tile(inv, 2 * per_row)[None, :]
    pos = jnp.repeat(positions.astype(F32).reshape(t // per_row, per_row), C_ROPE, axis=1)
    tm = 2048
    return pl.pallas_call(
        _rope_table_kernel,
        grid=(t // tm,),
        in_specs=[pl.BlockSpec((tm // per_row, LANES), lambda i: (i, 0)), _const_spec((1, LANES))],
        out_specs=[pl.BlockSpec((tm, LANES), lambda i: (i, 0))] * 2,
        out_shape=[jax.ShapeDtypeStruct((t, LANES), F32)] * 2,
        compiler_params=_cparams(("parallel",)),
        name="rope_tables",
    )(pos, inv_row)


def _bucket_thresholds():
    half = N_BUCKETS // 2
    exact = half // 2
    n = np.arange(1, 2 * MAX_DISTANCE + 2, dtype=np.float64)
    far = exact + (np.log(n / exact) / math.log(MAX_DISTANCE / exact) * (half - exact)).astype(np.int64)
    far = np.minimum(far, half - 1)
    return tuple(int(n[np.argmax(far >= exact + k)]) for k in range(1, half - exact))


def _bias_kernel(table_ref, shift_ref, out_ref, *, radius, dilation, head_cols):
    hsel = pl.program_id(0)
    width = SUB + 2 * radius
    row = lax.broadcasted_iota(jnp.int32, (SUB, width), 0)
    col = lax.broadcasted_iota(jnp.int32, (SUB, width), 1)
    rel = col - radius - row
    n = jnp.abs(rel) * dilation
    half = N_BUCKETS // 2
    exact = half // 2
    far = jnp.full(n.shape, exact, jnp.int32)
    for thr in _bucket_thresholds():
        far = far + (n >= thr).astype(jnp.int32)
    bucket = jnp.where(rel > 0, half, 0) + jnp.where(n < exact, n, far)
    for idx, hc in enumerate(head_cols):
        @pl.when(hsel == idx)
        def _(hc=hc):
            val = jnp.zeros(n.shape, F32)
            for b in range(N_BUCKETS):
                val = jnp.where(bucket == b, table_ref[b, hc], val)
            out_ref[...] = jnp.where(jnp.abs(rel) <= radius, val * LOG2E - shift_ref[0], NEG)


def _bias_tiles(table, shift, radius, dilation, head_cols):
    width = SUB + 2 * radius
    tiles = pl.pallas_call(
        functools.partial(_bias_kernel, radius=radius, dilation=dilation, head_cols=head_cols),
        grid=(len(head_cols),),
        in_specs=[pl.BlockSpec(memory_space=pltpu.SMEM), pl.BlockSpec(memory_space=pltpu.SMEM)],
        out_specs=pl.BlockSpec((None, SUB, width), lambda h: (h, 0, 0)),
        out_shape=jax.ShapeDtypeStruct((len(head_cols), SUB, width), F32),
        compiler_params=_cparams(("arbitrary",)),
        name="bias_tiles",
    )(table, shift)
    return tiles.reshape(len(head_cols) // 2, 2 * SUB, width)


def _group_mean_sq(y, ones_ref, group):
    sq = (y * y).astype(BF16)
    width = y.shape[1]
    parts = []
    for s in range(0, width, 2 * LANES):
        w = min(2 * LANES, width - s)
        parts.append(jnp.dot(sq[:, s:s + w], ones_ref[:w, :w], preferred_element_type=F32))
    out = parts[0] if len(parts) == 1 else jnp.concatenate(parts, axis=1)
    return out * (1.0 / group)


def _slab_roll(y, shift):
    parts = [pltpu.roll(y[:, s:s + LANES], shift, 1) for s in range(0, y.shape[1], LANES)]
    return parts[0] if len(parts) == 1 else jnp.concatenate(parts, axis=1)


def _rope(y, cos_t, sin_t, lane):
    swapped = jnp.where(lane < C_NOPE + C_ROPE // 2, _slab_roll(y, LANES - C_ROPE // 2), _slab_roll(y, C_ROPE // 2))
    return y * cos_t + swapped * sin_t


def _inproj_kernel(x_ref, cos_ref, sin_ref, gmix_ref, w_ref, gqk_ref, ones64_ref, ones128_ref,
                   glq_ref, wuq_ref, gqc_ref, glkv_ref, wuk_ref, wuvt_ref, gkc_ref, gkr_ref, qshift_ref, kone_ref,
                   qkv1_ref, qkv4_ref, qkv16_ref, qb_ref, kb_ref, vb_ref, qc_ref, kc_ref, vt_ref, stage_ref, stage4_ref):
    x = x_ref[...]
    h = x * lax.rsqrt(jnp.mean(x * x, axis=-1, keepdims=True) + EPS) * gmix_ref[...]
    y = jnp.dot(h.astype(BF16), w_ref[...], preferred_element_type=F32)

    o_qb = 3 * A_W
    o_kb = o_qb + B_QW
    o_vb = o_kb + B_KVW
    o_cq = o_vb + B_KVW
    o_ckv = o_cq + C_Q_RANK
    o_kr = o_ckv + C_KV_RANK
    yn = jnp.concatenate([y[:, :2 * A_W], y[:, o_qb:o_vb]], axis=1)
    yn = yn * lax.rsqrt(_group_mean_sq(yn, ones64_ref, HEAD_DIM) + EPS) * gqk_ref[...]
    qb_ref[...] = yn[:, 2 * A_W:2 * A_W + B_QW].astype(BF16)
    kb_ref[...] = yn[:, 2 * A_W + B_QW:].astype(BF16)
    n_slabs, tm = stage_ref.shape[0], stage_ref.shape[1]
    for s in range(n_slabs):
        src = yn if s * LANES < 2 * A_W else y
        stage_ref[s] = src[:, s * LANES:(s + 1) * LANES]
    for s in range(n_slabs):
        qkv1_ref[:, s * LANES:(s + 1) * LANES] = stage_ref[s].astype(BF16)
        for c4 in range(4):
            rows = stage_ref[s, pl.ds(c4, tm // 4, stride=4), :]
            qkv4_ref[:, c4 * 3 * A_W + s * LANES:c4 * 3 * A_W + (s + 1) * LANES] = rows.astype(BF16)
            stage4_ref[c4 * n_slabs + s] = rows
    for c16 in range(16):
        for s in range(n_slabs):
            rows = stage4_ref[(c16 % 4) * n_slabs + s, pl.ds(c16 // 4, tm // 16, stride=4), :]
            qkv16_ref[:, c16 * 3 * A_W + s * LANES:c16 * 3 * A_W + (s + 1) * LANES] = rows.astype(BF16)
    vb_ref[...] = y[:, o_vb:o_cq].astype(BF16)

    cos_t = jnp.concatenate([cos_ref[...]] * C_HEADS, axis=1)
    sin_t = jnp.concatenate([sin_ref[...]] * C_HEADS, axis=1)
    lane = lax.broadcasted_iota(jnp.int32, cos_t.shape, 1) % LANES
    cq = y[:, o_cq:o_ckv]
    cq = cq * lax.rsqrt(jnp.mean(cq * cq, axis=-1, keepdims=True) + EPS) * glq_ref[...]
    qc = jnp.dot(cq.astype(BF16), wuq_ref[...], preferred_element_type=F32)
    qc = qc * lax.rsqrt(_group_mean_sq(qc, ones128_ref, C_QK) + EPS) * gqc_ref[...]
    qc_ref[...] = (_rope(qc, cos_t, sin_t, lane) + qshift_ref[...]).astype(BF16)

    ckv = y[:, o_ckv:o_kr]
    ckv = (ckv * lax.rsqrt(jnp.mean(ckv * ckv, axis=-1, keepdims=True) + EPS) * glkv_ref[...]).astype(BF16)
    kr = y[:, o_kr:]
    kn = jnp.dot(ckv, wuk_ref[...], preferred_element_type=F32)
    inv_rms = lax.rsqrt(_group_mean_sq(kn + jnp.concatenate([kr] * C_HEADS, axis=1), ones128_ref, C_QK) + EPS)
    lane_slab = lax.broadcasted_iota(jnp.int32, kr.shape, 1)
    kr_rot = _rope(kr * gkr_ref[...], cos_ref[...], sin_ref[...], lane_slab)
    kc = (kn * gkc_ref[...] + jnp.concatenate([kr_rot] * C_HEADS, axis=1)) * inv_rms
    kc_ref[...] = (kc + kone_ref[...]).astype(BF16)
    vt = lax.dot_general(wuvt_ref[...], ckv, (((1,), (1,)), ((), ())), preferred_element_type=F32)
    row = lax.broadcasted_iota(jnp.int32, (VT_ROWS - C_VDIM, vt.shape[1]), 0)
    ones_rows = jnp.where(row == 0, 1.0, 0.0).astype(BF16)
    for hd in range(C_HEADS):
        vt_ref[0, hd * VT_ROWS:hd * VT_ROWS + C_VDIM, :] = vt[hd * C_VDIM:(hd + 1) * C_VDIM].astype(BF16)
        vt_ref[0, hd * VT_ROWS + C_VDIM:(hd + 1) * VT_ROWS, :] = ones_rows


def _inproj(x2d, cos_t, sin_t, p):
    t = x2d.shape[0]
    tm = TM_IN
    row = lambda w: pl.BlockSpec((tm, w), lambda i: (i, 0))
    outs = [(B_QW, BF16), (B_KVW, BF16), (B_KVW, BF16)] + [(C_SLOTS, BF16)] * 2
    a_specs = [pl.BlockSpec((tm // r, r * 3 * A_W), lambda i: (i, 0)) for _, r in A_CONFIGS]
    a_shapes = [jax.ShapeDtypeStruct((t // r, r * 3 * A_W), BF16) for _, r in A_CONFIGS]
    return pl.pallas_call(
        _inproj_kernel,
        grid=(t // tm,),
        in_specs=[row(D_MODEL), row(LANES), row(LANES),
                  _const_spec((1, D_MODEL)), _const_spec((D_MODEL, IN_COLS)),
                  _const_spec((1, 2 * A_W + B_QW + B_KVW)),
                  _const_spec((2 * LANES, 2 * LANES)), _const_spec((2 * LANES, 2 * LANES)),
                  _const_spec((1, C_Q_RANK)), _const_spec((C_Q_RANK, C_SLOTS)), _const_spec((1, C_SLOTS)),
                  _const_spec((1, C_KV_RANK)), _const_spec((C_KV_RANK, C_SLOTS)),
                  _const_spec((C_HEADS * C_VDIM, C_KV_RANK)), _const_spec((1, C_SLOTS)), _const_spec((1, LANES)),
                  _const_spec((1, C_SLOTS)), _const_spec((1, C_SLOTS))],
        out_specs=a_specs + [row(w) for w, _ in outs]
        + [pl.BlockSpec((1, C_HEADS * VT_ROWS, tm), lambda i: (i, 0, 0))],
        out_shape=a_shapes + [jax.ShapeDtypeStruct((t, w), d) for w, d in outs]
        + [jax.ShapeDtypeStruct((t // tm, C_HEADS * VT_ROWS, tm), BF16)],
        scratch_shapes=[pltpu.VMEM((3 * A_W // LANES, tm, LANES), F32),
                        pltpu.VMEM((4 * 3 * A_W // LANES, tm // 4, LANES), F32)],
        compiler_params=_cparams(("parallel",)),
        name="inproj",
    )(x2d, cos_t, sin_t, p["gmix"], p["w_in"], p["gqk"], p["ones64"], p["ones128"],
      p["glq"], p["wuq"], p["gqc"], p["glkv"], p["wuk"], p["wuvt"], p["gkc"], p["gkc"][:, :LANES], p["qshift"], p["kone"])


def _window(lo_ref, main_ref, hi_ref, a, radius, tq):
    start, end = a - radius, a + SUB + radius
    parts = []
    if start < 0:
        parts.append(lo_ref[radius + start:radius, :])
        start = 0
    parts.append(main_ref[start:min(end, tq), :])
    if end > tq:
        parts.append(hi_ref[0:end - tq, :])
    return parts[0] if len(parts) == 1 else jnp.concatenate(parts, axis=0)


def _banded_kernel(scal_ref, q_ref, klo_ref, k_ref, khi_ref, vlo_ref, v_ref, vhi_ref, bias_ref, *refs,
                   tq, radius, head_ids, kv_slab, has_sink, emit_ml, bounded, n_tiles):
    o_ref = refs[0]
    ml_ref = refs[1] if emit_ml else None
    p_scr = refs[-1]
    if not bounded:
        s_scr, m_scr = refs[-3:-1]
    sink = lambda hd: scal_ref[1 + hd]
    tile = pl.program_id(2)
    width = SUB + 2 * radius
    lane = lax.broadcasted_iota(jnp.int32, (SUB, LANES), 1)
    col = lax.broadcasted_iota(jnp.int32, (2 * SUB, width), 1)
    low = lane < HEAD_DIM
    subs = list(range(0, tq, SUB))
    slab = lambda w, pair: w[:, kv_slab[pair] * LANES:(kv_slab[pair] + 1) * LANES]
    tiles = [(a, pair) for a in subs for pair in range(len(head_ids))]

    kws = {a: _window(klo_ref, k_ref, khi_ref, a, radius, tq) for a in subs}
    for g, (a, pair) in enumerate(tiles):
        qs = q_ref[a:a + SUB, pair * LANES:(pair + 1) * LANES]
        zero = jnp.zeros_like(qs)
        q2 = jnp.concatenate([jnp.where(low, qs, zero), jnp.where(low, zero, qs)], axis=0)
        s = lax.dot_general(q2, slab(kws[a], pair), (((1,), (1,)), ((), ())), preferred_element_type=F32)
        s = s + bias_ref[pair]
        if a == 0:
            s = jnp.where(col < jnp.where(tile == 0, radius, 0), NEG, s)
        if a == tq - SUB:
            s = jnp.where(col >= jnp.where(tile == n_tiles - 1, SUB + radius, width), NEG, s)
        if bounded:
            p_scr[g] = jnp.exp2(s).astype(BF16)
            continue
        m = jnp.max(s, axis=-1, keepdims=True)
        if has_sink:
            sinks = jnp.concatenate([jnp.full((SUB, 1), sink(hd), F32) for hd in head_ids[pair]], axis=0)
            m = jnp.maximum(m, sinks)
        s_scr[g] = s
        m_scr[g] = jnp.broadcast_to(m, (2 * SUB, LANES))

    for g in range(0 if bounded else len(tiles)):
        m_wide = jnp.concatenate([m_scr[g]] * (width // LANES), axis=1)
        p_scr[g] = jnp.exp2(s_scr[g] - m_wide).astype(BF16)

    vws = {a: _window(vlo_ref, v_ref, vhi_ref, a, radius, tq) for a in subs}
    ml = None
    for g, (a, pair) in enumerate(tiles):
        hd0, hd1 = head_ids[pair]
        vs = slab(vws[a], pair)
        o2 = jnp.dot(p_scr[g], jnp.concatenate([vs, jnp.ones_like(vs)], axis=1), preferred_element_type=F32)
        if bounded:
            m0 = m1 = jnp.full((SUB, LANES), scal_ref[0], F32)
        else:
            m0, m1 = m_scr[g][:SUB], m_scr[g][SUB:]
        l0, l1 = o2[:SUB, LANES:], o2[SUB:, LANES:]
        if has_sink:
            l0 = l0 + jnp.exp2(sink(hd0) - m0)
            l1 = l1 + jnp.exp2(sink(hd1) - m1)
        out = jnp.where(low, o2[:SUB, :LANES], o2[SUB:, :LANES]) / jnp.where(low, l0, l1)
        o_ref[a:a + SUB, pair * LANES:(pair + 1) * LANES] = out.astype(o_ref.dtype)
        if emit_ml:
            if pair == 0:
                ml = jnp.where(lane < ML_L_OFFSET, m0, 0.0) if bounded else jnp.zeros((SUB, LANES), F32)
            if not bounded:
                ml = jnp.where(lane == hd0, m0, ml)
                ml = jnp.where(lane == hd1, m1, ml)
            ml = jnp.where(lane == ML_L_OFFSET + hd0, l0, ml)
            ml = jnp.where(lane == ML_L_OFFSET + hd1, l1, ml)
            if pair == len(head_ids) - 1:
                ml_ref[a:a + SUB, :] = ml


def _banded(q, k, v, bias, scal, *, batch, seq, dilation, radius, widths, head_ids, kv_slab, has_sink, emit_ml,
            bounded, out_dtype):
    n = seq // dilation
    tq = min(TQ_BAND, n)
    n_tiles = n // tq
    qw, kw = widths
    n_sub_tiles = (tq // SUB) * len(head_ids)
    view = lambda t: t[0].reshape(batch, n, t[0].shape[1])
    per_tile = tq // radius

    def main(w, t=(None, 1, 0)):
        return pl.BlockSpec((None, tq, w), lambda b, c, i: (b, i, t[1] * c + t[2]))

    def lo(t):
        return pl.BlockSpec((None, radius, kw),
                            lambda b, c, i: (b, jnp.maximum(i * per_tile - 1, 0), t[1] * c + t[2]))

    def hi(t):
        return pl.BlockSpec((None, radius, kw),
                            lambda b, c, i: (b, jnp.minimum((i + 1) * per_tile, n // radius - 1), t[1] * c + t[2]))

    in_specs = [pl.BlockSpec(memory_space=pltpu.SMEM),
                main(qw, q), lo(k), main(kw, k), hi(k), lo(v), main(kw, v), hi(v), _const_spec(bias.shape)]
    args = [scal, view(q), view(k), view(k), view(k), view(v), view(v), view(v), bias]
    out_specs = [main(qw)]
    out_shape = [jax.ShapeDtypeStruct((batch, n, dilation * qw), out_dtype)]
    if emit_ml:
        out_specs.append(main(LANES))
        out_shape.append(jax.ShapeDtypeStruct((batch, n, dilation * LANES), F32))
    outs = pl.pallas_call(
        functools.partial(_banded_kernel, tq=tq, radius=radius, head_ids=head_ids, kv_slab=kv_slab,
                          has_sink=has_sink, emit_ml=emit_ml, bounded=bounded, n_tiles=n_tiles),
        grid=(batch, dilation, n_tiles),
        in_specs=in_specs,
        out_specs=out_specs,
        out_shape=out_shape,
        scratch_shapes=([] if bounded else [pltpu.VMEM((n_sub_tiles, 2 * SUB, SUB + 2 * radius), F32),
                                            pltpu.VMEM((n_sub_tiles, 2 * SUB, LANES), F32)])
        + [pltpu.VMEM((n_sub_tiles, 2 * SUB, SUB + 2 * radius), BF16)],
        compiler_params=_cparams(("parallel", "parallel", "parallel")),
        name="banded%s_r%d_d%d" % ("_bounded" if bounded else "", radius, dilation),
    )(*args)
    return [o.reshape(batch * n, -1) for o in outs]


def _latent_kernel(q_ref, k_ref, vt_ref, o_ref, s_ref, cmax_ref, m_ref, acc_ref, *, tk):
    tq = q_ref.shape[0]
    nk = k_ref.shape[0] // tk
    ahead = LAT_SLOTS - 1
    qt = q_ref[...].astype(F32).T.astype(BF16)
    m_ref[...] = jnp.full(m_ref.shape, -jnp.inf, F32)
    acc_ref[...] = jnp.zeros(acc_ref.shape, F32)

    def scores(slot, j):
        start = pl.multiple_of(j * tk, tk)
        s = jnp.dot(k_ref[pl.ds(start, tk), :], qt, preferred_element_type=F32)
        s_ref[slot] = s
        cmax_ref[slot] = jnp.max(s, axis=0, keepdims=True)

    def consume(slot, j):
        m_old = m_ref[...]
        m_new = jnp.maximum(m_old, cmax_ref[slot])
        pr = jnp.exp2(s_ref[slot] - m_new).astype(BF16)
        acc_ref[...] = jnp.exp2(m_old - m_new) * acc_ref[...] + jnp.dot(
            vt_ref[j], pr, preferred_element_type=F32)
        m_ref[...] = m_new

    def step(j, u, with_scores):
        if with_scores:
            scores((u + ahead) % LAT_SLOTS, j + ahead)
        consume(u % LAT_SLOTS, j)

    for j in range(ahead):
        scores(j, j)
    trips = (nk - ahead) // LAT_UNROLL

    def body(jj, carry):
        for u in range(LAT_UNROLL):
            step(LAT_UNROLL * jj + u, u, True)
        return carry

    lax.fori_loop(0, trips, body, 0)
    for j in range(trips * LAT_UNROLL, nk):
        step(j, j, j + ahead < nk)
    acc = acc_ref[...]
    o = jnp.concatenate([acc[:C_VDIM] / acc[C_VDIM:C_VDIM + 1], jnp.zeros((LANES - C_VDIM, tq), F32)], axis=0)
    o_ref[...] = o.T.astype(o_ref.dtype)


def _latent_bounded_kernel(q_ref, k_ref, vt_ref, o_ref, acc_ref, *, tk):
    tq = q_ref.shape[0]
    nk = k_ref.shape[0] // tk
    qt = q_ref[...].astype(F32).T.astype(BF16)
    acc_ref[...] = jnp.zeros(acc_ref.shape, F32)

    def body(jj, carry):
        for u in range(LAT_BOUNDED_UNROLL):
            j = LAT_BOUNDED_UNROLL * jj + u
            start = pl.multiple_of(j * tk, tk)
            s = jnp.dot(k_ref[pl.ds(start, tk), :], qt, preferred_element_type=F32)
            acc_ref[...] += jnp.dot(vt_ref[j], jnp.exp2(s).astype(BF16), preferred_element_type=F32)
        return carry

    lax.fori_loop(0, nk // LAT_BOUNDED_UNROLL, body, 0)
    acc = acc_ref[...]
    o = jnp.concatenate([acc[:C_VDIM] / acc[C_VDIM:C_VDIM + 1], jnp.zeros((LANES - C_VDIM, tq), F32)], axis=0)
    o_ref[...] = o.T.astype(o_ref.dtype)


def _latent(qc, kc, vt, batch, seq, bounded):
    t = batch * seq
    tq, tk = TQ_LAT, vt.shape[2]
    nq = seq // tq
    if bounded:
        body = functools.partial(_latent_bounded_kernel, tk=tk)
        scratch = [pltpu.VMEM((VT_ROWS, tq), F32)]
    else:
        body = functools.partial(_latent_kernel, tk=tk)
        scratch = [pltpu.VMEM((LAT_SLOTS, tk, tq), F32), pltpu.VMEM((LAT_SLOTS, 1, tq), F32),
                   pltpu.VMEM((1, tq), F32), pltpu.VMEM((VT_ROWS, tq), F32)]
    return pl.pallas_call(
        body,
        grid=(batch, C_HEADS, nq),
        in_specs=[pl.BlockSpec((tq, LANES), lambda b, h, i: (b * nq + i, h)),
                  pl.BlockSpec((seq, LANES), lambda b, h, i: (b, h)),
                  pl.BlockSpec((seq // tk, VT_ROWS, tk), lambda b, h, i: (b, h, 0))],
        out_specs=pl.BlockSpec((tq, LANES), lambda b, h, i: (b * nq + i, h)),
        out_shape=jax.ShapeDtypeStruct((t, C_SLOTS), BF16),
        scratch_shapes=scratch,
        compiler_params=_cparams(("parallel", "parallel", "parallel")),
        name="latent_bounded" if bounded else "latent",
    )(qc, kc, vt)


def _merge_mlp_kernel(x_ref, oa1_ref, oa2_ref, oa3_ref, ml1_ref, ml2_ref, ml3_ref, ob_ref, oc_ref,
                      expand_ref, gout_ref, wout_ref, gmlp_ref, wup_ref, wdown_ref, out_ref, oa_s, ml_s):
    tm = x_ref.shape[0]
    oas, mls = [oa1_ref[...]], [ml1_ref[...]]
    for idx, (o_ref, l_ref) in enumerate(((oa2_ref, ml2_ref), (oa3_ref, ml3_ref))):
        r = A_CONFIGS[idx + 1][1]
        for c in range(r):
            for s in range(A_W // LANES):
                col = c * A_W + s * LANES
                oa_s[idx, s, pl.ds(c, tm // r, stride=r), :] = o_ref[:, col:col + LANES]
            ml_s[idx, pl.ds(c, tm // r, stride=r), :] = l_ref[:, c * LANES:(c + 1) * LANES]
        oas.append(jnp.concatenate([oa_s[idx, s] for s in range(A_W // LANES)], axis=1))
        mls.append(ml_s[idx])
    lane = lax.broadcasted_iota(jnp.int32, mls[0].shape, 1)
    m_all = jnp.maximum(jnp.maximum(mls[0], mls[1]), mls[2])
    ws = [pltpu.roll(ml, LANES - ML_L_OFFSET, 1) * jnp.exp2(ml - m_all) for ml in mls]
    wsum = ws[0] + ws[1] + ws[2]
    packed = jnp.zeros_like(wsum)
    for c, w in enumerate(ws):
        wn = jnp.where(lane < A_HEADS, w / wsum, 0.0)
        packed = packed + (wn if c == 0 else pltpu.roll(wn, ML_L_OFFSET * c, 1))
    hi = packed.astype(BF16)
    lo = (packed - hi.astype(F32)).astype(BF16)
    spread = (jnp.dot(hi, expand_ref[...], preferred_element_type=F32)
              + jnp.dot(lo, expand_ref[...], preferred_element_type=F32))
    oa = spread[:, :A_W] * oas[0] + spread[:, A_W:2 * A_W] * oas[1] + spread[:, 2 * A_W:] * oas[2]

    def group_norm(v, width):
        return v * lax.rsqrt(jnp.sum(v * v, axis=-1, keepdims=True) * (1.0 / width) + EPS)

    ob = ob_ref[...].astype(F32)
    oc_wide = oc_ref[...].astype(F32)
    half = lax.broadcasted_iota(jnp.int32, (tm, LANES), 1) < C_VDIM
    oc = jnp.concatenate(
        [jnp.where(half, oc_wide[:, 2 * p * LANES:(2 * p + 1) * LANES],
                   pltpu.roll(oc_wide[:, (2 * p + 1) * LANES:(2 * p + 2) * LANES], C_VDIM, 1))
         for p in range(C_HEADS // 2)], axis=1)
    mixed = jnp.concatenate([group_norm(oa, A_W), group_norm(ob, B_QW), group_norm(oc, C_HEADS * C_VDIM)], axis=1)
    mixed = (mixed * gout_ref[...]).astype(BF16)
    x = x_ref[...] + jnp.dot(mixed, wout_ref[...], preferred_element_type=F32)

    h = (x * lax.rsqrt(jnp.mean(x * x, axis=-1, keepdims=True) + EPS) * gmlp_ref[...]).astype(BF16)
    acc = x
    for s in range(0, D_FF, FF_CHUNK):
        u = jnp.dot(h, wup_ref[:, s:s + FF_CHUNK], preferred_element_type=F32)
        u = jnp.square(jnp.maximum(u, 0.0)).astype(BF16)
        acc = acc + jnp.dot(u, wdown_ref[s:s + FF_CHUNK, :], preferred_element_type=F32)
    out_ref[...] = acc


def _merge_mlp(x2d, oas, mls, ob, oc, p):
    t = x2d.shape[0]
    tm = TM_MLP
    row = lambda w: pl.BlockSpec((tm, w), lambda i: (i, 0))
    strided = lambda w, r: pl.BlockSpec((tm // r, r * w), lambda i: (i, 0))
    single = lambda shape: pl.BlockSpec(shape, lambda i: (0,) * len(shape), pipeline_mode=pl.Buffered(1))
    return pl.pallas_call(
        _merge_mlp_kernel,
        grid=(t // tm,),
        in_specs=[row(D_MODEL)] + [strided(A_W, r) for _, r in A_CONFIGS]
        + [strided(LANES, r) for _, r in A_CONFIGS] + [row(B_QW), row(C_SLOTS),
                  single((LANES, 3 * A_W)), single((1, MIX_COLS)), single((MIX_COLS, D_MODEL)),
                  single((1, D_MODEL)), single((D_MODEL, D_FF)), single((D_FF, D_MODEL))],
        out_specs=row(D_MODEL),
        out_shape=jax.ShapeDtypeStruct((t, D_MODEL), F32),
        scratch_shapes=[pltpu.VMEM((2, A_W // LANES, tm, LANES), F32), pltpu.VMEM((2, tm, LANES), F32)],
        compiler_params=_cparams(("parallel",)),
        name="merge_mlp",
    )(x2d, *oas, *mls, ob, oc, p["expand"], p["gout"], p["w_out"], p["gmlp"], p["w_up"], p["w_down"])


def _block_diag_ones(group):
    idx = np.arange(2 * LANES) // group
    return jnp.asarray((idx[:, None] == idx[None, :]).astype(np.float32), dtype=BF16)


def _expand_matrix():
    e = np.zeros((LANES, len(A_CONFIGS) * A_W), np.float32)
    for c in range(len(A_CONFIGS)):
        for h in range(A_HEADS):
            e[ML_L_OFFSET * c + h, c * A_W + h * HEAD_DIM:c * A_W + (h + 1) * HEAD_DIM] = 1.0
    return jnp.asarray(e, dtype=BF16)


def _pad_heads(w, heads, used, lo=0):
    lead = w.shape[:-1]
    w = w.reshape(lead + (heads, used))
    pad = [(0, 0)] * len(lead) + [(0, 0), (lo, LANES - lo - used)]
    return jnp.pad(w, pad).reshape(lead + (heads * LANES,))


def _layer_params(i, norm_mix, w_in, qk_gain_a, qk_gain_b, q_lat_gain, kv_lat_gain, w_uq, w_ukv, qk_gain_c,
                  out_norm, w_out, norm_mlp, w_up, w_down):
    w = w_in[i]
    o = np.cumsum((A_W, A_W, A_W, B_QW, B_KVW, B_KVW, C_Q_RANK, C_KV_RANK)).tolist()
    qb = w[:, o[2]:o[3]].reshape(D_MODEL, B_HEADS, HEAD_DIM)[:, B_HEAD_ORDER, :].reshape(D_MODEL, B_QW)
    kr = _pad_heads(w[:, o[7]:], 1, C_ROPE, lo=C_NOPE)
    w_in_p = jnp.concatenate([w[:, :o[2]], qb, w[:, o[3]:o[7]], kr], axis=1).astype(BF16)

    scale = HEAD_DIM ** -0.5 * LOG2E
    gqk = jnp.concatenate([jnp.tile(qk_gain_a[i, 0], A_HEADS) * scale, jnp.tile(qk_gain_a[i, 1], A_HEADS),
                           jnp.tile(qk_gain_b[i, 0], B_HEADS) * scale, jnp.tile(qk_gain_b[i, 1], B_KV_HEADS)])
    ukv = w_ukv[i].reshape(C_KV_RANK, C_HEADS, C_NOPE + C_VDIM)
    wuk = _pad_heads(ukv[:, :, :C_NOPE].reshape(C_KV_RANK, -1), C_HEADS, C_NOPE)
    wuvt = ukv[:, :, C_NOPE:].reshape(C_KV_RANK, C_HEADS * C_VDIM).T
    g = out_norm[i]
    gb = g[A_W:A_W + B_QW].reshape(B_HEADS, HEAD_DIM)[B_HEAD_ORDER, :].reshape(B_QW)
    gout = jnp.concatenate([g[:A_W], gb, g[A_W + B_QW:]])
    wo = w_out[i]
    wob = wo[A_W:A_W + B_QW].reshape(B_HEADS, HEAD_DIM, D_MODEL)[B_HEAD_ORDER, :, :].reshape(B_QW, D_MODEL)
    w_out_p = jnp.concatenate([wo[:A_W], wob, wo[A_W + B_QW:]], axis=0).astype(BF16)
    bound = (C_QK ** 0.5 * LOG2E * BOUND_MARGIN) * jnp.max(jnp.abs(qk_gain_c[i, 0])) * jnp.max(jnp.abs(qk_gain_c[i, 1]))
    return {
        "gmix": norm_mix[i][None, :],
        "w_in": w_in_p,
        "gqk": gqk[None, :],
        "ones64": _block_diag_ones(HEAD_DIM),
        "ones128": _block_diag_ones(LANES),
        "glq": q_lat_gain[i][None, :],
        "wuq": _pad_heads(w_uq[i], C_HEADS, C_QK).astype(BF16),
        "gqc": jnp.tile(_pad_heads(qk_gain_c[i, 0] * (C_QK ** -0.5 * LOG2E), 1, C_QK), C_HEADS)[None, :],
        "glkv": kv_lat_gain[i][None, :],
        "wuk": wuk.astype(BF16),
        "wuvt": wuvt.astype(BF16),
        "gkc": jnp.tile(_pad_heads(qk_gain_c[i, 1], 1, C_QK), C_HEADS)[None, :],
        "qshift": jnp.tile(_pad_heads(-bound[None], 1, 1, lo=C_QK), C_HEADS)[None, :],
        "kone": jnp.tile(_pad_heads(jnp.ones((1,), F32), 1, 1, lo=C_QK), C_HEADS)[None, :],
        "logit_bound": bound,
        "expand": _expand_matrix(),
        "gout": gout[None, :],
        "w_out": w_out_p,
        "gmlp": norm_mlp[i][None, :],
        "w_up": w_up[i].astype(BF16),
        "w_down": w_down[i].astype(BF16),
    }


def kernel(x, positions, rel_bias_table, norm_mix, w_in, qk_gain_a, qk_gain_b, sink_b, q_lat_gain, kv_lat_gain,
           w_uq, w_ukv, qk_gain_c, out_norm, w_out, norm_mlp, w_up, w_down):
    batch, seq, _ = x.shape
    depth = w_in.shape[0]
    x2d = x.reshape(batch * seq, D_MODEL)
    cos_t, sin_t = _rope_tables(positions)
    a_pairs = tuple((2 * p, 2 * p + 1) for p in range(A_HEADS // 2))
    b_pairs = (B_HEAD_ORDER[:2], B_HEAD_ORDER[2:])
    b_cols = tuple(A_HEADS + h for h in B_HEAD_ORDER)

    def bias_tables(shift):
        tiles_a = [_bias_tiles(rel_bias_table, shift, window // (2 * r), r, tuple(range(A_HEADS)))
                   for window, r in A_CONFIGS]
        return tiles_a, _bias_tiles(rel_bias_table, shift, B_RADIUS, 1, b_cols)

    gain_bound = lambda gains: jnp.max(jnp.max(jnp.abs(gains[:, 0]), axis=-1) * jnp.max(jnp.abs(gains[:, 1]), axis=-1))
    qk_bound = HEAD_DIM ** 0.5 * LOG2E * BOUND_MARGIN * jnp.maximum(gain_bound(qk_gain_a), gain_bound(qk_gain_b))
    band_shift = jnp.maximum(qk_bound + LOG2E * jnp.max(jnp.abs(rel_bias_table)), LOG2E * jnp.max(sink_b))
    shifted_tables = bias_tables(band_shift[None])

    def banded_all(bounded, qkv1, qkv4, qkv16, qb, kb, vb, sinks):
        shift = band_shift[None] if bounded else jnp.zeros((1,), F32)
        tables_a, table_b = shifted_tables if bounded else bias_tables(shift)
        scal = jnp.concatenate([shift, sinks * LOG2E])
        outs = []
        for (window, r), bias, qkv in zip(A_CONFIGS, tables_a, (qkv1, qkv4, qkv16)):
            outs += _banded((qkv, 3, 0), (qkv, 3, 1), (qkv, 3, 2), bias, scal, batch=batch, seq=seq, dilation=r,
                            radius=window // (2 * r), widths=(A_W, A_W), head_ids=a_pairs, kv_slab=(0, 1, 2),
                            has_sink=False, emit_ml=True, bounded=bounded, out_dtype=F32)
        outs += _banded((qb, 1, 0), (kb, 1, 0), (vb, 1, 0), table_b, scal, batch=batch, seq=seq, dilation=1,
                        radius=B_RADIUS, widths=(B_QW, B_KVW), head_ids=b_pairs, kv_slab=(0, 0),
                        has_sink=True, emit_ml=False, bounded=bounded, out_dtype=BF16)
        return tuple(outs)

    for i in range(depth):
        p = _layer_params(i, norm_mix, w_in, qk_gain_a, qk_gain_b, q_lat_gain, kv_lat_gain, w_uq, w_ukv,
                          qk_gain_c, out_norm, w_out, norm_mlp, w_up, w_down)
        qkv1, qkv4, qkv16, qb, kb, vb, qc, kc, vt = _inproj(x2d, cos_t, sin_t, p)
        o1, ml1, o4, ml4, o16, ml16, ob = lax.cond(
            band_shift <= MAX_LOGIT_BOUND, functools.partial(banded_all, True), functools.partial(banded_all, False),
            qkv1, qkv4, qkv16, qb, kb, vb, sink_b[i])
        oas, mls = [o1, o4, o16], [ml1, ml4, ml16]
        oc = lax.cond(p["logit_bound"] <= MAX_LOGIT_BOUND,
                      functools.partial(_latent, batch=batch, seq=seq, bounded=True),
                      functools.partial(_latent, batch=batch, seq=seq, bounded=False), qc, kc, vt)
        x2d = _merge_mlp(x2d, oas, mls, ob, oc, p)
    return x2d.reshape(batch, seq, D_MODEL)
```

```python
import functools
import math

import numpy as np
import jax
import jax.numpy as jnp
from jax import lax
from jax.experimental import pallas as pl
from jax.experimental.pallas import tpu as pltpu

F32 = jnp.float32
BF16 = jnp.bfloat16

D_MODEL = 1024
HEAD_DIM = 64
A_HEADS = 6
A_CONFIGS = ((128, 1), (512, 4), (2048, 16))
B_HEADS = 4
B_KV_HEADS = 2
B_RADIUS = 128
C_HEADS = 6
C_NOPE = 64
C_ROPE = 32
C_VDIM = 64
C_QK = C_NOPE + C_ROPE
C_Q_RANK = 256
C_KV_RANK = 128
ROPE_THETA = 10000.0
N_BUCKETS = 32
MAX_DISTANCE = 1024
D_FF = 4 * D_MODEL
EPS = 1e-6
NEG = -1e30

A_W = A_HEADS * HEAD_DIM
B_QW = B_HEADS * HEAD_DIM
B_KVW = B_KV_HEADS * HEAD_DIM
LANES = 128
C_SLOTS = C_HEADS * LANES
IN_COLS = 3 * A_W + B_QW + 2 * B_KVW + C_Q_RANK + C_KV_RANK + LANES
MIX_COLS = A_W + B_QW + C_HEADS * C_VDIM
VT_ROWS = 80
LOG2E = math.log2(math.e)
ML_L_OFFSET = 8

B_HEAD_ORDER = (0, 2, 1, 3)

TM_IN = 512
TQ_BAND = 1024
SUB = 128
TQ_LAT = 2048
BOUND_MARGIN = 1.02
MAX_LOGIT_BOUND = 50.0
LAT_BOUNDED_UNROLL = 16
LAT_SLOTS = 3
LAT_UNROLL = 6
TM_MLP = 512
FF_CHUNK = 1024
VMEM_LIMIT = 56 * 1024 * 1024


def _cparams(sem):
    return pltpu.CompilerParams(dimension_semantics=sem, vmem_limit_bytes=VMEM_LIMIT)


def _const_spec(shape):
    zeros = (0,) * len(shape)
    return pl.BlockSpec(shape, lambda *_: zeros)


def _rope_table_kernel(pos_ref, inv_ref, cos_ref, sin_ref):
    ang = pos_ref[...] * inv_ref[...]
    lane = lax.broadcasted_iota(jnp.int32, ang.shape, 1)
    c = jnp.cos(ang)
    s = jnp.sin(ang)
    s = jnp.where(lane % C_ROPE < C_ROPE // 2, -s, s)
    live = (lane >= C_NOPE) & (lane < C_QK)
    per_row = LANES // C_ROPE
    for a in range(per_row):
        shift = (C_NOPE - a * C_ROPE) % LANES
        rows = pl.ds(a, ang.shape[0], stride=per_row)
        cos_ref[rows, :] = jnp.where(live, c if shift == 0 else pltpu.roll(c, shift, 1), 1.0)
        sin_ref[rows, :] = jnp.where(live, s if shift == 0 else pltpu.roll(s, shift, 1), 0.0)


def _rope_tables(positions):
    t = positions.size
    half = C_ROPE // 2
    per_row = LANES // C_ROPE
    inv = ROPE_THETA ** (-jnp.arange(half, dtype=F32) / half)
    inv_row = jnp.tile(inv, 2 * per_row)[None, :]
    pos = jnp.repeat(positions.astype(F32).reshape(t // per_row, per_row), C_ROPE, axis=1)
    tm = 2048
    return pl.pallas_call(
        _rope_table_kernel,
        grid=(t // tm,),
        in_specs=[pl.BlockSpec((tm // per_row, LANES), lambda i: (i, 0)), _const_spec((1, LANES))],
        out_specs=[pl.BlockSpec((tm, LANES), lambda i: (i, 0))] * 2,
        out_shape=[jax.ShapeDtypeStruct((t, LANES), F32)] * 2,
        compiler_params=_cparams(("parallel",)),
        name="rope_tables",
    )(pos, inv_row)


def _bucket_thresholds():
    half = N_BUCKETS // 2
    exact = half // 2
    n = np.arange(1, 2 * MAX_DISTANCE + 2, dtype=np.float64)
    far = exact + (np.log(n / exact) / math.log(MAX_DISTANCE / exact) * (half - exact)).astype(np.int64)
    far = np.minimum(far, half - 1)
    return tuple(int(n[np.argmax(far >= exact + k)]) for k in range(1, half - exact))


def _bias_kernel(table_ref, shift_ref, out_ref, *, radius, dilation, head_cols):
    hsel = pl.program_id(0)
    width = SUB + 2 * radius
    row = lax.broadcasted_iota(jnp.int32, (SUB, width), 0)
    col = lax.broadcasted_iota(jnp.int32, (SUB, width), 1)
    rel = col - radius - row
    n = jnp.abs(rel) * dilation
    half = N_BUCKETS // 2
    exact = half // 2
    far = jnp.full(n.shape, exact, jnp.int32)
    for thr in _bucket_thresholds():
        far = far + (n >= thr).astype(jnp.int32)
    bucket = jnp.where(rel > 0, half, 0) + jnp.where(n < exact, n, far)
    for idx, hc in enumerate(head_cols):
        @pl.when(hsel == idx)
        def _(hc=hc):
            val = jnp.zeros(n.shape, F32)
            for b in range(N_BUCKETS):
                val = jnp.where(bucket == b, table_ref[b, hc], val)
            out_ref[...] = jnp.where(jnp.abs(rel) <= radius, val * LOG2E - shift_ref[0], NEG)


def _bias_tiles(table, shift, radius, dilation, head_cols):
    width = SUB + 2 * radius
    tiles = pl.pallas_call(
        functools.partial(_bias_kernel, radius=radius, dilation=dilation, head_cols=head_cols),
        grid=(len(head_cols),),
        in_specs=[pl.BlockSpec(memory_space=pltpu.SMEM), pl.BlockSpec(memory_space=pltpu.SMEM)],
        out_specs=pl.BlockSpec((None, SUB, width), lambda h: (h, 0, 0)),
        out_shape=jax.ShapeDtypeStruct((len(head_cols), SUB, width), F32),
        compiler_params=_cparams(("arbitrary",)),
        name="bias_tiles",
    )(table, shift)
    return tiles.reshape(len(head_cols) // 2, 2 * SUB, width)


def _group_mean_sq(y, ones_ref, group):
    sq = (y * y).astype(BF16)
    width = y.shape[1]
    parts = []
    for s in range(0, width, 2 * LANES):
        w = min(2 * LANES, width - s)
        parts.append(jnp.dot(sq[:, s:s + w], ones_ref[:w, :w], preferred_element_type=F32))
    out = parts[0] if len(parts) == 1 else jnp.concatenate(parts, axis=1)
    return out * (1.0 / group)


def _slab_roll(y, shift):
    parts = [pltpu.roll(y[:, s:s + LANES], shift, 1) for s in range(0, y.shape[1], LANES)]
    return parts[0] if len(parts) == 1 else jnp.concatenate(parts, axis=1)


def _rope(y, cos_t, sin_t, lane):
    swapped = jnp.where(lane < C_NOPE + C_ROPE // 2, _slab_roll(y, LANES - C_ROPE // 2), _slab_roll(y, C_ROPE // 2))
    return y * cos_t + swapped * sin_t


def _inproj_kernel(x_ref, cos_ref, sin_ref, gmix_ref, w_ref, gqk_ref, ones64_ref, ones128_ref,
                   glq_ref, wuq_ref, gqc_ref, glkv_ref, wuk_ref, wuvt_ref, gkc_ref, gkr_ref, qshift_ref, kone_ref,
                   qkv1_ref, qkv4_ref, qkv16_ref, qkvb_ref, qc_ref, kc_ref, vt_ref, stage_ref, stage4_ref):
    x = x_ref[...]
    h = x * lax.rsqrt(jnp.mean(x * x, axis=-1, keepdims=True) + EPS) * gmix_ref[...]
    y = jnp.dot(h.astype(BF16), w_ref[...], preferred_element_type=F32)

    o_qb = 3 * A_W
    o_kb = o_qb + B_QW
    o_vb = o_kb + B_KVW
    o_cq = o_vb + B_KVW
    o_ckv = o_cq + C_Q_RANK
    o_kr = o_ckv + C_KV_RANK
    yn = jnp.concatenate([y[:, :2 * A_W], y[:, o_qb:o_vb]], axis=1)
    yn = yn * lax.rsqrt(_group_mean_sq(yn, ones64_ref, HEAD_DIM) + EPS) * gqk_ref[...]
    qkvb_ref[:, :B_QW + B_KVW] = yn[:, 2 * A_W:].astype(BF16)
    n_slabs, tm = stage_ref.shape[0], stage_ref.shape[1]
    for s in range(n_slabs):
        src = yn if s * LANES < 2 * A_W else y
        stage_ref[s] = src[:, s * LANES:(s + 1) * LANES]
    for s in range(n_slabs):
        qkv1_ref[:, s * LANES:(s + 1) * LANES] = stage_ref[s].astype(BF16)
        for c4 in range(4):
            rows = stage_ref[s, pl.ds(c4, tm // 4, stride=4), :]
            qkv4_ref[:, c4 * 3 * A_W + s * LANES:c4 * 3 * A_W + (s + 1) * LANES] = rows.astype(BF16)
            stage4_ref[c4 * n_slabs + s] = rows
    for c16 in range(16):
        for s in range(n_slabs):
            rows = stage4_ref[(c16 % 4) * n_slabs + s, pl.ds(c16 // 4, tm // 16, stride=4), :]
            qkv16_ref[:, c16 * 3 * A_W + s * LANES:c16 * 3 * A_W + (s + 1) * LANES] = rows.astype(BF16)
    qkvb_ref[:, B_QW + B_KVW:] = y[:, o_vb:o_cq].astype(BF16)

    cos_t = jnp.concatenate([cos_ref[...]] * C_HEADS, axis=1)
    sin_t = jnp.concatenate([sin_ref[...]] * C_HEADS, axis=1)
    lane = lax.broadcasted_iota(jnp.int32, cos_t.shape, 1) % LANES
    cq = y[:, o_cq:o_ckv]
    cq = cq * lax.rsqrt(jnp.mean(cq * cq, axis=-1, keepdims=True) + EPS) * glq_ref[...]
    qc = jnp.dot(cq.astype(BF16), wuq_ref[...], preferred_element_type=F32)
    qc = qc * lax.rsqrt(_group_mean_sq(qc, ones128_ref, C_QK) + EPS) * gqc_ref[...]
    qc_ref[...] = (_rope(qc, cos_t, sin_t, lane) + qshift_ref[...]).astype(BF16)

    ckv = y[:, o_ckv:o_kr]
    ckv = (ckv * lax.rsqrt(jnp.mean(ckv * ckv, axis=-1, keepdims=True) + EPS) * glkv_ref[...]).astype(BF16)
    kr = y[:, o_kr:]
    kn = jnp.dot(ckv, wuk_ref[...], preferred_element_type=F32)
    inv_rms = lax.rsqrt(_group_mean_sq(kn + jnp.concatenate([kr] * C_HEADS, axis=1), ones128_ref, C_QK) + EPS)
    lane_slab = lax.broadcasted_iota(jnp.int32, kr.shape, 1)
    kr_rot = _rope(kr * gkr_ref[...], cos_ref[...], sin_ref[...], lane_slab)
    kc = (kn * gkc_ref[...] + jnp.concatenate([kr_rot] * C_HEADS, axis=1)) * inv_rms
    kc_ref[...] = (kc + kone_ref[...]).astype(BF16)
    vt = lax.dot_general(wuvt_ref[...], ckv, (((1,), (1,)), ((), ())), preferred_element_type=F32)
    row = lax.broadcasted_iota(jnp.int32, (VT_ROWS - C_VDIM, vt.shape[1]), 0)
    ones_rows = jnp.where(row == 0, 1.0, 0.0).astype(BF16)
    for hd in range(C_HEADS):
        vt_ref[0, hd * VT_ROWS:hd * VT_ROWS + C_VDIM, :] = vt[hd * C_VDIM:(hd + 1) * C_VDIM].astype(BF16)
        vt_ref[0, hd * VT_ROWS + C_VDIM:(hd + 1) * VT_ROWS, :] = ones_rows


def _inproj(x2d, cos_t, sin_t, p):
    t = x2d.shape[0]
    tm = TM_IN
    row = lambda w: pl.BlockSpec((tm, w), lambda i: (i, 0))
    outs = [(B_QW + 2 * B_KVW, BF16)] + [(C_SLOTS, BF16)] * 2
    a_specs = [pl.BlockSpec((tm // r, r * 3 * A_W), lambda i: (i, 0)) for _, r in A_CONFIGS]
    a_shapes = [jax.ShapeDtypeStruct((t // r, r * 3 * A_W), BF16) for _, r in A_CONFIGS]
    return pl.pallas_call(
        _inproj_kernel,
        grid=(t // tm,),
        in_specs=[row(D_MODEL), row(LANES), row(LANES),
                  _const_spec((1, D_MODEL)), _const_spec((D_MODEL, IN_COLS)),
                  _const_spec((1, 2 * A_W + B_QW + B_KVW)),
                  _const_spec((2 * LANES, 2 * LANES)), _const_spec((2 * LANES, 2 * LANES)),
                  _const_spec((1, C_Q_RANK)), _const_spec((C_Q_RANK, C_SLOTS)), _const_spec((1, C_SLOTS)),
                  _const_spec((1, C_KV_RANK)), _const_spec((C_KV_RANK, C_SLOTS)),
                  _const_spec((C_HEADS * C_VDIM, C_KV_RANK)), _const_spec((1, C_SLOTS)), _const_spec((1, LANES)),
                  _const_spec((1, C_SLOTS)), _const_spec((1, C_SLOTS))],
        out_specs=a_specs + [row(w) for w, _ in outs]
        + [pl.BlockSpec((1, C_HEADS * VT_ROWS, tm), lambda i: (i, 0, 0))],
        out_shape=a_shapes + [jax.ShapeDtypeStruct((t, w), d) for w, d in outs]
        + [jax.ShapeDtypeStruct((t // tm, C_HEADS * VT_ROWS, tm), BF16)],
        scratch_shapes=[pltpu.VMEM((3 * A_W // LANES, tm, LANES), F32),
                        pltpu.VMEM((4 * 3 * A_W // LANES, tm // 4, LANES), F32)],
        compiler_params=_cparams(("parallel",)),
        name="inproj",
    )(x2d, cos_t, sin_t, p["gmix"], p["w_in"], p["gqk"], p["ones64"], p["ones128"],
      p["glq"], p["wuq"], p["gqc"], p["glkv"], p["wuk"], p["wuvt"], p["gkc"], p["gkc"][:, :LANES], p["qshift"], p["kone"])


def _window(lo_ref, main_ref, hi_ref, a, radius, tq, cols):
    start, end = a - radius, a + SUB + radius
    parts = []
    if start < 0:
        parts.append(lo_ref[radius + start:radius, cols])
        start = 0
    parts.append(main_ref[start:min(end, tq), cols])
    if end > tq:
        parts.append(hi_ref[0:end - tq, cols])
    return parts[0] if len(parts) == 1 else jnp.concatenate(parts, axis=0)


def _banded_kernel(scal_ref, lo_ref, main_ref, hi_ref, bias_ref, *refs, tq, radius, head_ids, layout, group,
                   has_sink, emit_ml, bounded, n_tiles):
    class_w, q_off, k_off, v_off, kv_slab = layout
    n_pairs = len(head_ids)
    o_ref = refs[0]
    ml_ref = refs[1] if emit_ml else None
    p_scr = refs[-1]
    if not bounded:
        s_scr, m_scr = refs[-3:-1]
    sink = lambda hd: scal_ref[1 + hd]
    tile = pl.program_id(2)
    width = SUB + 2 * radius
    lane = lax.broadcasted_iota(jnp.int32, (SUB, LANES), 1)
    col = lax.broadcasted_iota(jnp.int32, (2 * SUB, width), 1)
    low = lane < HEAD_DIM
    slab = lambda base, idx: slice(base + idx * LANES, base + (idx + 1) * LANES)
    tiles = [(a, cls, pair) for a in range(0, tq, SUB) for cls in range(group) for pair in range(n_pairs)]

    for g, (a, cls, pair) in enumerate(tiles):
        qs = main_ref[a:a + SUB, slab(cls * class_w + q_off, pair)]
        ks = _window(lo_ref, main_ref, hi_ref, a, radius, tq, slab(cls * class_w + k_off, kv_slab[pair]))
        zero = jnp.zeros_like(qs)
        q2 = jnp.concatenate([jnp.where(low, qs, zero), jnp.where(low, zero, qs)], axis=0)
        s = lax.dot_general(q2, ks, (((1,), (1,)), ((), ())), preferred_element_type=F32)
        s = s + bias_ref[pair]
        if a == 0:
            s = jnp.where(col < jnp.where(tile == 0, radius, 0), NEG, s)
        if a == tq - SUB:
            s = jnp.where(col >= jnp.where(tile == n_tiles - 1, SUB + radius, width), NEG, s)
        if bounded:
            p_scr[g] = jnp.exp2(s).astype(BF16)
            continue
        m = jnp.max(s, axis=-1, keepdims=True)
        if has_sink:
            sinks = jnp.concatenate([jnp.full((SUB, 1), sink(hd), F32) for hd in head_ids[pair]], axis=0)
            m = jnp.maximum(m, sinks)
        s_scr[g] = s
        m_scr[g] = jnp.broadcast_to(m, (2 * SUB, LANES))

    for g in range(0 if bounded else len(tiles)):
        m_wide = jnp.concatenate([m_scr[g]] * (width // LANES), axis=1)
        p_scr[g] = jnp.exp2(s_scr[g] - m_wide).astype(BF16)

    ml = None
    for g, (a, cls, pair) in enumerate(tiles):
        hd0, hd1 = head_ids[pair]
        vs = _window(lo_ref, main_ref, hi_ref, a, radius, tq, slab(cls * class_w + v_off, kv_slab[pair]))
        o2 = jnp.dot(p_scr[g], jnp.concatenate([vs, jnp.ones_like(vs)], axis=1), preferred_element_type=F32)
        if bounded:
            m0 = m1 = jnp.full((SUB, LANES), scal_ref[0], F32)
        else:
            m0, m1 = m_scr[g][:SUB], m_scr[g][SUB:]
        l0, l1 = o2[:SUB, LANES:], o2[SUB:, LANES:]
        if has_sink:
            l0 = l0 + jnp.exp2(sink(hd0) - m0)
            l1 = l1 + jnp.exp2(sink(hd1) - m1)
        out = jnp.where(low, o2[:SUB, :LANES], o2[SUB:, :LANES]) / jnp.where(low, l0, l1)
        o_ref[a:a + SUB, slab(cls * n_pairs * LANES, pair)] = out.astype(o_ref.dtype)
        if emit_ml:
            if pair == 0:
                ml = jnp.where(lane < ML_L_OFFSET, m0, 0.0) if bounded else jnp.zeros((SUB, LANES), F32)
            if not bounded:
                ml = jnp.where(lane == hd0, m0, ml)
                ml = jnp.where(lane == hd1, m1, ml)
            ml = jnp.where(lane == ML_L_OFFSET + hd0, l0, ml)
            ml = jnp.where(lane == ML_L_OFFSET + hd1, l1, ml)
            if pair == n_pairs - 1:
                ml_ref[a:a + SUB, slab(0, cls)] = ml


def _banded(qkv, bias, scal, *, batch, seq, dilation, radius, layout, head_ids, has_sink, emit_ml, bounded, out_dtype):
    n = seq // dilation
    tq = min(TQ_BAND, n)
    n_tiles = n // tq
    group = min(dilation, TQ_BAND // tq)
    class_w = layout[0]
    qw = len(head_ids) * LANES
    n_sub_tiles = (tq // SUB) * group * len(head_ids)
    per_tile = tq // radius
    main = lambda w: pl.BlockSpec((None, tq, group * w), lambda b, c, i: (b, i, c))
    halo = lambda pick: pl.BlockSpec((None, radius, group * class_w), lambda b, c, i: (b, pick(i), c))
    lo = halo(lambda i: jnp.maximum(i * per_tile - 1, 0))
    hi = halo(lambda i: jnp.minimum((i + 1) * per_tile, n // radius - 1))
    qkv = qkv.reshape(batch, n, dilation * class_w)
    out_specs = [main(qw)]
    out_shape = [jax.ShapeDtypeStruct((batch, n, dilation * qw), out_dtype)]
    if emit_ml:
        out_specs.append(main(LANES))
        out_shape.append(jax.ShapeDtypeStruct((batch, n, dilation * LANES), F32))
    outs = pl.pallas_call(
        functools.partial(_banded_kernel, tq=tq, radius=radius, head_ids=head_ids, layout=layout, group=group,
                          has_sink=has_sink, emit_ml=emit_ml, bounded=bounded, n_tiles=n_tiles),
        grid=(batch, dilation // group, n_tiles),
        in_specs=[pl.BlockSpec(memory_space=pltpu.SMEM), lo, main(class_w), hi, _const_spec(bias.shape)],
        out_specs=out_specs,
        out_shape=out_shape,
        scratch_shapes=([] if bounded else [pltpu.VMEM((n_sub_tiles, 2 * SUB, SUB + 2 * radius), F32),
                                            pltpu.VMEM((n_sub_tiles, 2 * SUB, LANES), F32)])
        + [pltpu.VMEM((n_sub_tiles, 2 * SUB, SUB + 2 * radius), BF16)],
        compiler_params=_cparams(("parallel", "parallel", "parallel")),
        name="banded%s_r%d_d%d" % ("_bounded" if bounded else "", radius, dilation),
    )(scal, qkv, qkv, qkv, bias)
    return [o.reshape(batch * n, -1) for o in outs]


def _latent_kernel(q_ref, k_ref, vt_ref, o_ref, s_ref, cmax_ref, m_ref, acc_ref, *, tk):
    tq = q_ref.shape[0]
    nk = k_ref.shape[0] // tk
    ahead = LAT_SLOTS - 1
    qt = q_ref[...].astype(F32).T.astype(BF16)
    m_ref[...] = jnp.full(m_ref.shape, -jnp.inf, F32)
    acc_ref[...] = jnp.zeros(acc_ref.shape, F32)

    def scores(slot, j):
        start = pl.multiple_of(j * tk, tk)
        s = jnp.dot(k_ref[pl.ds(start, tk), :], qt, preferred_element_type=F32)
        s_ref[slot] = s
        cmax_ref[slot] = jnp.max(s, axis=0, keepdims=True)

    def consume(slot, j):
        m_old = m_ref[...]
        m_new = jnp.maximum(m_old, cmax_ref[slot])
        pr = jnp.exp2(s_ref[slot] - m_new).astype(BF16)
        acc_ref[...] = jnp.exp2(m_old - m_new) * acc_ref[...] + jnp.dot(
            vt_ref[j], pr, preferred_element_type=F32)
        m_ref[...] = m_new

    def step(j, u, with_scores):
        if with_scores:
            scores((u + ahead) % LAT_SLOTS, j + ahead)
        consume(u % LAT_SLOTS, j)

    for j in range(ahead):
        scores(j, j)
    trips = (nk - ahead) // LAT_UNROLL

    def body(jj, carry):
        for u in range(LAT_UNROLL):
            step(LAT_UNROLL * jj + u, u, True)
        return carry

    lax.fori_loop(0, trips, body, 0)
    for j in range(trips * LAT_UNROLL, nk):
        step(j, j, j + ahead < nk)
    acc = acc_ref[...]
    o = jnp.concatenate([acc[:C_VDIM] / acc[C_VDIM:C_VDIM + 1], jnp.zeros((LANES - C_VDIM, tq), F32)], axis=0)
    o_ref[...] = o.T.astype(o_ref.dtype)


def _latent_bounded_kernel(q_ref, k_ref, vt_ref, o_ref, acc_ref, *, tk):
    tq = q_ref.shape[0]
    nk = k_ref.shape[0] // tk
    qt = q_ref[...].astype(F32).T.astype(BF16)
    acc_ref[...] = jnp.zeros(acc_ref.shape, F32)

    def body(jj, carry):
        for u in range(LAT_BOUNDED_UNROLL):
            j = LAT_BOUNDED_UNROLL * jj + u
            start = pl.multiple_of(j * tk, tk)
            s = jnp.dot(k_ref[pl.ds(start, tk), :], qt, preferred_element_type=F32)
            acc_ref[...] += jnp.dot(vt_ref[j], jnp.exp2(s).astype(BF16), preferred_element_type=F32)
        return carry

    lax.fori_loop(0, nk // LAT_BOUNDED_UNROLL, body, 0)
    acc = acc_ref[...]
    o = jnp.concatenate([acc[:C_VDIM] / acc[C_VDIM:C_VDIM + 1], jnp.zeros((LANES - C_VDIM, tq), F32)], axis=0)
    o_ref[...] = o.T.astype(o_ref.dtype)


def _latent(qc, kc, vt, batch, seq, bounded):
    t = batch * seq
    tq, tk = TQ_LAT, vt.shape[2]
    nq = seq // tq
    if bounded:
        body = functools.partial(_latent_bounded_kernel, tk=tk)
        scratch = [pltpu.VMEM((VT_ROWS, tq), F32)]
    else:
        body = functools.partial(_latent_kernel, tk=tk)
        scratch = [pltpu.VMEM((LAT_SLOTS, tk, tq), F32), pltpu.VMEM((LAT_SLOTS, 1, tq), F32),
                   pltpu.VMEM((1, tq), F32), pltpu.VMEM((VT_ROWS, tq), F32)]
    return pl.pallas_call(
        body,
        grid=(batch, C_HEADS, nq),
        in_specs=[pl.BlockSpec((tq, LANES), lambda b, h, i: (b * nq + i, h)),
                  pl.BlockSpec((seq, LANES), lambda b, h, i: (b, h)),
                  pl.BlockSpec((seq // tk, VT_ROWS, tk), lambda b, h, i: (b, h, 0))],
        out_specs=pl.BlockSpec((tq, LANES), lambda b, h, i: (b * nq + i, h)),
        out_shape=jax.ShapeDtypeStruct((t, C_SLOTS), BF16),
        scratch_shapes=scratch,
        compiler_params=_cparams(("parallel", "parallel", "parallel")),
        name="latent_bounded" if bounded else "latent",
    )(qc, kc, vt)


def _merge_mlp_kernel(x_ref, oa1_ref, oa2_ref, oa3_ref, ml1_ref, ml2_ref, ml3_ref, ob_ref, oc_ref,
                      expand_ref, gout_ref, wout_ref, gmlp_ref, wup_ref, wdown_ref, out_ref, oa_s, ml_s):
    tm = x_ref.shape[0]
    oas, mls = [oa1_ref[...]], [ml1_ref[...]]
    for idx, (o_ref, l_ref) in enumerate(((oa2_ref, ml2_ref), (oa3_ref, ml3_ref))):
        r = A_CONFIGS[idx + 1][1]
        for c in range(r):
            for s in range(A_W // LANES):
                col = c * A_W + s * LANES
                oa_s[idx, s, pl.ds(c, tm // r, stride=r), :] = o_ref[:, col:col + LANES]
            ml_s[idx, pl.ds(c, tm // r, stride=r), :] = l_ref[:, c * LANES:(c + 1) * LANES]
        oas.append(jnp.concatenate([oa_s[idx, s] for s in range(A_W // LANES)], axis=1))
        mls.append(ml_s[idx])
    lane = lax.broadcasted_iota(jnp.int32, mls[0].shape, 1)
    m_all = jnp.maximum(jnp.maximum(mls[0], mls[1]), mls[2])
    ws = [pltpu.roll(ml, LANES - ML_L_OFFSET, 1) * jnp.exp2(ml - m_all) for ml in mls]
    wsum = ws[0] + ws[1] + ws[2]
    packed = jnp.zeros_like(wsum)
    for c, w in enumerate(ws):
        wn = jnp.where(lane < A_HEADS, w / wsum, 0.0)
        packed = packed + (wn if c == 0 else pltpu.roll(wn, ML_L_OFFSET * c, 1))
    hi = packed.astype(BF16)
    lo = (packed - hi.astype(F32)).astype(BF16)
    spread = (jnp.dot(hi, expand_ref[...], preferred_element_type=F32)
              + jnp.dot(lo, expand_ref[...], preferred_element_type=F32))
    oa = spread[:, :A_W] * oas[0] + spread[:, A_W:2 * A_W] * oas[1] + spread[:, 2 * A_W:] * oas[2]

    def group_norm(v, width):
        return v * lax.rsqrt(jnp.sum(v * v, axis=-1, keepdims=True) * (1.0 / width) + EPS)

    ob = ob_ref[...].astype(F32)
    oc_wide = oc_ref[...].astype(F32)
    half = lax.broadcasted_iota(jnp.int32, (tm, LANES), 1) < C_VDIM
    oc = jnp.concatenate(
        [jnp.where(half, oc_wide[:, 2 * p * LANES:(2 * p + 1) * LANES],
                   pltpu.roll(oc_wide[:, (2 * p + 1) * LANES:(2 * p + 2) * LANES], C_VDIM, 1))
         for p in range(C_HEADS // 2)], axis=1)
    mixed = jnp.concatenate([group_norm(oa, A_W), group_norm(ob, B_QW), group_norm(oc, C_HEADS * C_VDIM)], axis=1)
    mixed = (mixed * gout_ref[...]).astype(BF16)
    x = x_ref[...] + jnp.dot(mixed, wout_ref[...], preferred_element_type=F32)

    h = (x * lax.rsqrt(jnp.mean(x * x, axis=-1, keepdims=True) + EPS) * gmlp_ref[...]).astype(BF16)
    acc = x
    for s in range(0, D_FF, FF_CHUNK):
        u = jnp.dot(h, wup_ref[:, s:s + FF_CHUNK], preferred_element_type=F32)
        u = jnp.square(jnp.maximum(u, 0.0)).astype(BF16)
        acc = acc + jnp.dot(u, wdown_ref[s:s + FF_CHUNK, :], preferred_element_type=F32)
    out_ref[...] = acc


def _merge_mlp(x2d, oas, mls, ob, oc, p):
    t = x2d.shape[0]
    tm = TM_MLP
    row = lambda w: pl.BlockSpec((tm, w), lambda i: (i, 0))
    strided = lambda w, r: pl.BlockSpec((tm // r, r * w), lambda i: (i, 0))
    single = lambda shape: pl.BlockSpec(shape, lambda i: (0,) * len(shape), pipeline_mode=pl.Buffered(1))
    return pl.pallas_call(
        _merge_mlp_kernel,
        grid=(t // tm,),
        in_specs=[row(D_MODEL)] + [strided(A_W, r) for _, r in A_CONFIGS]
        + [strided(LANES, r) for _, r in A_CONFIGS] + [row(B_QW), row(C_SLOTS),
                  single((LANES, 3 * A_W)), single((1, MIX_COLS)), single((MIX_COLS, D_MODEL)),
                  single((1, D_MODEL)), single((D_MODEL, D_FF)), single((D_FF, D_MODEL))],
        out_specs=row(D_MODEL),
        out_shape=jax.ShapeDtypeStruct((t, D_MODEL), F32),
        scratch_shapes=[pltpu.VMEM((2, A_W // LANES, tm, LANES), F32), pltpu.VMEM((2, tm, LANES), F32)],
        compiler_params=_cparams(("parallel",)),
        name="merge_mlp",
    )(x2d, *oas, *mls, ob, oc, p["expand"], p["gout"], p["w_out"], p["gmlp"], p["w_up"], p["w_down"])


def _block_diag_ones(group):
    idx = np.arange(2 * LANES) // group
    return jnp.asarray((idx[:, None] == idx[None, :]).astype(np.float32), dtype=BF16)


def _expand_matrix():
    e = np.zeros((LANES, len(A_CONFIGS) * A_W), np.float32)
    for c in range(len(A_CONFIGS)):
        for h in range(A_HEADS):
            e[ML_L_OFFSET * c + h, c * A_W + h * HEAD_DIM:c * A_W + (h + 1) * HEAD_DIM] = 1.0
    return jnp.asarray(e, dtype=BF16)


def _pad_heads(w, heads, used, lo=0):
    lead = w.shape[:-1]
    w = w.reshape(lead + (heads, used))
    pad = [(0, 0)] * len(lead) + [(0, 0), (lo, LANES - lo - used)]
    return jnp.pad(w, pad).reshape(lead + (heads * LANES,))


def _layer_params(i, norm_mix, w_in, qk_gain_a, qk_gain_b, q_lat_gain, kv_lat_gain, w_uq, w_ukv, qk_gain_c,
                  out_norm, w_out, norm_mlp, w_up, w_down):
    w = w_in[i]
    o = np.cumsum((A_W, A_W, A_W, B_QW, B_KVW, B_KVW, C_Q_RANK, C_KV_RANK)).tolist()
    qb = w[:, o[2]:o[3]].reshape(D_MODEL, B_HEADS, HEAD_DIM)[:, B_HEAD_ORDER, :].reshape(D_MODEL, B_QW)
    kr = _pad_heads(w[:, o[7]:], 1, C_ROPE, lo=C_NOPE)
    w_in_p = jnp.concatenate([w[:, :o[2]], qb, w[:, o[3]:o[7]], kr], axis=1).astype(BF16)

    scale = HEAD_DIM ** -0.5 * LOG2E
    gqk = jnp.concatenate([jnp.tile(qk_gain_a[i, 0], A_HEADS) * scale, jnp.tile(qk_gain_a[i, 1], A_HEADS),
                           jnp.tile(qk_gain_b[i, 0], B_HEADS) * scale, jnp.tile(qk_gain_b[i, 1], B_KV_HEADS)])
    ukv = w_ukv[i].reshape(C_KV_RANK, C_HEADS, C_NOPE + C_VDIM)
    wuk = _pad_heads(ukv[:, :, :C_NOPE].reshape(C_KV_RANK, -1), C_HEADS, C_NOPE)
    wuvt = ukv[:, :, C_NOPE:].reshape(C_KV_RANK, C_HEADS * C_VDIM).T
    g = out_norm[i]
    gb = g[A_W:A_W + B_QW].reshape(B_HEADS, HEAD_DIM)[B_HEAD_ORDER, :].reshape(B_QW)
    gout = jnp.concatenate([g[:A_W], gb, g[A_W + B_QW:]])
    wo = w_out[i]
    wob = wo[A_W:A_W + B_QW].reshape(B_HEADS, HEAD_DIM, D_MODEL)[B_HEAD_ORDER, :, :].reshape(B_QW, D_MODEL)
    w_out_p = jnp.concatenate([wo[:A_W], wob, wo[A_W + B_QW:]], axis=0).astype(BF16)
    bound = (C_QK ** 0.5 * LOG2E * BOUND_MARGIN) * jnp.max(jnp.abs(qk_gain_c[i, 0])) * jnp.max(jnp.abs(qk_gain_c[i, 1]))
    return {
        "gmix": norm_mix[i][None, :],
        "w_in": w_in_p,
        "gqk": gqk[None, :],
        "ones64": _block_diag_ones(HEAD_DIM),
        "ones128": _block_diag_ones(LANES),
        "glq": q_lat_gain[i][None, :],
        "wuq": _pad_heads(w_uq[i], C_HEADS, C_QK).astype(BF16),
        "gqc": jnp.tile(_pad_heads(qk_gain_c[i, 0] * (C_QK ** -0.5 * LOG2E), 1, C_QK), C_HEADS)[None, :],
        "glkv": kv_lat_gain[i][None, :],
        "wuk": wuk.astype(BF16),
        "wuvt": wuvt.astype(BF16),
        "gkc": jnp.tile(_pad_heads(qk_gain_c[i, 1], 1, C_QK), C_HEADS)[None, :],
        "qshift": jnp.tile(_pad_heads(-bound[None], 1, 1, lo=C_QK), C_HEADS)[None, :],
        "kone": jnp.tile(_pad_heads(jnp.ones((1,), F32), 1, 1, lo=C_QK), C_HEADS)[None, :],
        "logit_bound": bound,
        "expand": _expand_matrix(),
        "gout": gout[None, :],
        "w_out": w_out_p,
        "gmlp": norm_mlp[i][None, :],
        "w_up": w_up[i].astype(BF16),
        "w_down": w_down[i].astype(BF16),
    }


def kernel(x, positions, rel_bias_table, norm_mix, w_in, qk_gain_a, qk_gain_b, sink_b, q_lat_gain, kv_lat_gain,
           w_uq, w_ukv, qk_gain_c, out_norm, w_out, norm_mlp, w_up, w_down):
    batch, seq, _ = x.shape
    depth = w_in.shape[0]
    x2d = x.reshape(batch * seq, D_MODEL)
    cos_t, sin_t = _rope_tables(positions)
    a_pairs = tuple((2 * p, 2 * p + 1) for p in range(A_HEADS // 2))
    b_pairs = (B_HEAD_ORDER[:2], B_HEAD_ORDER[2:])
    b_cols = tuple(A_HEADS + h for h in B_HEAD_ORDER)

    def bias_tables(shift):
        tiles_a = [_bias_tiles(rel_bias_table, shift, window // (2 * r), r, tuple(range(A_HEADS)))
                   for window, r in A_CONFIGS]
        return tiles_a, _bias_tiles(rel_bias_table, shift, B_RADIUS, 1, b_cols)

    gain_bound = lambda gains: jnp.max(jnp.max(jnp.abs(gains[:, 0]), axis=-1) * jnp.max(jnp.abs(gains[:, 1]), axis=-1))
    qk_bound = HEAD_DIM ** 0.5 * LOG2E * BOUND_MARGIN * jnp.maximum(gain_bound(qk_gain_a), gain_bound(qk_gain_b))
    band_shift = jnp.maximum(qk_bound + LOG2E * jnp.max(jnp.abs(rel_bias_table)), LOG2E * jnp.max(sink_b))
    shifted_tables = bias_tables(band_shift[None])

    a_layout = (3 * A_W, 0, A_W, 2 * A_W, (0, 1, 2))
    b_layout = (B_QW + 2 * B_KVW, 0, B_QW, B_QW + B_KVW, (0, 0))

    def banded_all(bounded, qkv1, qkv4, qkv16, qkvb, sinks):
        shift = band_shift[None] if bounded else jnp.zeros((1,), F32)
        tables_a, table_b = shifted_tables if bounded else bias_tables(shift)
        scal = jnp.concatenate([shift, sinks * LOG2E])
        outs = []
        for (window, r), bias, qkv in zip(A_CONFIGS, tables_a, (qkv1, qkv4, qkv16)):
            outs += _banded(qkv, bias, scal, batch=batch, seq=seq, dilation=r, radius=window // (2 * r),
                            layout=a_layout, head_ids=a_pairs, has_sink=False, emit_ml=True, bounded=bounded,
                            out_dtype=F32)
        outs += _banded(qkvb, table_b, scal, batch=batch, seq=seq, dilation=1, radius=B_RADIUS, layout=b_layout,
                        head_ids=b_pairs, has_sink=True, emit_ml=False, bounded=bounded, out_dtype=BF16)
        return tuple(outs)

    for i in range(depth):
        p = _layer_params(i, norm_mix, w_in, qk_gain_a, qk_gain_b, q_lat_gain, kv_lat_gain, w_uq, w_ukv,
                          qk_gain_c, out_norm, w_out, norm_mlp, w_up, w_down)
        qkv1, qkv4, qkv16, qkvb, qc, kc, vt = _inproj(x2d, cos_t, sin_t, p)
        o1, ml1, o4, ml4, o16, ml16, ob = lax.cond(
            band_shift <= MAX_LOGIT_BOUND, functools.partial(banded_all, True), functools.partial(banded_all, False),
            qkv1, qkv4, qkv16, qkvb, sink_b[i])
        oas, mls = [o1, o4, o16], [ml1, ml4, ml16]
        oc = lax.cond(p["logit_bound"] <= MAX_LOGIT_BOUND,
                      functools.partial(_latent, batch=batch, seq=seq, bounded=True),
                      functools.partial(_latent, batch=batch, seq=seq, bounded=False), qc, kc, vt)
        x2d = _merge_mlp(x2d, oas, mls, ob, oc, p)
    return x2d.reshape(batch, seq, D_MODEL)
```

```python
import functools
import math

import numpy as np
import jax
import jax.numpy as jnp
from jax import lax
from jax.experimental import pallas as pl
from jax.experimental.pallas import tpu as pltpu

F32 = jnp.float32
BF16 = jnp.bfloat16

D_MODEL = 1024
HEAD_DIM = 64
A_HEADS = 6
A_CONFIGS = ((128, 1), (512, 4), (2048, 16))
B_HEADS = 4
B_KV_HEADS = 2
B_RADIUS = 128
C_HEADS = 6
C_NOPE = 64
C_ROPE = 32
C_VDIM = 64
C_QK = C_NOPE + C_ROPE
C_Q_RANK = 256
C_KV_RANK = 128
ROPE_THETA = 10000.0
N_BUCKETS = 32
MAX_DISTANCE = 1024
D_FF = 4 * D_MODEL
EPS = 1e-6
NEG = -1e30

A_W = A_HEADS * HEAD_DIM
B_QW = B_HEADS * HEAD_DIM
B_KVW = B_KV_HEADS * HEAD_DIM
LANES = 128
C_SLOTS = C_HEADS * LANES
IN_COLS = 3 * A_W + B_QW + 2 * B_KVW + C_Q_RANK + C_KV_RANK + LANES
MIX_COLS = A_W + B_QW + C_HEADS * C_VDIM
VT_ROWS = 80
LOG2E = math.log2(math.e)
ML_L_OFFSET = 8

B_HEAD_ORDER = (0, 2, 1, 3)

TM_IN = 512
TQ_BAND = 2048
SUB = 128
TQ_LAT = 2048
BOUND_MARGIN = 1.02
MAX_LOGIT_BOUND = 50.0
LAT_BOUNDED_UNROLL = 16
LAT_SLOTS = 3
LAT_UNROLL = 6
TM_MLP = 512
FF_CHUNK = 1024
VMEM_LIMIT = 56 * 1024 * 1024


def _cparams(sem):
    return pltpu.CompilerParams(dimension_semantics=sem, vmem_limit_bytes=VMEM_LIMIT)


def _const_spec(shape):
    zeros = (0,) * len(shape)
    return pl.BlockSpec(shape, lambda *_: zeros)


def _rope_table_kernel(pos_ref, inv_ref, cos_ref, sin_ref):
    ang = pos_ref[...] * inv_ref[...]
    lane = lax.broadcasted_iota(jnp.int32, ang.shape, 1)
    c = jnp.cos(ang)
    s = jnp.sin(ang)
    s = jnp.where(lane % C_ROPE < C_ROPE // 2, -s, s)
    live = (lane >= C_NOPE) & (lane < C_QK)
    per_row = LANES // C_ROPE
    for a in range(per_row):
        shift = (C_NOPE - a * C_ROPE) % LANES
        rows = pl.ds(a, ang.shape[0], stride=per_row)
        cos_ref[rows, :] = jnp.where(live, c if shift == 0 else pltpu.roll(c, shift, 1), 1.0)
        sin_ref[rows, :] = jnp.where(live, s if shift == 0 else pltpu.roll(s, shift, 1), 0.0)


def _rope_tables(positions):
    t = positions.size
    half = C_ROPE // 2
    per_row = LANES // C_ROPE
    inv = ROPE_THETA ** (-jnp.arange(half, dtype=F32) / half)
    inv_row = jnp.tile(inv, 2 * per_row)[None, :]
    pos = jnp.repeat(positions.astype(F32).reshape(t // per_row, per_row), C_ROPE, axis=1)
    tm = 2048
    return pl.pallas_call(
        _rope_table_kernel,
        grid=(t // tm,),
        in_specs=[pl.BlockSpec((tm // per_row, LANES), lambda i: (i, 0)), _const_spec((1, LANES))],
        out_specs=[pl.BlockSpec((tm, LANES), lambda i: (i, 0))] * 2,
        out_shape=[jax.ShapeDtypeStruct((t, LANES), F32)] * 2,
        compiler_params=_cparams(("parallel",)),
        name="rope_tables",
    )(pos, inv_row)


def _bucket_thresholds():
    half = N_BUCKETS // 2
    exact = half // 2
    n = np.arange(1, 2 * MAX_DISTANCE + 2, dtype=np.float64)
    far = exact + (np.log(n / exact) / math.log(MAX_DISTANCE / exact) * (half - exact)).astype(np.int64)
    far = np.minimum(far, half - 1)
    return tuple(int(n[np.argmax(far >= exact + k)]) for k in range(1, half - exact))


def _bias_kernel(table_ref, shift_ref, out_ref, *, radius, dilation, head_cols):
    hsel = pl.program_id(0)
    width = SUB + 2 * radius
    row = lax.broadcasted_iota(jnp.int32, (SUB, width), 0)
    col = lax.broadcasted_iota(jnp.int32, (SUB, width), 1)
    rel = col - radius - row
    n = jnp.abs(rel) * dilation
    half = N_BUCKETS // 2
    exact = half // 2
    far = jnp.full(n.shape, exact, jnp.int32)
    for thr in _bucket_thresholds():
        far = far + (n >= thr).astype(jnp.int32)
    bucket = jnp.where(rel > 0, half, 0) + jnp.where(n < exact, n, far)
    for idx, hc in enumerate(head_cols):
        @pl.when(hsel == idx)
        def _(hc=hc):
            val = jnp.zeros(n.shape, F32)
            for b in range(N_BUCKETS):
                val = jnp.where(bucket == b, table_ref[b, hc], val)
            out_ref[...] = jnp.where(jnp.abs(rel) <= radius, val * LOG2E - shift_ref[0], NEG)


def _bias_tiles(table, shift, radius, dilation, head_cols):
    width = SUB + 2 * radius
    tiles = pl.pallas_call(
        functools.partial(_bias_kernel, radius=radius, dilation=dilation, head_cols=head_cols),
        grid=(len(head_cols),),
        in_specs=[pl.BlockSpec(memory_space=pltpu.SMEM), pl.BlockSpec(memory_space=pltpu.SMEM)],
        out_specs=pl.BlockSpec((None, SUB, width), lambda h: (h, 0, 0)),
        out_shape=jax.ShapeDtypeStruct((len(head_cols), SUB, width), F32),
        compiler_params=_cparams(("arbitrary",)),
        name="bias_tiles",
    )(table, shift)
    return tiles.reshape(len(head_cols) // 2, 2 * SUB, width)


def _group_mean_sq(y, ones_ref, group):
    sq = (y * y).astype(BF16)
    width = y.shape[1]
    parts = []
    for s in range(0, width, 2 * LANES):
        w = min(2 * LANES, width - s)
        parts.append(jnp.dot(sq[:, s:s + w], ones_ref[:w, :w], preferred_element_type=F32))
    out = parts[0] if len(parts) == 1 else jnp.concatenate(parts, axis=1)
    return out * (1.0 / group)


def _slab_roll(y, shift):
    parts = [pltpu.roll(y[:, s:s + LANES], shift, 1) for s in range(0, y.shape[1], LANES)]
    return parts[0] if len(parts) == 1 else jnp.concatenate(parts, axis=1)


def _rope(y, cos_t, sin_t, lane):
    swapped = jnp.where(lane < C_NOPE + C_ROPE // 2, _slab_roll(y, LANES - C_ROPE // 2), _slab_roll(y, C_ROPE // 2))
    return y * cos_t + swapped * sin_t


def _inproj_kernel(x_ref, cos_ref, sin_ref, gmix_ref, w_ref, gqk_ref, ones64_ref, ones128_ref,
                   glq_ref, wuq_ref, gqc_ref, glkv_ref, wuk_ref, wuvt_ref, gkc_ref, gkr_ref, qshift_ref, kone_ref,
                   qkv1_ref, qkv4_ref, qkv16_ref, qkvb_ref, qc_ref, kc_ref, vt_ref, stage_ref, stage4_ref):
    x = x_ref[...]
    h = x * lax.rsqrt(jnp.mean(x * x, axis=-1, keepdims=True) + EPS) * gmix_ref[...]
    y = jnp.dot(h.astype(BF16), w_ref[...], preferred_element_type=F32)

    o_qb = 3 * A_W
    o_kb = o_qb + B_QW
    o_vb = o_kb + B_KVW
    o_cq = o_vb + B_KVW
    o_ckv = o_cq + C_Q_RANK
    o_kr = o_ckv + C_KV_RANK
    yn = jnp.concatenate([y[:, :2 * A_W], y[:, o_qb:o_vb]], axis=1)
    yn = yn * lax.rsqrt(_group_mean_sq(yn, ones64_ref, HEAD_DIM) + EPS) * gqk_ref[...]
    qkvb_ref[:, :B_QW + B_KVW] = yn[:, 2 * A_W:].astype(BF16)
    n_slabs, tm = stage_ref.shape[0], stage_ref.shape[1]
    for s in range(n_slabs):
        src = yn if s * LANES < 2 * A_W else y
        stage_ref[s] = src[:, s * LANES:(s + 1) * LANES]
    for s in range(n_slabs):
        qkv1_ref[:, s * LANES:(s + 1) * LANES] = stage_ref[s].astype(BF16)
        for c4 in range(4):
            rows = stage_ref[s, pl.ds(c4, tm // 4, stride=4), :]
            qkv4_ref[:, c4 * 3 * A_W + s * LANES:c4 * 3 * A_W + (s + 1) * LANES] = rows.astype(BF16)
            stage4_ref[c4 * n_slabs + s] = rows
    for c16 in range(16):
        for s in range(n_slabs):
            rows = stage4_ref[(c16 % 4) * n_slabs + s, pl.ds(c16 // 4, tm // 16, stride=4), :]
            qkv16_ref[:, c16 * 3 * A_W + s * LANES:c16 * 3 * A_W + (s + 1) * LANES] = rows.astype(BF16)
    qkvb_ref[:, B_QW + B_KVW:] = y[:, o_vb:o_cq].astype(BF16)

    cos_t = jnp.concatenate([cos_ref[...]] * C_HEADS, axis=1)
    sin_t = jnp.concatenate([sin_ref[...]] * C_HEADS, axis=1)
    lane = lax.broadcasted_iota(jnp.int32, cos_t.shape, 1) % LANES
    cq = y[:, o_cq:o_ckv]
    cq = cq * lax.rsqrt(jnp.mean(cq * cq, axis=-1, keepdims=True) + EPS) * glq_ref[...]
    qc = jnp.dot(cq.astype(BF16), wuq_ref[...], preferred_element_type=F32)
    qc = qc * lax.rsqrt(_group_mean_sq(qc, ones128_ref, C_QK) + EPS) * gqc_ref[...]
    qc_ref[...] = (_rope(qc, cos_t, sin_t, lane) + qshift_ref[...]).astype(BF16)

    ckv = y[:, o_ckv:o_kr]
    ckv = (ckv * lax.rsqrt(jnp.mean(ckv * ckv, axis=-1, keepdims=True) + EPS) * glkv_ref[...]).astype(BF16)
    kr = y[:, o_kr:]
    kn = jnp.dot(ckv, wuk_ref[...], preferred_element_type=F32)
    inv_rms = lax.rsqrt(_group_mean_sq(kn + jnp.concatenate([kr] * C_HEADS, axis=1), ones128_ref, C_QK) + EPS)
    lane_slab = lax.broadcasted_iota(jnp.int32, kr.shape, 1)
    kr_rot = _rope(kr * gkr_ref[...], cos_ref[...], sin_ref[...], lane_slab)
    kc = (kn * gkc_ref[...] + jnp.concatenate([kr_rot] * C_HEADS, axis=1)) * inv_rms
    kc_ref[...] = (kc + kone_ref[...]).astype(BF16)
    vt = lax.dot_general(wuvt_ref[...], ckv, (((1,), (1,)), ((), ())), preferred_element_type=F32)
    row = lax.broadcasted_iota(jnp.int32, (VT_ROWS - C_VDIM, vt.shape[1]), 0)
    ones_rows = jnp.where(row == 0, 1.0, 0.0).astype(BF16)
    for hd in range(C_HEADS):
        vt_ref[0, hd * VT_ROWS:hd * VT_ROWS + C_VDIM, :] = vt[hd * C_VDIM:(hd + 1) * C_VDIM].astype(BF16)
        vt_ref[0, hd * VT_ROWS + C_VDIM:(hd + 1) * VT_ROWS, :] = ones_rows


def _inproj(x2d, cos_t, sin_t, p):
    t = x2d.shape[0]
    tm = TM_IN
    row = lambda w: pl.BlockSpec((tm, w), lambda i: (i, 0))
    outs = [(B_QW + 2 * B_KVW, BF16)] + [(C_SLOTS, BF16)] * 2
    a_specs = [pl.BlockSpec((tm // r, r * 3 * A_W), lambda i: (i, 0)) for _, r in A_CONFIGS]
    a_shapes = [jax.ShapeDtypeStruct((t // r, r * 3 * A_W), BF16) for _, r in A_CONFIGS]
    return pl.pallas_call(
        _inproj_kernel,
        grid=(t // tm,),
        in_specs=[row(D_MODEL), row(LANES), row(LANES),
                  _const_spec((1, D_MODEL)), _const_spec((D_MODEL, IN_COLS)),
                  _const_spec((1, 2 * A_W + B_QW + B_KVW)),
                  _const_spec((2 * LANES, 2 * LANES)), _const_spec((2 * LANES, 2 * LANES)),
                  _const_spec((1, C_Q_RANK)), _const_spec((C_Q_RANK, C_SLOTS)), _const_spec((1, C_SLOTS)),
                  _const_spec((1, C_KV_RANK)), _const_spec((C_KV_RANK, C_SLOTS)),
                  _const_spec((C_HEADS * C_VDIM, C_KV_RANK)), _const_spec((1, C_SLOTS)), _const_spec((1, LANES)),
                  _const_spec((1, C_SLOTS)), _const_spec((1, C_SLOTS))],
        out_specs=a_specs + [row(w) for w, _ in outs]
        + [pl.BlockSpec((1, C_HEADS * VT_ROWS, tm), lambda i: (i, 0, 0))],
        out_shape=a_shapes + [jax.ShapeDtypeStruct((t, w), d) for w, d in outs]
        + [jax.ShapeDtypeStruct((t // tm, C_HEADS * VT_ROWS, tm), BF16)],
        scratch_shapes=[pltpu.VMEM((3 * A_W // LANES, tm, LANES), F32),
                        pltpu.VMEM((4 * 3 * A_W // LANES, tm // 4, LANES), F32)],
        compiler_params=_cparams(("parallel",)),
        name="inproj",
    )(x2d, cos_t, sin_t, p["gmix"], p["w_in"], p["gqk"], p["ones64"], p["ones128"],
      p["glq"], p["wuq"], p["gqc"], p["glkv"], p["wuk"], p["wuvt"], p["gkc"], p["gkc"][:, :LANES], p["qshift"], p["kone"])


def _window(lo_ref, main_ref, hi_ref, a, radius, tq, cols):
    start, end = a - radius, a + SUB + radius
    parts = []
    if start < 0:
        parts.append(lo_ref[radius + start:radius, cols])
        start = 0
    parts.append(main_ref[start:min(end, tq), cols])
    if end > tq:
        parts.append(hi_ref[0:end - tq, cols])
    return parts[0] if len(parts) == 1 else jnp.concatenate(parts, axis=0)


def _banded_kernel(scal_ref, lo_ref, main_ref, hi_ref, bias_ref, *refs, tq, radius, head_ids, layout, group,
                   has_sink, emit_ml, bounded, n_tiles):
    class_w, q_off, k_off, v_off, kv_slab = layout
    n_pairs = len(head_ids)
    o_ref = refs[0]
    ml_ref = refs[1] if emit_ml else None
    p_scr = refs[-1]
    if not bounded:
        s_scr, m_scr = refs[-3:-1]
    sink = lambda hd: scal_ref[1 + hd]
    tile = pl.program_id(2)
    width = SUB + 2 * radius
    lane = lax.broadcasted_iota(jnp.int32, (SUB, LANES), 1)
    col = lax.broadcasted_iota(jnp.int32, (2 * SUB, width), 1)
    low = lane < HEAD_DIM
    slab = lambda base, idx: slice(base + idx * LANES, base + (idx + 1) * LANES)
    tiles = [(a, cls, pair) for a in range(0, tq, SUB) for cls in range(group) for pair in range(n_pairs)]

    for g, (a, cls, pair) in enumerate(tiles):
        qs = main_ref[a:a + SUB, slab(cls * class_w + q_off, pair)]
        ks = _window(lo_ref, main_ref, hi_ref, a, radius, tq, slab(cls * class_w + k_off, kv_slab[pair]))
        zero = jnp.zeros_like(qs)
        q2 = jnp.concatenate([jnp.where(low, qs, zero), jnp.where(low, zero, qs)], axis=0)
        s = lax.dot_general(q2, ks, (((1,), (1,)), ((), ())), preferred_element_type=F32)
        s = s + bias_ref[pair]
        if a == 0:
            s = jnp.where(col < jnp.where(tile == 0, radius, 0), NEG, s)
        if a == tq - SUB:
            s = jnp.where(col >= jnp.where(tile == n_tiles - 1, SUB + radius, width), NEG, s)
        if bounded:
            p_scr[g] = jnp.exp2(s).astype(BF16)
            continue
        m = jnp.max(s, axis=-1, keepdims=True)
        if has_sink:
            sinks = jnp.concatenate([jnp.full((SUB, 1), sink(hd), F32) for hd in head_ids[pair]], axis=0)
            m = jnp.maximum(m, sinks)
        s_scr[g] = s
        m_scr[g] = jnp.broadcast_to(m, (2 * SUB, LANES))

    for g in range(0 if bounded else len(tiles)):
        m_wide = jnp.concatenate([m_scr[g]] * (width // LANES), axis=1)
        p_scr[g] = jnp.exp2(s_scr[g] - m_wide).astype(BF16)

    ml = None
    for g, (a, cls, pair) in enumerate(tiles):
        hd0, hd1 = head_ids[pair]
        vs = _window(lo_ref, main_ref, hi_ref, a, radius, tq, slab(cls * class_w + v_off, kv_slab[pair]))
        o2 = jnp.dot(p_scr[g], jnp.concatenate([vs, jnp.ones_like(vs)], axis=1), preferred_element_type=F32)
        if bounded:
            m0 = m1 = jnp.full((SUB, LANES), scal_ref[0], F32)
        else:
            m0, m1 = m_scr[g][:SUB], m_scr[g][SUB:]
        l0, l1 = o2[:SUB, LANES:], o2[SUB:, LANES:]
        if has_sink:
            l0 = l0 + jnp.exp2(sink(hd0) - m0)
            l1 = l1 + jnp.exp2(sink(hd1) - m1)
        out = jnp.where(low, o2[:SUB, :LANES], o2[SUB:, :LANES]) / jnp.where(low, l0, l1)
        o_ref[a:a + SUB, slab(cls * n_pairs * LANES, pair)] = out.astype(o_ref.dtype)
        if emit_ml:
            if pair == 0:
                ml = jnp.where(lane < ML_L_OFFSET, m0, 0.0) if bounded else jnp.zeros((SUB, LANES), F32)
            if not bounded:
                ml = jnp.where(lane == hd0, m0, ml)
                ml = jnp.where(lane == hd1, m1, ml)
            ml = jnp.where(lane == ML_L_OFFSET + hd0, l0, ml)
            ml = jnp.where(lane == ML_L_OFFSET + hd1, l1, ml)
            if pair == n_pairs - 1:
                ml_ref[a:a + SUB, slab(0, cls)] = ml


def _banded(qkv, bias, scal, *, batch, seq, dilation, radius, layout, head_ids, has_sink, emit_ml, bounded, out_dtype):
    n = seq // dilation
    tq = min(TQ_BAND, n)
    n_tiles = n // tq
    group = min(dilation, TQ_BAND // tq)
    class_w = layout[0]
    qw = len(head_ids) * LANES
    n_sub_tiles = (tq // SUB) * group * len(head_ids)
    per_tile = tq // radius
    main = lambda w: pl.BlockSpec((None, tq, group * w), lambda b, c, i: (b, i, c))
    halo = lambda pick: pl.BlockSpec((None, radius, group * class_w), lambda b, c, i: (b, pick(i), c))
    lo = halo(lambda i: jnp.maximum(i * per_tile - 1, 0))
    hi = halo(lambda i: jnp.minimum((i + 1) * per_tile, n // radius - 1))
    qkv = qkv.reshape(batch, n, dilation * class_w)
    out_specs = [main(qw)]
    out_shape = [jax.ShapeDtypeStruct((batch, n, dilation * qw), out_dtype)]
    if emit_ml:
        out_specs.append(main(LANES))
        out_shape.append(jax.ShapeDtypeStruct((batch, n, dilation * LANES), F32))
    outs = pl.pallas_call(
        functools.partial(_banded_kernel, tq=tq, radius=radius, head_ids=head_ids, layout=layout, group=group,
                          has_sink=has_sink, emit_ml=emit_ml, bounded=bounded, n_tiles=n_tiles),
        grid=(batch, dilation // group, n_tiles),
        in_specs=[pl.BlockSpec(memory_space=pltpu.SMEM), lo, main(class_w), hi, _const_spec(bias.shape)],
        out_specs=out_specs,
        out_shape=out_shape,
        scratch_shapes=([] if bounded else [pltpu.VMEM((n_sub_tiles, 2 * SUB, SUB + 2 * radius), F32),
                                            pltpu.VMEM((n_sub_tiles, 2 * SUB, LANES), F32)])
        + [pltpu.VMEM((n_sub_tiles, 2 * SUB, SUB + 2 * radius), BF16)],
        compiler_params=_cparams(("parallel", "parallel", "parallel")),
        name="banded%s_r%d_d%d" % ("_bounded" if bounded else "", radius, dilation),
    )(scal, qkv, qkv, qkv, bias)
    return [o.reshape(batch * n, -1) for o in outs]


def _latent_kernel(q_ref, k_ref, vt_ref, o_ref, s_ref, cmax_ref, m_ref, acc_ref, *, tk):
    tq = q_ref.shape[0]
    nk = k_ref.shape[0] // tk
    ahead = LAT_SLOTS - 1
    qt = q_ref[...].astype(F32).T.astype(BF16)
    m_ref[...] = jnp.full(m_ref.shape, -jnp.inf, F32)
    acc_ref[...] = jnp.zeros(acc_ref.shape, F32)

    def scores(slot, j):
        start = pl.multiple_of(j * tk, tk)
        s = jnp.dot(k_ref[pl.ds(start, tk), :], qt, preferred_element_type=F32)
        s_ref[slot] = s
        cmax_ref[slot] = jnp.max(s, axis=0, keepdims=True)

    def consume(slot, j):
        m_old = m_ref[...]
        m_new = jnp.maximum(m_old, cmax_ref[slot])
        pr = jnp.exp2(s_ref[slot] - m_new).astype(BF16)
        acc_ref[...] = jnp.exp2(m_old - m_new) * acc_ref[...] + jnp.dot(
            vt_ref[j], pr, preferred_element_type=F32)
        m_ref[...] = m_new

    def step(j, u, with_scores):
        if with_scores:
            scores((u + ahead) % LAT_SLOTS, j + ahead)
        consume(u % LAT_SLOTS, j)

    for j in range(ahead):
        scores(j, j)
    trips = (nk - ahead) // LAT_UNROLL

    def body(jj, carry):
        for u in range(LAT_UNROLL):
            step(LAT_UNROLL * jj + u, u, True)
        return carry

    lax.fori_loop(0, trips, body, 0)
    for j in range(trips * LAT_UNROLL, nk):
        step(j, j, j + ahead < nk)
    acc = acc_ref[...]
    o = jnp.concatenate([acc[:C_VDIM] / acc[C_VDIM:C_VDIM + 1], jnp.zeros((LANES - C_VDIM, tq), F32)], axis=0)
    o_ref[...] = o.T.astype(o_ref.dtype)


def _latent_bounded_kernel(q_ref, k_ref, vt_ref, o_ref, acc_ref, *, tk):
    tq = q_ref.shape[0]
    nk = k_ref.shape[0] // tk
    qt = q_ref[...].astype(F32).T.astype(BF16)
    acc_ref[...] = jnp.zeros(acc_ref.shape, F32)

    def body(jj, carry):
        for u in range(LAT_BOUNDED_UNROLL):
            j = LAT_BOUNDED_UNROLL * jj + u
            start = pl.multiple_of(j * tk, tk)
            s = jnp.dot(k_ref[pl.ds(start, tk), :], qt, preferred_element_type=F32)
            acc_ref[...] += jnp.dot(vt_ref[j], jnp.exp2(s).astype(BF16), preferred_element_type=F32)
        return carry

    lax.fori_loop(0, nk // LAT_BOUNDED_UNROLL, body, 0)
    acc = acc_ref[...]
    o = jnp.concatenate([acc[:C_VDIM] / acc[C_VDIM:C_VDIM + 1], jnp.zeros((LANES - C_VDIM, tq), F32)], axis=0)
    o_ref[...] = o.T.astype(o_ref.dtype)


def _latent(qc, kc, vt, batch, seq, bounded):
    t = batch * seq
    tq, tk = TQ_LAT, vt.shape[2]
    nq = seq // tq
    if bounded:
        body = functools.partial(_latent_bounded_kernel, tk=tk)
        scratch = [pltpu.VMEM((VT_ROWS, tq), F32)]
    else:
        body = functools.partial(_latent_kernel, tk=tk)
        scratch = [pltpu.VMEM((LAT_SLOTS, tk, tq), F32), pltpu.VMEM((LAT_SLOTS, 1, tq), F32),
                   pltpu.VMEM((1, tq), F32), pltpu.VMEM((VT_ROWS, tq), F32)]
    return pl.pallas_call(
        body,
        grid=(batch, C_HEADS, nq),
        in_specs=[pl.BlockSpec((tq, LANES), lambda b, h, i: (b * nq + i, h)),
                  pl.BlockSpec((seq, LANES), lambda b, h, i: (b, h)),
                  pl.BlockSpec((seq // tk, VT_ROWS, tk), lambda b, h, i: (b, h, 0))],
        out_specs=pl.BlockSpec((tq, LANES), lambda b, h, i: (b * nq + i, h)),
        out_shape=jax.ShapeDtypeStruct((t, C_SLOTS), BF16),
        scratch_shapes=scratch,
        compiler_params=_cparams(("parallel", "parallel", "parallel")),
        name="latent_bounded" if bounded else "latent",
    )(qc, kc, vt)


def _merge_mlp_kernel(x_ref, oa1_ref, oa2_ref, oa3_ref, ml1_ref, ml2_ref, ml3_ref, ob_ref, oc_ref,
                      expand_ref, gout_ref, wout_ref, gmlp_ref, wup_ref, wdown_ref, out_ref, oa_s, ml_s):
    tm = x_ref.shape[0]
    oas, mls = [oa1_ref[...]], [ml1_ref[...]]
    for idx, (o_ref, l_ref) in enumerate(((oa2_ref, ml2_ref), (oa3_ref, ml3_ref))):
        r = A_CONFIGS[idx + 1][1]
        for c in range(r):
            for s in range(A_W // LANES):
                col = c * A_W + s * LANES
                oa_s[idx, s, pl.ds(c, tm // r, stride=r), :] = o_ref[:, col:col + LANES]
            ml_s[idx, pl.ds(c, tm // r, stride=r), :] = l_ref[:, c * LANES:(c + 1) * LANES]
        oas.append(jnp.concatenate([oa_s[idx, s] for s in range(A_W // LANES)], axis=1))
        mls.append(ml_s[idx])
    lane = lax.broadcasted_iota(jnp.int32, mls[0].shape, 1)
    m_all = jnp.maximum(jnp.maximum(mls[0], mls[1]), mls[2])
    ws = [pltpu.roll(ml, LANES - ML_L_OFFSET, 1) * jnp.exp2(ml - m_all) for ml in mls]
    wsum = ws[0] + ws[1] + ws[2]
    packed = jnp.zeros_like(wsum)
    for c, w in enumerate(ws):
        wn = jnp.where(lane < A_HEADS, w / wsum, 0.0)
        packed = packed + (wn if c == 0 else pltpu.roll(wn, ML_L_OFFSET * c, 1))
    hi = packed.astype(BF16)
    lo = (packed - hi.astype(F32)).astype(BF16)
    spread = (jnp.dot(hi, expand_ref[...], preferred_element_type=F32)
              + jnp.dot(lo, expand_ref[...], preferred_element_type=F32))
    oa = spread[:, :A_W] * oas[0] + spread[:, A_W:2 * A_W] * oas[1] + spread[:, 2 * A_W:] * oas[2]

    def group_norm(v, width):
        return v * lax.rsqrt(jnp.sum(v * v, axis=-1, keepdims=True) * (1.0 / width) + EPS)

    ob = ob_ref[...].astype(F32)
    oc_wide = oc_ref[...].astype(F32)
    half = lax.broadcasted_iota(jnp.int32, (tm, LANES), 1) < C_VDIM
    oc = jnp.concatenate(
        [jnp.where(half, oc_wide[:, 2 * p * LANES:(2 * p + 1) * LANES],
                   pltpu.roll(oc_wide[:, (2 * p + 1) * LANES:(2 * p + 2) * LANES], C_VDIM, 1))
         for p in range(C_HEADS // 2)], axis=1)
    mixed = jnp.concatenate([group_norm(oa, A_W), group_norm(ob, B_QW), group_norm(oc, C_HEADS * C_VDIM)], axis=1)
    mixed = (mixed * gout_ref[...]).astype(BF16)
    x = x_ref[...] + jnp.dot(mixed, wout_ref[...], preferred_element_type=F32)

    h = (x * lax.rsqrt(jnp.mean(x * x, axis=-1, keepdims=True) + EPS) * gmlp_ref[...]).astype(BF16)
    acc = x
    for s in range(0, D_FF, FF_CHUNK):
        u = jnp.dot(h, wup_ref[:, s:s + FF_CHUNK], preferred_element_type=F32)
        u = jnp.square(jnp.maximum(u, 0.0)).astype(BF16)
        acc = acc + jnp.dot(u, wdown_ref[s:s + FF_CHUNK, :], preferred_element_type=F32)
    out_ref[...] = acc


def _merge_mlp(x2d, oas, mls, ob, oc, p):
    t = x2d.shape[0]
    tm = TM_MLP
    row = lambda w: pl.BlockSpec((tm, w), lambda i: (i, 0))
    strided = lambda w, r: pl.BlockSpec((tm // r, r * w), lambda i: (i, 0))
    single = lambda shape: pl.BlockSpec(shape, lambda i: (0,) * len(shape), pipeline_mode=pl.Buffered(1))
    return pl.pallas_call(
        _merge_mlp_kernel,
        grid=(t // tm,),
        in_specs=[row(D_MODEL)] + [strided(A_W, r) for _, r in A_CONFIGS]
        + [strided(LANES, r) for _, r in A_CONFIGS] + [row(B_QW), row(C_SLOTS),
                  single((LANES, 3 * A_W)), single((1, MIX_COLS)), single((MIX_COLS, D_MODEL)),
                  single((1, D_MODEL)), single((D_MODEL, D_FF)), single((D_FF, D_MODEL))],
        out_specs=row(D_MODEL),
        out_shape=jax.ShapeDtypeStruct((t, D_MODEL), F32),
        scratch_shapes=[pltpu.VMEM((2, A_W // LANES, tm, LANES), F32), pltpu.VMEM((2, tm, LANES), F32)],
        compiler_params=_cparams(("parallel",)),
        name="merge_mlp",
    )(x2d, *oas, *mls, ob, oc, p["expand"], p["gout"], p["w_out"], p["gmlp"], p["w_up"], p["w_down"])


def _block_diag_ones(group):
    idx = np.arange(2 * LANES) // group
    return jnp.asarray((idx[:, None] == idx[None, :]).astype(np.float32), dtype=BF16)


def _expand_matrix():
    e = np.zeros((LANES, len(A_CONFIGS) * A_W), np.float32)
    for c in range(len(A_CONFIGS)):
        for h in range(A_HEADS):
            e[ML_L_OFFSET * c + h, c * A_W + h * HEAD_DIM:c * A_W + (h + 1) * HEAD_DIM] = 1.0
    return jnp.asarray(e, dtype=BF16)


def _pad_heads(w, heads, used, lo=0):
    lead = w.shape[:-1]
    w = w.reshape(lead + (heads, used))
    pad = [(0, 0)] * len(lead) + [(0, 0), (lo, LANES - lo - used)]
    return jnp.pad(w, pad).reshape(lead + (heads * LANES,))


def _layer_params(i, norm_mix, w_in, qk_gain_a, qk_gain_b, q_lat_gain, kv_lat_gain, w_uq, w_ukv, qk_gain_c,
                  out_norm, w_out, norm_mlp, w_up, w_down):
    w = w_in[i]
    o = np.cumsum((A_W, A_W, A_W, B_QW, B_KVW, B_KVW, C_Q_RANK, C_KV_RANK)).tolist()
    qb = w[:, o[2]:o[3]].reshape(D_MODEL, B_HEADS, HEAD_DIM)[:, B_HEAD_ORDER, :].reshape(D_MODEL, B_QW)
    kr = _pad_heads(w[:, o[7]:], 1, C_ROPE, lo=C_NOPE)
    w_in_p = jnp.concatenate([w[:, :o[2]], qb, w[:, o[3]:o[7]], kr], axis=1).astype(BF16)

    scale = HEAD_DIM ** -0.5 * LOG2E
    gqk = jnp.concatenate([jnp.tile(qk_gain_a[i, 0], A_HEADS) * scale, jnp.tile(qk_gain_a[i, 1], A_HEADS),
                           jnp.tile(qk_gain_b[i, 0], B_HEADS) * scale, jnp.tile(qk_gain_b[i, 1], B_KV_HEADS)])
    ukv = w_ukv[i].reshape(C_KV_RANK, C_HEADS, C_NOPE + C_VDIM)
    wuk = _pad_heads(ukv[:, :, :C_NOPE].reshape(C_KV_RANK, -1), C_HEADS, C_NOPE)
    wuvt = ukv[:, :, C_NOPE:].reshape(C_KV_RANK, C_HEADS * C_VDIM).T
    g = out_norm[i]
    gb = g[A_W:A_W + B_QW].reshape(B_HEADS, HEAD_DIM)[B_HEAD_ORDER, :].reshape(B_QW)
    gout = jnp.concatenate([g[:A_W], gb, g[A_W + B_QW:]])
    wo = w_out[i]
    wob = wo[A_W:A_W + B_QW].reshape(B_HEADS, HEAD_DIM, D_MODEL)[B_HEAD_ORDER, :, :].reshape(B_QW, D_MODEL)
    w_out_p = jnp.concatenate([wo[:A_W], wob, wo[A_W + B_QW:]], axis=0).astype(BF16)
    bound = (C_QK ** 0.5 * LOG2E * BOUND_MARGIN) * jnp.max(jnp.abs(qk_gain_c[i, 0])) * jnp.max(jnp.abs(qk_gain_c[i, 1]))
    return {
        "gmix": norm_mix[i][None, :],
        "w_in": w_in_p,
        "gqk": gqk[None, :],
        "ones64": _block_diag_ones(HEAD_DIM),
        "ones128": _block_diag_ones(LANES),
        "glq": q_lat_gain[i][None, :],
        "wuq": _pad_heads(w_uq[i], C_HEADS, C_QK).astype(BF16),
        "gqc": jnp.tile(_pad_heads(qk_gain_c[i, 0] * (C_QK ** -0.5 * LOG2E), 1, C_QK), C_HEADS)[None, :],
        "glkv": kv_lat_gain[i][None, :],
        "wuk": wuk.astype(BF16),
        "wuvt": wuvt.astype(BF16),
        "gkc": jnp.tile(_pad_heads(qk_gain_c[i, 1], 1, C_QK), C_HEADS)[None, :],
        "qshift": jnp.tile(_pad_heads(-bound[None], 1, 1, lo=C_QK), C_HEADS)[None, :],
        "kone": jnp.tile(_pad_heads(jnp.ones((1,), F32), 1, 1, lo=C_QK), C_HEADS)[None, :],
        "logit_bound": bound,
        "expand": _expand_matrix(),
        "gout": gout[None, :],
        "w_out": w_out_p,
        "gmlp": norm_mlp[i][None, :],
        "w_up": w_up[i].astype(BF16),
        "w_down": w_down[i].astype(BF16),
    }


def kernel(x, positions, rel_bias_table, norm_mix, w_in, qk_gain_a, qk_gain_b, sink_b, q_lat_gain, kv_lat_gain,
           w_uq, w_ukv, qk_gain_c, out_norm, w_out, norm_mlp, w_up, w_down):
    batch, seq, _ = x.shape
    depth = w_in.shape[0]
    x2d = x.reshape(batch * seq, D_MODEL)
    cos_t, sin_t = _rope_tables(positions)
    a_pairs = tuple((2 * p, 2 * p + 1) for p in range(A_HEADS // 2))
    b_pairs = (B_HEAD_ORDER[:2], B_HEAD_ORDER[2:])
    b_cols = tuple(A_HEADS + h for h in B_HEAD_ORDER)

    def bias_tables(shift):
        tiles_a = [_bias_tiles(rel_bias_table, shift, window // (2 * r), r, tuple(range(A_HEADS)))
                   for window, r in A_CONFIGS]
        return tiles_a, _bias_tiles(rel_bias_table, shift, B_RADIUS, 1, b_cols)

    gain_bound = lambda gains: jnp.max(jnp.max(jnp.abs(gains[:, 0]), axis=-1) * jnp.max(jnp.abs(gains[:, 1]), axis=-1))
    qk_bound = HEAD_DIM ** 0.5 * LOG2E * BOUND_MARGIN * jnp.maximum(gain_bound(qk_gain_a), gain_bound(qk_gain_b))
    band_shift = jnp.maximum(qk_bound + LOG2E * jnp.max(jnp.abs(rel_bias_table)), LOG2E * jnp.max(sink_b))
    shifted_tables = bias_tables(band_shift[None])

    a_layout = (3 * A_W, 0, A_W, 2 * A_W, (0, 1, 2))
    b_layout = (B_QW + 2 * B_KVW, 0, B_QW, B_QW + B_KVW, (0, 0))

    def banded_all(bounded, qkv1, qkv4, qkv16, qkvb, sinks):
        shift = band_shift[None] if bounded else jnp.zeros((1,), F32)
        tables_a, table_b = shifted_tables if bounded else bias_tables(shift)
        scal = jnp.concatenate([shift, sinks * LOG2E])
        outs = []
        for (window, r), bias, qkv in zip(A_CONFIGS, tables_a, (qkv1, qkv4, qkv16)):
            outs += _banded(qkv, bias, scal, batch=batch, seq=seq, dilation=r, radius=window // (2 * r),
                            layout=a_layout, head_ids=a_pairs, has_sink=False, emit_ml=True, bounded=bounded,
                            out_dtype=F32)
        outs += _banded(qkvb, table_b, scal, batch=batch, seq=seq, dilation=1, radius=B_RADIUS, layout=b_layout,
                        head_ids=b_pairs, has_sink=True, emit_ml=False, bounded=bounded, out_dtype=BF16)
        return tuple(outs)

    for i in range(depth):
        p = _layer_params(i, norm_mix, w_in, qk_gain_a, qk_gain_b, q_lat_gain, kv_lat_gain, w_uq, w_ukv,
                          qk_gain_c, out_norm, w_out, norm_mlp, w_up, w_down)
        qkv1, qkv4, qkv16, qkvb, qc, kc, vt = _inproj(x2d, cos_t, sin_t, p)
        o1, ml1, o4, ml4, o16, ml16, ob = lax.cond(
            band_shift <= MAX_LOGIT_BOUND, functools.partial(banded_all, True), functools.partial(banded_all, False),
            qkv1, qkv4, qkv16, qkvb, sink_b[i])
        oas, mls = [o1, o4, o16], [ml1, ml4, ml16]
        oc = lax.cond(p["logit_bound"] <= MAX_LOGIT_BOUND,
                      functools.partial(_latent, batch=batch, seq=seq, bounded=True),
                      functools.partial(_latent, batch=batch, seq=seq, bounded=False), qc, kc, vt)
        x2d = _merge_mlp(x2d, oas, mls, ob, oc, p)
    return x2d.reshape(batch, seq, D_MODEL)
```

```python
import functools
import math

import numpy as np
import jax
import jax.numpy as jnp
from jax import lax
from jax.experimental import pallas as pl
from jax.experimental.pallas import tpu as pltpu

F32 = jnp.float32
BF16 = jnp.bfloat16

D_MODEL = 1024
HEAD_DIM = 64
A_HEADS = 6
A_CONFIGS = ((128, 1), (512, 4), (2048, 16))
B_HEADS = 4
B_KV_HEADS = 2
B_RADIUS = 128
C_HEADS = 6
C_NOPE = 64
C_ROPE = 32
C_VDIM = 64
C_QK = C_NOPE + C_ROPE
C_Q_RANK = 256
C_KV_RANK = 128
ROPE_THETA = 10000.0
N_BUCKETS = 32
MAX_DISTANCE = 1024
D_FF = 4 * D_MODEL
EPS = 1e-6
NEG = -1e30

A_W = A_HEADS * HEAD_DIM
B_QW = B_HEADS * HEAD_DIM
B_KVW = B_KV_HEADS * HEAD_DIM
LANES = 128
C_SLOTS = C_HEADS * LANES
IN_COLS = 3 * A_W + B_QW + 2 * B_KVW + C_Q_RANK + C_KV_RANK + LANES
MIX_COLS = A_W + B_QW + C_HEADS * C_VDIM
VT_ROWS = 80
LOG2E = math.log2(math.e)
ML_L_OFFSET = 8

B_HEAD_ORDER = (0, 2, 1, 3)

TM_IN = 512
TQ_BAND = 2048
SUB = 128
TQ_LAT = 2048
BOUND_MARGIN = 1.02
MAX_LOGIT_BOUND = 50.0
LAT_BOUNDED_UNROLL = 16
LAT_SLOTS = 3
LAT_UNROLL = 6
TM_MLP = 512
FF_CHUNK = 1024
TM_ROPE = 2048
VMEM_LIMIT = 56 * 1024 * 1024


def _cparams(sem):
    return pltpu.CompilerParams(dimension_semantics=sem, vmem_limit_bytes=VMEM_LIMIT)


def _const_spec(shape):
    zeros = (0,) * len(shape)
    return pl.BlockSpec(shape, lambda *_: zeros)


def _rope_table_kernel(pos_ref, inv_ref, cos_ref, sin_ref):
    ang = pos_ref[...] * inv_ref[...]
    lane = lax.broadcasted_iota(jnp.int32, ang.shape, 1)
    c = jnp.cos(ang)
    s = jnp.sin(ang)
    s = jnp.where(lane % C_ROPE < C_ROPE // 2, -s, s)
    live = (lane >= C_NOPE) & (lane < C_QK)
    per_row = LANES // C_ROPE
    for a in range(per_row):
        shift = (C_NOPE - a * C_ROPE) % LANES
        rows = pl.ds(a, ang.shape[0], stride=per_row)
        cos_ref[rows, :] = jnp.where(live, c if shift == 0 else pltpu.roll(c, shift, 1), 1.0)
        sin_ref[rows, :] = jnp.where(live, s if shift == 0 else pltpu.roll(s, shift, 1), 0.0)


def _rope_tables(positions):
    t = positions.size
    half = C_ROPE // 2
    per_row = LANES // C_ROPE
    inv = ROPE_THETA ** (-jnp.arange(half, dtype=F32) / half)
    inv_row = jnp.tile(inv, 2 * per_row)[None, :]
    pos = jnp.repeat(positions.astype(F32).reshape(t // per_row, per_row), C_ROPE, axis=1)
    tm = TM_ROPE
    return pl.pallas_call(
        _rope_table_kernel,
        grid=(t // tm,),
        in_specs=[pl.BlockSpec((tm // per_row, LANES), lambda i: (i, 0)), _const_spec((1, LANES))],
        out_specs=[pl.BlockSpec((tm, LANES), lambda i: (i, 0))] * 2,
        out_shape=[jax.ShapeDtypeStruct((t, LANES), F32)] * 2,
        compiler_params=_cparams(("parallel",)),
        name="rope_tables",
    )(pos, inv_row)


def _bucket_thresholds():
    half = N_BUCKETS // 2
    exact = half // 2
    n = np.arange(1, 2 * MAX_DISTANCE + 2, dtype=np.float64)
    far = exact + (np.log(n / exact) / math.log(MAX_DISTANCE / exact) * (half - exact)).astype(np.int64)
    far = np.minimum(far, half - 1)
    return tuple(int(n[np.argmax(far >= exact + k)]) for k in range(1, half - exact))


def _bias_kernel(table_ref, shift_ref, out_ref, *, radius, dilation, head_cols):
    hsel = pl.program_id(0)
    width = SUB + 2 * radius
    row = lax.broadcasted_iota(jnp.int32, (SUB, width), 0)
    col = lax.broadcasted_iota(jnp.int32, (SUB, width), 1)
    rel = col - radius - row
    n = jnp.abs(rel) * dilation
    half = N_BUCKETS // 2
    exact = half // 2
    far = jnp.full(n.shape, exact, jnp.int32)
    for thr in _bucket_thresholds():
        far = far + (n >= thr).astype(jnp.int32)
    bucket = jnp.where(rel > 0, half, 0) + jnp.where(n < exact, n, far)
    for idx, hc in enumerate(head_cols):
        @pl.when(hsel == idx)
        def _(hc=hc):
            val = jnp.zeros(n.shape, F32)
            for b in range(N_BUCKETS):
                val = jnp.where(bucket == b, table_ref[b, hc], val)
            out_ref[...] = jnp.where(jnp.abs(rel) <= radius, val * LOG2E - shift_ref[0], NEG)


def _bias_tiles(table, shift, radius, dilation, head_cols):
    width = SUB + 2 * radius
    tiles = pl.pallas_call(
        functools.partial(_bias_kernel, radius=radius, dilation=dilation, head_cols=head_cols),
        grid=(len(head_cols),),
        in_specs=[pl.BlockSpec(memory_space=pltpu.SMEM), pl.BlockSpec(memory_space=pltpu.SMEM)],
        out_specs=pl.BlockSpec((None, SUB, width), lambda h: (h, 0, 0)),
        out_shape=jax.ShapeDtypeStruct((len(head_cols), SUB, width), F32),
        compiler_params=_cparams(("arbitrary",)),
        name="bias_tiles",
    )(table, shift)
    return tiles.reshape(len(head_cols) // 2, 2 * SUB, width)


def _group_mean_sq(y, ones_ref, group):
    sq = (y * y).astype(BF16)
    width = y.shape[1]
    parts = []
    for s in range(0, width, 2 * LANES):
        w = min(2 * LANES, width - s)
        parts.append(jnp.dot(sq[:, s:s + w], ones_ref[:w, :w], preferred_element_type=F32))
    out = parts[0] if len(parts) == 1 else jnp.concatenate(parts, axis=1)
    return out * (1.0 / group)


def _slab_roll(y, shift):
    parts = [pltpu.roll(y[:, s:s + LANES], shift, 1) for s in range(0, y.shape[1], LANES)]
    return parts[0] if len(parts) == 1 else jnp.concatenate(parts, axis=1)


def _rope(y, cos_t, sin_t, lane):
    swapped = jnp.where(lane < C_NOPE + C_ROPE // 2, _slab_roll(y, LANES - C_ROPE // 2), _slab_roll(y, C_ROPE // 2))
    return y * cos_t + swapped * sin_t


def _inproj_kernel(x_ref, cos_ref, sin_ref, gmix_ref, w_ref, gqk_ref, ones64_ref, ones128_ref,
                   glq_ref, wuq_ref, gqc_ref, glkv_ref, wuk_ref, wuvt_ref, gkc_ref, gkr_ref, qshift_ref, kone_ref,
                   qkv1_ref, qkv4_ref, qkv16_ref, qkvb_ref, qc_ref, kc_ref, vt_ref, stage_ref, stage4_ref):
    x = x_ref[...]
    h = x * lax.rsqrt(jnp.mean(x * x, axis=-1, keepdims=True) + EPS) * gmix_ref[...]
    y = jnp.dot(h.astype(BF16), w_ref[...], preferred_element_type=F32)

    o_qb = 3 * A_W
    o_kb = o_qb + B_QW
    o_vb = o_kb + B_KVW
    o_cq = o_vb + B_KVW
    o_ckv = o_cq + C_Q_RANK
    o_kr = o_ckv + C_KV_RANK
    yn = jnp.concatenate([y[:, :2 * A_W], y[:, o_qb:o_vb]], axis=1)
    yn = yn * lax.rsqrt(_group_mean_sq(yn, ones64_ref, HEAD_DIM) + EPS) * gqk_ref[...]
    qkvb_ref[:, :B_QW + B_KVW] = yn[:, 2 * A_W:].astype(BF16)
    n_slabs, tm = stage_ref.shape[0], stage_ref.shape[1]
    for s in range(n_slabs):
        src = yn if s * LANES < 2 * A_W else y
        stage_ref[s] = src[:, s * LANES:(s + 1) * LANES]
    for s in range(n_slabs):
        qkv1_ref[:, s * LANES:(s + 1) * LANES] = stage_ref[s].astype(BF16)
        for c4 in range(4):
            rows = stage_ref[s, pl.ds(c4, tm // 4, stride=4), :]
            qkv4_ref[:, c4 * 3 * A_W + s * LANES:c4 * 3 * A_W + (s + 1) * LANES] = rows.astype(BF16)
            stage4_ref[c4 * n_slabs + s] = rows
    for c16 in range(16):
        for s in range(n_slabs):
            rows = stage4_ref[(c16 % 4) * n_slabs + s, pl.ds(c16 // 4, tm // 16, stride=4), :]
            qkv16_ref[:, c16 * 3 * A_W + s * LANES:c16 * 3 * A_W + (s + 1) * LANES] = rows.astype(BF16)
    qkvb_ref[:, B_QW + B_KVW:] = y[:, o_vb:o_cq].astype(BF16)

    cos_t = jnp.concatenate([cos_ref[...]] * C_HEADS, axis=1)
    sin_t = jnp.concatenate([sin_ref[...]] * C_HEADS, axis=1)
    lane = lax.broadcasted_iota(jnp.int32, cos_t.shape, 1) % LANES
    cq = y[:, o_cq:o_ckv]
    cq = cq * lax.rsqrt(jnp.mean(cq * cq, axis=-1, keepdims=True) + EPS) * glq_ref[...]
    qc = jnp.dot(cq.astype(BF16), wuq_ref[...], preferred_element_type=F32)
    qc = qc * lax.rsqrt(_group_mean_sq(qc, ones128_ref, C_QK) + EPS) * gqc_ref[...]
    qc_ref[...] = (_rope(qc, cos_t, sin_t, lane) + qshift_ref[...]).astype(BF16)

    ckv = y[:, o_ckv:o_kr]
    ckv = (ckv * lax.rsqrt(jnp.mean(ckv * ckv, axis=-1, keepdims=True) + EPS) * glkv_ref[...]).astype(BF16)
    kr = y[:, o_kr:]
    kn = jnp.dot(ckv, wuk_ref[...], preferred_element_type=F32)
    inv_rms = lax.rsqrt(_group_mean_sq(kn + jnp.concatenate([kr] * C_HEADS, axis=1), ones128_ref, C_QK) + EPS)
    lane_slab = lax.broadcasted_iota(jnp.int32, kr.shape, 1)
    kr_rot = _rope(kr * gkr_ref[...], cos_ref[...], sin_ref[...], lane_slab)
    kc = (kn * gkc_ref[...] + jnp.concatenate([kr_rot] * C_HEADS, axis=1)) * inv_rms
    kc_ref[...] = (kc + kone_ref[...]).astype(BF16)
    vt = lax.dot_general(wuvt_ref[...], ckv, (((1,), (1,)), ((), ())), preferred_element_type=F32)
    row = lax.broadcasted_iota(jnp.int32, (VT_ROWS - C_VDIM, vt.shape[1]), 0)
    ones_rows = jnp.where(row == 0, 1.0, 0.0).astype(BF16)
    for hd in range(C_HEADS):
        vt_ref[0, hd * VT_ROWS:hd * VT_ROWS + C_VDIM, :] = vt[hd * C_VDIM:(hd + 1) * C_VDIM].astype(BF16)
        vt_ref[0, hd * VT_ROWS + C_VDIM:(hd + 1) * VT_ROWS, :] = ones_rows


def _inproj(x2d, cos_t, sin_t, p):
    t = x2d.shape[0]
    tm = TM_IN
    row = lambda w: pl.BlockSpec((tm, w), lambda i: (i, 0))
    outs = [(B_QW + 2 * B_KVW, BF16)] + [(C_SLOTS, BF16)] * 2
    a_specs = [pl.BlockSpec((tm // r, r * 3 * A_W), lambda i: (i, 0)) for _, r in A_CONFIGS]
    a_shapes = [jax.ShapeDtypeStruct((t // r, r * 3 * A_W), BF16) for _, r in A_CONFIGS]
    return pl.pallas_call(
        _inproj_kernel,
        grid=(t // tm,),
        in_specs=[row(D_MODEL), row(LANES), row(LANES),
                  _const_spec((1, D_MODEL)), _const_spec((D_MODEL, IN_COLS)),
                  _const_spec((1, 2 * A_W + B_QW + B_KVW)),
                  _const_spec((2 * LANES, 2 * LANES)), _const_spec((2 * LANES, 2 * LANES)),
                  _const_spec((1, C_Q_RANK)), _const_spec((C_Q_RANK, C_SLOTS)), _const_spec((1, C_SLOTS)),
                  _const_spec((1, C_KV_RANK)), _const_spec((C_KV_RANK, C_SLOTS)),
                  _const_spec((C_HEADS * C_VDIM, C_KV_RANK)), _const_spec((1, C_SLOTS)), _const_spec((1, LANES)),
                  _const_spec((1, C_SLOTS)), _const_spec((1, C_SLOTS))],
        out_specs=a_specs + [row(w) for w, _ in outs]
        + [pl.BlockSpec((1, C_HEADS * VT_ROWS, tm), lambda i: (i, 0, 0))],
        out_shape=a_shapes + [jax.ShapeDtypeStruct((t, w), d) for w, d in outs]
        + [jax.ShapeDtypeStruct((t // tm, C_HEADS * VT_ROWS, tm), BF16)],
        scratch_shapes=[pltpu.VMEM((3 * A_W // LANES, tm, LANES), F32),
                        pltpu.VMEM((4 * 3 * A_W // LANES, tm // 4, LANES), F32)],
        compiler_params=_cparams(("parallel",)),
        name="inproj",
    )(x2d, cos_t, sin_t, p["gmix"], p["w_in"], p["gqk"], p["ones64"], p["ones128"],
      p["glq"], p["wuq"], p["gqc"], p["glkv"], p["wuk"], p["wuvt"], p["gkc"], p["gkc"][:, :LANES], p["qshift"], p["kone"])


def _window(lo_ref, main_ref, hi_ref, a, radius, tq, cols):
    start, end = a - radius, a + SUB + radius
    parts = []
    if start < 0:
        parts.append(lo_ref[radius + start:radius, cols])
        start = 0
    parts.append(main_ref[start:min(end, tq), cols])
    if end > tq:
        parts.append(hi_ref[0:end - tq, cols])
    return parts[0] if len(parts) == 1 else jnp.concatenate(parts, axis=0)


def _banded_kernel(scal_ref, lo_ref, main_ref, hi_ref, bias_ref, *refs, tq, radius, head_ids, layout, group,
                   has_sink, emit_ml, bounded, n_tiles):
    class_w, q_off, k_off, v_off, kv_slab = layout
    n_pairs = len(head_ids)
    o_ref = refs[0]
    ml_ref = refs[1] if emit_ml else None
    p_scr = refs[-1]
    if not bounded:
        s_scr, m_scr = refs[-3:-1]
    sink = lambda hd: scal_ref[1 + hd]
    tile = pl.program_id(2)
    width = SUB + 2 * radius
    lane = lax.broadcasted_iota(jnp.int32, (SUB, LANES), 1)
    col = lax.broadcasted_iota(jnp.int32, (2 * SUB, width), 1)
    low = lane < HEAD_DIM
    slab = lambda base, idx: slice(base + idx * LANES, base + (idx + 1) * LANES)
    tiles = [(a, cls, pair) for a in range(0, tq, SUB) for cls in range(group) for pair in range(n_pairs)]

    for g, (a, cls, pair) in enumerate(tiles):
        qs = main_ref[a:a + SUB, slab(cls * class_w + q_off, pair)]
        ks = _window(lo_ref, main_ref, hi_ref, a, radius, tq, slab(cls * class_w + k_off, kv_slab[pair]))
        zero = jnp.zeros_like(qs)
        q2 = jnp.concatenate([jnp.where(low, qs, zero), jnp.where(low, zero, qs)], axis=0)
        s = lax.dot_general(q2, ks, (((1,), (1,)), ((), ())), preferred_element_type=F32)
        s = s + bias_ref[pair]
        if a == 0:
            s = jnp.where(col < jnp.where(tile == 0, radius, 0), NEG, s)
        if a == tq - SUB:
            s = jnp.where(col >= jnp.where(tile == n_tiles - 1, SUB + radius, width), NEG, s)
        if bounded:
            p_scr[g] = jnp.exp2(s).astype(BF16)
            continue
        m = jnp.max(s, axis=-1, keepdims=True)
        if has_sink:
            sinks = jnp.concatenate([jnp.full((SUB, 1), sink(hd), F32) for hd in head_ids[pair]], axis=0)
            m = jnp.maximum(m, sinks)
        s_scr[g] = s
        m_scr[g] = jnp.broadcast_to(m, (2 * SUB, LANES))

    for g in range(0 if bounded else len(tiles)):
        m_wide = jnp.concatenate([m_scr[g]] * (width // LANES), axis=1)
        p_scr[g] = jnp.exp2(s_scr[g] - m_wide).astype(BF16)

    ml = None
    for g, (a, cls, pair) in enumerate(tiles):
        hd0, hd1 = head_ids[pair]
        vs = _window(lo_ref, main_ref, hi_ref, a, radius, tq, slab(cls * class_w + v_off, kv_slab[pair]))
        o2 = jnp.dot(p_scr[g], jnp.concatenate([vs, jnp.ones_like(vs)], axis=1), preferred_element_type=F32)
        if bounded:
            m0 = m1 = jnp.full((SUB, LANES), scal_ref[0], F32)
        else:
            m0, m1 = m_scr[g][:SUB], m_scr[g][SUB:]
        l0, l1 = o2[:SUB, LANES:], o2[SUB:, LANES:]
        if has_sink:
            l0 = l0 + jnp.exp2(sink(hd0) - m0)
            l1 = l1 + jnp.exp2(sink(hd1) - m1)
        out = jnp.where(low, o2[:SUB, :LANES], o2[SUB:, :LANES]) / jnp.where(low, l0, l1)
        o_ref[a:a + SUB, slab(cls * n_pairs * LANES, pair)] = out.astype(o_ref.dtype)
        if emit_ml:
            if pair == 0:
                ml = jnp.where(lane < ML_L_OFFSET, m0, 0.0) if bounded else jnp.zeros((SUB, LANES), F32)
            if not bounded:
                ml = jnp.where(lane == hd0, m0, ml)
                ml = jnp.where(lane == hd1, m1, ml)
            ml = jnp.where(lane == ML_L_OFFSET + hd0, l0, ml)
            ml = jnp.where(lane == ML_L_OFFSET + hd1, l1, ml)
            if pair == n_pairs - 1:
                ml_ref[a:a + SUB, slab(0, cls)] = ml


def _banded(qkv, bias, scal, *, batch, seq, dilation, radius, layout, head_ids, has_sink, emit_ml, bounded, out_dtype):
    n = seq // dilation
    tq = min(TQ_BAND, n)
    n_tiles = n // tq
    group = min(dilation, TQ_BAND // tq)
    class_w = layout[0]
    qw = len(head_ids) * LANES
    n_sub_tiles = (tq // SUB) * group * len(head_ids)
    per_tile = tq // radius
    main = lambda w: pl.BlockSpec((None, tq, group * w), lambda b, c, i: (b, i, c))
    halo = lambda pick: pl.BlockSpec((None, radius, group * class_w), lambda b, c, i: (b, pick(i), c))
    lo = halo(lambda i: jnp.maximum(i * per_tile - 1, 0))
    hi = halo(lambda i: jnp.minimum((i + 1) * per_tile, n // radius - 1))
    qkv = qkv.reshape(batch, n, dilation * class_w)
    out_specs = [main(qw)]
    out_shape = [jax.ShapeDtypeStruct((batch, n, dilation * qw), out_dtype)]
    if emit_ml:
        out_specs.append(main(LANES))
        out_shape.append(jax.ShapeDtypeStruct((batch, n, dilation * LANES), F32))
    outs = pl.pallas_call(
        functools.partial(_banded_kernel, tq=tq, radius=radius, head_ids=head_ids, layout=layout, group=group,
                          has_sink=has_sink, emit_ml=emit_ml, bounded=bounded, n_tiles=n_tiles),
        grid=(batch, dilation // group, n_tiles),
        in_specs=[pl.BlockSpec(memory_space=pltpu.SMEM), lo, main(class_w), hi, _const_spec(bias.shape)],
        out_specs=out_specs,
        out_shape=out_shape,
        scratch_shapes=([] if bounded else [pltpu.VMEM((n_sub_tiles, 2 * SUB, SUB + 2 * radius), F32),
                                            pltpu.VMEM((n_sub_tiles, 2 * SUB, LANES), F32)])
        + [pltpu.VMEM((n_sub_tiles, 2 * SUB, SUB + 2 * radius), BF16)],
        compiler_params=_cparams(("parallel", "parallel", "parallel")),
        name="banded%s_r%d_d%d" % ("_bounded" if bounded else "", radius, dilation),
    )(scal, qkv, qkv, qkv, bias)
    return [o.reshape(batch * n, -1) for o in outs]


def _latent_kernel(q_ref, k_ref, vt_ref, o_ref, s_ref, cmax_ref, m_ref, acc_ref, *, tk):
    tq = q_ref.shape[0]
    nk = k_ref.shape[0] // tk
    ahead = LAT_SLOTS - 1
    qt = q_ref[...].astype(F32).T.astype(BF16)
    m_ref[...] = jnp.full(m_ref.shape, -jnp.inf, F32)
    acc_ref[...] = jnp.zeros(acc_ref.shape, F32)

    def scores(slot, j):
        start = pl.multiple_of(j * tk, tk)
        s = jnp.dot(k_ref[pl.ds(start, tk), :], qt, preferred_element_type=F32)
        s_ref[slot] = s
        cmax_ref[slot] = jnp.max(s, axis=0, keepdims=True)

    def consume(slot, j):
        m_old = m_ref[...]
        m_new = jnp.maximum(m_old, cmax_ref[slot])
        pr = jnp.exp2(s_ref[slot] - m_new).astype(BF16)
        acc_ref[...] = jnp.exp2(m_old - m_new) * acc_ref[...] + jnp.dot(
            vt_ref[j], pr, preferred_element_type=F32)
        m_ref[...] = m_new

    def step(j, u, with_scores):
        if with_scores:
            scores((u + ahead) % LAT_SLOTS, j + ahead)
        consume(u % LAT_SLOTS, j)

    for j in range(ahead):
        scores(j, j)
    trips = (nk - ahead) // LAT_UNROLL

    def body(jj, carry):
        for u in range(LAT_UNROLL):
            step(LAT_UNROLL * jj + u, u, True)
        return carry

    lax.fori_loop(0, trips, body, 0)
    for j in range(trips * LAT_UNROLL, nk):
        step(j, j, j + ahead < nk)
    acc = acc_ref[...]
    o = jnp.concatenate([acc[:C_VDIM] / acc[C_VDIM:C_VDIM + 1], jnp.zeros((LANES - C_VDIM, tq), F32)], axis=0)
    o_ref[...] = o.T.astype(o_ref.dtype)


def _latent_bounded_kernel(q_ref, k_ref, vt_ref, o_ref, acc_ref, *, tk):
    tq = q_ref.shape[0]
    nk = k_ref.shape[0] // tk
    qt = q_ref[...].astype(F32).T.astype(BF16)
    acc_ref[...] = jnp.zeros(acc_ref.shape, F32)

    def body(jj, carry):
        for u in range(LAT_BOUNDED_UNROLL):
            j = LAT_BOUNDED_UNROLL * jj + u
            start = pl.multiple_of(j * tk, tk)
            s = jnp.dot(k_ref[pl.ds(start, tk), :], qt, preferred_element_type=F32)
            acc_ref[...] += jnp.dot(vt_ref[j], jnp.exp2(s).astype(BF16), preferred_element_type=F32)
        return carry

    lax.fori_loop(0, nk // LAT_BOUNDED_UNROLL, body, 0)
    acc = acc_ref[...]
    o = jnp.concatenate([acc[:C_VDIM] / acc[C_VDIM:C_VDIM + 1], jnp.zeros((LANES - C_VDIM, tq), F32)], axis=0)
    o_ref[...] = o.T.astype(o_ref.dtype)


def _latent(qc, kc, vt, batch, seq, bounded):
    t = batch * seq
    tq, tk = TQ_LAT, vt.shape[2]
    nq = seq // tq
    if bounded:
        body = functools.partial(_latent_bounded_kernel, tk=tk)
        scratch = [pltpu.VMEM((VT_ROWS, tq), F32)]
    else:
        body = functools.partial(_latent_kernel, tk=tk)
        scratch = [pltpu.VMEM((LAT_SLOTS, tk, tq), F32), pltpu.VMEM((LAT_SLOTS, 1, tq), F32),
                   pltpu.VMEM((1, tq), F32), pltpu.VMEM((VT_ROWS, tq), F32)]
    return pl.pallas_call(
        body,
        grid=(batch, C_HEADS, nq),
        in_specs=[pl.BlockSpec((tq, LANES), lambda b, h, i: (b * nq + i, h)),
                  pl.BlockSpec((seq, LANES), lambda b, h, i: (b, h)),
                  pl.BlockSpec((seq // tk, VT_ROWS, tk), lambda b, h, i: (b, h, 0))],
        out_specs=pl.BlockSpec((tq, LANES), lambda b, h, i: (b * nq + i, h)),
        out_shape=jax.ShapeDtypeStruct((t, C_SLOTS), BF16),
        scratch_shapes=scratch,
        compiler_params=_cparams(("parallel", "parallel", "parallel")),
        name="latent_bounded" if bounded else "latent",
    )(qc, kc, vt)


def _merge_mlp_kernel(x_ref, oa1_ref, oa2_ref, oa3_ref, ml1_ref, ml2_ref, ml3_ref, ob_ref, oc_ref,
                      expand_ref, gout_ref, wout_ref, gmlp_ref, wup_ref, wdown_ref, out_ref, oa_s, ml_s):
    tm = x_ref.shape[0]
    oas, mls = [oa1_ref[...]], [ml1_ref[...]]
    for idx, (o_ref, l_ref) in enumerate(((oa2_ref, ml2_ref), (oa3_ref, ml3_ref))):
        r = A_CONFIGS[idx + 1][1]
        for c in range(r):
            for s in range(A_W // LANES):
                col = c * A_W + s * LANES
                oa_s[idx, s, pl.ds(c, tm // r, stride=r), :] = o_ref[:, col:col + LANES]
            ml_s[idx, pl.ds(c, tm // r, stride=r), :] = l_ref[:, c * LANES:(c + 1) * LANES]
        oas.append(jnp.concatenate([oa_s[idx, s] for s in range(A_W // LANES)], axis=1))
        mls.append(ml_s[idx])
    lane = lax.broadcasted_iota(jnp.int32, mls[0].shape, 1)
    m_all = jnp.maximum(jnp.maximum(mls[0], mls[1]), mls[2])
    ws = [pltpu.roll(ml, LANES - ML_L_OFFSET, 1) * jnp.exp2(ml - m_all) for ml in mls]
    wsum = ws[0] + ws[1] + ws[2]
    packed = jnp.zeros_like(wsum)
    for c, w in enumerate(ws):
        wn = jnp.where(lane < A_HEADS, w / wsum, 0.0)
        packed = packed + (wn if c == 0 else pltpu.roll(wn, ML_L_OFFSET * c, 1))
    hi = packed.astype(BF16)
    lo = (packed - hi.astype(F32)).astype(BF16)
    spread = (jnp.dot(hi, expand_ref[...], preferred_element_type=F32)
              + jnp.dot(lo, expand_ref[...], preferred_element_type=F32))
    oa = spread[:, :A_W] * oas[0] + spread[:, A_W:2 * A_W] * oas[1] + spread[:, 2 * A_W:] * oas[2]

    def group_norm(v, width):
        return v * lax.rsqrt(jnp.sum(v * v, axis=-1, keepdims=True) * (1.0 / width) + EPS)

    ob = ob_ref[...].astype(F32)
    oc_wide = oc_ref[...].astype(F32)
    half = lax.broadcasted_iota(jnp.int32, (tm, LANES), 1) < C_VDIM
    oc = jnp.concatenate(
        [jnp.where(half, oc_wide[:, 2 * p * LANES:(2 * p + 1) * LANES],
                   pltpu.roll(oc_wide[:, (2 * p + 1) * LANES:(2 * p + 2) * LANES], C_VDIM, 1))
         for p in range(C_HEADS // 2)], axis=1)
    mixed = jnp.concatenate([group_norm(oa, A_W), group_norm(ob, B_QW), group_norm(oc, C_HEADS * C_VDIM)], axis=1)
    mixed = (mixed * gout_ref[...]).astype(BF16)
    x = x_ref[...] + jnp.dot(mixed, wout_ref[...], preferred_element_type=F32)

    h = (x * lax.rsqrt(jnp.mean(x * x, axis=-1, keepdims=True) + EPS) * gmlp_ref[...]).astype(BF16)
    acc = x
    for s in range(0, D_FF, FF_CHUNK):
        u = jnp.dot(h, wup_ref[:, s:s + FF_CHUNK], preferred_element_type=F32)
        u = jnp.square(jnp.maximum(u, 0.0)).astype(BF16)
        acc = acc + jnp.dot(u, wdown_ref[s:s + FF_CHUNK, :], preferred_element_type=F32)
    out_ref[...] = acc


def _merge_mlp(x2d, oas, mls, ob, oc, p):
    t = x2d.shape[0]
    tm = TM_MLP
    row = lambda w: pl.BlockSpec((tm, w), lambda i: (i, 0))
    strided = lambda w, r: pl.BlockSpec((tm // r, r * w), lambda i: (i, 0))
    single = lambda shape: pl.BlockSpec(shape, lambda i: (0,) * len(shape), pipeline_mode=pl.Buffered(1))
    return pl.pallas_call(
        _merge_mlp_kernel,
        grid=(t // tm,),
        in_specs=[row(D_MODEL)] + [strided(A_W, r) for _, r in A_CONFIGS]
        + [strided(LANES, r) for _, r in A_CONFIGS] + [row(B_QW), row(C_SLOTS),
                  single((LANES, 3 * A_W)), single((1, MIX_COLS)), single((MIX_COLS, D_MODEL)),
                  single((1, D_MODEL)), single((D_MODEL, D_FF)), single((D_FF, D_MODEL))],
        out_specs=row(D_MODEL),
        out_shape=jax.ShapeDtypeStruct((t, D_MODEL), F32),
        scratch_shapes=[pltpu.VMEM((2, A_W // LANES, tm, LANES), F32), pltpu.VMEM((2, tm, LANES), F32)],
        compiler_params=_cparams(("parallel",)),
        name="merge_mlp",
    )(x2d, *oas, *mls, ob, oc, p["expand"], p["gout"], p["w_out"], p["gmlp"], p["w_up"], p["w_down"])


def _block_diag_ones(group):
    idx = np.arange(2 * LANES) // group
    return jnp.asarray((idx[:, None] == idx[None, :]).astype(np.float32), dtype=BF16)


def _expand_matrix():
    e = np.zeros((LANES, len(A_CONFIGS) * A_W), np.float32)
    for c in range(len(A_CONFIGS)):
        for h in range(A_HEADS):
            e[ML_L_OFFSET * c + h, c * A_W + h * HEAD_DIM:c * A_W + (h + 1) * HEAD_DIM] = 1.0
    return jnp.asarray(e, dtype=BF16)


def _pad_heads(w, heads, used, lo=0):
    lead = w.shape[:-1]
    w = w.reshape(lead + (heads, used))
    pad = [(0, 0)] * len(lead) + [(0, 0), (lo, LANES - lo - used)]
    return jnp.pad(w, pad).reshape(lead + (heads * LANES,))


def _layer_params(i, norm_mix, w_in, qk_gain_a, qk_gain_b, q_lat_gain, kv_lat_gain, w_uq, w_ukv, qk_gain_c,
                  out_norm, w_out, norm_mlp, w_up, w_down):
    w = w_in[i]
    o = np.cumsum((A_W, A_W, A_W, B_QW, B_KVW, B_KVW, C_Q_RANK, C_KV_RANK)).tolist()
    qb = w[:, o[2]:o[3]].reshape(D_MODEL, B_HEADS, HEAD_DIM)[:, B_HEAD_ORDER, :].reshape(D_MODEL, B_QW)
    kr = _pad_heads(w[:, o[7]:], 1, C_ROPE, lo=C_NOPE)
    w_in_p = jnp.concatenate([w[:, :o[2]], qb, w[:, o[3]:o[7]], kr], axis=1).astype(BF16)

    scale = HEAD_DIM ** -0.5 * LOG2E
    gqk = jnp.concatenate([jnp.tile(qk_gain_a[i, 0], A_HEADS) * scale, jnp.tile(qk_gain_a[i, 1], A_HEADS),
                           jnp.tile(qk_gain_b[i, 0], B_HEADS) * scale, jnp.tile(qk_gain_b[i, 1], B_KV_HEADS)])
    ukv = w_ukv[i].reshape(C_KV_RANK, C_HEADS, C_NOPE + C_VDIM)
    wuk = _pad_heads(ukv[:, :, :C_NOPE].reshape(C_KV_RANK, -1), C_HEADS, C_NOPE)
    wuvt = ukv[:, :, C_NOPE:].reshape(C_KV_RANK, C_HEADS * C_VDIM).T
    g = out_norm[i]
    gb = g[A_W:A_W + B_QW].reshape(B_HEADS, HEAD_DIM)[B_HEAD_ORDER, :].reshape(B_QW)
    gout = jnp.concatenate([g[:A_W], gb, g[A_W + B_QW:]])
    wo = w_out[i]
    wob = wo[A_W:A_W + B_QW].reshape(B_HEADS, HEAD_DIM, D_MODEL)[B_HEAD_ORDER, :, :].reshape(B_QW, D_MODEL)
    w_out_p = jnp.concatenate([wo[:A_W], wob, wo[A_W + B_QW:]], axis=0).astype(BF16)
    bound = (C_QK ** 0.5 * LOG2E * BOUND_MARGIN) * jnp.max(jnp.abs(qk_gain_c[i, 0])) * jnp.max(jnp.abs(qk_gain_c[i, 1]))
    return {
        "gmix": norm_mix[i][None, :],
        "w_in": w_in_p,
        "gqk": gqk[None, :],
        "ones64": _block_diag_ones(HEAD_DIM),
        "ones128": _block_diag_ones(LANES),
        "glq": q_lat_gain[i][None, :],
        "wuq": _pad_heads(w_uq[i], C_HEADS, C_QK).astype(BF16),
        "gqc": jnp.tile(_pad_heads(qk_gain_c[i, 0] * (C_QK ** -0.5 * LOG2E), 1, C_QK), C_HEADS)[None, :],
        "glkv": kv_lat_gain[i][None, :],
        "wuk": wuk.astype(BF16),
        "wuvt": wuvt.astype(BF16),
        "gkc": jnp.tile(_pad_heads(qk_gain_c[i, 1], 1, C_QK), C_HEADS)[None, :],
        "qshift": jnp.tile(_pad_heads(-bound[None], 1, 1, lo=C_QK), C_HEADS)[None, :],
        "kone": jnp.tile(_pad_heads(jnp.ones((1,), F32), 1, 1, lo=C_QK), C_HEADS)[None, :],
        "logit_bound": bound,
        "expand": _expand_matrix(),
        "gout": gout[None, :],
        "w_out": w_out_p,
        "gmlp": norm_mlp[i][None, :],
        "w_up": w_up[i].astype(BF16),
        "w_down": w_down[i].astype(BF16),
    }


def kernel(x, positions, rel_bias_table, norm_mix, w_in, qk_gain_a, qk_gain_b, sink_b, q_lat_gain, kv_lat_gain,
           w_uq, w_ukv, qk_gain_c, out_norm, w_out, norm_mlp, w_up, w_down):
    batch, seq, _ = x.shape
    depth = w_in.shape[0]
    x2d = x.reshape(batch * seq, D_MODEL)
    cos_t, sin_t = _rope_tables(positions)
    a_pairs = tuple((2 * p, 2 * p + 1) for p in range(A_HEADS // 2))
    b_pairs = (B_HEAD_ORDER[:2], B_HEAD_ORDER[2:])
    b_cols = tuple(A_HEADS + h for h in B_HEAD_ORDER)

    def bias_tables(shift):
        tiles_a = [_bias_tiles(rel_bias_table, shift, window // (2 * r), r, tuple(range(A_HEADS)))
                   for window, r in A_CONFIGS]
        return tiles_a, _bias_tiles(rel_bias_table, shift, B_RADIUS, 1, b_cols)

    gain_bound = lambda gains: jnp.max(jnp.max(jnp.abs(gains[:, 0]), axis=-1) * jnp.max(jnp.abs(gains[:, 1]), axis=-1))
    qk_bound = HEAD_DIM ** 0.5 * LOG2E * BOUND_MARGIN * jnp.maximum(gain_bound(qk_gain_a), gain_bound(qk_gain_b))
    band_shift = jnp.maximum(qk_bound + LOG2E * jnp.max(jnp.abs(rel_bias_table)), LOG2E * jnp.max(sink_b))
    shifted_tables = bias_tables(band_shift[None])

    a_layout = (3 * A_W, 0, A_W, 2 * A_W, (0, 1, 2))
    b_layout = (B_QW + 2 * B_KVW, 0, B_QW, B_QW + B_KVW, (0, 0))

    def banded_all(bounded, qkv1, qkv4, qkv16, qkvb, sinks):
        shift = band_shift[None] if bounded else jnp.zeros((1,), F32)
        tables_a, table_b = shifted_tables if bounded else bias_tables(shift)
        scal = jnp.concatenate([shift, sinks * LOG2E])
        outs = []
        for (window, r), bias, qkv in zip(A_CONFIGS, tables_a, (qkv1, qkv4, qkv16)):
            outs += _banded(qkv, bias, scal, batch=batch, seq=seq, dilation=r, radius=window // (2 * r),
                            layout=a_layout, head_ids=a_pairs, has_sink=False, emit_ml=True, bounded=bounded,
                            out_dtype=F32)
        outs += _banded(qkvb, table_b, scal, batch=batch, seq=seq, dilation=1, radius=B_RADIUS, layout=b_layout,
                        head_ids=b_pairs, has_sink=True, emit_ml=False, bounded=bounded, out_dtype=BF16)
        return tuple(outs)

    for i in range(depth):
        p = _layer_params(i, norm_mix, w_in, qk_gain_a, qk_gain_b, q_lat_gain, kv_lat_gain, w_uq, w_ukv,
                          qk_gain_c, out_norm, w_out, norm_mlp, w_up, w_down)
        qkv1, qkv4, qkv16, qkvb, qc, kc, vt = _inproj(x2d, cos_t, sin_t, p)
        o1, ml1, o4, ml4, o16, ml16, ob = lax.cond(
            band_shift <= MAX_LOGIT_BOUND, functools.partial(banded_all, True), functools.partial(banded_all, False),
            qkv1, qkv4, qkv16, qkvb, sink_b[i])
        oas, mls = [o1, o4, o16], [ml1, ml4, ml16]
        oc = lax.cond(p["logit_bound"] <= MAX_LOGIT_BOUND,
                      functools.partial(_latent, batch=batch, seq=seq, bounded=True),
                      functools.partial(_latent, batch=batch, seq=seq, bounded=False), qc, kc, vt)
        x2d = _merge_mlp(x2d, oas, mls, ob, oc, p)
    return x2d.reshape(batch, seq, D_MODEL)
```

```python
import functools
import math

import numpy as np
import jax
import jax.numpy as jnp
from jax import lax
from jax.experimental import pallas as pl
from jax.experimental.pallas import tpu as pltpu

F32 = jnp.float32
BF16 = jnp.bfloat16

D_MODEL = 1024
HEAD_DIM = 64
A_HEADS = 6
A_CONFIGS = ((128, 1), (512, 4), (2048, 16))
B_HEADS = 4
B_KV_HEADS = 2
B_RADIUS = 128
C_HEADS = 6
C_NOPE = 64
C_ROPE = 32
C_VDIM = 64
C_QK = C_NOPE + C_ROPE
C_Q_RANK = 256
C_KV_RANK = 128
ROPE_THETA = 10000.0
N_BUCKETS = 32
MAX_DISTANCE = 1024
D_FF = 4 * D_MODEL
EPS = 1e-6
NEG = -1e30

A_W = A_HEADS * HEAD_DIM
B_QW = B_HEADS * HEAD_DIM
B_KVW = B_KV_HEADS * HEAD_DIM
LANES = 128
C_SLOTS = C_HEADS * LANES
IN_COLS = 3 * A_W + B_QW + 2 * B_KVW + C_Q_RANK + C_KV_RANK + LANES
MIX_COLS = A_W + B_QW + C_HEADS * C_VDIM
VT_ROWS = 80
LOG2E = math.log2(math.e)
ML_L_OFFSET = 8

B_HEAD_ORDER = (0, 2, 1, 3)

TM_IN = 512
TQ_BAND = 2048
SUB = 128
BAND_RING = 3
TQ_LAT = 2048
BOUND_MARGIN = 1.02
MAX_LOGIT_BOUND = 50.0
LAT_BOUNDED_UNROLL = 16
LAT_SLOTS = 3
LAT_UNROLL = 6
TM_MLP = 512
FF_CHUNK = 1024
TM_ROPE = 2048
VMEM_LIMIT = 56 * 1024 * 1024


def _cparams(sem):
    return pltpu.CompilerParams(dimension_semantics=sem, vmem_limit_bytes=VMEM_LIMIT)


def _const_spec(shape):
    zeros = (0,) * len(shape)
    return pl.BlockSpec(shape, lambda *_: zeros)


def _rope_table_kernel(pos_ref, inv_ref, cos_ref, sin_ref):
    ang = pos_ref[...] * inv_ref[...]
    lane = lax.broadcasted_iota(jnp.int32, ang.shape, 1)
    c = jnp.cos(ang)
    s = jnp.sin(ang)
    s = jnp.where(lane % C_ROPE < C_ROPE // 2, -s, s)
    live = (lane >= C_NOPE) & (lane < C_QK)
    per_row = LANES // C_ROPE
    for a in range(per_row):
        shift = (C_NOPE - a * C_ROPE) % LANES
        rows = pl.ds(a, ang.shape[0], stride=per_row)
        cos_ref[rows, :] = jnp.where(live, c if shift == 0 else pltpu.roll(c, shift, 1), 1.0)
        sin_ref[rows, :] = jnp.where(live, s if shift == 0 else pltpu.roll(s, shift, 1), 0.0)


def _rope_tables(positions):
    t = positions.size
    half = C_ROPE // 2
    per_row = LANES // C_ROPE
    inv = ROPE_THETA ** (-jnp.arange(half, dtype=F32) / half)
    inv_row = jnp.tile(inv, 2 * per_row)[None, :]
    pos = jnp.repeat(positions.astype(F32).reshape(t // per_row, per_row), C_ROPE, axis=1)
    tm = TM_ROPE
    return pl.pallas_call(
        _rope_table_kernel,
        grid=(t // tm,),
        in_specs=[pl.BlockSpec((tm // per_row, LANES), lambda i: (i, 0)), _const_spec((1, LANES))],
        out_specs=[pl.BlockSpec((tm, LANES), lambda i: (i, 0))] * 2,
        out_shape=[jax.ShapeDtypeStruct((t, LANES), F32)] * 2,
        compiler_params=_cparams(("parallel",)),
        name="rope_tables",
    )(pos, inv_row)


def _bucket_thresholds():
    half = N_BUCKETS // 2
    exact = half // 2
    n = np.arange(1, 2 * MAX_DISTANCE + 2, dtype=np.float64)
    far = exact + (np.log(n / exact) / math.log(MAX_DISTANCE / exact) * (half - exact)).astype(np.int64)
    far = np.minimum(far, half - 1)
    return tuple(int(n[np.argmax(far >= exact + k)]) for k in range(1, half - exact))


def _bias_kernel(table_ref, shift_ref, out_ref, *, radius, dilation, head_cols):
    hsel = pl.program_id(0)
    width = SUB + 2 * radius
    row = lax.broadcasted_iota(jnp.int32, (SUB, width), 0)
    col = lax.broadcasted_iota(jnp.int32, (SUB, width), 1)
    rel = col - radius - row
    n = jnp.abs(rel) * dilation
    half = N_BUCKETS // 2
    exact = half // 2
    far = jnp.full(n.shape, exact, jnp.int32)
    for thr in _bucket_thresholds():
        far = far + (n >= thr).astype(jnp.int32)
    bucket = jnp.where(rel > 0, half, 0) + jnp.where(n < exact, n, far)
    for idx, hc in enumerate(head_cols):
        @pl.when(hsel == idx)
        def _(hc=hc):
            val = jnp.zeros(n.shape, F32)
            for b in range(N_BUCKETS):
                val = jnp.where(bucket == b, table_ref[b, hc], val)
            out_ref[...] = jnp.where(jnp.abs(rel) <= radius, val * LOG2E - shift_ref[0], NEG)


def _bias_tiles(table, shift, radius, dilation, head_cols):
    width = SUB + 2 * radius
    tiles = pl.pallas_call(
        functools.partial(_bias_kernel, radius=radius, dilation=dilation, head_cols=head_cols),
        grid=(len(head_cols),),
        in_specs=[pl.BlockSpec(memory_space=pltpu.SMEM), pl.BlockSpec(memory_space=pltpu.SMEM)],
        out_specs=pl.BlockSpec((None, SUB, width), lambda h: (h, 0, 0)),
        out_shape=jax.ShapeDtypeStruct((len(head_cols), SUB, width), F32),
        compiler_params=_cparams(("arbitrary",)),
        name="bias_tiles",
    )(table, shift)
    return tiles.reshape(len(head_cols) // 2, 2 * SUB, width)


def _group_mean_sq(y, ones_ref, group):
    sq = (y * y).astype(BF16)
    width = y.shape[1]
    parts = []
    for s in range(0, width, 2 * LANES):
        w = min(2 * LANES, width - s)
        parts.append(jnp.dot(sq[:, s:s + w], ones_ref[:w, :w], preferred_element_type=F32))
    out = parts[0] if len(parts) == 1 else jnp.concatenate(parts, axis=1)
    return out * (1.0 / group)


def _slab_roll(y, shift):
    parts = [pltpu.roll(y[:, s:s + LANES], shift, 1) for s in range(0, y.shape[1], LANES)]
    return parts[0] if len(parts) == 1 else jnp.concatenate(parts, axis=1)


def _rope(y, cos_t, sin_t, lane):
    swapped = jnp.where(lane < C_NOPE + C_ROPE // 2, _slab_roll(y, LANES - C_ROPE // 2), _slab_roll(y, C_ROPE // 2))
    return y * cos_t + swapped * sin_t


def _inproj_kernel(x_ref, cos_ref, sin_ref, gmix_ref, w_ref, gqk_ref, ones64_ref, ones128_ref,
                   glq_ref, wuq_ref, gqc_ref, glkv_ref, wuk_ref, wuvt_ref, gkc_ref, gkr_ref, qshift_ref, kone_ref,
                   qkv1_ref, qkv4_ref, qkv16_ref, qkvb_ref, qc_ref, kc_ref, vt_ref, stage_ref, stage4_ref):
    x = x_ref[...]
    h = x * lax.rsqrt(jnp.mean(x * x, axis=-1, keepdims=True) + EPS) * gmix_ref[...]
    y = jnp.dot(h.astype(BF16), w_ref[...], preferred_element_type=F32)

    o_qb = 3 * A_W
    o_kb = o_qb + B_QW
    o_vb = o_kb + B_KVW
    o_cq = o_vb + B_KVW
    o_ckv = o_cq + C_Q_RANK
    o_kr = o_ckv + C_KV_RANK
    yn = jnp.concatenate([y[:, :2 * A_W], y[:, o_qb:o_vb]], axis=1)
    yn = yn * lax.rsqrt(_group_mean_sq(yn, ones64_ref, HEAD_DIM) + EPS) * gqk_ref[...]
    qkvb_ref[:, :B_QW + B_KVW] = yn[:, 2 * A_W:].astype(BF16)
    n_slabs, tm = stage_ref.shape[0], stage_ref.shape[1]
    for s in range(n_slabs):
        src = yn if s * LANES < 2 * A_W else y
        stage_ref[s] = src[:, s * LANES:(s + 1) * LANES]
    for s in range(n_slabs):
        qkv1_ref[:, s * LANES:(s + 1) * LANES] = stage_ref[s].astype(BF16)
        for c4 in range(4):
            rows = stage_ref[s, pl.ds(c4, tm // 4, stride=4), :]
            qkv4_ref[:, c4 * 3 * A_W + s * LANES:c4 * 3 * A_W + (s + 1) * LANES] = rows.astype(BF16)
            stage4_ref[c4 * n_slabs + s] = rows
    for c16 in range(16):
        for s in range(n_slabs):
            rows = stage4_ref[(c16 % 4) * n_slabs + s, pl.ds(c16 // 4, tm // 16, stride=4), :]
            qkv16_ref[:, c16 * 3 * A_W + s * LANES:c16 * 3 * A_W + (s + 1) * LANES] = rows.astype(BF16)
    qkvb_ref[:, B_QW + B_KVW:] = y[:, o_vb:o_cq].astype(BF16)

    cos_t = jnp.concatenate([cos_ref[...]] * C_HEADS, axis=1)
    sin_t = jnp.concatenate([sin_ref[...]] * C_HEADS, axis=1)
    lane = lax.broadcasted_iota(jnp.int32, cos_t.shape, 1) % LANES
    cq = y[:, o_cq:o_ckv]
    cq = cq * lax.rsqrt(jnp.mean(cq * cq, axis=-1, keepdims=True) + EPS) * glq_ref[...]
    qc = jnp.dot(cq.astype(BF16), wuq_ref[...], preferred_element_type=F32)
    qc = qc * lax.rsqrt(_group_mean_sq(qc, ones128_ref, C_QK) + EPS) * gqc_ref[...]
    qc_ref[...] = (_rope(qc, cos_t, sin_t, lane) + qshift_ref[...]).astype(BF16)

    ckv = y[:, o_ckv:o_kr]
    ckv = (ckv * lax.rsqrt(jnp.mean(ckv * ckv, axis=-1, keepdims=True) + EPS) * glkv_ref[...]).astype(BF16)
    kr = y[:, o_kr:]
    kn = jnp.dot(ckv, wuk_ref[...], preferred_element_type=F32)
    inv_rms = lax.rsqrt(_group_mean_sq(kn + jnp.concatenate([kr] * C_HEADS, axis=1), ones128_ref, C_QK) + EPS)
    lane_slab = lax.broadcasted_iota(jnp.int32, kr.shape, 1)
    kr_rot = _rope(kr * gkr_ref[...], cos_ref[...], sin_ref[...], lane_slab)
    kc = (kn * gkc_ref[...] + jnp.concatenate([kr_rot] * C_HEADS, axis=1)) * inv_rms
    kc_ref[...] = (kc + kone_ref[...]).astype(BF16)
    vt = lax.dot_general(wuvt_ref[...], ckv, (((1,), (1,)), ((), ())), preferred_element_type=F32)
    row = lax.broadcasted_iota(jnp.int32, (VT_ROWS - C_VDIM, vt.shape[1]), 0)
    ones_rows = jnp.where(row == 0, 1.0, 0.0).astype(BF16)
    for hd in range(C_HEADS):
        vt_ref[0, hd * VT_ROWS:hd * VT_ROWS + C_VDIM, :] = vt[hd * C_VDIM:(hd + 1) * C_VDIM].astype(BF16)
        vt_ref[0, hd * VT_ROWS + C_VDIM:(hd + 1) * VT_ROWS, :] = ones_rows


def _inproj(x2d, cos_t, sin_t, p):
    t = x2d.shape[0]
    tm = TM_IN
    row = lambda w: pl.BlockSpec((tm, w), lambda i: (i, 0))
    outs = [(B_QW + 2 * B_KVW, BF16)] + [(C_SLOTS, BF16)] * 2
    a_specs = [pl.BlockSpec((tm // r, r * 3 * A_W), lambda i: (i, 0)) for _, r in A_CONFIGS]
    a_shapes = [jax.ShapeDtypeStruct((t // r, r * 3 * A_W), BF16) for _, r in A_CONFIGS]
    return pl.pallas_call(
        _inproj_kernel,
        grid=(t // tm,),
        in_specs=[row(D_MODEL), row(LANES), row(LANES),
                  _const_spec((1, D_MODEL)), _const_spec((D_MODEL, IN_COLS)),
                  _const_spec((1, 2 * A_W + B_QW + B_KVW)),
                  _const_spec((2 * LANES, 2 * LANES)), _const_spec((2 * LANES, 2 * LANES)),
                  _const_spec((1, C_Q_RANK)), _const_spec((C_Q_RANK, C_SLOTS)), _const_spec((1, C_SLOTS)),
                  _const_spec((1, C_KV_RANK)), _const_spec((C_KV_RANK, C_SLOTS)),
                  _const_spec((C_HEADS * C_VDIM, C_KV_RANK)), _const_spec((1, C_SLOTS)), _const_spec((1, LANES)),
                  _const_spec((1, C_SLOTS)), _const_spec((1, C_SLOTS))],
        out_specs=a_specs + [row(w) for w, _ in outs]
        + [pl.BlockSpec((1, C_HEADS * VT_ROWS, tm), lambda i: (i, 0, 0))],
        out_shape=a_shapes + [jax.ShapeDtypeStruct((t, w), d) for w, d in outs]
        + [jax.ShapeDtypeStruct((t // tm, C_HEADS * VT_ROWS, tm), BF16)],
        scratch_shapes=[pltpu.VMEM((3 * A_W // LANES, tm, LANES), F32),
                        pltpu.VMEM((4 * 3 * A_W // LANES, tm // 4, LANES), F32)],
        compiler_params=_cparams(("parallel",)),
        name="inproj",
    )(x2d, cos_t, sin_t, p["gmix"], p["w_in"], p["gqk"], p["ones64"], p["ones128"],
      p["glq"], p["wuq"], p["gqc"], p["glkv"], p["wuk"], p["wuvt"], p["gkc"], p["gkc"][:, :LANES], p["qshift"], p["kone"])


def _window(lo_ref, main_ref, hi_ref, a, radius, tq, cols):
    start, end = a - radius, a + SUB + radius
    parts = []
    if start < 0:
        parts.append(lo_ref[radius + start:radius, cols])
        start = 0
    parts.append(main_ref[start:min(end, tq), cols])
    if end > tq:
        parts.append(hi_ref[0:end - tq, cols])
    return parts[0] if len(parts) == 1 else jnp.concatenate(parts, axis=0)


def _banded_kernel(scal_ref, lo_ref, main_ref, hi_ref, bias_ref, *refs, tq, radius, head_ids, layout, group,
                   has_sink, emit_ml, bounded, n_tiles):
    class_w, q_off, k_off, v_off, kv_slab = layout
    n_pairs = len(head_ids)
    o_ref = refs[0]
    ml_ref = refs[1] if emit_ml else None
    p_scr = refs[-1]
    if bounded:
        hbm, ring, sems = main_ref, refs[-3], refs[-2]
        n_c, n_t = pl.num_programs(1), pl.num_programs(2)
        step = (pl.program_id(0) * n_c + pl.program_id(1)) * n_t + pl.program_id(2)
        total = pl.num_programs(0) * n_c * n_t
        block_w = group * class_w

        def fetch(st):
            col0 = pl.multiple_of(((st // n_t) % n_c) * block_w, LANES)
            src = hbm.at[st // (n_c * n_t), pl.ds((st % n_t) * tq, tq), pl.ds(col0, block_w)]
            return pltpu.make_async_copy(src, ring.at[st % BAND_RING], sems.at[st % BAND_RING])

        @pl.when(step == 0)
        def _():
            for ahead in range(BAND_RING - 1):
                fetch(step + ahead).start()

        @pl.when(step + BAND_RING - 1 < total)
        def _():
            fetch(step + BAND_RING - 1).start()

        fetch(step).wait()
        main_ref = ring.at[step % BAND_RING]
    else:
        s_scr, m_scr = refs[-3:-1]
    sink = lambda hd: scal_ref[1 + hd]
    tile = pl.program_id(2)
    width = SUB + 2 * radius
    lane = lax.broadcasted_iota(jnp.int32, (SUB, LANES), 1)
    col = lax.broadcasted_iota(jnp.int32, (2 * SUB, width), 1)
    low = lane < HEAD_DIM
    slab = lambda base, idx: slice(base + idx * LANES, base + (idx + 1) * LANES)
    tiles = [(a, cls, pair) for a in range(0, tq, SUB) for cls in range(group) for pair in range(n_pairs)]

    for g, (a, cls, pair) in enumerate(tiles):
        qs = main_ref[a:a + SUB, slab(cls * class_w + q_off, pair)]
        ks = _window(lo_ref, main_ref, hi_ref, a, radius, tq, slab(cls * class_w + k_off, kv_slab[pair]))
        zero = jnp.zeros_like(qs)
        q2 = jnp.concatenate([jnp.where(low, qs, zero), jnp.where(low, zero, qs)], axis=0)
        s = lax.dot_general(q2, ks, (((1,), (1,)), ((), ())), preferred_element_type=F32)
        s = s + bias_ref[pair]
        if a == 0:
            s = jnp.where(col < jnp.where(tile == 0, radius, 0), NEG, s)
        if a == tq - SUB:
            s = jnp.where(col >= jnp.where(tile == n_tiles - 1, SUB + radius, width), NEG, s)
        if bounded:
            p_scr[g] = jnp.exp2(s).astype(BF16)
            continue
        m = jnp.max(s, axis=-1, keepdims=True)
        if has_sink:
            sinks = jnp.concatenate([jnp.full((SUB, 1), sink(hd), F32) for hd in head_ids[pair]], axis=0)
            m = jnp.maximum(m, sinks)
        s_scr[g] = s
        m_scr[g] = jnp.broadcast_to(m, (2 * SUB, LANES))

    for g in range(0 if bounded else len(tiles)):
        m_wide = jnp.concatenate([m_scr[g]] * (width // LANES), axis=1)
        p_scr[g] = jnp.exp2(s_scr[g] - m_wide).astype(BF16)

    ml = None
    for g, (a, cls, pair) in enumerate(tiles):
        hd0, hd1 = head_ids[pair]
        vs = _window(lo_ref, main_ref, hi_ref, a, radius, tq, slab(cls * class_w + v_off, kv_slab[pair]))
        o2 = jnp.dot(p_scr[g], jnp.concatenate([vs, jnp.ones_like(vs)], axis=1), preferred_element_type=F32)
        if bounded:
            m0 = m1 = jnp.full((SUB, LANES), scal_ref[0], F32)
        else:
            m0, m1 = m_scr[g][:SUB], m_scr[g][SUB:]
        l0, l1 = o2[:SUB, LANES:], o2[SUB:, LANES:]
        if has_sink:
            l0 = l0 + jnp.exp2(sink(hd0) - m0)
            l1 = l1 + jnp.exp2(sink(hd1) - m1)
        out = jnp.where(low, o2[:SUB, :LANES], o2[SUB:, :LANES]) / jnp.where(low, l0, l1)
        o_ref[a:a + SUB, slab(cls * n_pairs * LANES, pair)] = out.astype(o_ref.dtype)
        if emit_ml:
            if pair == 0:
                ml = jnp.where(lane < ML_L_OFFSET, m0, 0.0) if bounded else jnp.zeros((SUB, LANES), F32)
            if not bounded:
                ml = jnp.where(lane == hd0, m0, ml)
                ml = jnp.where(lane == hd1, m1, ml)
            ml = jnp.where(lane == ML_L_OFFSET + hd0, l0, ml)
            ml = jnp.where(lane == ML_L_OFFSET + hd1, l1, ml)
            if pair == n_pairs - 1:
                ml_ref[a:a + SUB, slab(0, cls)] = ml


def _banded(qkv, bias, scal, *, batch, seq, dilation, radius, layout, head_ids, has_sink, emit_ml, bounded, out_dtype):
    n = seq // dilation
    tq = min(TQ_BAND, n)
    n_tiles = n // tq
    group = min(dilation, TQ_BAND // tq)
    class_w = layout[0]
    qw = len(head_ids) * LANES
    n_sub_tiles = (tq // SUB) * group * len(head_ids)
    per_tile = tq // radius
    main = lambda w: pl.BlockSpec((None, tq, group * w), lambda b, c, i: (b, i, c))
    streamed = pl.BlockSpec(memory_space=pl.ANY) if bounded else main(class_w)
    halo = lambda pick: pl.BlockSpec((None, radius, group * class_w), lambda b, c, i: (b, pick(i), c))
    lo = halo(lambda i: jnp.maximum(i * per_tile - 1, 0))
    hi = halo(lambda i: jnp.minimum((i + 1) * per_tile, n // radius - 1))
    qkv = qkv.reshape(batch, n, dilation * class_w)
    out_specs = [main(qw)]
    out_shape = [jax.ShapeDtypeStruct((batch, n, dilation * qw), out_dtype)]
    if emit_ml:
        out_specs.append(main(LANES))
        out_shape.append(jax.ShapeDtypeStruct((batch, n, dilation * LANES), F32))
    outs = pl.pallas_call(
        functools.partial(_banded_kernel, tq=tq, radius=radius, head_ids=head_ids, layout=layout, group=group,
                          has_sink=has_sink, emit_ml=emit_ml, bounded=bounded, n_tiles=n_tiles),
        grid=(batch, dilation // group, n_tiles),
        in_specs=[pl.BlockSpec(memory_space=pltpu.SMEM), lo, streamed, hi, _const_spec(bias.shape)],
        out_specs=out_specs,
        out_shape=out_shape,
        scratch_shapes=([pltpu.VMEM((BAND_RING, tq, group * class_w), BF16), pltpu.SemaphoreType.DMA((BAND_RING,))]
                        if bounded else [pltpu.VMEM((n_sub_tiles, 2 * SUB, SUB + 2 * radius), F32),
                                         pltpu.VMEM((n_sub_tiles, 2 * SUB, LANES), F32)])
        + [pltpu.VMEM((n_sub_tiles, 2 * SUB, SUB + 2 * radius), BF16)],
        compiler_params=_cparams(("arbitrary", "arbitrary", "arbitrary")),
        name="banded%s_r%d_d%d" % ("_bounded" if bounded else "", radius, dilation),
    )(scal, qkv, qkv, qkv, bias)
    return [o.reshape(batch * n, -1) for o in outs]


def _latent_kernel(q_ref, k_ref, vt_ref, o_ref, s_ref, cmax_ref, m_ref, acc_ref, *, tk):
    tq = q_ref.shape[0]
    nk = k_ref.shape[0] // tk
    ahead = LAT_SLOTS - 1
    qt = q_ref[...].astype(F32).T.astype(BF16)
    m_ref[...] = jnp.full(m_ref.shape, -jnp.inf, F32)
    acc_ref[...] = jnp.zeros(acc_ref.shape, F32)

    def scores(slot, j):
        start = pl.multiple_of(j * tk, tk)
        s = jnp.dot(k_ref[pl.ds(start, tk), :], qt, preferred_element_type=F32)
        s_ref[slot] = s
        cmax_ref[slot] = jnp.max(s, axis=0, keepdims=True)

    def consume(slot, j):
        m_old = m_ref[...]
        m_new = jnp.maximum(m_old, cmax_ref[slot])
        pr = jnp.exp2(s_ref[slot] - m_new).astype(BF16)
        acc_ref[...] = jnp.exp2(m_old - m_new) * acc_ref[...] + jnp.dot(
            vt_ref[j], pr, preferred_element_type=F32)
        m_ref[...] = m_new

    def step(j, u, with_scores):
        if with_scores:
            scores((u + ahead) % LAT_SLOTS, j + ahead)
        consume(u % LAT_SLOTS, j)

    for j in range(ahead):
        scores(j, j)
    trips = (nk - ahead) // LAT_UNROLL

    def body(jj, carry):
        for u in range(LAT_UNROLL):
            step(LAT_UNROLL * jj + u, u, True)
        return carry

    lax.fori_loop(0, trips, body, 0)
    for j in range(trips * LAT_UNROLL, nk):
        step(j, j, j + ahead < nk)
    acc = acc_ref[...]
    o = jnp.concatenate([acc[:C_VDIM] / acc[C_VDIM:C_VDIM + 1], jnp.zeros((LANES - C_VDIM, tq), F32)], axis=0)
    o_ref[...] = o.T.astype(o_ref.dtype)


def _latent_bounded_kernel(q_ref, k_ref, vt_ref, o_ref, acc_ref, *, tk):
    tq = q_ref.shape[0]
    nk = k_ref.shape[0] // tk
    qt = q_ref[...].astype(F32).T.astype(BF16)
    acc_ref[...] = jnp.zeros(acc_ref.shape, F32)

    def body(jj, carry):
        for u in range(LAT_BOUNDED_UNROLL):
            j = LAT_BOUNDED_UNROLL * jj + u
            start = pl.multiple_of(j * tk, tk)
            s = jnp.dot(k_ref[pl.ds(start, tk), :], qt, preferred_element_type=F32)
            acc_ref[...] += jnp.dot(vt_ref[j], jnp.exp2(s).astype(BF16), preferred_element_type=F32)
        return carry

    lax.fori_loop(0, nk // LAT_BOUNDED_UNROLL, body, 0)
    acc = acc_ref[...]
    o = jnp.concatenate([acc[:C_VDIM] / acc[C_VDIM:C_VDIM + 1], jnp.zeros((LANES - C_VDIM, tq), F32)], axis=0)
    o_ref[...] = o.T.astype(o_ref.dtype)


def _latent(qc, kc, vt, batch, seq, bounded):
    t = batch * seq
    tq, tk = TQ_LAT, vt.shape[2]
    nq = seq // tq
    if bounded:
        body = functools.partial(_latent_bounded_kernel, tk=tk)
        scratch = [pltpu.VMEM((VT_ROWS, tq), F32)]
    else:
        body = functools.partial(_latent_kernel, tk=tk)
        scratch = [pltpu.VMEM((LAT_SLOTS, tk, tq), F32), pltpu.VMEM((LAT_SLOTS, 1, tq), F32),
                   pltpu.VMEM((1, tq), F32), pltpu.VMEM((VT_ROWS, tq), F32)]
    return pl.pallas_call(
        body,
        grid=(batch, C_HEADS, nq),
        in_specs=[pl.BlockSpec((tq, LANES), lambda b, h, i: (b * nq + i, h)),
                  pl.BlockSpec((seq, LANES), lambda b, h, i: (b, h)),
                  pl.BlockSpec((seq // tk, VT_ROWS, tk), lambda b, h, i: (b, h, 0))],
        out_specs=pl.BlockSpec((tq, LANES), lambda b, h, i: (b * nq + i, h)),
        out_shape=jax.ShapeDtypeStruct((t, C_SLOTS), BF16),
        scratch_shapes=scratch,
        compiler_params=_cparams(("parallel", "parallel", "parallel")),
        name="latent_bounded" if bounded else "latent",
    )(qc, kc, vt)


def _merge_mlp_kernel(x_ref, oa1_ref, oa2_ref, oa3_ref, ml1_ref, ml2_ref, ml3_ref, ob_ref, oc_ref,
                      expand_ref, gout_ref, wout_ref, gmlp_ref, wup_ref, wdown_ref, out_ref, oa_s, ml_s):
    tm = x_ref.shape[0]
    oas, mls = [oa1_ref[...]], [ml1_ref[...]]
    for idx, (o_ref, l_ref) in enumerate(((oa2_ref, ml2_ref), (oa3_ref, ml3_ref))):
        r = A_CONFIGS[idx + 1][1]
        for c in range(r):
            for s in range(A_W // LANES):
                col = c * A_W + s * LANES
                oa_s[idx, s, pl.ds(c, tm // r, stride=r), :] = o_ref[:, col:col + LANES]
            ml_s[idx, pl.ds(c, tm // r, stride=r), :] = l_ref[:, c * LANES:(c + 1) * LANES]
        oas.append(jnp.concatenate([oa_s[idx, s] for s in range(A_W // LANES)], axis=1))
        mls.append(ml_s[idx])
    lane = lax.broadcasted_iota(jnp.int32, mls[0].shape, 1)
    m_all = jnp.maximum(jnp.maximum(mls[0], mls[1]), mls[2])
    ws = [pltpu.roll(ml, LANES - ML_L_OFFSET, 1) * jnp.exp2(ml - m_all) for ml in mls]
    wsum = ws[0] + ws[1] + ws[2]
    packed = jnp.zeros_like(wsum)
    for c, w in enumerate(ws):
        wn = jnp.where(lane < A_HEADS, w / wsum, 0.0)
        packed = packed + (wn if c == 0 else pltpu.roll(wn, ML_L_OFFSET * c, 1))
    hi = packed.astype(BF16)
    lo = (packed - hi.astype(F32)).astype(BF16)
    spread = (jnp.dot(hi, expand_ref[...], preferred_element_type=F32)
              + jnp.dot(lo, expand_ref[...], preferred_element_type=F32))
    oa = spread[:, :A_W] * oas[0] + spread[:, A_W:2 * A_W] * oas[1] + spread[:, 2 * A_W:] * oas[2]

    def group_norm(v, width):
        return v * lax.rsqrt(jnp.sum(v * v, axis=-1, keepdims=True) * (1.0 / width) + EPS)

    ob = ob_ref[...].astype(F32)
    oc_wide = oc_ref[...].astype(F32)
    half = lax.broadcasted_iota(jnp.int32, (tm, LANES), 1) < C_VDIM
    oc = jnp.concatenate(
        [jnp.where(half, oc_wide[:, 2 * p * LANES:(2 * p + 1) * LANES],
                   pltpu.roll(oc_wide[:, (2 * p + 1) * LANES:(2 * p + 2) * LANES], C_VDIM, 1))
         for p in range(C_HEADS // 2)], axis=1)
    mixed = jnp.concatenate([group_norm(oa, A_W), group_norm(ob, B_QW), group_norm(oc, C_HEADS * C_VDIM)], axis=1)
    mixed = (mixed * gout_ref[...]).astype(BF16)
    x = x_ref[...] + jnp.dot(mixed, wout_ref[...], preferred_element_type=F32)

    h = (x * lax.rsqrt(jnp.mean(x * x, axis=-1, keepdims=True) + EPS) * gmlp_ref[...]).astype(BF16)
    acc = x
    for s in range(0, D_FF, FF_CHUNK):
        u = jnp.dot(h, wup_ref[:, s:s + FF_CHUNK], preferred_element_type=F32)
        u = jnp.square(jnp.maximum(u, 0.0)).astype(BF16)
        acc = acc + jnp.dot(u, wdown_ref[s:s + FF_CHUNK, :], preferred_element_type=F32)
    out_ref[...] = acc


def _merge_mlp(x2d, oas, mls, ob, oc, p):
    t = x2d.shape[0]
    tm = TM_MLP
    row = lambda w: pl.BlockSpec((tm, w), lambda i: (i, 0))
    strided = lambda w, r: pl.BlockSpec((tm // r, r * w), lambda i: (i, 0))
    single = lambda shape: pl.BlockSpec(shape, lambda i: (0,) * len(shape), pipeline_mode=pl.Buffered(1))
    return pl.pallas_call(
        _merge_mlp_kernel,
        grid=(t // tm,),
        in_specs=[row(D_MODEL)] + [strided(A_W, r) for _, r in A_CONFIGS]
        + [strided(LANES, r) for _, r in A_CONFIGS] + [row(B_QW), row(C_SLOTS),
                  single((LANES, 3 * A_W)), single((1, MIX_COLS)), single((MIX_COLS, D_MODEL)),
                  single((1, D_MODEL)), single((D_MODEL, D_FF)), single((D_FF, D_MODEL))],
        out_specs=row(D_MODEL),
        out_shape=jax.ShapeDtypeStruct((t, D_MODEL), F32),
        scratch_shapes=[pltpu.VMEM((2, A_W // LANES, tm, LANES), F32), pltpu.VMEM((2, tm, LANES), F32)],
        compiler_params=_cparams(("parallel",)),
        name="merge_mlp",
    )(x2d, *oas, *mls, ob, oc, p["expand"], p["gout"], p["w_out"], p["gmlp"], p["w_up"], p["w_down"])


def _block_diag_ones(group):
    idx = np.arange(2 * LANES) // group
    return jnp.asarray((idx[:, None] == idx[None, :]).astype(np.float32), dtype=BF16)


def _expand_matrix():
    e = np.zeros((LANES, len(A_CONFIGS) * A_W), np.float32)
    for c in range(len(A_CONFIGS)):
        for h in range(A_HEADS):
            e[ML_L_OFFSET * c + h, c * A_W + h * HEAD_DIM:c * A_W + (h + 1) * HEAD_DIM] = 1.0
    return jnp.asarray(e, dtype=BF16)


def _pad_heads(w, heads, used, lo=0):
    lead = w.shape[:-1]
    w = w.reshape(lead + (heads, used))
    pad = [(0, 0)] * len(lead) + [(0, 0), (lo, LANES - lo - used)]
    return jnp.pad(w, pad).reshape(lead + (heads * LANES,))


def _layer_params(i, norm_mix, w_in, qk_gain_a, qk_gain_b, q_lat_gain, kv_lat_gain, w_uq, w_ukv, qk_gain_c,
                  out_norm, w_out, norm_mlp, w_up, w_down):
    w = w_in[i]
    o = np.cumsum((A_W, A_W, A_W, B_QW, B_KVW, B_KVW, C_Q_RANK, C_KV_RANK)).tolist()
    qb = w[:, o[2]:o[3]].reshape(D_MODEL, B_HEADS, HEAD_DIM)[:, B_HEAD_ORDER, :].reshape(D_MODEL, B_QW)
    kr = _pad_heads(w[:, o[7]:], 1, C_ROPE, lo=C_NOPE)
    w_in_p = jnp.concatenate([w[:, :o[2]], qb, w[:, o[3]:o[7]], kr], axis=1).astype(BF16)

    scale = HEAD_DIM ** -0.5 * LOG2E
    gqk = jnp.concatenate([jnp.tile(qk_gain_a[i, 0], A_HEADS) * scale, jnp.tile(qk_gain_a[i, 1], A_HEADS),
                           jnp.tile(qk_gain_b[i, 0], B_HEADS) * scale, jnp.tile(qk_gain_b[i, 1], B_KV_HEADS)])
    ukv = w_ukv[i].reshape(C_KV_RANK, C_HEADS, C_NOPE + C_VDIM)
    wuk = _pad_heads(ukv[:, :, :C_NOPE].reshape(C_KV_RANK, -1), C_HEADS, C_NOPE)
    wuvt = ukv[:, :, C_NOPE:].reshape(C_KV_RANK, C_HEADS * C_VDIM).T
    g = out_norm[i]
    gb = g[A_W:A_W + B_QW].reshape(B_HEADS, HEAD_DIM)[B_HEAD_ORDER, :].reshape(B_QW)
    gout = jnp.concatenate([g[:A_W], gb, g[A_W + B_QW:]])
    wo = w_out[i]
    wob = wo[A_W:A_W + B_QW].reshape(B_HEADS, HEAD_DIM, D_MODEL)[B_HEAD_ORDER, :, :].reshape(B_QW, D_MODEL)
    w_out_p = jnp.concatenate([wo[:A_W], wob, wo[A_W + B_QW:]], axis=0).astype(BF16)
    bound = (C_QK ** 0.5 * LOG2E * BOUND_MARGIN) * jnp.max(jnp.abs(qk_gain_c[i, 0])) * jnp.max(jnp.abs(qk_gain_c[i, 1]))
    return {
        "gmix": norm_mix[i][None, :],
        "w_in": w_in_p,
        "gqk": gqk[None, :],
        "ones64": _block_diag_ones(HEAD_DIM),
        "ones128": _block_diag_ones(LANES),
        "glq": q_lat_gain[i][None, :],
        "wuq": _pad_heads(w_uq[i], C_HEADS, C_QK).astype(BF16),
        "gqc": jnp.tile(_pad_heads(qk_gain_c[i, 0] * (C_QK ** -0.5 * LOG2E), 1, C_QK), C_HEADS)[None, :],
        "glkv": kv_lat_gain[i][None, :],
        "wuk": wuk.astype(BF16),
        "wuvt": wuvt.astype(BF16),
        "gkc": jnp.tile(_pad_heads(qk_gain_c[i, 1], 1, C_QK), C_HEADS)[None, :],
        "qshift": jnp.tile(_pad_heads(-bound[None], 1, 1, lo=C_QK), C_HEADS)[None, :],
        "kone": jnp.tile(_pad_heads(jnp.ones((1,), F32), 1, 1, lo=C_QK), C_HEADS)[None, :],
        "logit_bound": bound,
        "expand": _expand_matrix(),
        "gout": gout[None, :],
        "w_out": w_out_p,
        "gmlp": norm_mlp[i][None, :],
        "w_up": w_up[i].astype(BF16),
        "w_down": w_down[i].astype(BF16),
    }


def kernel(x, positions, rel_bias_table, norm_mix, w_in, qk_gain_a, qk_gain_b, sink_b, q_lat_gain, kv_lat_gain,
           w_uq, w_ukv, qk_gain_c, out_norm, w_out, norm_mlp, w_up, w_down):
    batch, seq, _ = x.shape
    depth = w_in.shape[0]
    x2d = x.reshape(batch * seq, D_MODEL)
    cos_t, sin_t = _rope_tables(positions)
    a_pairs = tuple((2 * p, 2 * p + 1) for p in range(A_HEADS // 2))
    b_pairs = (B_HEAD_ORDER[:2], B_HEAD_ORDER[2:])
    b_cols = tuple(A_HEADS + h for h in B_HEAD_ORDER)

    def bias_tables(shift):
        tiles_a = [_bias_tiles(rel_bias_table, shift, window // (2 * r), r, tuple(range(A_HEADS)))
                   for window, r in A_CONFIGS]
        return tiles_a, _bias_tiles(rel_bias_table, shift, B_RADIUS, 1, b_cols)

    gain_bound = lambda gains: jnp.max(jnp.max(jnp.abs(gains[:, 0]), axis=-1) * jnp.max(jnp.abs(gains[:, 1]), axis=-1))
    qk_bound = HEAD_DIM ** 0.5 * LOG2E * BOUND_MARGIN * jnp.maximum(gain_bound(qk_gain_a), gain_bound(qk_gain_b))
    band_shift = jnp.maximum(qk_bound + LOG2E * jnp.max(jnp.abs(rel_bias_table)), LOG2E * jnp.max(sink_b))
    shifted_tables = bias_tables(band_shift[None])

    a_layout = (3 * A_W, 0, A_W, 2 * A_W, (0, 1, 2))
    b_layout = (B_QW + 2 * B_KVW, 0, B_QW, B_QW + B_KVW, (0, 0))

    def banded_all(bounded, qkv1, qkv4, qkv16, qkvb, sinks):
        shift = band_shift[None] if bounded else jnp.zeros((1,), F32)
        tables_a, table_b = shifted_tables if bounded else bias_tables(shift)
        scal = jnp.concatenate([shift, sinks * LOG2E])
        outs = []
        for (window, r), bias, qkv in zip(A_CONFIGS, tables_a, (qkv1, qkv4, qkv16)):
            outs += _banded(qkv, bias, scal, batch=batch, seq=seq, dilation=r, radius=window // (2 * r),
                            layout=a_layout, head_ids=a_pairs, has_sink=False, emit_ml=True, bounded=bounded,
                            out_dtype=F32)
        outs += _banded(qkvb, table_b, scal, batch=batch, seq=seq, dilation=1, radius=B_RADIUS, layout=b_layout,
                        head_ids=b_pairs, has_sink=True, emit_ml=False, bounded=bounded, out_dtype=BF16)
        return tuple(outs)

    for i in range(depth):
        p = _layer_params(i, norm_mix, w_in, qk_gain_a, qk_gain_b, q_lat_gain, kv_lat_gain, w_uq, w_ukv,
                          qk_gain_c, out_norm, w_out, norm_mlp, w_up, w_down)
        qkv1, qkv4, qkv16, qkvb, qc, kc, vt = _inproj(x2d, cos_t, sin_t, p)
        o1, ml1, o4, ml4, o16, ml16, ob = lax.cond(
            band_shift <= MAX_LOGIT_BOUND, functools.partial(banded_all, True), functools.partial(banded_all, False),
            qkv1, qkv4, qkv16, qkvb, sink_b[i])
        oas, mls = [o1, o4, o16], [ml1, ml4, ml16]
        oc = lax.cond(p["logit_bound"] <= MAX_LOGIT_BOUND,
                      functools.partial(_latent, batch=batch, seq=seq, bounded=True),
                      functools.partial(_latent, batch=batch, seq=seq, bounded=False), qc, kc, vt)
        x2d = _merge_mlp(x2d, oas, mls, ob, oc, p)
    return x2d.reshape(batch, seq, D_MODEL)
```

```python
import functools
import math

import numpy as np
import jax
import jax.numpy as jnp
from jax import lax
from jax.experimental import pallas as pl
from jax.experimental.pallas import tpu as pltpu

F32 = jnp.float32
BF16 = jnp.bfloat16

D_MODEL = 1024
HEAD_DIM = 64
A_HEADS = 6
A_CONFIGS = ((128, 1), (512, 4), (2048, 16))
B_HEADS = 4
B_KV_HEADS = 2
B_RADIUS = 128
C_HEADS = 6
C_NOPE = 64
C_ROPE = 32
C_VDIM = 64
C_QK = C_NOPE + C_ROPE
C_Q_RANK = 256
C_KV_RANK = 128
ROPE_THETA = 10000.0
N_BUCKETS = 32
MAX_DISTANCE = 1024
D_FF = 4 * D_MODEL
EPS = 1e-6
NEG = -1e30

A_W = A_HEADS * HEAD_DIM
B_QW = B_HEADS * HEAD_DIM
B_KVW = B_KV_HEADS * HEAD_DIM
LANES = 128
C_SLOTS = C_HEADS * LANES
IN_COLS = 3 * A_W + B_QW + 2 * B_KVW + C_Q_RANK + C_KV_RANK + LANES
MIX_COLS = A_W + B_QW + C_HEADS * C_VDIM
VT_ROWS = 80
LOG2E = math.log2(math.e)
ML_L_OFFSET = 8

B_HEAD_ORDER = (0, 2, 1, 3)

TM_IN = 512
TQ_BAND = 2048
SUB = 128
TQ_LAT = 2048
BOUND_MARGIN = 1.02
MAX_LOGIT_BOUND = 50.0
LAT_BOUNDED_UNROLL = 16
LAT_SLOTS = 3
LAT_UNROLL = 6
TM_MLP = 512
FF_CHUNK = 1024
TM_ROPE = 2048
VMEM_LIMIT = 56 * 1024 * 1024


def _cparams(sem):
    return pltpu.CompilerParams(dimension_semantics=sem, vmem_limit_bytes=VMEM_LIMIT)


def _const_spec(shape):
    zeros = (0,) * len(shape)
    return pl.BlockSpec(shape, lambda *_: zeros)


def _rope_table_kernel(pos_ref, inv_ref, cos_ref, sin_ref):
    ang = pos_ref[...] * inv_ref[...]
    lane = lax.broadcasted_iota(jnp.int32, ang.shape, 1)
    c = jnp.cos(ang)
    s = jnp.sin(ang)
    s = jnp.where(lane % C_ROPE < C_ROPE // 2, -s, s)
    live = (lane >= C_NOPE) & (lane < C_QK)
    per_row = LANES // C_ROPE
    for a in range(per_row):
        shift = (C_NOPE - a * C_ROPE) % LANES
        rows = pl.ds(a, ang.shape[0], stride=per_row)
        cos_ref[rows, :] = jnp.where(live, c if shift == 0 else pltpu.roll(c, shift, 1), 1.0)
        sin_ref[rows, :] = jnp.where(live, s if shift == 0 else pltpu.roll(s, shift, 1), 0.0)


def _rope_tables(positions):
    t = positions.size
    half = C_ROPE // 2
    per_row = LANES // C_ROPE
    inv = ROPE_THETA ** (-jnp.arange(half, dtype=F32) / half)
    inv_row = jnp.tile(inv, 2 * per_row)[None, :]
    pos = jnp.repeat(positions.astype(F32).reshape(t // per_row, per_row), C_ROPE, axis=1)
    tm = TM_ROPE
    return pl.pallas_call(
        _rope_table_kernel,
        grid=(t // tm,),
        in_specs=[pl.BlockSpec((tm // per_row, LANES), lambda i: (i, 0)), _const_spec((1, LANES))],
        out_specs=[pl.BlockSpec((tm, LANES), lambda i: (i, 0))] * 2,
        out_shape=[jax.ShapeDtypeStruct((t, LANES), F32)] * 2,
        compiler_params=_cparams(("parallel",)),
        name="rope_tables",
    )(pos, inv_row)


def _bucket_thresholds():
    half = N_BUCKETS // 2
    exact = half // 2
    n = np.arange(1, 2 * MAX_DISTANCE + 2, dtype=np.float64)
    far = exact + (np.log(n / exact) / math.log(MAX_DISTANCE / exact) * (half - exact)).astype(np.int64)
    far = np.minimum(far, half - 1)
    return tuple(int(n[np.argmax(far >= exact + k)]) for k in range(1, half - exact))


def _bias_kernel(table_ref, shift_ref, out_ref, *, radius, dilation, head_cols):
    hsel = pl.program_id(0)
    width = SUB + 2 * radius
    row = lax.broadcasted_iota(jnp.int32, (SUB, width), 0)
    col = lax.broadcasted_iota(jnp.int32, (SUB, width), 1)
    rel = col - radius - row
    n = jnp.abs(rel) * dilation
    half = N_BUCKETS // 2
    exact = half // 2
    far = jnp.full(n.shape, exact, jnp.int32)
    for thr in _bucket_thresholds():
        far = far + (n >= thr).astype(jnp.int32)
    bucket = jnp.where(rel > 0, half, 0) + jnp.where(n < exact, n, far)
    for idx, hc in enumerate(head_cols):
        @pl.when(hsel == idx)
        def _(hc=hc):
            val = jnp.zeros(n.shape, F32)
            for b in range(N_BUCKETS):
                val = jnp.where(bucket == b, table_ref[b, hc], val)
            out_ref[...] = jnp.where(jnp.abs(rel) <= radius, val * LOG2E - shift_ref[0], NEG)


def _bias_tiles(table, shift, radius, dilation, head_cols):
    width = SUB + 2 * radius
    tiles = pl.pallas_call(
        functools.partial(_bias_kernel, radius=radius, dilation=dilation, head_cols=head_cols),
        grid=(len(head_cols),),
        in_specs=[pl.BlockSpec(memory_space=pltpu.SMEM), pl.BlockSpec(memory_space=pltpu.SMEM)],
        out_specs=pl.BlockSpec((None, SUB, width), lambda h: (h, 0, 0)),
        out_shape=jax.ShapeDtypeStruct((len(head_cols), SUB, width), F32),
        compiler_params=_cparams(("arbitrary",)),
        name="bias_tiles",
    )(table, shift)
    return tiles.reshape(len(head_cols) // 2, 2 * SUB, width)


def _group_mean_sq(y, ones_ref, group):
    sq = (y * y).astype(BF16)
    width = y.shape[1]
    parts = []
    for s in range(0, width, 2 * LANES):
        w = min(2 * LANES, width - s)
        parts.append(jnp.dot(sq[:, s:s + w], ones_ref[:w, :w], preferred_element_type=F32))
    out = parts[0] if len(parts) == 1 else jnp.concatenate(parts, axis=1)
    return out * (1.0 / group)


def _slab_roll(y, shift):
    parts = [pltpu.roll(y[:, s:s + LANES], shift, 1) for s in range(0, y.shape[1], LANES)]
    return parts[0] if len(parts) == 1 else jnp.concatenate(parts, axis=1)


def _rope(y, cos_t, sin_t, lane):
    swapped = jnp.where(lane < C_NOPE + C_ROPE // 2, _slab_roll(y, LANES - C_ROPE // 2), _slab_roll(y, C_ROPE // 2))
    return y * cos_t + swapped * sin_t


def _inproj_kernel(x_ref, cos_ref, sin_ref, gmix_ref, w_ref, gqk_ref, ones64_ref, ones128_ref,
                   glq_ref, wuq_ref, gqc_ref, glkv_ref, wuk_ref, wuvt_ref, gkc_ref, gkr_ref, qshift_ref, kone_ref,
                   qkv1_ref, qkv4_ref, qkv16_ref, qkvb_ref, qc_ref, kc_ref, vt_ref, stage_ref, stage4_ref):
    x = x_ref[...]
    h = x * lax.rsqrt(jnp.mean(x * x, axis=-1, keepdims=True) + EPS) * gmix_ref[...]
    hb = h.astype(BF16)
    bounds = (0, 3 * A_W, 3 * A_W + B_QW + 2 * B_KVW, IN_COLS)
    y = jnp.concatenate([jnp.dot(hb, w_ref[:, lo:hi], preferred_element_type=F32)
                         for lo, hi in zip(bounds[:-1], bounds[1:])], axis=1)

    o_qb = 3 * A_W
    o_kb = o_qb + B_QW
    o_vb = o_kb + B_KVW
    o_cq = o_vb + B_KVW
    o_ckv = o_cq + C_Q_RANK
    o_kr = o_ckv + C_KV_RANK
    yn = jnp.concatenate([y[:, :2 * A_W], y[:, o_qb:o_vb]], axis=1)
    yn = yn * lax.rsqrt(_group_mean_sq(yn, ones64_ref, HEAD_DIM) + EPS) * gqk_ref[...]
    qkvb_ref[:, :B_QW + B_KVW] = yn[:, 2 * A_W:].astype(BF16)
    n_slabs, tm = stage_ref.shape[0], stage_ref.shape[1]
    for s in range(n_slabs):
        src = yn if s * LANES < 2 * A_W else y
        stage_ref[s] = src[:, s * LANES:(s + 1) * LANES]
    for s in range(n_slabs):
        qkv1_ref[:, s * LANES:(s + 1) * LANES] = stage_ref[s].astype(BF16)
        for c4 in range(4):
            rows = stage_ref[s, pl.ds(c4, tm // 4, stride=4), :]
            qkv4_ref[:, c4 * 3 * A_W + s * LANES:c4 * 3 * A_W + (s + 1) * LANES] = rows.astype(BF16)
            stage4_ref[c4 * n_slabs + s] = rows
    for c16 in range(16):
        for s in range(n_slabs):
            rows = stage4_ref[(c16 % 4) * n_slabs + s, pl.ds(c16 // 4, tm // 16, stride=4), :]
            qkv16_ref[:, c16 * 3 * A_W + s * LANES:c16 * 3 * A_W + (s + 1) * LANES] = rows.astype(BF16)
    qkvb_ref[:, B_QW + B_KVW:] = y[:, o_vb:o_cq].astype(BF16)

    cos_t = jnp.concatenate([cos_ref[...]] * C_HEADS, axis=1)
    sin_t = jnp.concatenate([sin_ref[...]] * C_HEADS, axis=1)
    lane = lax.broadcasted_iota(jnp.int32, cos_t.shape, 1) % LANES
    cq = y[:, o_cq:o_ckv]
    cq = cq * lax.rsqrt(jnp.mean(cq * cq, axis=-1, keepdims=True) + EPS) * glq_ref[...]
    qc = jnp.dot(cq.astype(BF16), wuq_ref[...], preferred_element_type=F32)
    qc = qc * lax.rsqrt(_group_mean_sq(qc, ones128_ref, C_QK) + EPS) * gqc_ref[...]
    qc_ref[...] = (_rope(qc, cos_t, sin_t, lane) + qshift_ref[...]).astype(BF16)

    ckv = y[:, o_ckv:o_kr]
    ckv = (ckv * lax.rsqrt(jnp.mean(ckv * ckv, axis=-1, keepdims=True) + EPS) * glkv_ref[...]).astype(BF16)
    kr = y[:, o_kr:]
    kn = jnp.dot(ckv, wuk_ref[...], preferred_element_type=F32)
    inv_rms = lax.rsqrt(_group_mean_sq(kn + jnp.concatenate([kr] * C_HEADS, axis=1), ones128_ref, C_QK) + EPS)
    lane_slab = lax.broadcasted_iota(jnp.int32, kr.shape, 1)
    kr_rot = _rope(kr * gkr_ref[...], cos_ref[...], sin_ref[...], lane_slab)
    kc = (kn * gkc_ref[...] + jnp.concatenate([kr_rot] * C_HEADS, axis=1)) * inv_rms
    kc_ref[...] = (kc + kone_ref[...]).astype(BF16)
    vt = lax.dot_general(wuvt_ref[...], ckv, (((1,), (1,)), ((), ())), preferred_element_type=F32)
    row = lax.broadcasted_iota(jnp.int32, (VT_ROWS - C_VDIM, vt.shape[1]), 0)
    ones_rows = jnp.where(row == 0, 1.0, 0.0).astype(BF16)
    for hd in range(C_HEADS):
        vt_ref[0, hd * VT_ROWS:hd * VT_ROWS + C_VDIM, :] = vt[hd * C_VDIM:(hd + 1) * C_VDIM].astype(BF16)
        vt_ref[0, hd * VT_ROWS + C_VDIM:(hd + 1) * VT_ROWS, :] = ones_rows


def _inproj(x2d, cos_t, sin_t, p):
    t = x2d.shape[0]
    tm = TM_IN
    row = lambda w: pl.BlockSpec((tm, w), lambda i: (i, 0))
    outs = [(B_QW + 2 * B_KVW, BF16)] + [(C_SLOTS, BF16)] * 2
    a_specs = [pl.BlockSpec((tm // r, r * 3 * A_W), lambda i: (i, 0)) for _, r in A_CONFIGS]
    a_shapes = [jax.ShapeDtypeStruct((t // r, r * 3 * A_W), BF16) for _, r in A_CONFIGS]
    return pl.pallas_call(
        _inproj_kernel,
        grid=(t // tm,),
        in_specs=[row(D_MODEL), row(LANES), row(LANES),
                  _const_spec((1, D_MODEL)), _const_spec((D_MODEL, IN_COLS)),
                  _const_spec((1, 2 * A_W + B_QW + B_KVW)),
                  _const_spec((2 * LANES, 2 * LANES)), _const_spec((2 * LANES, 2 * LANES)),
                  _const_spec((1, C_Q_RANK)), _const_spec((C_Q_RANK, C_SLOTS)), _const_spec((1, C_SLOTS)),
                  _const_spec((1, C_KV_RANK)), _const_spec((C_KV_RANK, C_SLOTS)),
                  _const_spec((C_HEADS * C_VDIM, C_KV_RANK)), _const_spec((1, C_SLOTS)), _const_spec((1, LANES)),
                  _const_spec((1, C_SLOTS)), _const_spec((1, C_SLOTS))],
        out_specs=a_specs + [row(w) for w, _ in outs]
        + [pl.BlockSpec((1, C_HEADS * VT_ROWS, tm), lambda i: (i, 0, 0))],
        out_shape=a_shapes + [jax.ShapeDtypeStruct((t, w), d) for w, d in outs]
        + [jax.ShapeDtypeStruct((t // tm, C_HEADS * VT_ROWS, tm), BF16)],
        scratch_shapes=[pltpu.VMEM((3 * A_W // LANES, tm, LANES), F32),
                        pltpu.VMEM((4 * 3 * A_W // LANES, tm // 4, LANES), F32)],
        compiler_params=_cparams(("parallel",)),
        name="inproj",
    )(x2d, cos_t, sin_t, p["gmix"], p["w_in"], p["gqk"], p["ones64"], p["ones128"],
      p["glq"], p["wuq"], p["gqc"], p["glkv"], p["wuk"], p["wuvt"], p["gkc"], p["gkc"][:, :LANES], p["qshift"], p["kone"])


def _window(lo_ref, main_ref, hi_ref, a, radius, tq, cols):
    start, end = a - radius, a + SUB + radius
    parts = []
    if start < 0:
        parts.append(lo_ref[radius + start:radius, cols])
        start = 0
    parts.append(main_ref[start:min(end, tq), cols])
    if end > tq:
        parts.append(hi_ref[0:end - tq, cols])
    return parts[0] if len(parts) == 1 else jnp.concatenate(parts, axis=0)


def _banded_kernel(scal_ref, lo_ref, main_ref, hi_ref, bias_ref, *refs, tq, radius, head_ids, layout, group,
                   has_sink, emit_ml, bounded, n_tiles):
    class_w, q_off, k_off, v_off, kv_slab = layout
    n_pairs = len(head_ids)
    o_ref = refs[0]
    ml_ref = refs[1] if emit_ml else None
    p_scr = refs[-1]
    if not bounded:
        s_scr, m_scr = refs[-3:-1]
    sink = lambda hd: scal_ref[1 + hd]
    tile = pl.program_id(2)
    width = SUB + 2 * radius
    lane = lax.broadcasted_iota(jnp.int32, (SUB, LANES), 1)
    col = lax.broadcasted_iota(jnp.int32, (2 * SUB, width), 1)
    low = lane < HEAD_DIM
    slab = lambda base, idx: slice(base + idx * LANES, base + (idx + 1) * LANES)
    tiles = [(a, cls, pair) for a in range(0, tq, SUB) for cls in range(group) for pair in range(n_pairs)]

    for g, (a, cls, pair) in enumerate(tiles):
        qs = main_ref[a:a + SUB, slab(cls * class_w + q_off, pair)]
        ks = _window(lo_ref, main_ref, hi_ref, a, radius, tq, slab(cls * class_w + k_off, kv_slab[pair]))
        zero = jnp.zeros_like(qs)
        q2 = jnp.concatenate([jnp.where(low, qs, zero), jnp.where(low, zero, qs)], axis=0)
        s = lax.dot_general(q2, ks, (((1,), (1,)), ((), ())), preferred_element_type=F32)
        s = s + bias_ref[pair]
        if a == 0:
            s = jnp.where(col < jnp.where(tile == 0, radius, 0), NEG, s)
        if a == tq - SUB:
            s = jnp.where(col >= jnp.where(tile == n_tiles - 1, SUB + radius, width), NEG, s)
        if bounded:
            p_scr[g] = jnp.exp2(s).astype(BF16)
            continue
        m = jnp.max(s, axis=-1, keepdims=True)
        if has_sink:
            sinks = jnp.concatenate([jnp.full((SUB, 1), sink(hd), F32) for hd in head_ids[pair]], axis=0)
            m = jnp.maximum(m, sinks)
        s_scr[g] = s
        m_scr[g] = jnp.broadcast_to(m, (2 * SUB, LANES))

    for g in range(0 if bounded else len(tiles)):
        m_wide = jnp.concatenate([m_scr[g]] * (width // LANES), axis=1)
        p_scr[g] = jnp.exp2(s_scr[g] - m_wide).astype(BF16)

    ml = None
    for g, (a, cls, pair) in enumerate(tiles):
        hd0, hd1 = head_ids[pair]
        vs = _window(lo_ref, main_ref, hi_ref, a, radius, tq, slab(cls * class_w + v_off, kv_slab[pair]))
        o2 = jnp.dot(p_scr[g], jnp.concatenate([vs, jnp.ones_like(vs)], axis=1), preferred_element_type=F32)
        if bounded:
            m0 = m1 = jnp.full((SUB, LANES), scal_ref[0], F32)
        else:
            m0, m1 = m_scr[g][:SUB], m_scr[g][SUB:]
        l0, l1 = o2[:SUB, LANES:], o2[SUB:, LANES:]
        if has_sink:
            l0 = l0 + jnp.exp2(sink(hd0) - m0)
            l1 = l1 + jnp.exp2(sink(hd1) - m1)
        out = jnp.where(low, o2[:SUB, :LANES], o2[SUB:, :LANES]) / jnp.where(low, l0, l1)
        o_ref[a:a + SUB, slab(cls * n_pairs * LANES, pair)] = out.astype(o_ref.dtype)
        if emit_ml:
            if pair == 0:
                ml = jnp.where(lane < ML_L_OFFSET, m0, 0.0) if bounded else jnp.zeros((SUB, LANES), F32)
            if not bounded:
                ml = jnp.where(lane == hd0, m0, ml)
                ml = jnp.where(lane == hd1, m1, ml)
            ml = jnp.where(lane == ML_L_OFFSET + hd0, l0, ml)
            ml = jnp.where(lane == ML_L_OFFSET + hd1, l1, ml)
            if pair == n_pairs - 1:
                ml_ref[a:a + SUB, slab(0, cls)] = ml


def _banded(qkv, bias, scal, *, batch, seq, dilation, radius, layout, head_ids, has_sink, emit_ml, bounded, out_dtype):
    n = seq // dilation
    tq = min(TQ_BAND, n)
    n_tiles = n // tq
    group = min(dilation, TQ_BAND // tq)
    class_w = layout[0]
    qw = len(head_ids) * LANES
    n_sub_tiles = (tq // SUB) * group * len(head_ids)
    per_tile = tq // radius
    main = lambda w: pl.BlockSpec((None, tq, group * w), lambda b, c, i: (b, i, c))
    halo = lambda pick: pl.BlockSpec((None, radius, group * class_w), lambda b, c, i: (b, pick(i), c))
    lo = halo(lambda i: jnp.maximum(i * per_tile - 1, 0))
    hi = halo(lambda i: jnp.minimum((i + 1) * per_tile, n // radius - 1))
    qkv = qkv.reshape(batch, n, dilation * class_w)
    out_specs = [main(qw)]
    out_shape = [jax.ShapeDtypeStruct((batch, n, dilation * qw), out_dtype)]
    if emit_ml:
        out_specs.append(main(LANES))
        out_shape.append(jax.ShapeDtypeStruct((batch, n, dilation * LANES), F32))
    outs = pl.pallas_call(
        functools.partial(_banded_kernel, tq=tq, radius=radius, head_ids=head_ids, layout=layout, group=group,
                          has_sink=has_sink, emit_ml=emit_ml, bounded=bounded, n_tiles=n_tiles),
        grid=(batch, dilation // group, n_tiles),
        in_specs=[pl.BlockSpec(memory_space=pltpu.SMEM), lo, main(class_w), hi, _const_spec(bias.shape)],
        out_specs=out_specs,
        out_shape=out_shape,
        scratch_shapes=([] if bounded else [pltpu.VMEM((n_sub_tiles, 2 * SUB, SUB + 2 * radius), F32),
                                            pltpu.VMEM((n_sub_tiles, 2 * SUB, LANES), F32)])
        + [pltpu.VMEM((n_sub_tiles, 2 * SUB, SUB + 2 * radius), BF16)],
        compiler_params=_cparams(("parallel", "parallel", "parallel")),
        name="banded%s_r%d_d%d" % ("_bounded" if bounded else "", radius, dilation),
    )(scal, qkv, qkv, qkv, bias)
    return [o.reshape(batch * n, -1) for o in outs]


def _latent_kernel(q_ref, k_ref, vt_ref, o_ref, s_ref, cmax_ref, m_ref, acc_ref, *, tk):
    tq = q_ref.shape[0]
    nk = k_ref.shape[0] // tk
    ahead = LAT_SLOTS - 1
    qt = q_ref[...].astype(F32).T.astype(BF16)
    m_ref[...] = jnp.full(m_ref.shape, -jnp.inf, F32)
    acc_ref[...] = jnp.zeros(acc_ref.shape, F32)

    def scores(slot, j):
        start = pl.multiple_of(j * tk, tk)
        s = jnp.dot(k_ref[pl.ds(start, tk), :], qt, preferred_element_type=F32)
        s_ref[slot] = s
        cmax_ref[slot] = jnp.max(s, axis=0, keepdims=True)

    def consume(slot, j):
        m_old = m_ref[...]
        m_new = jnp.maximum(m_old, cmax_ref[slot])
        pr = jnp.exp2(s_ref[slot] - m_new).astype(BF16)
        acc_ref[...] = jnp.exp2(m_old - m_new) * acc_ref[...] + jnp.dot(
            vt_ref[j], pr, preferred_element_type=F32)
        m_ref[...] = m_new

    def step(j, u, with_scores):
        if with_scores:
            scores((u + ahead) % LAT_SLOTS, j + ahead)
        consume(u % LAT_SLOTS, j)

    for j in range(ahead):
        scores(j, j)
    trips = (nk - ahead) // LAT_UNROLL

    def body(jj, carry):
        for u in range(LAT_UNROLL):
            step(LAT_UNROLL * jj + u, u, True)
        return carry

    lax.fori_loop(0, trips, body, 0)
    for j in range(trips * LAT_UNROLL, nk):
        step(j, j, j + ahead < nk)
    acc = acc_ref[...]
    o = jnp.concatenate([acc[:C_VDIM] / acc[C_VDIM:C_VDIM + 1], jnp.zeros((LANES - C_VDIM, tq), F32)], axis=0)
    o_ref[...] = o.T.astype(o_ref.dtype)


def _latent_bounded_kernel(q_ref, k_ref, vt_ref, o_ref, acc_ref, *, tk):
    tq = q_ref.shape[0]
    nk = k_ref.shape[0] // tk
    qt = q_ref[...].astype(F32).T.astype(BF16)
    acc_ref[...] = jnp.zeros(acc_ref.shape, F32)

    def body(jj, carry):
        for u in range(LAT_BOUNDED_UNROLL):
            j = LAT_BOUNDED_UNROLL * jj + u
            start = pl.multiple_of(j * tk, tk)
            s = jnp.dot(k_ref[pl.ds(start, tk), :], qt, preferred_element_type=F32)
            acc_ref[...] += jnp.dot(vt_ref[j], jnp.exp2(s).astype(BF16), preferred_element_type=F32)
        return carry

    lax.fori_loop(0, nk // LAT_BOUNDED_UNROLL, body, 0)
    acc = acc_ref[...]
    o = jnp.concatenate([acc[:C_VDIM] / acc[C_VDIM:C_VDIM + 1], jnp.zeros((LANES - C_VDIM, tq), F32)], axis=0)
    o_ref[...] = o.T.astype(o_ref.dtype)


def _latent(qc, kc, vt, batch, seq, bounded):
    t = batch * seq
    tq, tk = TQ_LAT, vt.shape[2]
    nq = seq // tq
    if bounded:
        body = functools.partial(_latent_bounded_kernel, tk=tk)
        scratch = [pltpu.VMEM((VT_ROWS, tq), F32)]
    else:
        body = functools.partial(_latent_kernel, tk=tk)
        scratch = [pltpu.VMEM((LAT_SLOTS, tk, tq), F32), pltpu.VMEM((LAT_SLOTS, 1, tq), F32),
                   pltpu.VMEM((1, tq), F32), pltpu.VMEM((VT_ROWS, tq), F32)]
    return pl.pallas_call(
        body,
        grid=(batch, C_HEADS, nq),
        in_specs=[pl.BlockSpec((tq, LANES), lambda b, h, i: (b * nq + i, h)),
                  pl.BlockSpec((seq, LANES), lambda b, h, i: (b, h)),
                  pl.BlockSpec((seq // tk, VT_ROWS, tk), lambda b, h, i: (b, h, 0))],
        out_specs=pl.BlockSpec((tq, LANES), lambda b, h, i: (b * nq + i, h)),
        out_shape=jax.ShapeDtypeStruct((t, C_SLOTS), BF16),
        scratch_shapes=scratch,
        compiler_params=_cparams(("parallel", "parallel", "parallel")),
        name="latent_bounded" if bounded else "latent",
    )(qc, kc, vt)


def _merge_mlp_kernel(x_ref, oa1_ref, oa2_ref, oa3_ref, ml1_ref, ml2_ref, ml3_ref, ob_ref, oc_ref,
                      expand_ref, gout_ref, wout_ref, gmlp_ref, wup_ref, wdown_ref, out_ref, oa_s, ml_s):
    tm = x_ref.shape[0]
    oas, mls = [oa1_ref[...]], [ml1_ref[...]]
    for idx, (o_ref, l_ref) in enumerate(((oa2_ref, ml2_ref), (oa3_ref, ml3_ref))):
        r = A_CONFIGS[idx + 1][1]
        for c in range(r):
            for s in range(A_W // LANES):
                col = c * A_W + s * LANES
                oa_s[idx, s, pl.ds(c, tm // r, stride=r), :] = o_ref[:, col:col + LANES]
            ml_s[idx, pl.ds(c, tm // r, stride=r), :] = l_ref[:, c * LANES:(c + 1) * LANES]
        oas.append(jnp.concatenate([oa_s[idx, s] for s in range(A_W // LANES)], axis=1))
        mls.append(ml_s[idx])
    lane = lax.broadcasted_iota(jnp.int32, mls[0].shape, 1)
    m_all = jnp.maximum(jnp.maximum(mls[0], mls[1]), mls[2])
    ws = [pltpu.roll(ml, LANES - ML_L_OFFSET, 1) * jnp.exp2(ml - m_all) for ml in mls]
    wsum = ws[0] + ws[1] + ws[2]
    packed = jnp.zeros_like(wsum)
    for c, w in enumerate(ws):
        wn = jnp.where(lane < A_HEADS, w / wsum, 0.0)
        packed = packed + (wn if c == 0 else pltpu.roll(wn, ML_L_OFFSET * c, 1))
    hi = packed.astype(BF16)
    lo = (packed - hi.astype(F32)).astype(BF16)
    spread = (jnp.dot(hi, expand_ref[...], preferred_element_type=F32)
              + jnp.dot(lo, expand_ref[...], preferred_element_type=F32))
    oa = spread[:, :A_W] * oas[0] + spread[:, A_W:2 * A_W] * oas[1] + spread[:, 2 * A_W:] * oas[2]

    def group_norm(v, width):
        return v * lax.rsqrt(jnp.sum(v * v, axis=-1, keepdims=True) * (1.0 / width) + EPS)

    ob = ob_ref[...].astype(F32)
    oc_wide = oc_ref[...].astype(F32)
    half = lax.broadcasted_iota(jnp.int32, (tm, LANES), 1) < C_VDIM
    oc = jnp.concatenate(
        [jnp.where(half, oc_wide[:, 2 * p * LANES:(2 * p + 1) * LANES],
                   pltpu.roll(oc_wide[:, (2 * p + 1) * LANES:(2 * p + 2) * LANES], C_VDIM, 1))
         for p in range(C_HEADS // 2)], axis=1)
    mixed = jnp.concatenate([group_norm(oa, A_W), group_norm(ob, B_QW), group_norm(oc, C_HEADS * C_VDIM)], axis=1)
    mixed = (mixed * gout_ref[...]).astype(BF16)
    x = x_ref[...] + jnp.dot(mixed, wout_ref[...], preferred_element_type=F32)

    h = (x * lax.rsqrt(jnp.mean(x * x, axis=-1, keepdims=True) + EPS) * gmlp_ref[...]).astype(BF16)
    acc = x
    for s in range(0, D_FF, FF_CHUNK):
        u = jnp.dot(h, wup_ref[:, s:s + FF_CHUNK], preferred_element_type=F32)
        u = jnp.square(jnp.maximum(u, 0.0)).astype(BF16)
        acc = acc + jnp.dot(u, wdown_ref[s:s + FF_CHUNK, :], preferred_element_type=F32)
    out_ref[...] = acc


def _merge_mlp(x2d, oas, mls, ob, oc, p):
    t = x2d.shape[0]
    tm = TM_MLP
    row = lambda w: pl.BlockSpec((tm, w), lambda i: (i, 0))
    strided = lambda w, r: pl.BlockSpec((tm // r, r * w), lambda i: (i, 0))
    single = lambda shape: pl.BlockSpec(shape, lambda i: (0,) * len(shape), pipeline_mode=pl.Buffered(1))
    return pl.pallas_call(
        _merge_mlp_kernel,
        grid=(t // tm,),
        in_specs=[row(D_MODEL)] + [strided(A_W, r) for _, r in A_CONFIGS]
        + [strided(LANES, r) for _, r in A_CONFIGS] + [row(B_QW), row(C_SLOTS),
                  single((LANES, 3 * A_W)), single((1, MIX_COLS)), single((MIX_COLS, D_MODEL)),
                  single((1, D_MODEL)), single((D_MODEL, D_FF)), single((D_FF, D_MODEL))],
        out_specs=row(D_MODEL),
        out_shape=jax.ShapeDtypeStruct((t, D_MODEL), F32),
        scratch_shapes=[pltpu.VMEM((2, A_W // LANES, tm, LANES), F32), pltpu.VMEM((2, tm, LANES), F32)],
        compiler_params=_cparams(("parallel",)),
        name="merge_mlp",
    )(x2d, *oas, *mls, ob, oc, p["expand"], p["gout"], p["w_out"], p["gmlp"], p["w_up"], p["w_down"])


def _block_diag_ones(group):
    idx = np.arange(2 * LANES) // group
    return jnp.asarray((idx[:, None] == idx[None, :]).astype(np.float32), dtype=BF16)


def _expand_matrix():
    e = np.zeros((LANES, len(A_CONFIGS) * A_W), np.float32)
    for c in range(len(A_CONFIGS)):
        for h in range(A_HEADS):
            e[ML_L_OFFSET * c + h, c * A_W + h * HEAD_DIM:c * A_W + (h + 1) * HEAD_DIM] = 1.0
    return jnp.asarray(e, dtype=BF16)


def _pad_heads(w, heads, used, lo=0):
    lead = w.shape[:-1]
    w = w.reshape(lead + (heads, used))
    pad = [(0, 0)] * len(lead) + [(0, 0), (lo, LANES - lo - used)]
    return jnp.pad(w, pad).reshape(lead + (heads * LANES,))


def _layer_params(i, norm_mix, w_in, qk_gain_a, qk_gain_b, q_lat_gain, kv_lat_gain, w_uq, w_ukv, qk_gain_c,
                  out_norm, w_out, norm_mlp, w_up, w_down):
    w = w_in[i]
    o = np.cumsum((A_W, A_W, A_W, B_QW, B_KVW, B_KVW, C_Q_RANK, C_KV_RANK)).tolist()
    qb = w[:, o[2]:o[3]].reshape(D_MODEL, B_HEADS, HEAD_DIM)[:, B_HEAD_ORDER, :].reshape(D_MODEL, B_QW)
    kr = _pad_heads(w[:, o[7]:], 1, C_ROPE, lo=C_NOPE)
    w_in_p = jnp.concatenate([w[:, :o[2]], qb, w[:, o[3]:o[7]], kr], axis=1).astype(BF16)

    scale = HEAD_DIM ** -0.5 * LOG2E
    gqk = jnp.concatenate([jnp.tile(qk_gain_a[i, 0], A_HEADS) * scale, jnp.tile(qk_gain_a[i, 1], A_HEADS),
                           jnp.tile(qk_gain_b[i, 0], B_HEADS) * scale, jnp.tile(qk_gain_b[i, 1], B_KV_HEADS)])
    ukv = w_ukv[i].reshape(C_KV_RANK, C_HEADS, C_NOPE + C_VDIM)
    wuk = _pad_heads(ukv[:, :, :C_NOPE].reshape(C_KV_RANK, -1), C_HEADS, C_NOPE)
    wuvt = ukv[:, :, C_NOPE:].reshape(C_KV_RANK, C_HEADS * C_VDIM).T
    g = out_norm[i]
    gb = g[A_W:A_W + B_QW].reshape(B_HEADS, HEAD_DIM)[B_HEAD_ORDER, :].reshape(B_QW)
    gout = jnp.concatenate([g[:A_W], gb, g[A_W + B_QW:]])
    wo = w_out[i]
    wob = wo[A_W:A_W + B_QW].reshape(B_HEADS, HEAD_DIM, D_MODEL)[B_HEAD_ORDER, :, :].reshape(B_QW, D_MODEL)
    w_out_p = jnp.concatenate([wo[:A_W], wob, wo[A_W + B_QW:]], axis=0).astype(BF16)
    bound = (C_QK ** 0.5 * LOG2E * BOUND_MARGIN) * jnp.max(jnp.abs(qk_gain_c[i, 0])) * jnp.max(jnp.abs(qk_gain_c[i, 1]))
    return {
        "gmix": norm_mix[i][None, :],
        "w_in": w_in_p,
        "gqk": gqk[None, :],
        "ones64": _block_diag_ones(HEAD_DIM),
        "ones128": _block_diag_ones(LANES),
        "glq": q_lat_gain[i][None, :],
        "wuq": _pad_heads(w_uq[i], C_HEADS, C_QK).astype(BF16),
        "gqc": jnp.tile(_pad_heads(qk_gain_c[i, 0] * (C_QK ** -0.5 * LOG2E), 1, C_QK), C_HEADS)[None, :],
        "glkv": kv_lat_gain[i][None, :],
        "wuk": wuk.astype(BF16),
        "wuvt": wuvt.astype(BF16),
        "gkc": jnp.tile(_pad_heads(qk_gain_c[i, 1], 1, C_QK), C_HEADS)[None, :],
        "qshift": jnp.tile(_pad_heads(-bound[None], 1, 1, lo=C_QK), C_HEADS)[None, :],
        "kone": jnp.tile(_pad_heads(jnp.ones((1,), F32), 1, 1, lo=C_QK), C_HEADS)[None, :],
        "logit_bound": bound,
        "expand": _expand_matrix(),
        "gout": gout[None, :],
        "w_out": w_out_p,
        "gmlp": norm_mlp[i][None, :],
        "w_up": w_up[i].astype(BF16),
        "w_down": w_down[i].astype(BF16),
    }


def kernel(x, positions, rel_bias_table, norm_mix, w_in, qk_gain_a, qk_gain_b, sink_b, q_lat_gain, kv_lat_gain,
           w_uq, w_ukv, qk_gain_c, out_norm, w_out, norm_mlp, w_up, w_down):
    batch, seq, _ = x.shape
    depth = w_in.shape[0]
    x2d = x.reshape(batch * seq, D_MODEL)
    cos_t, sin_t = _rope_tables(positions)
    a_pairs = tuple((2 * p, 2 * p + 1) for p in range(A_HEADS // 2))
    b_pairs = (B_HEAD_ORDER[:2], B_HEAD_ORDER[2:])
    b_cols = tuple(A_HEADS + h for h in B_HEAD_ORDER)

    def bias_tables(shift):
        tiles_a = [_bias_tiles(rel_bias_table, shift, window // (2 * r), r, tuple(range(A_HEADS)))
                   for window, r in A_CONFIGS]
        return tiles_a, _bias_tiles(rel_bias_table, shift, B_RADIUS, 1, b_cols)

    gain_bound = lambda gains: jnp.max(jnp.max(jnp.abs(gains[:, 0]), axis=-1) * jnp.max(jnp.abs(gains[:, 1]), axis=-1))
    qk_bound = HEAD_DIM ** 0.5 * LOG2E * BOUND_MARGIN * jnp.maximum(gain_bound(qk_gain_a), gain_bound(qk_gain_b))
    band_shift = jnp.maximum(qk_bound + LOG2E * jnp.max(jnp.abs(rel_bias_table)), LOG2E * jnp.max(sink_b))
    shifted_tables = bias_tables(band_shift[None])

    a_layout = (3 * A_W, 0, A_W, 2 * A_W, (0, 1, 2))
    b_layout = (B_QW + 2 * B_KVW, 0, B_QW, B_QW + B_KVW, (0, 0))

    def banded_all(bounded, qkv1, qkv4, qkv16, qkvb, sinks):
        shift = band_shift[None] if bounded else jnp.zeros((1,), F32)
        tables_a, table_b = shifted_tables if bounded else bias_tables(shift)
        scal = jnp.concatenate([shift, sinks * LOG2E])
        outs = []
        for (window, r), bias, qkv in zip(A_CONFIGS, tables_a, (qkv1, qkv4, qkv16)):
            outs += _banded(qkv, bias, scal, batch=batch, seq=seq, dilation=r, radius=window // (2 * r),
                            layout=a_layout, head_ids=a_pairs, has_sink=False, emit_ml=True, bounded=bounded,
                            out_dtype=F32)
        outs += _banded(qkvb, table_b, scal, batch=batch, seq=seq, dilation=1, radius=B_RADIUS, layout=b_layout,
                        head_ids=b_pairs, has_sink=True, emit_ml=False, bounded=bounded, out_dtype=BF16)
        return tuple(outs)

    for i in range(depth):
        p = _layer_params(i, norm_mix, w_in, qk_gain_a, qk_gain_b, q_lat_gain, kv_lat_gain, w_uq, w_ukv,
                          qk_gain_c, out_norm, w_out, norm_mlp, w_up, w_down)
        qkv1, qkv4, qkv16, qkvb, qc, kc, vt = _inproj(x2d, cos_t, sin_t, p)
        o1, ml1, o4, ml4, o16, ml16, ob = lax.cond(
            band_shift <= MAX_LOGIT_BOUND, functools.partial(banded_all, True), functools.partial(banded_all, False),
            qkv1, qkv4, qkv16, qkvb, sink_b[i])
        oas, mls = [o1, o4, o16], [ml1, ml4, ml16]
        oc = lax.cond(p["logit_bound"] <= MAX_LOGIT_BOUND,
                      functools.partial(_latent, batch=batch, seq=seq, bounded=True),
                      functools.partial(_latent, batch=batch, seq=seq, bounded=False), qc, kc, vt)
        x2d = _merge_mlp(x2d, oas, mls, ob, oc, p)
    return x2d.reshape(batch, seq, D_MODEL)
```
